```python
import math
import jax
import jax.numpy as jnp
from jax import lax
import numpy as np

D_MODEL = 1024
BATCH = 2
SEQ = 16384
DEPTH = 4

CTX_LEN = 256
GRID_W = 64
N_MIXERS = 3
N_A_LAYERS = len(range(0, DEPTH, N_MIXERS))
N_B_LAYERS = len(range(1, DEPTH, N_MIXERS))
N_C_LAYERS = len(range(2, DEPTH, N_MIXERS))
DN_ALPHA = (2 * DEPTH) ** 0.25
DN_BETA = (8 * DEPTH) ** -0.25
LN_EPS = 1e-5
NORM_EPS = 1e-6
ADA_CHUNKS = 6

DA_HEADS = 8
DA_HEAD_DIM = D_MODEL // DA_HEADS // 2
ROPE_BASE = 10000.0
Q_BLOCK = 128

GDN_K_HEADS = 8
GDN_V_HEADS = 16
GDN_HEAD_DIM = 128
GDN_KEY_DIM = GDN_K_HEADS * GDN_HEAD_DIM
GDN_VAL_DIM = GDN_V_HEADS * GDN_HEAD_DIM
GDN_CONV_W = 5
GDN_IN_DIM = 2 * GDN_KEY_DIM + 2 * GDN_VAL_DIM + 4 * GDN_V_HEADS
CHUNK = 64

GLA_HEADS = 4
GLA_KEY_DIM = D_MODEL // 2
GLA_VAL_DIM = D_MODEL
GLA_DK = GLA_KEY_DIM // GLA_HEADS
GLA_DV = GLA_VAL_DIM // GLA_HEADS
GLA_GATE_RANK = 16
GLA_TAU = 16.0
GLA_IN_DIM = 2 * GLA_KEY_DIM + 2 * GLA_VAL_DIM + 2 * GLA_GATE_RANK

N_EXPERTS = 64
TOP_K = 8
EXPERT_FF = 256
SHARED_FF = 256
ROUTE_SCALE = 2.5
MOE_BLOCK = 128

kernel_name = 'hybrid_diffattn_gdn_gla_moe_flow_backbone'

F32 = jnp.float32


def layer_norm(x, g, b):
    xf = x.astype(F32)
    mu = jnp.mean(xf, -1, keepdims=True)
    var = jnp.mean(jnp.square(xf - mu), -1, keepdims=True)
    return ((xf - mu) * lax.rsqrt(var + LN_EPS) * g + b).astype(x.dtype)


def rms_norm(x, g):
    xf = x.astype(F32)
    return (xf * lax.rsqrt(jnp.mean(xf * xf, -1, keepdims=True) + NORM_EPS) * g).astype(x.dtype)


def l2_norm(x):
    xf = x.astype(F32)
    return (xf * lax.rsqrt(jnp.sum(xf * xf, -1, keepdims=True) + NORM_EPS)).astype(x.dtype)


def post_norm(res, y, gate, g, b):
    return layer_norm(DN_ALPHA * res + gate * y, g, b)


def axial_rope(n_tok, dim):
    rows = n_tok // GRID_W
    row = jnp.repeat(jnp.arange(rows), GRID_W).astype(F32)
    col = jnp.tile(jnp.arange(GRID_W), rows).astype(F32)
    n_freq = dim // 4
    inv = 1.0 / (ROPE_BASE ** (jnp.arange(n_freq, dtype=F32) / n_freq))
    ang = jnp.concatenate([row[:, None] * inv, col[:, None] * inv], -1)
    return jnp.cos(ang), jnp.sin(ang)


def apply_rope(x, cos, sin):
    x1, x2 = jnp.split(x, 2, -1)
    cs = cos[:, None, None, :]
    sn = sin[:, None, None, :]
    return jnp.concatenate([x1 * cs - x2 * sn, x1 * sn + x2 * cs], -1).astype(x.dtype)


def short_conv(x, w):
    ch = x.shape[-1]
    pad = GDN_CONV_W // 2
    return lax.conv_general_dilated(x, w[:, None, :].astype(x.dtype), window_strides=(1,),
                                    padding=[(pad, pad)], dimension_numbers=('NWC', 'WIO', 'NWC'),
                                    feature_group_count=ch)


def to_chunks(t):
    b, l, h = t.shape[:3]
    t = jnp.moveaxis(t, 2, 1)
    return t.reshape(b, h, l // CHUNK, CHUNK, *t.shape[3:])


def from_chunks(t):
    b, h, n, c, d = t.shape
    return jnp.moveaxis(t.reshape(b, h, n * c, d), 1, 2)


def gdn_chunked(q, k, v, g, beta, s0):
    out_dtype = v.dtype
    q, k, v = (to_chunks(t.astype(F32)) for t in (q, k, v))
    g, beta = to_chunks(g.astype(F32)), to_chunks(beta.astype(F32))
    tril = jnp.tril(jnp.ones((CHUNK, CHUNK), bool))
    strict = jnp.tril(jnp.ones((CHUNK, CHUNK), bool), -1)
    eye = jnp.eye(CHUNK, dtype=F32)
    gcum = jnp.cumsum(g, -1)
    decay = jnp.where(tril, jnp.exp(jnp.where(tril, gcum[..., :, None] - gcum[..., None, :], 0.0)), 0.0)
    kb = k * beta[..., None]
    m = jnp.where(strict, jnp.einsum('bhncd,bhnsd->bhncs', kb, k) * decay, 0.0)
    tmat = lax.linalg.triangular_solve(eye + m, jnp.broadcast_to(eye, m.shape), left_side=True, lower=True)
    u = tmat @ (v * beta[..., None])
    w = tmat @ (kb * jnp.exp(gcum)[..., None])
    attn = jnp.where(tril, jnp.einsum('bhncd,bhnsd->bhncs', q, k) * decay, 0.0)
    qg = q * jnp.exp(gcum)[..., None]
    kg = k * jnp.exp(gcum[..., -1:] - gcum)[..., None]
    glast = jnp.exp(gcum[..., -1])

    def step(state, xs):
        u_i, w_i, qg_i, kg_i, a_i, gl_i = xs
        v_new = u_i - w_i @ state
        o = qg_i @ state + a_i @ v_new
        state = state * gl_i[..., None, None] + jnp.swapaxes(kg_i, -1, -2) @ v_new
        return state, o

    xs = tuple(jnp.moveaxis(t, 2, 0) for t in (u, w, qg, kg, attn, glast))
    state, o = lax.scan(step, s0, xs)
    return from_chunks(jnp.moveaxis(o, 0, 2)).astype(out_dtype), state


def gla_chunked(q, k, v, glog, s0):
    out_dtype = v.dtype
    q, k, v, glog = (to_chunks(t.astype(F32)) for t in (q, k, v, glog))
    tril = jnp.tril(jnp.ones((CHUNK, CHUNK), bool))
    bcum = jnp.cumsum(glog, -2)
    qe = q * jnp.exp(bcum)
    ke = k * jnp.exp(-bcum)
    attn = jnp.where(tril, jnp.einsum('bhncd,bhnsd->bhncs', qe, ke), 0.0)
    o_intra = attn @ v
    kg = k * jnp.exp(bcum[..., -1:, :] - bcum)
    glast = jnp.exp(bcum[..., -1, :])

    def step(state, xs):
        qe_i, kg_i, v_i, gl_i = xs
        o = qe_i @ state
        state = state * gl_i[..., :, None] + jnp.swapaxes(kg_i, -1, -2) @ v_i
        return state, o

    xs = tuple(jnp.moveaxis(t, 2, 0) for t in (qe, kg, v, glast))
    state, o_inter = lax.scan(step, s0, xs)
    o = o_intra + jnp.moveaxis(o_inter, 0, 2)
    return from_chunks(o).astype(out_dtype), state


def diff_attention(h_lat, h_ctx, w_in, w_o, lam_vec, subln, lam_init, rope, need_ctx):
    b, n, _ = h_lat.shape
    hh, d = DA_HEADS, DA_HEAD_DIM

    def proj(h):
        q, k, v = jnp.split(h @ w_in, 3, -1)
        l = h.shape[1]
        return q.reshape(b, l, hh, 2, d), k.reshape(b, l, hh, 2, d), v.reshape(b, l, hh, 2 * d)

    ql, kl, vl = proj(h_lat)
    qc, kc, vc = proj(h_ctx)
    cos, sin = rope
    ql = apply_rope(ql, cos, sin)
    kl = apply_rope(kl, cos, sin)
    lv = lam_vec.astype(F32)
    lam = jnp.exp(jnp.sum(lv[0] * lv[1])) - jnp.exp(jnp.sum(lv[2] * lv[3])) + lam_init
    k_all = jnp.concatenate([kc, kl], 1)
    v_all = jnp.concatenate([vc, vl], 1)

    def attend(q, k, v):
        s = jnp.einsum('bqhcd,bkhcd->bhcqk', q, k).astype(F32) * (d ** -0.5)
        p = jax.nn.softmax(s, -1)
        a = (p[:, :, 0] - lam * p[:, :, 1]).astype(v.dtype)
        return jnp.einsum('bhqk,bkhe->bqhe', a, v)

    qb = jnp.swapaxes(ql.reshape(b, n // Q_BLOCK, Q_BLOCK, hh, 2, d), 0, 1)
    ol = lax.map(lambda qblk: attend(qblk, k_all, v_all), qb)
    ol = jnp.swapaxes(ol, 0, 1).reshape(b, n, hh, 2 * d)

    def out(o):
        o = rms_norm(o, subln) * (1.0 - lam_init)
        return o.reshape(o.shape[0], o.shape[1], -1) @ w_o

    y_ctx = out(attend(qc, kc, vc)) if need_ctx else None
    return out(ol), y_ctx


def gated_deltanet(h_lat, h_ctx, w_in, conv_w, a_log, dt_bias, norm_g, w_o, need_ctx):
    hk, hv, dh = GDN_K_HEADS, GDN_V_HEADS, GDN_HEAD_DIM

    def prep(h):
        b, l, _ = h.shape
        p = h @ w_in
        qkv, z, a, bb = jnp.split(p, [2 * GDN_KEY_DIM + GDN_VAL_DIM, 2 * GDN_KEY_DIM + 2 * GDN_VAL_DIM,
                                      2 * GDN_KEY_DIM + 2 * GDN_VAL_DIM + 2 * hv], -1)
        qkv = jax.nn.silu(short_conv(qkv, conv_w))
        q, k, v = jnp.split(qkv, [GDN_KEY_DIM, 2 * GDN_KEY_DIM], -1)
        q = l2_norm(q.reshape(b, l, hk, dh)) * (dh ** -0.5)
        k = l2_norm(k.reshape(b, l, hk, dh))
        q = jnp.repeat(q, hv // hk, axis=2)
        k = jnp.repeat(k, hv // hk, axis=2)
        v = v.reshape(b, l, hv, dh)
        a = a.astype(F32).reshape(b, l, 2, hv)
        g = -jnp.exp(a_log.astype(F32)) * jax.nn.softplus(a + dt_bias.astype(F32))
        beta = jax.nn.sigmoid(bb.astype(F32).reshape(b, l, 2, hv))
        return q, k, v, z.reshape(b, l, hv, dh), g, beta

    def flip(t):
        return jnp.flip(t, 1)

    qc, kc, vc, zc, gc, bc = prep(h_ctx)
    ql, kl, vl, zl, gl, bl = prep(h_lat)
    s0 = jnp.zeros((h_lat.shape[0], hv, dh, dh), F32)
    oc_f, sc_f = gdn_chunked(qc, kc, vc, gc[:, :, 0], bc[:, :, 0], s0)
    oc_b, sc_b = gdn_chunked(flip(qc), flip(kc), flip(vc), flip(gc[:, :, 1]), flip(bc[:, :, 1]), s0)
    ol_f, _ = gdn_chunked(ql, kl, vl, gl[:, :, 0], bl[:, :, 0], sc_f)
    ol_b, _ = gdn_chunked(flip(ql), flip(kl), flip(vl), flip(gl[:, :, 1]), flip(bl[:, :, 1]), sc_b)

    def out(o, z):
        o = rms_norm(o, norm_g) * jax.nn.silu(z)
        return o.reshape(o.shape[0], o.shape[1], -1) @ w_o

    y_ctx = out(oc_f + flip(oc_b), zc) if need_ctx else None
    return out(ol_f + flip(ol_b), zl), y_ctx


def gla_mixer(h_lat, h_ctx, w_in, w_gate, gate_b, norm_g, w_o, need_ctx):
    hh, dk, dv = GLA_HEADS, GLA_DK, GLA_DV

    def prep(h):
        b, l, _ = h.shape
        p = h @ w_in
        q, k, v, r, gr = jnp.split(p, [GLA_KEY_DIM, 2 * GLA_KEY_DIM, 2 * GLA_KEY_DIM + GLA_VAL_DIM,
                                       2 * GLA_KEY_DIM + 2 * GLA_VAL_DIM], -1)
        q = q.reshape(b, l, hh, dk) * (dk ** -0.5)
        k = k.reshape(b, l, hh, dk)
        v = v.reshape(b, l, hh, dv)
        gr = gr.reshape(b, l, 2, GLA_GATE_RANK)
        glog = jax.nn.log_sigmoid(jnp.einsum('blzr,zrk->blzk', gr, w_gate).astype(F32) + gate_b.astype(F32)) / GLA_TAU
        return q, k, v, r.reshape(b, l, hh, dv), glog.reshape(b, l, 2, hh, dk)

    def flip(t):
        return jnp.flip(t, 1)

    qc, kc, vc, rc, gc = prep(h_ctx)
    ql, kl, vl, rl, gl = prep(h_lat)
    s0 = jnp.zeros((h_lat.shape[0], hh, dk, dv), F32)
    oc_f, sc_f = gla_chunked(qc, kc, vc, gc[:, :, 0], s0)
    oc_b, sc_b = gla_chunked(flip(qc), flip(kc), flip(vc), flip(gc[:, :, 1]), s0)
    ol_f, _ = gla_chunked(ql, kl, vl, gl[:, :, 0], sc_f)
    ol_b, _ = gla_chunked(flip(ql), flip(kl), flip(vl), flip(gl[:, :, 1]), sc_b)

    def out(o, r):
        o = rms_norm(o, norm_g) * jax.nn.silu(r)
        return o.reshape(o.shape[0], o.shape[1], -1) @ w_o

    y_ctx = out(oc_f + flip(oc_b), rc) if need_ctx else None
    return out(ol_f + flip(ol_b), rl), y_ctx


def moe_ffn(h, router_w, router_b, w_gu, w_dn, ws_gu, ws_dn):
    t, d = h.shape
    e = w_gu.shape[0]
    scores = jax.nn.sigmoid((h @ router_w).astype(F32))
    _, idx = lax.top_k(scores + router_b.astype(F32), TOP_K)
    wt = jnp.take_along_axis(scores, idx, -1)
    wt = wt / jnp.sum(wt, -1, keepdims=True) * ROUTE_SCALE
    n_pairs = t * TOP_K
    flat_e = idx.reshape(-1)
    order = jnp.argsort(flat_e)
    e_sorted = flat_e[order]
    sizes = jnp.bincount(flat_e, length=e)
    padded = (sizes + MOE_BLOCK - 1) // MOE_BLOCK * MOE_BLOCK
    pad_end = jnp.cumsum(padded)
    pad_start = pad_end - padded
    start = jnp.cumsum(sizes) - sizes
    dest = pad_start[e_sorted] + jnp.arange(n_pairs) - start[e_sorted]
    n_blocks = (n_pairs + e * (MOE_BLOCK - 1) + MOE_BLOCK - 1) // MOE_BLOCK
    n_slots = n_blocks * MOE_BLOCK
    slot_tok = jnp.full((n_slots,), t, jnp.int32).at[dest].set((order // TOP_K).astype(jnp.int32))
    slot_w = jnp.zeros((n_slots,), F32).at[dest].set(wt.reshape(-1)[order])
    block_e = jnp.minimum(jnp.searchsorted(pad_end, jnp.arange(n_blocks) * MOE_BLOCK, side='right'), e - 1)
    h_pad = jnp.concatenate([h, jnp.zeros((1, d), h.dtype)], 0)

    def step(acc, xs):
        tok, wgt, ex = xs
        xb = h_pad[tok]
        g, u = jnp.split(xb @ w_gu[ex], 2, -1)
        yb = ((jax.nn.silu(g) * u) @ w_dn[ex]) * wgt[:, None].astype(h.dtype)
        return acc.at[tok].add(yb), None

    acc, _ = lax.scan(step, jnp.zeros((t + 1, d), h.dtype),
                      (slot_tok.reshape(n_blocks, MOE_BLOCK), slot_w.reshape(n_blocks, MOE_BLOCK), block_e))
    gs, us = jnp.split(h @ ws_gu, 2, -1)
    return acc[:t] + (jax.nn.silu(gs) * us) @ ws_dn


def setup_inputs(seed: int = 0) -> dict:
    key = jax.random.key(seed)
    ks = iter(jax.random.split(key, 40))
    nrm = lambda shape, s: jax.random.normal(next(ks), shape, F32) * s
    gain = lambda shape: 1.0 + nrm(shape, 0.02)
    dt = jnp.exp(jax.random.uniform(next(ks), (N_B_LAYERS, 2, GDN_V_HEADS), F32, math.log(1e-3), math.log(1e-1)))
    return {
        'x': nrm((BATCH, SEQ, D_MODEL), 1.0),
        'c': nrm((BATCH, D_MODEL), 1.0),
        'ctx': nrm((BATCH, CTX_LEN, D_MODEL), 1.0),
        'c_ctx': nrm((D_MODEL,), 1.0),
        'ada_w': nrm((DEPTH, D_MODEL, ADA_CHUNKS * D_MODEL), 0.5 * D_MODEL ** -0.5),
        'ada_b': nrm((DEPTH, ADA_CHUNKS * D_MODEL), 0.02),
        'ln_g': gain((DEPTH, 2, D_MODEL)),
        'ln_b': nrm((DEPTH, 2, D_MODEL), 0.02),
        'da_w_in': nrm((N_A_LAYERS, D_MODEL, 3 * D_MODEL), D_MODEL ** -0.5),
        'da_w_o': nrm((N_A_LAYERS, D_MODEL, D_MODEL), DN_BETA * D_MODEL ** -0.5),
        'da_lambda': nrm((N_A_LAYERS, 4, DA_HEAD_DIM), 0.1),
        'da_subln': gain((N_A_LAYERS, 2 * DA_HEAD_DIM)),
        'gdn_w_in': nrm((N_B_LAYERS, D_MODEL, GDN_IN_DIM), D_MODEL ** -0.5),
        'gdn_conv': nrm((N_B_LAYERS, GDN_CONV_W, 2 * GDN_KEY_DIM + GDN_VAL_DIM), GDN_CONV_W ** -0.5),
        'gdn_a_log': jnp.log(jax.random.uniform(next(ks), (N_B_LAYERS, 2, GDN_V_HEADS), F32, 1.0, 16.0)),
        'gdn_dt_bias': dt + jnp.log(-jnp.expm1(-dt)),
        'gdn_norm': gain((N_B_LAYERS, GDN_HEAD_DIM)),
        'gdn_w_o': nrm((N_B_LAYERS, GDN_VAL_DIM, D_MODEL), DN_BETA * GDN_VAL_DIM ** -0.5),
        'gla_w_in': nrm((N_C_LAYERS, D_MODEL, GLA_IN_DIM), D_MODEL ** -0.5),
        'gla_w_gate': nrm((N_C_LAYERS, 2, GLA_GATE_RANK, GLA_KEY_DIM), GLA_GATE_RANK ** -0.5),
        'gla_gate_b': nrm((N_C_LAYERS, 2, GLA_KEY_DIM), 0.1),
        'gla_norm': gain((N_C_LAYERS, GLA_DV)),
        'gla_w_o': nrm((N_C_LAYERS, GLA_VAL_DIM, D_MODEL), DN_BETA * GLA_VAL_DIM ** -0.5),
        'moe_router': nrm((DEPTH, D_MODEL, N_EXPERTS), D_MODEL ** -0.5),
        'moe_router_b': nrm((DEPTH, N_EXPERTS), 0.01),
        'moe_w_gu': nrm((DEPTH, N_EXPERTS, D_MODEL, 2 * EXPERT_FF), D_MODEL ** -0.5),
        'moe_w_dn': nrm((DEPTH, N_EXPERTS, EXPERT_FF, D_MODEL), DN_BETA * EXPERT_FF ** -0.5),
        'moe_ws_gu': nrm((DEPTH, D_MODEL, 2 * SHARED_FF), D_MODEL ** -0.5),
        'moe_ws_dn': nrm((DEPTH, SHARED_FF, D_MODEL), DN_BETA * SHARED_FF ** -0.5),
    }


def reference(x, c, ctx, c_ctx, ada_w, ada_b, ln_g, ln_b, da_w_in, da_w_o, da_lambda, da_subln,
              gdn_w_in, gdn_conv, gdn_a_log, gdn_dt_bias, gdn_norm, gdn_w_o,
              gla_w_in, gla_w_gate, gla_gate_b, gla_norm, gla_w_o,
              moe_router, moe_router_b, moe_w_gu, moe_w_dn, moe_ws_gu, moe_ws_dn):
    b, n, d = x.shape
    lc = ctx.shape[1]
    rope = axial_rope(n, DA_HEAD_DIM)
    s_lat = jax.nn.silu(c)
    s_ctx = jax.nn.silu(c_ctx)
    for i in range(DEPTH):
        kind, j = i % N_MIXERS, i // N_MIXERS
        need_ctx = i < DEPTH - 1
        m_lat = jnp.split((s_lat @ ada_w[i] + ada_b[i])[:, None, :], ADA_CHUNKS, -1)
        m_ctx = jnp.split((s_ctx @ ada_w[i] + ada_b[i])[None, None, :], ADA_CHUNKS, -1)
        h_lat = x * (1 + m_lat[1]) + m_lat[0]
        h_ctx = ctx * (1 + m_ctx[1]) + m_ctx[0]
        if kind == 0:
            lam_init = 0.8 - 0.6 * math.exp(-0.3 * i)
            y_lat, y_ctx = diff_attention(h_lat, h_ctx, da_w_in[j], da_w_o[j], da_lambda[j], da_subln[j],
                                          lam_init, rope, need_ctx)
        elif kind == 1:
            y_lat, y_ctx = gated_deltanet(h_lat, h_ctx, gdn_w_in[j], gdn_conv[j], gdn_a_log[j], gdn_dt_bias[j],
                                          gdn_norm[j], gdn_w_o[j], need_ctx)
        else:
            y_lat, y_ctx = gla_mixer(h_lat, h_ctx, gla_w_in[j], gla_w_gate[j], gla_gate_b[j], gla_norm[j],
                                     gla_w_o[j], need_ctx)
        x = post_norm(x, y_lat, m_lat[2], ln_g[i, 0], ln_b[i, 0])
        h_lat = x * (1 + m_lat[4]) + m_lat[3]
        moe_args = (moe_router[i], moe_router_b[i], moe_w_gu[i], moe_w_dn[i], moe_ws_gu[i], moe_ws_dn[i])
        if need_ctx:
            ctx = post_norm(ctx, y_ctx, m_ctx[2], ln_g[i, 0], ln_b[i, 0])
            h_ctx = ctx * (1 + m_ctx[4]) + m_ctx[3]
            n_ctx = b * lc
            f = moe_ffn(jnp.concatenate([h_ctx.reshape(n_ctx, d), h_lat.reshape(b * n, d)], 0), *moe_args)
            ctx = post_norm(ctx, f[:n_ctx].reshape(b, lc, d), m_ctx[5], ln_g[i, 1], ln_b[i, 1])
            f_lat = f[n_ctx:].reshape(b, n, d)
        else:
            f_lat = moe_ffn(h_lat.reshape(b * n, d), *moe_args).reshape(b, n, d)
        x = post_norm(x, f_lat, m_lat[5], ln_g[i, 1], ln_b[i, 1])
    return x
```

```python
import functools
import math

import jax
import jax.numpy as jnp
from jax import lax
from jax.experimental import pallas as pl
from jax.experimental.pallas import tpu as pltpu

F32 = jnp.float32
BF16 = jnp.bfloat16

D_MODEL = 1024
DEPTH = 4
GRID_W = 64
N_MIXERS = 3
DN_ALPHA = (2 * DEPTH) ** 0.25
LN_EPS = 1e-5
NORM_EPS = 1e-6
ADA_CHUNKS = 6

DA_HEADS = 8
DA_HEAD_DIM = 64
ROPE_BASE = 10000.0

GDN_K_HEADS = 8
GDN_V_HEADS = 16
GDN_HEAD_DIM = 128
GDN_KEY_DIM = 1024
GDN_VAL_DIM = 2048
GDN_CONV_W = 5
CHUNK = 64

GLA_HEADS = 4
GLA_KEY_DIM = 512
GLA_VAL_DIM = 1024
GLA_DK = 128
GLA_DV = 256
GLA_GATE_RANK = 16
GLA_TAU = 16.0

N_EXPERTS = 64
TOP_K = 8
EXPERT_FF = 256
ROUTE_SCALE = 2.5

LANES = 128
ROW_TILE = 256
SCAN_TILE = 256
HALO = 16
VMEM_LIMIT = 56 * 1024 * 1024


def _cparams(sem):
    return pltpu.CompilerParams(dimension_semantics=sem, vmem_limit_bytes=VMEM_LIMIT)


def _dot(a, b):
    return jnp.dot(a, b, preferred_element_type=F32)


def _dot_nt(a, b):
    return lax.dot_general(a, b, (((1,), (1,)), ((), ())), preferred_element_type=F32)


def _dot_tn(a, b):
    return lax.dot_general(a, b, (((0,), (0,)), ((), ())), preferred_element_type=F32)


def _split2(a):
    hi = a.astype(BF16)
    lo = (a - hi.astype(F32)).astype(BF16)
    return hi, lo


def _split3(a):
    a0 = a.astype(BF16)
    r = a - a0.astype(F32)
    a1 = r.astype(BF16)
    a2 = (r - a1.astype(F32)).astype(BF16)
    return a0, a1, a2


def _dot3(a, b):
    a0, a1 = _split2(a)
    b0, b1 = _split2(b)
    return _dot(a0, b0) + (_dot(a0, b1) + _dot(a1, b0))


def _silu(x):
    return x * jax.nn.sigmoid(x)


def _softplus(x):
    return jnp.maximum(x, 0.0) + jnp.log(1.0 + jnp.exp(-jnp.abs(x)))


def _ada_kernel(c_ref, w_ref, b_ref, o_ref):
    s = _silu(c_ref[...])
    o_ref[0] = _dot3(s, w_ref[0]) + b_ref[0]


def ada_modulation(c8, ada_w, ada_b):
    depth, d, n = ada_w.shape
    tn = 1536
    return pl.pallas_call(
        _ada_kernel,
        out_shape=jax.ShapeDtypeStruct((depth, 8, n), F32),
        grid=(depth, n // tn),
        in_specs=[
            pl.BlockSpec((8, d), lambda i, j: (0, 0)),
            pl.BlockSpec((1, d, tn), lambda i, j: (i, 0, j)),
            pl.BlockSpec((1, 1, tn), lambda i, j: (i, 0, j)),
        ],
        out_specs=pl.BlockSpec((1, 8, tn), lambda i, j: (i, 0, j)),
        compiler_params=_cparams(("arbitrary", "arbitrary")),
        name="ada_modulation",
    )(c8, ada_w, ada_b.reshape(depth, 1, n))


def _proj_kernel(*refs, n_main, n_rope, has_small, cn):
    it = iter(refs)
    x_ref, sc_ref, sh_ref, w_ref = next(it), next(it), next(it), next(it)
    ws_ref = next(it) if has_small else None
    cos_ref = next(it) if n_rope else None
    sin_ref = next(it) if n_rope else None
    o_ref = next(it)
    os_ref = next(it) if has_small else None

    h = x_ref[0] * (1.0 + sc_ref[0, 0]) + sh_ref[0, 0]
    hb = h.astype(BF16)
    tm = hb.shape[0]
    if n_rope:
        cos = cos_ref[...]
        sin = sin_ref[...]
        lane = lax.broadcasted_iota(jnp.int32, (tm, LANES), 1)
        low_half = (lane & 32) == 0
    for j in range(n_main // cn):
        p = _dot(hb, w_ref[:, j * cn:(j + 1) * cn])
        for g in range(cn // LANES):
            pg = p[:, g * LANES:(g + 1) * LANES]
            col = j * cn + g * LANES
            if col < n_rope:
                partner = jnp.where(low_half, pltpu.roll(pg, LANES - 32, 1), pltpu.roll(pg, 32, 1))
                pg = pg * cos + partner * sin
            o_ref[0, :, col:col + LANES] = pg.astype(BF16)
    if has_small:
        os_ref[0] = _dot(hb, ws_ref[...])


def modulated_projection(x, sc, sh, w, w_small=None, rope=None, n_rope=0, n_lat_tiles=1):
    b, s, d = x.shape
    n_main = w.shape[1]
    tm = ROW_TILE
    grid = (b, s // tm)
    mod_spec = pl.BlockSpec((1, 1, 1, d), lambda bi, i: (bi, i // n_lat_tiles, 0, 0))
    in_specs = [pl.BlockSpec((1, tm, d), lambda bi, i: (bi, i, 0)), mod_spec, mod_spec,
                pl.BlockSpec((d, n_main), lambda bi, i: (0, 0))]
    args = [x, sc, sh, w]
    out_shape = [jax.ShapeDtypeStruct((b, s, n_main), BF16)]
    out_specs = [pl.BlockSpec((1, tm, n_main), lambda bi, i: (bi, i, 0))]
    if w_small is not None:
        in_specs.append(pl.BlockSpec((d, LANES), lambda bi, i: (0, 0)))
        args.append(w_small)
        out_shape.append(jax.ShapeDtypeStruct((b, s, LANES), F32))
        out_specs.append(pl.BlockSpec((1, tm, LANES), lambda bi, i: (bi, i, 0)))
    if n_rope:
        tab = pl.BlockSpec((tm, LANES), lambda bi, i: (i, 0))
        in_specs += [tab, tab]
        args += [rope[0], rope[1]]
    kern = functools.partial(_proj_kernel, n_main=n_main, n_rope=n_rope,
                             has_small=w_small is not None, cn=512)
    out = pl.pallas_call(
        kern, out_shape=out_shape, grid=grid, in_specs=in_specs, out_specs=out_specs,
        compiler_params=_cparams(("arbitrary", "arbitrary")), name="modulated_projection",
    )(*args)
    return out if w_small is not None else out[0]


def _flash_kernel(lam_ref, q_ref, k_ref, v_ref, sub_ref, *rest, lam_init, nk, aliased):
    if aliased:
        rest = rest[1:]
    o_ref, m_sc, l_sc, acc_sc = rest
    ki = pl.program_id(3)

    @pl.when(ki == 0)
    def _():
        m_sc[...] = jnp.full(m_sc.shape, -jnp.inf, F32)
        l_sc[...] = jnp.zeros(l_sc.shape, F32)
        acc_sc[...] = jnp.zeros(acc_sc.shape, F32)

    q = q_ref[0]
    k = k_ref[0]
    v = v_ref[0]
    hd = DA_HEAD_DIM
    for c in range(2):
        s = _dot_nt(q[:, c * hd:(c + 1) * hd], k[:, c * hd:(c + 1) * hd])
        m_prev = m_sc[c]
        m_new = jnp.maximum(m_prev, jnp.max(s, axis=-1, keepdims=True))
        alpha = jnp.exp(m_prev - m_new)
        p = jnp.exp(s - m_new)
        l_sc[c] = alpha * l_sc[c] + jnp.sum(p, axis=-1, keepdims=True)
        acc_sc[c] = alpha * acc_sc[c] + _dot(p.astype(BF16), v)
        m_sc[c] = m_new

    @pl.when(ki == nk - 1)
    def _():
        lv = lam_ref[...]
        lam = (jnp.exp(jnp.sum(lv[0:1] * lv[1:2], axis=-1, keepdims=True))
               - jnp.exp(jnp.sum(lv[2:3] * lv[3:4], axis=-1, keepdims=True)) + lam_init)
        o = acc_sc[0] / l_sc[0] - lam * (acc_sc[1] / l_sc[1])
        ms = jnp.mean(o * o, axis=-1, keepdims=True)
        o = o * lax.rsqrt(ms + NORM_EPS) * sub_ref[...] * (1.0 - lam_init)
        o_ref[0] = o.astype(BF16)


def diff_flash_attention(p_all, lam_vec, subln, lam_init, *, tq, tk, nq, nk, q_off, k_off, prev=None):
    b, s, _ = p_all.shape
    hh = DA_HEADS
    in_specs = [
        pl.BlockSpec((4, DA_HEAD_DIM), lambda bi, h, qi, ki: (0, 0)),
        pl.BlockSpec((1, tq, LANES), lambda bi, h, qi, ki: (bi, qi + q_off, h)),
        pl.BlockSpec((1, tk, LANES), lambda bi, h, qi, ki: (bi, ki + k_off, hh + h)),
        pl.BlockSpec((1, tk, LANES), lambda bi, h, qi, ki: (bi, ki + k_off, 2 * hh + h)),
        pl.BlockSpec((1, LANES), lambda bi, h, qi, ki: (0, 0)),
    ]
    args = [lam_vec, p_all, p_all, p_all, subln.reshape(1, LANES)]
    aliases = {}
    if prev is not None:
        in_specs.append(pl.BlockSpec(memory_space=pl.ANY))
        args.append(prev)
        aliases = {5: 0}
    kern = functools.partial(_flash_kernel, lam_init=lam_init, nk=nk, aliased=prev is not None)
    return pl.pallas_call(
        kern,
        out_shape=jax.ShapeDtypeStruct((b, s, hh * LANES), BF16),
        grid=(b, hh, nq, nk),
        in_specs=in_specs,
        out_specs=pl.BlockSpec((1, tq, LANES), lambda bi, h, qi, ki: (bi, qi + q_off, h)),
        scratch_shapes=[pltpu.VMEM((2, tq, 1), F32), pltpu.VMEM((2, tq, 1), F32),
                        pltpu.VMEM((2, tq, LANES), F32)],
        input_output_aliases=aliases,
        compiler_params=_cparams(("arbitrary", "arbitrary", "arbitrary", "arbitrary")),
        name="diff_flash_attention",
    )(*args)


def _post_norm(x, y, gate, lg, lb):
    r = DN_ALPHA * x + gate * y
    mu = jnp.mean(r, axis=-1, keepdims=True)
    rc = r - mu
    var = jnp.mean(rc * rc, axis=-1, keepdims=True)
    return rc * lax.rsqrt(var + LN_EPS) * lg + lb


def _outproj_kernel(*refs, gated, n_heads, dh):
    if gated:
        of_ref, ob_ref, z_ref, ng_ref, w_ref, x_ref, gate_ref, lg_ref, lb_ref, out_ref = refs
        y = None
        for h in range(n_heads):
            sl = slice(h * dh, (h + 1) * dh)
            o = of_ref[0, :, sl].astype(F32) + ob_ref[0, :, sl].astype(F32)
            ms = jnp.mean(o * o, axis=-1, keepdims=True)
            o = o * lax.rsqrt(ms + NORM_EPS) * ng_ref[...] * _silu(z_ref[0, :, sl].astype(F32))
            t = _dot(o.astype(BF16), w_ref[sl, :])
            y = t if y is None else y + t
    else:
        o_ref, w_ref, x_ref, gate_ref, lg_ref, lb_ref, out_ref = refs
        y = _dot(o_ref[0], w_ref[...])
    out_ref[0] = _post_norm(x_ref[0], y, gate_ref[0, 0], lg_ref[...], lb_ref[...])


def out_projection_post_norm(o_args, w_o, x, gate, lg, lb, n_lat_tiles, gated=None):
    b, s, d = x.shape
    kdim = w_o.shape[0]
    tm = ROW_TILE
    row = lambda width, cb=0: pl.BlockSpec((1, tm, width), lambda bi, i: (bi, i, cb))
    full = lambda shape: pl.BlockSpec(shape, lambda bi, i: (0,) * len(shape))
    if gated is None:
        in_specs = [row(kdim)]
        args = list(o_args)
        kern = functools.partial(_outproj_kernel, gated=False, n_heads=0, dh=0)
    else:
        n_heads, dh, z_cb = gated
        of, ob, z, ng = o_args
        in_specs = [row(kdim), row(kdim), row(kdim, z_cb), full((1, dh))]
        args = [of, ob, z, ng.reshape(1, dh)]
        kern = functools.partial(_outproj_kernel, gated=True, n_heads=n_heads, dh=dh)
    in_specs += [full((kdim, d)), row(d),
                 pl.BlockSpec((1, 1, 1, d), lambda bi, i: (bi, i // n_lat_tiles, 0, 0)),
                 full((1, d)), full((1, d))]
    args += [w_o, x, gate, lg.reshape(1, d), lb.reshape(1, d)]
    return pl.pallas_call(
        kern, out_shape=jax.ShapeDtypeStruct((b, s, d), F32), grid=(b, s // tm),
        in_specs=in_specs, out_specs=row(d),
        compiler_params=_cparams(("arbitrary", "arbitrary")), name="out_projection_post_norm",
    )(*args)


def _gdn_prep_kernel(cur_ref, prev_ref, next_ref, gate_ref, cw_ref, alog_ref, dtb_ref,
                     q_ref, k_ref, v_ref, g_ref, ext, *, n_lat_tiles, n_tiles, cn):
    i = pl.program_id(1)
    tm = cur_ref.shape[1]
    first = jnp.logical_or(i == 0, i == n_lat_tiles)
    last = jnp.logical_or(i == n_lat_tiles - 1, i == n_tiles - 1)
    pmask = jnp.where(first, 0.0, 1.0)
    nmask = jnp.where(last, 0.0, 1.0)
    pad = GDN_CONV_W // 2
    dh = GDN_HEAD_DIM
    n_qk = 2 * GDN_KEY_DIM
    for cc in range(cur_ref.shape[2] // cn):
        cs = slice(cc * cn, (cc + 1) * cn)
        ext[0:HALO, :] = prev_ref[0, :, cs].astype(F32) * pmask
        ext[HALO:HALO + tm, :] = cur_ref[0, :, cs].astype(F32)
        ext[HALO + tm:2 * HALO + tm, :] = next_ref[0, :, cs].astype(F32) * nmask
        acc = None
        for j in range(GDN_CONV_W):
            t = ext[pl.ds(HALO - pad + j, tm), :] * cw_ref[j:j + 1, cs]
            acc = t if acc is None else acc + t
        y = _silu(acc)
        for g in range(cn // dh):
            col = cc * cn + g * dh
            yg = y[:, g * dh:(g + 1) * dh]
            if col < n_qk:
                yg = yg * lax.rsqrt(jnp.sum(yg * yg, axis=-1, keepdims=True) + NORM_EPS)
                if col < GDN_KEY_DIM:
                    q_ref[0, col // dh] = (yg * (dh ** -0.5)).astype(BF16)
                else:
                    k_ref[0, (col - GDN_KEY_DIM) // dh] = yg.astype(BF16)
            else:
                v_ref[0, (col - n_qk) // dh] = yg.astype(BF16)
    a = gate_ref[0]
    lane = lax.broadcasted_iota(jnp.int32, a.shape, 1)
    gdec = -jnp.exp(alog_ref[...]) * _softplus(a + dtb_ref[...])
    g_ref[0] = jnp.where(lane < 2 * GDN_V_HEADS, gdec, jax.nn.sigmoid(a))


def gdn_prep(p_main, gates_raw, conv_w, alog_row, dtb_row, n_lat_tiles):
    b, s, _ = p_main.shape
    tm = ROW_TILE
    nt = s // tm
    nch = 2 * GDN_KEY_DIM + GDN_VAL_DIM
    hpt = tm // HALO
    nh = s // HALO
    kern = functools.partial(_gdn_prep_kernel, n_lat_tiles=n_lat_tiles, n_tiles=nt, cn=512)
    head_out = lambda nheads: pl.BlockSpec((1, nheads, tm, GDN_HEAD_DIM), lambda bi, i: (bi, 0, i, 0))
    return pl.pallas_call(
        kern,
        out_shape=[jax.ShapeDtypeStruct((b, GDN_K_HEADS, s, GDN_HEAD_DIM), BF16),
                   jax.ShapeDtypeStruct((b, GDN_K_HEADS, s, GDN_HEAD_DIM), BF16),
                   jax.ShapeDtypeStruct((b, GDN_V_HEADS, s, GDN_HEAD_DIM), BF16),
                   jax.ShapeDtypeStruct((b, s, LANES), F32)],
        grid=(b, nt),
        in_specs=[
            pl.BlockSpec((1, tm, nch), lambda bi, i: (bi, i, 0)),
            pl.BlockSpec((1, HALO, nch), lambda bi, i: (bi, jnp.maximum(i * hpt - 1, 0), 0)),
            pl.BlockSpec((1, HALO, nch), lambda bi, i: (bi, jnp.minimum((i + 1) * hpt, nh - 1), 0)),
            pl.BlockSpec((1, tm, LANES), lambda bi, i: (bi, i, 0)),
            pl.BlockSpec((GDN_CONV_W, nch), lambda bi, i: (0, 0)),
            pl.BlockSpec((1, LANES), lambda bi, i: (0, 0)),
            pl.BlockSpec((1, LANES), lambda bi, i: (0, 0)),
        ],
        out_specs=[head_out(GDN_K_HEADS), head_out(GDN_K_HEADS), head_out(GDN_V_HEADS),
                   pl.BlockSpec((1, tm, LANES), lambda bi, i: (bi, i, 0))],
        scratch_shapes=[pltpu.VMEM((tm + 2 * HALO, 512), F32)],
        compiler_params=_cparams(("arbitrary", "arbitrary")),
        name="gdn_prep",
    )(p_main, p_main, p_main, gates_raw, conv_w, alog_row, dtb_row)


def _chunk_masks(n, reverse):
    r = lax.broadcasted_iota(jnp.int32, (n, n), 0)
    c = lax.broadcasted_iota(jnp.int32, (n, n), 1)
    if reverse:
        return r <= c, r < c
    return r >= c, r > c


def _block_cumsum_mats(ct, reverse):
    r = lax.broadcasted_iota(jnp.int32, (ct, ct), 0)
    c = lax.broadcasted_iota(jnp.int32, (ct, ct), 1)
    same = (r // CHUNK) == (c // CHUNK)
    lower = jnp.logical_and(same, c <= r)
    upper = jnp.logical_and(same, c >= r)
    lo = jnp.where(lower, 1.0, 0.0).astype(BF16)
    up = jnp.where(upper, 1.0, 0.0).astype(BF16)
    return (up, lo) if reverse else (lo, up)


def _cumsum_cols(mat, x):
    x0, x1, x2 = _split3(x)
    return _dot(mat, x0) + (_dot(mat, x1) + _dot(mat, x2))


def _cumsum_rows(x, mat):
    x0, x1, x2 = _split3(x)
    return _dot(x0, mat) + (_dot(x1, mat) + _dot(x2, mat))


def _gdn_direction(q_ref, k_ref, v_ref, gc_ref, gr_ref, o_ref, s_sc, reverse):
    ct = q_ref.shape[2]
    nchunk = ct // CHUNK
    goff = 2 if reverse else 0
    gcol = gc_ref[0, 0]
    grow = gr_ref[0, 0]
    m_col, m_row = _block_cumsum_mats(ct, reverse)
    gcum_c = _cumsum_cols(m_col, gcol)
    gcum_r = _cumsum_rows(grow, m_row)
    tril, strict = _chunk_masks(CHUNK, reverse)
    eye = jnp.where(lax.broadcasted_iota(jnp.int32, (CHUNK, CHUNK), 0)
                    == lax.broadcasted_iota(jnp.int32, (CHUNK, CHUNK), 1), 1.0, 0.0).astype(F32)
    order = range(nchunk - 1, -1, -1) if reverse else range(nchunk)
    for c in order:
        rs = slice(c * CHUNK, (c + 1) * CHUNK)
        last = c * CHUNK if reverse else (c + 1) * CHUNK - 1
        q = q_ref[0, 0, rs, :]
        k = k_ref[0, 0, rs, :]
        kf = k.astype(F32)
        kk = _dot_nt(k, k)
        qk = _dot_nt(q, k)
        for hh in range(2):
            gi = goff + hh
            gc = gcum_c[rs, gi:gi + 1]
            gr = gcum_r[gi:gi + 1, rs]
            glast = gcum_c[last:last + 1, gi:gi + 1]
            beta = gcol[rs, 4 + gi:5 + gi]
            decay = jnp.where(tril, jnp.exp(jnp.where(tril, gc - gr, 0.0)), 0.0)
            a = -jnp.where(strict, kk * beta * decay, 0.0)
            tmat = eye + a
            pw = a
            for _ in range(5):
                pwb = pw.astype(BF16)
                pw = _dot(pwb, pwb)
                tmat = tmat + _dot(tmat.astype(BF16), pw.astype(BF16))
            eg = jnp.exp(gc)
            vb = v_ref[0, hh, rs, :].astype(F32) * beta
            kbg = kf * (beta * eg)
            uw = _dot(tmat.astype(BF16), jnp.concatenate([vb, kbg], axis=-1).astype(BF16))
            u = uw[:, :GDN_HEAD_DIM]
            w = uw[:, GDN_HEAD_DIM:]
            attn = jnp.where(tril, qk * decay, 0.0)
            state = s_sc[hh]
            sb = state.astype(BF16)
            v_new = u - _dot(w.astype(BF16), sb)
            o = eg * _dot(q, sb) + _dot(attn.astype(BF16), v_new.astype(BF16))
            kgv = (v_new * jnp.exp(glast - gc)).astype(BF16)
            s_sc[hh] = state * jnp.exp(glast) + _dot_tn(k, kgv)
            o_ref[0, rs, hh * GDN_HEAD_DIM:(hh + 1) * GDN_HEAD_DIM] = o.astype(BF16)


def _gdn_scan_kernel(qf, kf, vf, gcf, grf, qb, kb, vb, gcb, grb, s0f, s0b, *rest, nblk, aliased):
    if aliased:
        rest = rest[2:]
    of_ref, ob_ref, sff, sfb, sf_sc, sb_sc = rest
    j = pl.program_id(2)

    @pl.when(j == 0)
    def _():
        sf_sc[...] = s0f[0]
        sb_sc[...] = s0b[0]

    _gdn_direction(qf, kf, vf, gcf, grf, of_ref, sf_sc, False)
    _gdn_direction(qb, kb, vb, gcb, grb, ob_ref, sb_sc, True)

    @pl.when(j == nblk - 1)
    def _():
        sff[0] = sf_sc[...]
        sfb[0] = sb_sc[...]


def gdn_scan(qn, kn, vv, gcol, grow, s0f, s0b, *, nblk, off, prev=None):
    b, _, s, dh = qn.shape
    ct = SCAN_TILE
    fwd = lambda j: j + off
    bwd = lambda j: nblk - 1 - j + off
    def specs(pos):
        return [
            pl.BlockSpec((1, 1, ct, dh), lambda bi, h, j: (bi, h, pos(j), 0)),
            pl.BlockSpec((1, 1, ct, dh), lambda bi, h, j: (bi, h, pos(j), 0)),
            pl.BlockSpec((1, 2, ct, dh), lambda bi, h, j: (bi, h, pos(j), 0)),
            pl.BlockSpec((1, 1, ct, 8), lambda bi, h, j: (bi, h, pos(j), 0)),
            pl.BlockSpec((1, 1, 8, ct), lambda bi, h, j: (bi, h, 0, pos(j))),
        ]
    st_spec = pl.BlockSpec((1, 2, dh, dh), lambda bi, h, j: (bi, h, 0, 0))
    in_specs = specs(fwd) + specs(bwd) + [st_spec, st_spec]
    args = [qn, kn, vv, gcol, grow] * 2 + [s0f, s0b]
    aliases = {}
    if prev is not None:
        in_specs += [pl.BlockSpec(memory_space=pl.ANY)] * 2
        args += list(prev)
        aliases = {12: 0, 13: 1}
    o_shape = jax.ShapeDtypeStruct((b, s, GDN_VAL_DIM), BF16)
    st_shape = jax.ShapeDtypeStruct((b, GDN_V_HEADS, dh, dh), F32)
    kern = functools.partial(_gdn_scan_kernel, nblk=nblk, aliased=prev is not None)
    return pl.pallas_call(
        kern,
        out_shape=[o_shape, o_shape, st_shape, st_shape],
        grid=(b, GDN_K_HEADS, nblk),
        in_specs=in_specs,
        out_specs=[pl.BlockSpec((1, ct, 2 * dh), lambda bi, h, j: (bi, fwd(j), h)),
                   pl.BlockSpec((1, ct, 2 * dh), lambda bi, h, j: (bi, bwd(j), h)),
                   st_spec, st_spec],
        scratch_shapes=[pltpu.VMEM((2, dh, dh), F32), pltpu.VMEM((2, dh, dh), F32)],
        input_output_aliases=aliases,
        compiler_params=_cparams(("arbitrary", "arbitrary", "arbitrary")),
        name="gdn_scan",
    )(*args)


def _gla_direction(q_ref, k_ref, v_ref, gr_ref, wg_ref, gb_ref, o_ref, st_sc, z):
    reverse = z == 1
    ct = q_ref.shape[1]
    nchunk = ct // CHUNK
    logit = _dot3(gr_ref[0], wg_ref[z]) + gb_ref[z:z + 1, :]
    glog = -_softplus(-logit) / GLA_TAU
    m_col, _ = _block_cumsum_mats(ct, reverse)
    bcum = _cumsum_cols(m_col, glog)
    tril, _ = _chunk_masks(CHUNK, reverse)
    order = range(nchunk - 1, -1, -1) if reverse else range(nchunk)
    for c in order:
        rs = slice(c * CHUNK, (c + 1) * CHUNK)
        last = c * CHUNK if reverse else (c + 1) * CHUNK - 1
        bc = bcum[rs, :]
        bl = bcum[last:last + 1, :]
        qf = q_ref[0, rs, :].astype(F32) * (GLA_DK ** -0.5)
        kf = k_ref[0, rs, :].astype(F32)
        v = v_ref[0, rs, :]
        qe = (qf * jnp.exp(bc)).astype(BF16)
        ke = (kf * jnp.exp(-bc)).astype(BF16)
        kg = (kf * jnp.exp(bl - bc)).astype(BF16)
        attn = jnp.where(tril, _dot_nt(qe, ke), 0.0)
        st = st_sc[...]
        o = _dot(attn.astype(BF16), v) + _dot_nt(qe, st.astype(BF16))
        st_sc[...] = st * jnp.exp(bl) + _dot_tn(v, kg)
        o_ref[0, rs, :] = o.astype(BF16)


def _gla_scan_kernel(qf, kf, vf, grf, qb, kb, vb, grb, wg, gb, s0f, s0b, *rest, nblk, aliased):
    if aliased:
        rest = rest[2:]
    of_ref, ob_ref, sff, sfb, sf_sc, sb_sc = rest
    j = pl.program_id(2)

    @pl.when(j == 0)
    def _():
        sf_sc[...] = s0f[0, 0]
        sb_sc[...] = s0b[0, 0]

    _gla_direction(qf, kf, vf, grf, wg, gb, of_ref, sf_sc, 0)
    _gla_direction(qb, kb, vb, grb, wg, gb, ob_ref, sb_sc, 1)

    @pl.when(j == nblk - 1)
    def _():
        sff[0, 0] = sf_sc[...]
        sfb[0, 0] = sb_sc[...]


def gla_scan(p_main, gr, wg_pad, gate_b, s0f, s0b, *, nblk, off, prev=None):
    b, s, _ = p_main.shape
    ct = SCAN_TILE
    nh, dk, dv = GLA_HEADS, GLA_DK, GLA_DV
    fwd = lambda j: j + off
    bwd = lambda j: nblk - 1 - j + off
    def specs(pos):
        return [
            pl.BlockSpec((1, ct, dk), lambda bi, h, j: (bi, pos(j), h)),
            pl.BlockSpec((1, ct, dk), lambda bi, h, j: (bi, pos(j), nh + h)),
            pl.BlockSpec((1, ct, dv), lambda bi, h, j: (bi, pos(j), nh + h)),
            pl.BlockSpec((1, ct, LANES), lambda bi, h, j: (bi, pos(j), 0)),
        ]
    st_spec = pl.BlockSpec((1, 1, dv, dk), lambda bi, h, j: (bi, h, 0, 0))
    in_specs = specs(fwd) + specs(bwd) + [
        pl.BlockSpec((2, LANES, dk), lambda bi, h, j: (0, 0, h)),
        pl.BlockSpec((2, dk), lambda bi, h, j: (0, h)),
        st_spec, st_spec]
    args = [p_main, p_main, p_main, gr] * 2 + [wg_pad, gate_b, s0f, s0b]
    aliases = {}
    if prev is not None:
        in_specs += [pl.BlockSpec(memory_space=pl.ANY)] * 2
        args += list(prev)
        aliases = {12: 0, 13: 1}
    o_shape = jax.ShapeDtypeStruct((b, s, GLA_VAL_DIM), BF16)
    st_shape = jax.ShapeDtypeStruct((b, nh, dv, dk), F32)
    kern = functools.partial(_gla_scan_kernel, nblk=nblk, aliased=prev is not None)
    return pl.pallas_call(
        kern,
        out_shape=[o_shape, o_shape, st_shape, st_shape],
        grid=(b, nh, nblk),
        in_specs=in_specs,
        out_specs=[pl.BlockSpec((1, ct, dv), lambda bi, h, j: (bi, fwd(j), h)),
                   pl.BlockSpec((1, ct, dv), lambda bi, h, j: (bi, bwd(j), h)),
                   st_spec, st_spec],
        scratch_shapes=[pltpu.VMEM((dv, dk), F32), pltpu.VMEM((dv, dk), F32)],
        input_output_aliases=aliases,
        compiler_params=_cparams(("arbitrary", "arbitrary", "arbitrary")),
        name="gla_scan",
    )(*args)


def _router_kernel(x_ref, sc_ref, sh_ref, rw_ref, rb_ref, h_ref, wd_ref):
    h = x_ref[0] * (1.0 + sc_ref[0, 0]) + sh_ref[0, 0]
    h_ref[0] = h.astype(BF16)
    scores = jax.nn.sigmoid(_dot3(h, rw_ref[...]))
    lane = lax.broadcasted_iota(jnp.int32, scores.shape, 1)
    neg = jnp.float32(-jnp.inf)
    sel = jnp.where(lane < N_EXPERTS, scores + rb_ref[...], neg)
    chosen = jnp.zeros(scores.shape, jnp.bool_)
    for _ in range(TOP_K):
        mx = jnp.max(sel, axis=-1, keepdims=True)
        first = jnp.min(jnp.where(sel == mx, lane, LANES), axis=-1, keepdims=True)
        pick = lane == first
        chosen = jnp.logical_or(chosen, pick)
        sel = jnp.where(pick, neg, sel)
    picked = jnp.where(chosen, scores, 0.0)
    wt = picked / jnp.sum(picked, axis=-1, keepdims=True) * ROUTE_SCALE
    wd_ref[0] = jnp.where(lane == N_EXPERTS, 1.0, wt)


def moe_route(x, sc, sh, rw_pad, rb_pad, n_lat_tiles):
    b, s, d = x.shape
    tm = ROW_TILE
    mod_spec = pl.BlockSpec((1, 1, 1, d), lambda bi, i: (bi, i // n_lat_tiles, 0, 0))
    return pl.pallas_call(
        _router_kernel,
        out_shape=[jax.ShapeDtypeStruct((b, s, d), BF16), jax.ShapeDtypeStruct((b, s, LANES), F32)],
        grid=(b, s // tm),
        in_specs=[pl.BlockSpec((1, tm, d), lambda bi, i: (bi, i, 0)), mod_spec, mod_spec,
                  pl.BlockSpec((d, LANES), lambda bi, i: (0, 0)),
                  pl.BlockSpec((1, LANES), lambda bi, i: (0, 0))],
        out_specs=[pl.BlockSpec((1, tm, d), lambda bi, i: (bi, i, 0)),
                   pl.BlockSpec((1, tm, LANES), lambda bi, i: (bi, i, 0))],
        compiler_params=_cparams(("arbitrary", "arbitrary")),
        name="moe_router",
    )(x, sc, sh, rw_pad, rb_pad)


def _moe_kernel(h_ref, wd_ref, wgu_ref, wdn_ref, x_ref, gate_ref, lg_ref, lb_ref, out_ref, acc, *,
                n_exp, n_lat_rows):
    e = pl.program_id(2)

    @pl.when(e == 0)
    def _():
        acc[...] = jnp.zeros(acc.shape, F32)

    gu = _dot(h_ref[0], wgu_ref[0])
    act = _silu(gu[:, :EXPERT_FF]) * gu[:, EXPERT_FF:]
    wd = wd_ref[0]
    lane = lax.broadcasted_iota(jnp.int32, wd.shape, 1)
    wcol = jnp.sum(jnp.where(lane == e, wd, 0.0), axis=-1, keepdims=True)
    acc[...] += wcol * _dot(act.astype(BF16), wdn_ref[0])

    @pl.when(e == n_exp - 1)
    def _():
        tm = acc.shape[0]
        row = pl.program_id(1) * tm + lax.broadcasted_iota(jnp.int32, (tm, 1), 0)
        gate = jnp.where(row >= n_lat_rows, gate_ref[0, 1], gate_ref[0, 0])
        out_ref[0] = _post_norm(x_ref[0], acc[...], gate, lg_ref[...], lb_ref[...])


def moe_experts_post_norm(h, wd, wgu, wdn, x, gate, lg, lb, n_lat_rows, tm):
    b, s, d = x.shape
    n_exp = wgu.shape[0]
    row = lambda width: pl.BlockSpec((1, tm, width), lambda bi, i, e: (bi, i, 0))
    kern = functools.partial(_moe_kernel, n_exp=n_exp, n_lat_rows=n_lat_rows)
    return pl.pallas_call(
        kern,
        out_shape=jax.ShapeDtypeStruct((b, s, d), F32),
        grid=(b, s // tm, n_exp),
        in_specs=[row(d), row(LANES),
                  pl.BlockSpec((1, d, 2 * EXPERT_FF), lambda bi, i, e: (e, 0, 0)),
                  pl.BlockSpec((1, EXPERT_FF, d), lambda bi, i, e: (e, 0, 0)),
                  row(d),
                  pl.BlockSpec((1, 2, 1, d), lambda bi, i, e: (bi, 0, 0, 0)),
                  pl.BlockSpec((1, d), lambda bi, i, e: (0, 0)),
                  pl.BlockSpec((1, d), lambda bi, i, e: (0, 0))],
        out_specs=row(d),
        scratch_shapes=[pltpu.VMEM((tm, d), F32)],
        compiler_params=_cparams(("arbitrary", "arbitrary", "arbitrary")),
        name="moe_experts_post_norm",
    )(h, wd, wgu, wdn, x, gate, lg.reshape(1, d), lb.reshape(1, d))


def _rope_tables(n_lat, n_ctx):
    rows = n_lat // GRID_W
    rowp = jnp.repeat(jnp.arange(rows), GRID_W).astype(F32)
    colp = jnp.tile(jnp.arange(GRID_W), rows).astype(F32)
    n_freq = DA_HEAD_DIM // 4
    inv = 1.0 / (ROPE_BASE ** (jnp.arange(n_freq, dtype=F32) / n_freq))
    ang = jnp.concatenate([rowp[:, None] * inv, colp[:, None] * inv], -1)
    cos, sin = jnp.cos(ang), jnp.sin(ang)
    cos_t = jnp.tile(cos, (1, 4))
    sin_t = jnp.tile(jnp.concatenate([-sin, sin], -1), (1, 2))
    cos_t = jnp.concatenate([cos_t, jnp.ones((n_ctx, LANES), F32)], 0)
    sin_t = jnp.concatenate([sin_t, jnp.zeros((n_ctx, LANES), F32)], 0)
    return cos_t, sin_t


def _pick_tile(total, cands):
    for t in cands:
        if total % t == 0:
            return t
    raise ValueError(f"no tile for {total}")


def _pad_cols(w, n):
    return jnp.pad(w, ((0, 0), (0, n - w.shape[1])))


def _flash_both(p_all, lam_vec, subln, lam_init, n_lat, n_ctx):
    s = n_lat + n_ctx
    tq = _pick_tile(n_lat, (512, 256))
    tk = _pick_tile(s, (1280, 1024, 512, 256))
    o = diff_flash_attention(p_all, lam_vec, subln, lam_init, tq=tq, tk=tk, nq=n_lat // tq, nk=s // tk,
                             q_off=0, k_off=0)
    return diff_flash_attention(p_all, lam_vec, subln, lam_init, tq=n_ctx, tk=n_ctx, nq=1, nk=1,
                                q_off=n_lat // n_ctx, k_off=n_lat // n_ctx, prev=o)


def kernel(x, c, ctx, c_ctx, ada_w, ada_b, ln_g, ln_b, da_w_in, da_w_o, da_lambda, da_subln, gdn_w_in, gdn_conv, gdn_a_log, gdn_dt_bias, gdn_norm, gdn_w_o, gla_w_in, gla_w_gate, gla_gate_b, gla_norm, gla_w_o, moe_router, moe_router_b, moe_w_gu, moe_w_dn, moe_ws_gu, moe_ws_dn):
    b, n, d = x.shape
    lc = ctx.shape[1]
    assert lc == ROW_TILE and n % SCAN_TILE == 0 and d == D_MODEL
    s = n + lc
    n_lat_tiles = n // ROW_TILE
    depth = ada_w.shape[0]

    xall = jnp.concatenate([x, ctx], axis=1)
    c8 = jnp.concatenate([c, c_ctx[None], jnp.zeros((8 - b - 1, d), F32)], 0)
    mods = ada_modulation(c8, ada_w, ada_b)
    rope = _rope_tables(n, lc)
    moe_tm = _pick_tile(s, (1280, 1024, 512, 256))

    for i in range(depth):
        kind, j = i % N_MIXERS, i // N_MIXERS
        m = mods[i].reshape(8, ADA_CHUNKS, d)
        mod = jnp.stack([m[:b], jnp.broadcast_to(m[b], (b, ADA_CHUNKS, d))], axis=1)
        mod = [mod[:, :, k][:, :, None, :] for k in range(ADA_CHUNKS)]

        if kind == 0:
            lam_init = 0.8 - 0.6 * math.exp(-0.3 * i)
            w = da_w_in[j]
            w = jnp.concatenate([w[:, :d] * (DA_HEAD_DIM ** -0.5), w[:, d:]], 1).astype(BF16)
            p_all = modulated_projection(xall, mod[1], mod[0], w, rope=rope, n_rope=2 * d,
                                         n_lat_tiles=n_lat_tiles)
            o = _flash_both(p_all, da_lambda[j], da_subln[j], lam_init, n, lc)
            xall = out_projection_post_norm([o], da_w_o[j].astype(BF16), xall, mod[2],
                                            ln_g[i, 0], ln_b[i, 0], n_lat_tiles)
        elif kind == 1:
            w = gdn_w_in[j]
            n_main = 2 * GDN_KEY_DIM + 2 * GDN_VAL_DIM
            p_main, gates_raw = modulated_projection(
                xall, mod[1], mod[0], w[:, :n_main].astype(BF16),
                w_small=_pad_cols(w[:, n_main:], LANES).astype(BF16), n_lat_tiles=n_lat_tiles)
            alog_row = _pad_cols(gdn_a_log[j].reshape(1, -1), LANES)
            dtb_row = _pad_cols(gdn_dt_bias[j].reshape(1, -1), LANES)
            qn, kn, vv, gates = gdn_prep(p_main, gates_raw, gdn_conv[j], alog_row, dtb_row, n_lat_tiles)
            hv, hk = GDN_V_HEADS, GDN_K_HEADS
            def per_khead(t):
                t = t.reshape(b, s, 2, hk, 2)
                return jnp.transpose(t, (0, 3, 1, 2, 4)).reshape(b, hk, s, 4)
            gcol = jnp.concatenate([per_khead(gates[..., :2 * hv]), per_khead(gates[..., 2 * hv:4 * hv])], -1)
            grow = jnp.swapaxes(gcol, 2, 3)
            zeros = jnp.zeros((b, hv, GDN_HEAD_DIM, GDN_HEAD_DIM), F32)
            of, ob, scf, scb = gdn_scan(qn, kn, vv, gcol, grow, zeros, zeros, nblk=lc // SCAN_TILE,
                                        off=n // SCAN_TILE)
            of, ob, _, _ = gdn_scan(qn, kn, vv, gcol, grow, scf, scb, nblk=n // SCAN_TILE, off=0,
                                    prev=(of, ob))
            xall = out_projection_post_norm([of, ob, p_main, gdn_norm[j]], gdn_w_o[j].astype(BF16), xall,
                                            mod[2], ln_g[i, 0], ln_b[i, 0], n_lat_tiles,
                                            gated=(GDN_V_HEADS, GDN_HEAD_DIM, 2))
        else:
            w = gla_w_in[j]
            n_main = 2 * GLA_KEY_DIM + 2 * GLA_VAL_DIM
            p_main, gr = modulated_projection(
                xall, mod[1], mod[0], w[:, :n_main].astype(BF16),
                w_small=_pad_cols(w[:, n_main:], LANES).astype(BF16), n_lat_tiles=n_lat_tiles)
            wg = gla_w_gate[j]
            wg_pad = jnp.zeros((2, LANES, GLA_KEY_DIM), F32)
            wg_pad = wg_pad.at[0, :GLA_GATE_RANK].set(wg[0]).at[1, GLA_GATE_RANK:2 * GLA_GATE_RANK].set(wg[1])
            zeros = jnp.zeros((b, GLA_HEADS, GLA_DV, GLA_DK), F32)
            of, ob, scf, scb = gla_scan(p_main, gr, wg_pad, gla_gate_b[j], zeros, zeros,
                                        nblk=lc // SCAN_TILE, off=n // SCAN_TILE)
            of, ob, _, _ = gla_scan(p_main, gr, wg_pad, gla_gate_b[j], scf, scb, nblk=n // SCAN_TILE, off=0,
                                    prev=(of, ob))
            xall = out_projection_post_norm([of, ob, p_main, gla_norm[j]], gla_w_o[j].astype(BF16), xall,
                                            mod[2], ln_g[i, 0], ln_b[i, 0], n_lat_tiles,
                                            gated=(GLA_HEADS, GLA_DV, 2))

        rw_pad = _pad_cols(moe_router[i], LANES)
        rb_pad = _pad_cols(moe_router_b[i].reshape(1, -1), LANES)
        h, wd = moe_route(xall, mod[4], mod[3], rw_pad, rb_pad, n_lat_tiles)
        wgu = jnp.concatenate([moe_w_gu[i], moe_ws_gu[i][None]], 0).astype(BF16)
        wdn = jnp.concatenate([moe_w_dn[i], moe_ws_dn[i][None]], 0).astype(BF16)
        xall = moe_experts_post_norm(h, wd, wgu, wdn, xall, mod[5], ln_g[i, 1], ln_b[i, 1], n, moe_tm)

    return xall[:, :n]
```

```python
import functools
import math

import jax
import jax.numpy as jnp
from jax import lax
from jax.experimental import pallas as pl
from jax.experimental.pallas import tpu as pltpu

F32 = jnp.float32
BF16 = jnp.bfloat16

D_MODEL = 1024
DEPTH = 4
GRID_W = 64
N_MIXERS = 3
DN_ALPHA = (2 * DEPTH) ** 0.25
LN_EPS = 1e-5
NORM_EPS = 1e-6
ADA_CHUNKS = 6

DA_HEADS = 8
DA_HEAD_DIM = 64
ROPE_BASE = 10000.0

GDN_K_HEADS = 8
GDN_V_HEADS = 16
GDN_HEAD_DIM = 128
GDN_KEY_DIM = 1024
GDN_VAL_DIM = 2048
GDN_CONV_W = 5
CHUNK = 64

GLA_HEADS = 4
GLA_KEY_DIM = 512
GLA_VAL_DIM = 1024
GLA_DK = 128
GLA_DV = 256
GLA_GATE_RANK = 16
GLA_TAU = 16.0

N_EXPERTS = 64
TOP_K = 8
EXPERT_FF = 256
ROUTE_SCALE = 2.5

LANES = 128
ROW_TILE = 256
SCAN_TILE = 256
HALO = 16
VMEM_LIMIT = 56 * 1024 * 1024


def _cparams(sem):
    return pltpu.CompilerParams(dimension_semantics=sem, vmem_limit_bytes=VMEM_LIMIT)


def _dot(a, b):
    return jnp.dot(a, b, preferred_element_type=F32)


def _dot_nt(a, b):
    return lax.dot_general(a, b, (((1,), (1,)), ((), ())), preferred_element_type=F32)


def _dot_tn(a, b):
    return lax.dot_general(a, b, (((0,), (0,)), ((), ())), preferred_element_type=F32)


def _split2(a):
    hi = a.astype(BF16)
    lo = (a - hi.astype(F32)).astype(BF16)
    return hi, lo


def _split3(a):
    a0 = a.astype(BF16)
    r = a - a0.astype(F32)
    a1 = r.astype(BF16)
    a2 = (r - a1.astype(F32)).astype(BF16)
    return a0, a1, a2


def _dot3(a, b):
    a0, a1 = _split2(a)
    b0, b1 = _split2(b)
    return _dot(a0, b0) + (_dot(a0, b1) + _dot(a1, b0))


def _silu(x):
    return x * jax.nn.sigmoid(x)


def _softplus(x):
    return jnp.maximum(x, 0.0) + jnp.log(1.0 + jnp.exp(-jnp.abs(x)))


def _ada_kernel(c_ref, w_ref, b_ref, o_ref):
    s = _silu(c_ref[...])
    o_ref[0] = _dot3(s, w_ref[0]) + b_ref[0]


def ada_modulation(c8, ada_w, ada_b):
    depth, d, n = ada_w.shape
    tn = 1536
    return pl.pallas_call(
        _ada_kernel,
        out_shape=jax.ShapeDtypeStruct((depth, 8, n), F32),
        grid=(depth, n // tn),
        in_specs=[
            pl.BlockSpec((8, d), lambda i, j: (0, 0)),
            pl.BlockSpec((1, d, tn), lambda i, j: (i, 0, j)),
            pl.BlockSpec((1, 1, tn), lambda i, j: (i, 0, j)),
        ],
        out_specs=pl.BlockSpec((1, 8, tn), lambda i, j: (i, 0, j)),
        compiler_params=_cparams(("arbitrary", "arbitrary")),
        name="ada_modulation",
    )(c8, ada_w, ada_b.reshape(depth, 1, n))


def _proj_kernel(*refs, n_main, n_rope, has_small, cn):
    it = iter(refs)
    x_ref, sc_ref, sh_ref, w_ref = next(it), next(it), next(it), next(it)
    ws_ref = next(it) if has_small else None
    cos_ref = next(it) if n_rope else None
    sin_ref = next(it) if n_rope else None
    o_ref = next(it)
    os_ref = next(it) if has_small else None

    h = x_ref[0] * (1.0 + sc_ref[0, 0]) + sh_ref[0, 0]
    hb = h.astype(BF16)
    tm = hb.shape[0]
    if n_rope:
        cos = cos_ref[...]
        sin = sin_ref[...]
        lane = lax.broadcasted_iota(jnp.int32, (tm, LANES), 1)
        low_half = (lane & 32) == 0
    for j in range(n_main // cn):
        p = _dot(hb, w_ref[:, j * cn:(j + 1) * cn])
        for g in range(cn // LANES):
            pg = p[:, g * LANES:(g + 1) * LANES]
            col = j * cn + g * LANES
            if col < n_rope:
                partner = jnp.where(low_half, pltpu.roll(pg, LANES - 32, 1), pltpu.roll(pg, 32, 1))
                pg = pg * cos + partner * sin
            o_ref[0, :, col:col + LANES] = pg.astype(BF16)
    if has_small:
        os_ref[0] = _dot(hb, ws_ref[...])


def modulated_projection(x, sc, sh, w, w_small=None, rope=None, n_rope=0, n_lat_tiles=1):
    b, s, d = x.shape
    n_main = w.shape[1]
    tm = ROW_TILE
    grid = (b, s // tm)
    mod_spec = pl.BlockSpec((1, 1, 1, d), lambda bi, i: (bi, i // n_lat_tiles, 0, 0))
    in_specs = [pl.BlockSpec((1, tm, d), lambda bi, i: (bi, i, 0)), mod_spec, mod_spec,
                pl.BlockSpec((d, n_main), lambda bi, i: (0, 0))]
    args = [x, sc, sh, w]
    out_shape = [jax.ShapeDtypeStruct((b, s, n_main), BF16)]
    out_specs = [pl.BlockSpec((1, tm, n_main), lambda bi, i: (bi, i, 0))]
    if w_small is not None:
        in_specs.append(pl.BlockSpec((d, LANES), lambda bi, i: (0, 0)))
        args.append(w_small)
        out_shape.append(jax.ShapeDtypeStruct((b, s, LANES), F32))
        out_specs.append(pl.BlockSpec((1, tm, LANES), lambda bi, i: (bi, i, 0)))
    if n_rope:
        tab = pl.BlockSpec((tm, LANES), lambda bi, i: (i, 0))
        in_specs += [tab, tab]
        args += [rope[0], rope[1]]
    kern = functools.partial(_proj_kernel, n_main=n_main, n_rope=n_rope,
                             has_small=w_small is not None, cn=512)
    out = pl.pallas_call(
        kern, out_shape=out_shape, grid=grid, in_specs=in_specs, out_specs=out_specs,
        compiler_params=_cparams(("arbitrary", "arbitrary")), name="modulated_projection",
    )(*args)
    return out if w_small is not None else out[0]


def _flash_kernel(lam_ref, q_ref, k_ref, v_ref, sub_ref, *rest, lam_init, nk, aliased):
    if aliased:
        rest = rest[1:]
    o_ref, m_sc, l_sc, acc_sc = rest
    ki = pl.program_id(3)

    @pl.when(ki == 0)
    def _():
        m_sc[...] = jnp.full(m_sc.shape, -jnp.inf, F32)
        l_sc[...] = jnp.zeros(l_sc.shape, F32)
        acc_sc[...] = jnp.zeros(acc_sc.shape, F32)

    q = q_ref[0]
    k = k_ref[0]
    v = v_ref[0]
    hd = DA_HEAD_DIM
    scores = [_dot_nt(q[:, c * hd:(c + 1) * hd], k[:, c * hd:(c + 1) * hd]) for c in range(2)]
    for c in range(2):
        s = scores[c]
        m_prev = m_sc[c]
        m_new = jnp.maximum(m_prev, jnp.max(s, axis=-1, keepdims=True))
        alpha = jnp.exp(m_prev - m_new)
        p = jnp.exp(s - m_new)
        l_sc[c] = alpha * l_sc[c] + jnp.sum(p, axis=-1, keepdims=True)
        acc_sc[c] = alpha * acc_sc[c] + _dot(p.astype(BF16), v)
        m_sc[c] = m_new

    @pl.when(ki == nk - 1)
    def _():
        lv = lam_ref[...]
        lam = (jnp.exp(jnp.sum(lv[0:1] * lv[1:2], axis=-1, keepdims=True))
               - jnp.exp(jnp.sum(lv[2:3] * lv[3:4], axis=-1, keepdims=True)) + lam_init)
        o = acc_sc[0] / l_sc[0] - lam * (acc_sc[1] / l_sc[1])
        ms = jnp.mean(o * o, axis=-1, keepdims=True)
        o = o * lax.rsqrt(ms + NORM_EPS) * sub_ref[...] * (1.0 - lam_init)
        o_ref[0] = o.astype(BF16)


def diff_flash_attention(p_all, lam_vec, subln, lam_init, *, tq, tk, nq, nk, q_off, k_off, prev=None):
    b, s, _ = p_all.shape
    hh = DA_HEADS
    in_specs = [
        pl.BlockSpec((4, DA_HEAD_DIM), lambda bi, h, qi, ki: (0, 0)),
        pl.BlockSpec((1, tq, LANES), lambda bi, h, qi, ki: (bi, qi + q_off, h)),
        pl.BlockSpec((1, tk, LANES), lambda bi, h, qi, ki: (bi, ki + k_off, hh + h)),
        pl.BlockSpec((1, tk, LANES), lambda bi, h, qi, ki: (bi, ki + k_off, 2 * hh + h)),
        pl.BlockSpec((1, LANES), lambda bi, h, qi, ki: (0, 0)),
    ]
    args = [lam_vec, p_all, p_all, p_all, subln.reshape(1, LANES)]
    aliases = {}
    if prev is not None:
        in_specs.append(pl.BlockSpec(memory_space=pl.ANY))
        args.append(prev)
        aliases = {5: 0}
    kern = functools.partial(_flash_kernel, lam_init=lam_init, nk=nk, aliased=prev is not None)
    return pl.pallas_call(
        kern,
        out_shape=jax.ShapeDtypeStruct((b, s, hh * LANES), BF16),
        grid=(b, hh, nq, nk),
        in_specs=in_specs,
        out_specs=pl.BlockSpec((1, tq, LANES), lambda bi, h, qi, ki: (bi, qi + q_off, h)),
        scratch_shapes=[pltpu.VMEM((2, tq, 1), F32), pltpu.VMEM((2, tq, 1), F32),
                        pltpu.VMEM((2, tq, LANES), F32)],
        input_output_aliases=aliases,
        compiler_params=_cparams(("arbitrary", "arbitrary", "arbitrary", "arbitrary")),
        name="diff_flash_attention",
    )(*args)


def _post_norm(x, y, gate, lg, lb):
    r = DN_ALPHA * x + gate * y
    mu = jnp.mean(r, axis=-1, keepdims=True)
    rc = r - mu
    var = jnp.mean(rc * rc, axis=-1, keepdims=True)
    return rc * lax.rsqrt(var + LN_EPS) * lg + lb


def _outproj_kernel(*refs, gated, n_heads, dh):
    if gated:
        of_ref, ob_ref, z_ref, ng_ref, w_ref, x_ref, gate_ref, lg_ref, lb_ref, out_ref = refs
        y = None
        for h in range(n_heads):
            sl = slice(h * dh, (h + 1) * dh)
            o = of_ref[0, :, sl].astype(F32) + ob_ref[0, :, sl].astype(F32)
            ms = jnp.mean(o * o, axis=-1, keepdims=True)
            o = o * lax.rsqrt(ms + NORM_EPS) * ng_ref[...] * _silu(z_ref[0, :, sl].astype(F32))
            t = _dot(o.astype(BF16), w_ref[sl, :])
            y = t if y is None else y + t
    else:
        o_ref, w_ref, x_ref, gate_ref, lg_ref, lb_ref, out_ref = refs
        y = _dot(o_ref[0], w_ref[...])
    out_ref[0] = _post_norm(x_ref[0], y, gate_ref[0, 0], lg_ref[...], lb_ref[...])


def out_projection_post_norm(o_args, w_o, x, gate, lg, lb, n_lat_tiles, gated=None):
    b, s, d = x.shape
    kdim = w_o.shape[0]
    tm = ROW_TILE
    row = lambda width, cb=0: pl.BlockSpec((1, tm, width), lambda bi, i: (bi, i, cb))
    full = lambda shape: pl.BlockSpec(shape, lambda bi, i: (0,) * len(shape))
    if gated is None:
        in_specs = [row(kdim)]
        args = list(o_args)
        kern = functools.partial(_outproj_kernel, gated=False, n_heads=0, dh=0)
    else:
        n_heads, dh, z_cb = gated
        of, ob, z, ng = o_args
        in_specs = [row(kdim), row(kdim), row(kdim, z_cb), full((1, dh))]
        args = [of, ob, z, ng.reshape(1, dh)]
        kern = functools.partial(_outproj_kernel, gated=True, n_heads=n_heads, dh=dh)
    in_specs += [full((kdim, d)), row(d),
                 pl.BlockSpec((1, 1, 1, d), lambda bi, i: (bi, i // n_lat_tiles, 0, 0)),
                 full((1, d)), full((1, d))]
    args += [w_o, x, gate, lg.reshape(1, d), lb.reshape(1, d)]
    return pl.pallas_call(
        kern, out_shape=jax.ShapeDtypeStruct((b, s, d), F32), grid=(b, s // tm),
        in_specs=in_specs, out_specs=row(d),
        compiler_params=_cparams(("arbitrary", "arbitrary")), name="out_projection_post_norm",
    )(*args)


def _gdn_prep_kernel(cur_ref, prev_ref, next_ref, gate_ref, cw_ref, alog_ref, dtb_ref,
                     q_ref, k_ref, v_ref, g_ref, ext, *, n_lat_tiles, n_tiles, cn):
    i = pl.program_id(1)
    tm = cur_ref.shape[1]
    first = jnp.logical_or(i == 0, i == n_lat_tiles)
    last = jnp.logical_or(i == n_lat_tiles - 1, i == n_tiles - 1)
    pmask = jnp.where(first, 0.0, 1.0)
    nmask = jnp.where(last, 0.0, 1.0)
    pad = GDN_CONV_W // 2
    dh = GDN_HEAD_DIM
    n_qk = 2 * GDN_KEY_DIM
    for cc in range(cur_ref.shape[2] // cn):
        cs = slice(cc * cn, (cc + 1) * cn)
        ext[0:HALO, :] = prev_ref[0, :, cs].astype(F32) * pmask
        ext[HALO:HALO + tm, :] = cur_ref[0, :, cs].astype(F32)
        ext[HALO + tm:2 * HALO + tm, :] = next_ref[0, :, cs].astype(F32) * nmask
        acc = None
        for j in range(GDN_CONV_W):
            t = ext[pl.ds(HALO - pad + j, tm), :] * cw_ref[j:j + 1, cs]
            acc = t if acc is None else acc + t
        y = _silu(acc)
        for g in range(cn // dh):
            col = cc * cn + g * dh
            yg = y[:, g * dh:(g + 1) * dh]
            if col < n_qk:
                yg = yg * lax.rsqrt(jnp.sum(yg * yg, axis=-1, keepdims=True) + NORM_EPS)
                if col < GDN_KEY_DIM:
                    q_ref[0, col // dh] = (yg * (dh ** -0.5)).astype(BF16)
                else:
                    k_ref[0, (col - GDN_KEY_DIM) // dh] = yg.astype(BF16)
            else:
                v_ref[0, (col - n_qk) // dh] = yg.astype(BF16)
    a = gate_ref[0]
    lane = lax.broadcasted_iota(jnp.int32, a.shape, 1)
    gdec = -jnp.exp(alog_ref[...]) * _softplus(a + dtb_ref[...])
    g_ref[0] = jnp.where(lane < 2 * GDN_V_HEADS, gdec, jax.nn.sigmoid(a))


def gdn_prep(p_main, gates_raw, conv_w, alog_row, dtb_row, n_lat_tiles):
    b, s, _ = p_main.shape
    tm = ROW_TILE
    nt = s // tm
    nch = 2 * GDN_KEY_DIM + GDN_VAL_DIM
    hpt = tm // HALO
    nh = s // HALO
    kern = functools.partial(_gdn_prep_kernel, n_lat_tiles=n_lat_tiles, n_tiles=nt, cn=512)
    head_out = lambda nheads: pl.BlockSpec((1, nheads, tm, GDN_HEAD_DIM), lambda bi, i: (bi, 0, i, 0))
    return pl.pallas_call(
        kern,
        out_shape=[jax.ShapeDtypeStruct((b, GDN_K_HEADS, s, GDN_HEAD_DIM), BF16),
                   jax.ShapeDtypeStruct((b, GDN_K_HEADS, s, GDN_HEAD_DIM), BF16),
                   jax.ShapeDtypeStruct((b, GDN_V_HEADS, s, GDN_HEAD_DIM), BF16),
                   jax.ShapeDtypeStruct((b, s, LANES), F32)],
        grid=(b, nt),
        in_specs=[
            pl.BlockSpec((1, tm, nch), lambda bi, i: (bi, i, 0)),
            pl.BlockSpec((1, HALO, nch), lambda bi, i: (bi, jnp.maximum(i * hpt - 1, 0), 0)),
            pl.BlockSpec((1, HALO, nch), lambda bi, i: (bi, jnp.minimum((i + 1) * hpt, nh - 1), 0)),
            pl.BlockSpec((1, tm, LANES), lambda bi, i: (bi, i, 0)),
            pl.BlockSpec((GDN_CONV_W, nch), lambda bi, i: (0, 0)),
            pl.BlockSpec((1, LANES), lambda bi, i: (0, 0)),
            pl.BlockSpec((1, LANES), lambda bi, i: (0, 0)),
        ],
        out_specs=[head_out(GDN_K_HEADS), head_out(GDN_K_HEADS), head_out(GDN_V_HEADS),
                   pl.BlockSpec((1, tm, LANES), lambda bi, i: (bi, i, 0))],
        scratch_shapes=[pltpu.VMEM((tm + 2 * HALO, 512), F32)],
        compiler_params=_cparams(("arbitrary", "arbitrary")),
        name="gdn_prep",
    )(p_main, p_main, p_main, gates_raw, conv_w, alog_row, dtb_row)


def _chunk_masks(n, reverse):
    r = lax.broadcasted_iota(jnp.int32, (n, n), 0)
    c = lax.broadcasted_iota(jnp.int32, (n, n), 1)
    if reverse:
        return r <= c, r < c
    return r >= c, r > c


def _block_cumsum_mats(ct, reverse):
    r = lax.broadcasted_iota(jnp.int32, (ct, ct), 0)
    c = lax.broadcasted_iota(jnp.int32, (ct, ct), 1)
    same = (r // CHUNK) == (c // CHUNK)
    lower = jnp.logical_and(same, c <= r)
    upper = jnp.logical_and(same, c >= r)
    lo = jnp.where(lower, 1.0, 0.0).astype(BF16)
    up = jnp.where(upper, 1.0, 0.0).astype(BF16)
    return (up, lo) if reverse else (lo, up)


def _cumsum_cols(mat, x):
    x0, x1, x2 = _split3(x)
    return _dot(mat, x0) + (_dot(mat, x1) + _dot(mat, x2))


def _cumsum_rows(x, mat):
    x0, x1, x2 = _split3(x)
    return _dot(x0, mat) + (_dot(x1, mat) + _dot(x2, mat))


def _gdn_block(dirs):
    dh = GDN_HEAD_DIM
    eye = jnp.where(lax.broadcasted_iota(jnp.int32, (CHUNK, CHUNK), 0)
                    == lax.broadcasted_iota(jnp.int32, (CHUNK, CHUNK), 1), 1.0, 0.0).astype(F32)
    chains = []
    for q_ref, k_ref, v_ref, gc_ref, gr_ref, o_ref, s_sc, reverse in dirs:
        ct = q_ref.shape[2]
        goff = 2 if reverse else 0
        gcol = gc_ref[0, 0]
        grow = gr_ref[0, 0]
        m_col, m_row = _block_cumsum_mats(ct, reverse)
        gcum_c = _cumsum_cols(m_col, gcol)
        gcum_r = _cumsum_rows(grow, m_row)
        tril, strict = _chunk_masks(CHUNK, reverse)
        for c in range(ct // CHUNK):
            rs = slice(c * CHUNK, (c + 1) * CHUNK)
            last = c * CHUNK if reverse else (c + 1) * CHUNK - 1
            q = q_ref[0, 0, rs, :]
            k = k_ref[0, 0, rs, :]
            kk = _dot_nt(k, k)
            qk = _dot_nt(q, k)
            for hh in range(2):
                gi = goff + hh
                chains.append(dict(
                    q=q, k=k, kk=kk, qk=qk, rs=rs, c=c, hh=hh, reverse=reverse, tril=tril, strict=strict,
                    v_ref=v_ref, o_ref=o_ref, s_sc=s_sc,
                    gc=gcum_c[rs, gi:gi + 1], gr=gcum_r[gi:gi + 1, rs],
                    glast=gcum_c[last:last + 1, gi:gi + 1], beta=gcol[rs, 4 + gi:5 + gi]))
    for ch in chains:
        tril = ch["tril"]
        ch["decay"] = jnp.where(tril, jnp.exp(jnp.where(tril, ch["gc"] - ch["gr"], 0.0)), 0.0)
        a = -jnp.where(ch["strict"], ch["kk"] * ch["beta"] * ch["decay"], 0.0)
        ch["tmat"] = eye + a
        ch["pw"] = a
    for _ in range(5):
        for ch in chains:
            pwb = ch["pw"].astype(BF16)
            ch["pw"] = _dot(pwb, pwb)
        for ch in chains:
            ch["tmat"] = ch["tmat"] + _dot(ch["tmat"].astype(BF16), ch["pw"].astype(BF16))
    for ch in chains:
        beta = ch["beta"]
        eg = jnp.exp(ch["gc"])
        vb = ch["v_ref"][0, ch["hh"], ch["rs"], :].astype(F32) * beta
        kbg = ch["k"].astype(F32) * (beta * eg)
        uw = _dot(ch["tmat"].astype(BF16), jnp.concatenate([vb, kbg], axis=-1).astype(BF16))
        ch["u"] = uw[:, :dh]
        ch["w"] = uw[:, dh:].astype(BF16)
        ch["eg"] = eg
        ch["attn"] = jnp.where(ch["tril"], ch["qk"] * ch["decay"], 0.0).astype(BF16)
    nchunk = max(ch["c"] for ch in chains) + 1
    for step in range(nchunk):
        cur = [ch for ch in chains if ch["c"] == (nchunk - 1 - step if ch["reverse"] else step)]
        for ch in cur:
            state = ch["s_sc"][ch["hh"]]
            sb = state.astype(BF16)
            ch["state"] = state
            ch["ws"] = _dot(ch["w"], sb)
            ch["qs"] = _dot(ch["q"], sb)
        for ch in cur:
            v_new = ch["u"] - ch["ws"]
            ch["kgv"] = (v_new * jnp.exp(ch["glast"] - ch["gc"])).astype(BF16)
            ch["o"] = ch["eg"] * ch["qs"] + _dot(ch["attn"], v_new.astype(BF16))
        for ch in cur:
            ch["s_sc"][ch["hh"]] = ch["state"] * jnp.exp(ch["glast"]) + _dot_tn(ch["k"], ch["kgv"])
            ch["o_ref"][0, ch["rs"], ch["hh"] * dh:(ch["hh"] + 1) * dh] = ch["o"].astype(BF16)


def _gdn_scan_kernel(qf, kf, vf, gcf, grf, qb, kb, vb, gcb, grb, s0f, s0b, *rest, nblk, aliased):
    if aliased:
        rest = rest[2:]
    of_ref, ob_ref, sff, sfb, sf_sc, sb_sc = rest
    j = pl.program_id(2)

    @pl.when(j == 0)
    def _():
        sf_sc[...] = s0f[0]
        sb_sc[...] = s0b[0]

    _gdn_block([(qf, kf, vf, gcf, grf, of_ref, sf_sc, False),
                (qb, kb, vb, gcb, grb, ob_ref, sb_sc, True)])

    @pl.when(j == nblk - 1)
    def _():
        sff[0] = sf_sc[...]
        sfb[0] = sb_sc[...]


def gdn_scan(qn, kn, vv, gcol, grow, s0f, s0b, *, nblk, off, prev=None):
    b, _, s, dh = qn.shape
    ct = SCAN_TILE
    fwd = lambda j: j + off
    bwd = lambda j: nblk - 1 - j + off
    def specs(pos):
        return [
            pl.BlockSpec((1, 1, ct, dh), lambda bi, h, j: (bi, h, pos(j), 0)),
            pl.BlockSpec((1, 1, ct, dh), lambda bi, h, j: (bi, h, pos(j), 0)),
            pl.BlockSpec((1, 2, ct, dh), lambda bi, h, j: (bi, h, pos(j), 0)),
            pl.BlockSpec((1, 1, ct, 8), lambda bi, h, j: (bi, h, pos(j), 0)),
            pl.BlockSpec((1, 1, 8, ct), lambda bi, h, j: (bi, h, 0, pos(j))),
        ]
    st_spec = pl.BlockSpec((1, 2, dh, dh), lambda bi, h, j: (bi, h, 0, 0))
    in_specs = specs(fwd) + specs(bwd) + [st_spec, st_spec]
    args = [qn, kn, vv, gcol, grow] * 2 + [s0f, s0b]
    aliases = {}
    if prev is not None:
        in_specs += [pl.BlockSpec(memory_space=pl.ANY)] * 2
        args += list(prev)
        aliases = {12: 0, 13: 1}
    o_shape = jax.ShapeDtypeStruct((b, s, GDN_VAL_DIM), BF16)
    st_shape = jax.ShapeDtypeStruct((b, GDN_V_HEADS, dh, dh), F32)
    kern = functools.partial(_gdn_scan_kernel, nblk=nblk, aliased=prev is not None)
    return pl.pallas_call(
        kern,
        out_shape=[o_shape, o_shape, st_shape, st_shape],
        grid=(b, GDN_K_HEADS, nblk),
        in_specs=in_specs,
        out_specs=[pl.BlockSpec((1, ct, 2 * dh), lambda bi, h, j: (bi, fwd(j), h)),
                   pl.BlockSpec((1, ct, 2 * dh), lambda bi, h, j: (bi, bwd(j), h)),
                   st_spec, st_spec],
        scratch_shapes=[pltpu.VMEM((2, dh, dh), F32), pltpu.VMEM((2, dh, dh), F32)],
        input_output_aliases=aliases,
        compiler_params=_cparams(("arbitrary", "arbitrary", "arbitrary")),
        name="gdn_scan",
    )(*args)


def _gla_direction(q_ref, k_ref, v_ref, gr_ref, wg_ref, gb_ref, o_ref, st_sc, z):
    reverse = z == 1
    ct = q_ref.shape[1]
    nchunk = ct // CHUNK
    logit = _dot3(gr_ref[0], wg_ref[z]) + gb_ref[z:z + 1, :]
    glog = -_softplus(-logit) / GLA_TAU
    m_col, _ = _block_cumsum_mats(ct, reverse)
    bcum = _cumsum_cols(m_col, glog)
    tril, _ = _chunk_masks(CHUNK, reverse)
    order = range(nchunk - 1, -1, -1) if reverse else range(nchunk)
    for c in order:
        rs = slice(c * CHUNK, (c + 1) * CHUNK)
        last = c * CHUNK if reverse else (c + 1) * CHUNK - 1
        bc = bcum[rs, :]
        bl = bcum[last:last + 1, :]
        qf = q_ref[0, rs, :].astype(F32) * (GLA_DK ** -0.5)
        kf = k_ref[0, rs, :].astype(F32)
        v = v_ref[0, rs, :]
        qe = (qf * jnp.exp(bc)).astype(BF16)
        ke = (kf * jnp.exp(-bc)).astype(BF16)
        kg = (kf * jnp.exp(bl - bc)).astype(BF16)
        attn = jnp.where(tril, _dot_nt(qe, ke), 0.0)
        st = st_sc[...]
        o = _dot(attn.astype(BF16), v) + _dot_nt(qe, st.astype(BF16))
        st_sc[...] = st * jnp.exp(bl) + _dot_tn(v, kg)
        o_ref[0, rs, :] = o.astype(BF16)


def _gla_scan_kernel(qf, kf, vf, grf, qb, kb, vb, grb, wg, gb, s0f, s0b, *rest, nblk, aliased):
    if aliased:
        rest = rest[2:]
    of_ref, ob_ref, sff, sfb, sf_sc, sb_sc = rest
    j = pl.program_id(2)

    @pl.when(j == 0)
    def _():
        sf_sc[...] = s0f[0, 0]
        sb_sc[...] = s0b[0, 0]

    _gla_direction(qf, kf, vf, grf, wg, gb, of_ref, sf_sc, 0)
    _gla_direction(qb, kb, vb, grb, wg, gb, ob_ref, sb_sc, 1)

    @pl.when(j == nblk - 1)
    def _():
        sff[0, 0] = sf_sc[...]
        sfb[0, 0] = sb_sc[...]


def gla_scan(p_main, gr, wg_pad, gate_b, s0f, s0b, *, nblk, off, prev=None):
    b, s, _ = p_main.shape
    ct = SCAN_TILE
    nh, dk, dv = GLA_HEADS, GLA_DK, GLA_DV
    fwd = lambda j: j + off
    bwd = lambda j: nblk - 1 - j + off
    def specs(pos):
        return [
            pl.BlockSpec((1, ct, dk), lambda bi, h, j: (bi, pos(j), h)),
            pl.BlockSpec((1, ct, dk), lambda bi, h, j: (bi, pos(j), nh + h)),
            pl.BlockSpec((1, ct, dv), lambda bi, h, j: (bi, pos(j), nh + h)),
            pl.BlockSpec((1, ct, LANES), lambda bi, h, j: (bi, pos(j), 0)),
        ]
    st_spec = pl.BlockSpec((1, 1, dv, dk), lambda bi, h, j: (bi, h, 0, 0))
    in_specs = specs(fwd) + specs(bwd) + [
        pl.BlockSpec((2, LANES, dk), lambda bi, h, j: (0, 0, h)),
        pl.BlockSpec((2, dk), lambda bi, h, j: (0, h)),
        st_spec, st_spec]
    args = [p_main, p_main, p_main, gr] * 2 + [wg_pad, gate_b, s0f, s0b]
    aliases = {}
    if prev is not None:
        in_specs += [pl.BlockSpec(memory_space=pl.ANY)] * 2
        args += list(prev)
        aliases = {12: 0, 13: 1}
    o_shape = jax.ShapeDtypeStruct((b, s, GLA_VAL_DIM), BF16)
    st_shape = jax.ShapeDtypeStruct((b, nh, dv, dk), F32)
    kern = functools.partial(_gla_scan_kernel, nblk=nblk, aliased=prev is not None)
    return pl.pallas_call(
        kern,
        out_shape=[o_shape, o_shape, st_shape, st_shape],
        grid=(b, nh, nblk),
        in_specs=in_specs,
        out_specs=[pl.BlockSpec((1, ct, dv), lambda bi, h, j: (bi, fwd(j), h)),
                   pl.BlockSpec((1, ct, dv), lambda bi, h, j: (bi, bwd(j), h)),
                   st_spec, st_spec],
        scratch_shapes=[pltpu.VMEM((dv, dk), F32), pltpu.VMEM((dv, dk), F32)],
        input_output_aliases=aliases,
        compiler_params=_cparams(("arbitrary", "arbitrary", "arbitrary")),
        name="gla_scan",
    )(*args)


def _router_kernel(x_ref, sc_ref, sh_ref, rw_ref, rb_ref, h_ref, wd_ref):
    h = x_ref[0] * (1.0 + sc_ref[0, 0]) + sh_ref[0, 0]
    h_ref[0] = h.astype(BF16)
    scores = jax.nn.sigmoid(_dot3(h, rw_ref[...]))
    lane = lax.broadcasted_iota(jnp.int32, scores.shape, 1)
    neg = jnp.float32(-jnp.inf)
    sel = jnp.where(lane < N_EXPERTS, scores + rb_ref[...], neg)
    chosen = jnp.zeros(scores.shape, jnp.bool_)
    for _ in range(TOP_K):
        mx = jnp.max(sel, axis=-1, keepdims=True)
        first = jnp.min(jnp.where(sel == mx, lane, LANES), axis=-1, keepdims=True)
        pick = lane == first
        chosen = jnp.logical_or(chosen, pick)
        sel = jnp.where(pick, neg, sel)
    picked = jnp.where(chosen, scores, 0.0)
    wt = picked / jnp.sum(picked, axis=-1, keepdims=True) * ROUTE_SCALE
    wd_ref[0] = jnp.where(lane == N_EXPERTS, 1.0, wt)


def moe_route(x, sc, sh, rw_pad, rb_pad, n_lat_tiles):
    b, s, d = x.shape
    tm = ROW_TILE
    mod_spec = pl.BlockSpec((1, 1, 1, d), lambda bi, i: (bi, i // n_lat_tiles, 0, 0))
    return pl.pallas_call(
        _router_kernel,
        out_shape=[jax.ShapeDtypeStruct((b, s, d), BF16), jax.ShapeDtypeStruct((b, s, LANES), F32)],
        grid=(b, s // tm),
        in_specs=[pl.BlockSpec((1, tm, d), lambda bi, i: (bi, i, 0)), mod_spec, mod_spec,
                  pl.BlockSpec((d, LANES), lambda bi, i: (0, 0)),
                  pl.BlockSpec((1, LANES), lambda bi, i: (0, 0))],
        out_specs=[pl.BlockSpec((1, tm, d), lambda bi, i: (bi, i, 0)),
                   pl.BlockSpec((1, tm, LANES), lambda bi, i: (bi, i, 0))],
        compiler_params=_cparams(("arbitrary", "arbitrary")),
        name="moe_router",
    )(x, sc, sh, rw_pad, rb_pad)


def _moe_kernel(h_ref, wd_ref, wgu_ref, wdn_ref, x_ref, gate_ref, lg_ref, lb_ref, out_ref, acc, *,
                n_exp, n_lat_rows):
    e = pl.program_id(2)

    @pl.when(e == 0)
    def _():
        acc[...] = jnp.zeros(acc.shape, F32)

    gu = _dot(h_ref[0], wgu_ref[0])
    act = _silu(gu[:, :EXPERT_FF]) * gu[:, EXPERT_FF:]
    wd = wd_ref[0]
    lane = lax.broadcasted_iota(jnp.int32, wd.shape, 1)
    wcol = jnp.sum(jnp.where(lane == e, wd, 0.0), axis=-1, keepdims=True)
    acc[...] += wcol * _dot(act.astype(BF16), wdn_ref[0])

    @pl.when(e == n_exp - 1)
    def _():
        tm = acc.shape[0]
        row = pl.program_id(1) * tm + lax.broadcasted_iota(jnp.int32, (tm, 1), 0)
        gate = jnp.where(row >= n_lat_rows, gate_ref[0, 1], gate_ref[0, 0])
        out_ref[0] = _post_norm(x_ref[0], acc[...], gate, lg_ref[...], lb_ref[...])


def moe_experts_post_norm(h, wd, wgu, wdn, x, gate, lg, lb, n_lat_rows, tm):
    b, s, d = x.shape
    n_exp = wgu.shape[0]
    row = lambda width: pl.BlockSpec((1, tm, width), lambda bi, i, e: (bi, i, 0))
    kern = functools.partial(_moe_kernel, n_exp=n_exp, n_lat_rows=n_lat_rows)
    return pl.pallas_call(
        kern,
        out_shape=jax.ShapeDtypeStruct((b, s, d), F32),
        grid=(b, s // tm, n_exp),
        in_specs=[row(d), row(LANES),
                  pl.BlockSpec((1, d, 2 * EXPERT_FF), lambda bi, i, e: (e, 0, 0)),
                  pl.BlockSpec((1, EXPERT_FF, d), lambda bi, i, e: (e, 0, 0)),
                  row(d),
                  pl.BlockSpec((1, 2, 1, d), lambda bi, i, e: (bi, 0, 0, 0)),
                  pl.BlockSpec((1, d), lambda bi, i, e: (0, 0)),
                  pl.BlockSpec((1, d), lambda bi, i, e: (0, 0))],
        out_specs=row(d),
        scratch_shapes=[pltpu.VMEM((tm, d), F32)],
        compiler_params=_cparams(("arbitrary", "arbitrary", "arbitrary")),
        name="moe_experts_post_norm",
    )(h, wd, wgu, wdn, x, gate, lg.reshape(1, d), lb.reshape(1, d))


def _rope_tables(n_lat, n_ctx):
    rows = n_lat // GRID_W
    rowp = jnp.repeat(jnp.arange(rows), GRID_W).astype(F32)
    colp = jnp.tile(jnp.arange(GRID_W), rows).astype(F32)
    n_freq = DA_HEAD_DIM // 4
    inv = 1.0 / (ROPE_BASE ** (jnp.arange(n_freq, dtype=F32) / n_freq))
    ang = jnp.concatenate([rowp[:, None] * inv, colp[:, None] * inv], -1)
    cos, sin = jnp.cos(ang), jnp.sin(ang)
    cos_t = jnp.tile(cos, (1, 4))
    sin_t = jnp.tile(jnp.concatenate([-sin, sin], -1), (1, 2))
    cos_t = jnp.concatenate([cos_t, jnp.ones((n_ctx, LANES), F32)], 0)
    sin_t = jnp.concatenate([sin_t, jnp.zeros((n_ctx, LANES), F32)], 0)
    return cos_t, sin_t


def _pick_tile(total, cands):
    for t in cands:
        if total % t == 0:
            return t
    raise ValueError(f"no tile for {total}")


def _pad_cols(w, n):
    return jnp.pad(w, ((0, 0), (0, n - w.shape[1])))


def _flash_both(p_all, lam_vec, subln, lam_init, n_lat, n_ctx):
    s = n_lat + n_ctx
    tq = _pick_tile(n_lat, (512, 256))
    tk = _pick_tile(s, (1280, 1024, 512, 256))
    o = diff_flash_attention(p_all, lam_vec, subln, lam_init, tq=tq, tk=tk, nq=n_lat // tq, nk=s // tk,
                             q_off=0, k_off=0)
    return diff_flash_attention(p_all, lam_vec, subln, lam_init, tq=n_ctx, tk=n_ctx, nq=1, nk=1,
                                q_off=n_lat // n_ctx, k_off=n_lat // n_ctx, prev=o)


def kernel(x, c, ctx, c_ctx, ada_w, ada_b, ln_g, ln_b, da_w_in, da_w_o, da_lambda, da_subln, gdn_w_in, gdn_conv, gdn_a_log, gdn_dt_bias, gdn_norm, gdn_w_o, gla_w_in, gla_w_gate, gla_gate_b, gla_norm, gla_w_o, moe_router, moe_router_b, moe_w_gu, moe_w_dn, moe_ws_gu, moe_ws_dn):
    b, n, d = x.shape
    lc = ctx.shape[1]
    assert lc == ROW_TILE and n % SCAN_TILE == 0 and d == D_MODEL
    s = n + lc
    n_lat_tiles = n // ROW_TILE
    depth = ada_w.shape[0]

    xall = jnp.concatenate([x, ctx], axis=1)
    c8 = jnp.concatenate([c, c_ctx[None], jnp.zeros((8 - b - 1, d), F32)], 0)
    mods = ada_modulation(c8, ada_w, ada_b)
    rope = _rope_tables(n, lc)
    moe_tm = _pick_tile(s, (1280, 1024, 512, 256))

    for i in range(depth):
        kind, j = i % N_MIXERS, i // N_MIXERS
        m = mods[i].reshape(8, ADA_CHUNKS, d)
        mod = jnp.stack([m[:b], jnp.broadcast_to(m[b], (b, ADA_CHUNKS, d))], axis=1)
        mod = [mod[:, :, k][:, :, None, :] for k in range(ADA_CHUNKS)]

        if kind == 0:
            lam_init = 0.8 - 0.6 * math.exp(-0.3 * i)
            w = da_w_in[j]
            w = jnp.concatenate([w[:, :d] * (DA_HEAD_DIM ** -0.5), w[:, d:]], 1).astype(BF16)
            p_all = modulated_projection(xall, mod[1], mod[0], w, rope=rope, n_rope=2 * d,
                                         n_lat_tiles=n_lat_tiles)
            o = _flash_both(p_all, da_lambda[j], da_subln[j], lam_init, n, lc)
            xall = out_projection_post_norm([o], da_w_o[j].astype(BF16), xall, mod[2],
                                            ln_g[i, 0], ln_b[i, 0], n_lat_tiles)
        elif kind == 1:
            w = gdn_w_in[j]
            n_main = 2 * GDN_KEY_DIM + 2 * GDN_VAL_DIM
            p_main, gates_raw = modulated_projection(
                xall, mod[1], mod[0], w[:, :n_main].astype(BF16),
                w_small=_pad_cols(w[:, n_main:], LANES).astype(BF16), n_lat_tiles=n_lat_tiles)
            alog_row = _pad_cols(gdn_a_log[j].reshape(1, -1), LANES)
            dtb_row = _pad_cols(gdn_dt_bias[j].reshape(1, -1), LANES)
            qn, kn, vv, gates = gdn_prep(p_main, gates_raw, gdn_conv[j], alog_row, dtb_row, n_lat_tiles)
            hv, hk = GDN_V_HEADS, GDN_K_HEADS
            def per_khead(t):
                t = t.reshape(b, s, 2, hk, 2)
                return jnp.transpose(t, (0, 3, 1, 2, 4)).reshape(b, hk, s, 4)
            gcol = jnp.concatenate([per_khead(gates[..., :2 * hv]), per_khead(gates[..., 2 * hv:4 * hv])], -1)
            grow = jnp.swapaxes(gcol, 2, 3)
            zeros = jnp.zeros((b, hv, GDN_HEAD_DIM, GDN_HEAD_DIM), F32)
            of, ob, scf, scb = gdn_scan(qn, kn, vv, gcol, grow, zeros, zeros, nblk=lc // SCAN_TILE,
                                        off=n // SCAN_TILE)
            of, ob, _, _ = gdn_scan(qn, kn, vv, gcol, grow, scf, scb, nblk=n // SCAN_TILE, off=0,
                                    prev=(of, ob))
            xall = out_projection_post_norm([of, ob, p_main, gdn_norm[j]], gdn_w_o[j].astype(BF16), xall,
                                            mod[2], ln_g[i, 0], ln_b[i, 0], n_lat_tiles,
                                            gated=(GDN_V_HEADS, GDN_HEAD_DIM, 2))
        else:
            w = gla_w_in[j]
            n_main = 2 * GLA_KEY_DIM + 2 * GLA_VAL_DIM
            p_main, gr = modulated_projection(
                xall, mod[1], mod[0], w[:, :n_main].astype(BF16),
                w_small=_pad_cols(w[:, n_main:], LANES).astype(BF16), n_lat_tiles=n_lat_tiles)
            wg = gla_w_gate[j]
            wg_pad = jnp.zeros((2, LANES, GLA_KEY_DIM), F32)
            wg_pad = wg_pad.at[0, :GLA_GATE_RANK].set(wg[0]).at[1, GLA_GATE_RANK:2 * GLA_GATE_RANK].set(wg[1])
            zeros = jnp.zeros((b, GLA_HEADS, GLA_DV, GLA_DK), F32)
            of, ob, scf, scb = gla_scan(p_main, gr, wg_pad, gla_gate_b[j], zeros, zeros,
                                        nblk=lc // SCAN_TILE, off=n // SCAN_TILE)
            of, ob, _, _ = gla_scan(p_main, gr, wg_pad, gla_gate_b[j], scf, scb, nblk=n // SCAN_TILE, off=0,
                                    prev=(of, ob))
            xall = out_projection_post_norm([of, ob, p_main, gla_norm[j]], gla_w_o[j].astype(BF16), xall,
                                            mod[2], ln_g[i, 0], ln_b[i, 0], n_lat_tiles,
                                            gated=(GLA_HEADS, GLA_DV, 2))

        rw_pad = _pad_cols(moe_router[i], LANES)
        rb_pad = _pad_cols(moe_router_b[i].reshape(1, -1), LANES)
        h, wd = moe_route(xall, mod[4], mod[3], rw_pad, rb_pad, n_lat_tiles)
        wgu = jnp.concatenate([moe_w_gu[i], moe_ws_gu[i][None]], 0).astype(BF16)
        wdn = jnp.concatenate([moe_w_dn[i], moe_ws_dn[i][None]], 0).astype(BF16)
        xall = moe_experts_post_norm(h, wd, wgu, wdn, xall, mod[5], ln_g[i, 1], ln_b[i, 1], n, moe_tm)

    return xall[:, :n]
```

```python
import functools
import math

import jax
import jax.numpy as jnp
from jax import lax
from jax.experimental import pallas as pl
from jax.experimental.pallas import tpu as pltpu

F32 = jnp.float32
BF16 = jnp.bfloat16

D_MODEL = 1024
DEPTH = 4
GRID_W = 64
N_MIXERS = 3
DN_ALPHA = (2 * DEPTH) ** 0.25
LN_EPS = 1e-5
NORM_EPS = 1e-6
ADA_CHUNKS = 6

DA_HEADS = 8
DA_HEAD_DIM = 64
ROPE_BASE = 10000.0

GDN_K_HEADS = 8
GDN_V_HEADS = 16
GDN_HEAD_DIM = 128
GDN_KEY_DIM = 1024
GDN_VAL_DIM = 2048
GDN_CONV_W = 5
CHUNK = 64

GLA_HEADS = 4
GLA_KEY_DIM = 512
GLA_VAL_DIM = 1024
GLA_DK = 128
GLA_DV = 256
GLA_GATE_RANK = 16
GLA_TAU = 16.0

N_EXPERTS = 64
TOP_K = 8
EXPERT_FF = 256
ROUTE_SCALE = 2.5

LANES = 128
ROW_TILE = 256
SCAN_TILE = 256
HALO = 16
VMEM_LIMIT = 56 * 1024 * 1024


def _cparams(sem):
    return pltpu.CompilerParams(dimension_semantics=sem, vmem_limit_bytes=VMEM_LIMIT)


def _dot(a, b):
    return jnp.dot(a, b, preferred_element_type=F32)


def _dot_nt(a, b):
    return lax.dot_general(a, b, (((1,), (1,)), ((), ())), preferred_element_type=F32)


def _dot_tn(a, b):
    return lax.dot_general(a, b, (((0,), (0,)), ((), ())), preferred_element_type=F32)


def _split2(a):
    hi = a.astype(BF16)
    lo = (a - hi.astype(F32)).astype(BF16)
    return hi, lo


def _split3(a):
    a0 = a.astype(BF16)
    r = a - a0.astype(F32)
    a1 = r.astype(BF16)
    a2 = (r - a1.astype(F32)).astype(BF16)
    return a0, a1, a2


def _dot3(a, b):
    a0, a1 = _split2(a)
    b0, b1 = _split2(b)
    return _dot(a0, b0) + (_dot(a0, b1) + _dot(a1, b0))


def _silu(x):
    return x * jax.nn.sigmoid(x)


def _softplus(x):
    return jnp.maximum(x, 0.0) + jnp.log(1.0 + jnp.exp(-jnp.abs(x)))


def _ada_kernel(c_ref, w_ref, b_ref, o_ref):
    s = _silu(c_ref[...])
    o_ref[0] = _dot3(s, w_ref[0]) + b_ref[0]


def ada_modulation(c8, ada_w, ada_b):
    depth, d, n = ada_w.shape
    tn = 1536
    return pl.pallas_call(
        _ada_kernel,
        out_shape=jax.ShapeDtypeStruct((depth, 8, n), F32),
        grid=(depth, n // tn),
        in_specs=[
            pl.BlockSpec((8, d), lambda i, j: (0, 0)),
            pl.BlockSpec((1, d, tn), lambda i, j: (i, 0, j)),
            pl.BlockSpec((1, 1, tn), lambda i, j: (i, 0, j)),
        ],
        out_specs=pl.BlockSpec((1, 8, tn), lambda i, j: (i, 0, j)),
        compiler_params=_cparams(("arbitrary", "arbitrary")),
        name="ada_modulation",
    )(c8, ada_w, ada_b.reshape(depth, 1, n))


def _proj_kernel(*refs, n_main, n_rope, has_small, cn):
    it = iter(refs)
    x_ref, sc_ref, sh_ref, w_ref = next(it), next(it), next(it), next(it)
    ws_ref = next(it) if has_small else None
    cos_ref = next(it) if n_rope else None
    sin_ref = next(it) if n_rope else None
    o_ref = next(it)
    os_ref = next(it) if has_small else None

    h = x_ref[0] * (1.0 + sc_ref[0, 0]) + sh_ref[0, 0]
    hb = h.astype(BF16)
    tm = hb.shape[0]
    if n_rope:
        cos = cos_ref[...]
        sin = sin_ref[...]
        lane = lax.broadcasted_iota(jnp.int32, (tm, LANES), 1)
        low_half = (lane & 32) == 0
    for j in range(n_main // cn):
        p = _dot(hb, w_ref[:, j * cn:(j + 1) * cn])
        for g in range(cn // LANES):
            pg = p[:, g * LANES:(g + 1) * LANES]
            col = j * cn + g * LANES
            if col < n_rope:
                partner = jnp.where(low_half, pltpu.roll(pg, LANES - 32, 1), pltpu.roll(pg, 32, 1))
                pg = pg * cos + partner * sin
            o_ref[0, :, col:col + LANES] = pg.astype(BF16)
    if has_small:
        os_ref[0] = _dot(hb, ws_ref[...])


def modulated_projection(x, sc, sh, w, w_small=None, rope=None, n_rope=0, n_lat_tiles=1):
    b, s, d = x.shape
    n_main = w.shape[1]
    tm = ROW_TILE
    grid = (b, s // tm)
    mod_spec = pl.BlockSpec((1, 1, 1, d), lambda bi, i: (bi, i // n_lat_tiles, 0, 0))
    in_specs = [pl.BlockSpec((1, tm, d), lambda bi, i: (bi, i, 0)), mod_spec, mod_spec,
                pl.BlockSpec((d, n_main), lambda bi, i: (0, 0))]
    args = [x, sc, sh, w]
    out_shape = [jax.ShapeDtypeStruct((b, s, n_main), BF16)]
    out_specs = [pl.BlockSpec((1, tm, n_main), lambda bi, i: (bi, i, 0))]
    if w_small is not None:
        in_specs.append(pl.BlockSpec((d, LANES), lambda bi, i: (0, 0)))
        args.append(w_small)
        out_shape.append(jax.ShapeDtypeStruct((b, s, LANES), F32))
        out_specs.append(pl.BlockSpec((1, tm, LANES), lambda bi, i: (bi, i, 0)))
    if n_rope:
        tab = pl.BlockSpec((tm, LANES), lambda bi, i: (i, 0))
        in_specs += [tab, tab]
        args += [rope[0], rope[1]]
    kern = functools.partial(_proj_kernel, n_main=n_main, n_rope=n_rope,
                             has_small=w_small is not None, cn=512)
    out = pl.pallas_call(
        kern, out_shape=out_shape, grid=grid, in_specs=in_specs, out_specs=out_specs,
        compiler_params=_cparams(("arbitrary", "arbitrary")), name="modulated_projection",
    )(*args)
    return out if w_small is not None else out[0]


NEG_INIT = -1e30


def _flash_kernel(lam_ref, q_ref, k_ref, v_ref, sub_ref, *rest, lam_init, nk, aliased):
    if aliased:
        rest = rest[1:]
    o_ref, s_a, s_b, mt_a, mt_b, m_sc, l_sc, acc_sc = rest
    ki = pl.program_id(3)
    tq, tk = s_a.shape[1], s_a.shape[2]

    @pl.when(ki == 0)
    def _():
        m_sc[...] = jnp.full(m_sc.shape, NEG_INIT, F32)
        l_sc[...] = jnp.zeros(l_sc.shape, F32)
        acc_sc[...] = jnp.zeros(acc_sc.shape, F32)
        s_b[...] = jnp.full(s_b.shape, NEG_INIT, F32)
        mt_b[...] = jnp.full(mt_b.shape, NEG_INIT, F32)

    def step(s_w, mt_w, s_r, mt_r):
        q = q_ref[0]
        k = k_ref[0]
        v = v_ref[0]
        hd = DA_HEAD_DIM
        for c in range(2):
            s = _dot_nt(q[:, c * hd:(c + 1) * hd], k[:, c * hd:(c + 1) * hd])
            s_w[c] = s
            mt_w[c] = jnp.broadcast_to(jnp.max(s, axis=-1, keepdims=True), (tq, LANES))
            m_prev = m_sc[c]
            m_new = jnp.maximum(m_prev, mt_r[c])
            alpha = jnp.exp2(m_prev - m_new)
            lsum = None
            ps = []
            for j in range(tk // LANES):
                pj = jnp.exp2(s_r[c, :, j * LANES:(j + 1) * LANES] - m_new)
                lsum = pj if lsum is None else lsum + pj
                ps.append(pj.astype(BF16))
            l_sc[c] = alpha * l_sc[c] + lsum
            acc_sc[c] = alpha * acc_sc[c] + _dot(jnp.concatenate(ps, axis=-1), v)
            m_sc[c] = m_new

    @pl.when(ki % 2 == 0)
    def _():
        step(s_a, mt_a, s_b, mt_b)

    @pl.when(ki % 2 == 1)
    def _():
        step(s_b, mt_b, s_a, mt_a)

    @pl.when(ki == nk)
    def _():
        lv = lam_ref[...]
        lam = (jnp.exp(jnp.sum(lv[0:1] * lv[1:2], axis=-1, keepdims=True))
               - jnp.exp(jnp.sum(lv[2:3] * lv[3:4], axis=-1, keepdims=True)) + lam_init)
        l0 = jnp.sum(l_sc[0], axis=-1, keepdims=True)
        l1 = jnp.sum(l_sc[1], axis=-1, keepdims=True)
        o = acc_sc[0] / l0 - lam * (acc_sc[1] / l1)
        ms = jnp.mean(o * o, axis=-1, keepdims=True)
        o = o * lax.rsqrt(ms + NORM_EPS) * sub_ref[...] * (1.0 - lam_init)
        o_ref[0] = o.astype(BF16)


def diff_flash_attention(p_all, lam_vec, subln, lam_init, *, tq, tk, nq, nk, q_off, k_off, prev=None):
    b, s, _ = p_all.shape
    hh = DA_HEADS
    in_specs = [
        pl.BlockSpec((4, DA_HEAD_DIM), lambda bi, h, qi, ki: (0, 0)),
        pl.BlockSpec((1, tq, LANES), lambda bi, h, qi, ki: (bi, qi + q_off, h)),
        pl.BlockSpec((1, tk, LANES), lambda bi, h, qi, ki: (bi, jnp.minimum(ki, nk - 1) + k_off, hh + h)),
        pl.BlockSpec((1, tk, LANES), lambda bi, h, qi, ki: (bi, jnp.maximum(ki - 1, 0) + k_off, 2 * hh + h)),
        pl.BlockSpec((1, LANES), lambda bi, h, qi, ki: (0, 0)),
    ]
    args = [lam_vec, p_all, p_all, p_all, subln.reshape(1, LANES)]
    aliases = {}
    if prev is not None:
        in_specs.append(pl.BlockSpec(memory_space=pl.ANY))
        args.append(prev)
        aliases = {5: 0}
    kern = functools.partial(_flash_kernel, lam_init=lam_init, nk=nk, aliased=prev is not None)
    return pl.pallas_call(
        kern,
        out_shape=jax.ShapeDtypeStruct((b, s, hh * LANES), BF16),
        grid=(b, hh, nq, nk + 1),
        in_specs=in_specs,
        out_specs=pl.BlockSpec((1, tq, LANES), lambda bi, h, qi, ki: (bi, qi + q_off, h)),
        scratch_shapes=[pltpu.VMEM((2, tq, tk), F32), pltpu.VMEM((2, tq, tk), F32),
                        pltpu.VMEM((2, tq, LANES), F32), pltpu.VMEM((2, tq, LANES), F32),
                        pltpu.VMEM((2, tq, LANES), F32), pltpu.VMEM((2, tq, LANES), F32),
                        pltpu.VMEM((2, tq, LANES), F32)],
        input_output_aliases=aliases,
        compiler_params=_cparams(("arbitrary", "arbitrary", "arbitrary", "arbitrary")),
        name="diff_flash_attention",
    )(*args)


def _post_norm(x, y, gate, lg, lb):
    r = DN_ALPHA * x + gate * y
    mu = jnp.mean(r, axis=-1, keepdims=True)
    rc = r - mu
    var = jnp.mean(rc * rc, axis=-1, keepdims=True)
    return rc * lax.rsqrt(var + LN_EPS) * lg + lb


def _outproj_kernel(*refs, gated, n_heads, dh):
    if gated:
        of_ref, ob_ref, z_ref, ng_ref, w_ref, x_ref, gate_ref, lg_ref, lb_ref, out_ref = refs
        y = None
        for h in range(n_heads):
            sl = slice(h * dh, (h + 1) * dh)
            o = of_ref[0, :, sl].astype(F32) + ob_ref[0, :, sl].astype(F32)
            ms = jnp.mean(o * o, axis=-1, keepdims=True)
            o = o * lax.rsqrt(ms + NORM_EPS) * ng_ref[...] * _silu(z_ref[0, :, sl].astype(F32))
            t = _dot(o.astype(BF16), w_ref[sl, :])
            y = t if y is None else y + t
    else:
        o_ref, w_ref, x_ref, gate_ref, lg_ref, lb_ref, out_ref = refs
        y = _dot(o_ref[0], w_ref[...])
    out_ref[0] = _post_norm(x_ref[0], y, gate_ref[0, 0], lg_ref[...], lb_ref[...])


def out_projection_post_norm(o_args, w_o, x, gate, lg, lb, n_lat_tiles, gated=None):
    b, s, d = x.shape
    kdim = w_o.shape[0]
    tm = ROW_TILE
    row = lambda width, cb=0: pl.BlockSpec((1, tm, width), lambda bi, i: (bi, i, cb))
    full = lambda shape: pl.BlockSpec(shape, lambda bi, i: (0,) * len(shape))
    if gated is None:
        in_specs = [row(kdim)]
        args = list(o_args)
        kern = functools.partial(_outproj_kernel, gated=False, n_heads=0, dh=0)
    else:
        n_heads, dh, z_cb = gated
        of, ob, z, ng = o_args
        in_specs = [row(kdim), row(kdim), row(kdim, z_cb), full((1, dh))]
        args = [of, ob, z, ng.reshape(1, dh)]
        kern = functools.partial(_outproj_kernel, gated=True, n_heads=n_heads, dh=dh)
    in_specs += [full((kdim, d)), row(d),
                 pl.BlockSpec((1, 1, 1, d), lambda bi, i: (bi, i // n_lat_tiles, 0, 0)),
                 full((1, d)), full((1, d))]
    args += [w_o, x, gate, lg.reshape(1, d), lb.reshape(1, d)]
    return pl.pallas_call(
        kern, out_shape=jax.ShapeDtypeStruct((b, s, d), F32), grid=(b, s // tm),
        in_specs=in_specs, out_specs=row(d),
        compiler_params=_cparams(("arbitrary", "arbitrary")), name="out_projection_post_norm",
    )(*args)


def _gdn_prep_kernel(cur_ref, prev_ref, next_ref, gate_ref, cw_ref, alog_ref, dtb_ref,
                     q_ref, k_ref, v_ref, g_ref, ext, *, n_lat_tiles, n_tiles, cn):
    i = pl.program_id(1)
    tm = cur_ref.shape[1]
    first = jnp.logical_or(i == 0, i == n_lat_tiles)
    last = jnp.logical_or(i == n_lat_tiles - 1, i == n_tiles - 1)
    pmask = jnp.where(first, 0.0, 1.0)
    nmask = jnp.where(last, 0.0, 1.0)
    pad = GDN_CONV_W // 2
    dh = GDN_HEAD_DIM
    n_qk = 2 * GDN_KEY_DIM
    for cc in range(cur_ref.shape[2] // cn):
        cs = slice(cc * cn, (cc + 1) * cn)
        ext[0:HALO, :] = prev_ref[0, :, cs].astype(F32) * pmask
        ext[HALO:HALO + tm, :] = cur_ref[0, :, cs].astype(F32)
        ext[HALO + tm:2 * HALO + tm, :] = next_ref[0, :, cs].astype(F32) * nmask
        acc = None
        for j in range(GDN_CONV_W):
            t = ext[pl.ds(HALO - pad + j, tm), :] * cw_ref[j:j + 1, cs]
            acc = t if acc is None else acc + t
        y = _silu(acc)
        for g in range(cn // dh):
            col = cc * cn + g * dh
            yg = y[:, g * dh:(g + 1) * dh]
            if col < n_qk:
                yg = yg * lax.rsqrt(jnp.sum(yg * yg, axis=-1, keepdims=True) + NORM_EPS)
                if col < GDN_KEY_DIM:
                    q_ref[0, col // dh] = (yg * (dh ** -0.5)).astype(BF16)
                else:
                    k_ref[0, (col - GDN_KEY_DIM) // dh] = yg.astype(BF16)
            else:
                v_ref[0, (col - n_qk) // dh] = yg.astype(BF16)
    a = gate_ref[0]
    lane = lax.broadcasted_iota(jnp.int32, a.shape, 1)
    gdec = -jnp.exp(alog_ref[...]) * _softplus(a + dtb_ref[...])
    g_ref[0] = jnp.where(lane < 2 * GDN_V_HEADS, gdec, jax.nn.sigmoid(a))


def gdn_prep(p_main, gates_raw, conv_w, alog_row, dtb_row, n_lat_tiles):
    b, s, _ = p_main.shape
    tm = ROW_TILE
    nt = s // tm
    nch = 2 * GDN_KEY_DIM + GDN_VAL_DIM
    hpt = tm // HALO
    nh = s // HALO
    kern = functools.partial(_gdn_prep_kernel, n_lat_tiles=n_lat_tiles, n_tiles=nt, cn=512)
    head_out = lambda nheads: pl.BlockSpec((1, nheads, tm, GDN_HEAD_DIM), lambda bi, i: (bi, 0, i, 0))
    return pl.pallas_call(
        kern,
        out_shape=[jax.ShapeDtypeStruct((b, GDN_K_HEADS, s, GDN_HEAD_DIM), BF16),
                   jax.ShapeDtypeStruct((b, GDN_K_HEADS, s, GDN_HEAD_DIM), BF16),
                   jax.ShapeDtypeStruct((b, GDN_V_HEADS, s, GDN_HEAD_DIM), BF16),
                   jax.ShapeDtypeStruct((b, s, LANES), F32)],
        grid=(b, nt),
        in_specs=[
            pl.BlockSpec((1, tm, nch), lambda bi, i: (bi, i, 0)),
            pl.BlockSpec((1, HALO, nch), lambda bi, i: (bi, jnp.maximum(i * hpt - 1, 0), 0)),
            pl.BlockSpec((1, HALO, nch), lambda bi, i: (bi, jnp.minimum((i + 1) * hpt, nh - 1), 0)),
            pl.BlockSpec((1, tm, LANES), lambda bi, i: (bi, i, 0)),
            pl.BlockSpec((GDN_CONV_W, nch), lambda bi, i: (0, 0)),
            pl.BlockSpec((1, LANES), lambda bi, i: (0, 0)),
            pl.BlockSpec((1, LANES), lambda bi, i: (0, 0)),
        ],
        out_specs=[head_out(GDN_K_HEADS), head_out(GDN_K_HEADS), head_out(GDN_V_HEADS),
                   pl.BlockSpec((1, tm, LANES), lambda bi, i: (bi, i, 0))],
        scratch_shapes=[pltpu.VMEM((tm + 2 * HALO, 512), F32)],
        compiler_params=_cparams(("arbitrary", "arbitrary")),
        name="gdn_prep",
    )(p_main, p_main, p_main, gates_raw, conv_w, alog_row, dtb_row)


def _chunk_masks(n, reverse):
    r = lax.broadcasted_iota(jnp.int32, (n, n), 0)
    c = lax.broadcasted_iota(jnp.int32, (n, n), 1)
    if reverse:
        return r <= c, r < c
    return r >= c, r > c


def _block_cumsum_mats(ct, reverse):
    r = lax.broadcasted_iota(jnp.int32, (ct, ct), 0)
    c = lax.broadcasted_iota(jnp.int32, (ct, ct), 1)
    same = (r // CHUNK) == (c // CHUNK)
    lower = jnp.logical_and(same, c <= r)
    upper = jnp.logical_and(same, c >= r)
    lo = jnp.where(lower, 1.0, 0.0).astype(BF16)
    up = jnp.where(upper, 1.0, 0.0).astype(BF16)
    return (up, lo) if reverse else (lo, up)


def _cumsum_cols(mat, x):
    x0, x1, x2 = _split3(x)
    return _dot(mat, x0) + (_dot(mat, x1) + _dot(mat, x2))


def _cumsum_rows(x, mat):
    x0, x1, x2 = _split3(x)
    return _dot(x0, mat) + (_dot(x1, mat) + _dot(x2, mat))


def _gdn_block(dirs):
    dh = GDN_HEAD_DIM
    eye = jnp.where(lax.broadcasted_iota(jnp.int32, (CHUNK, CHUNK), 0)
                    == lax.broadcasted_iota(jnp.int32, (CHUNK, CHUNK), 1), 1.0, 0.0).astype(F32)
    chains = []
    for q_ref, k_ref, v_ref, gc_ref, gr_ref, o_ref, s_sc, reverse in dirs:
        ct = q_ref.shape[2]
        goff = 2 if reverse else 0
        gcol = gc_ref[0, 0]
        grow = gr_ref[0, 0]
        m_col, m_row = _block_cumsum_mats(ct, reverse)
        gcum_c = _cumsum_cols(m_col, gcol)
        gcum_r = _cumsum_rows(grow, m_row)
        tril, strict = _chunk_masks(CHUNK, reverse)
        for c in range(ct // CHUNK):
            rs = slice(c * CHUNK, (c + 1) * CHUNK)
            last = c * CHUNK if reverse else (c + 1) * CHUNK - 1
            q = q_ref[0, 0, rs, :]
            k = k_ref[0, 0, rs, :]
            kk = _dot_nt(k, k)
            qk = _dot_nt(q, k)
            for hh in range(2):
                gi = goff + hh
                chains.append(dict(
                    q=q, k=k, kk=kk, qk=qk, rs=rs, c=c, hh=hh, reverse=reverse, tril=tril, strict=strict,
                    v_ref=v_ref, o_ref=o_ref, s_sc=s_sc,
                    gc=gcum_c[rs, gi:gi + 1], gr=gcum_r[gi:gi + 1, rs],
                    glast=gcum_c[last:last + 1, gi:gi + 1], beta=gcol[rs, 4 + gi:5 + gi]))
    for ch in chains:
        tril = ch["tril"]
        ch["decay"] = jnp.where(tril, jnp.exp(jnp.where(tril, ch["gc"] - ch["gr"], 0.0)), 0.0)
        a = -jnp.where(ch["strict"], ch["kk"] * ch["beta"] * ch["decay"], 0.0)
        ch["tmat"] = eye + a
        ch["pw"] = a
    for _ in range(5):
        for ch in chains:
            pwb = ch["pw"].astype(BF16)
            ch["pw"] = _dot(pwb, pwb)
        for ch in chains:
            ch["tmat"] = ch["tmat"] + _dot(ch["tmat"].astype(BF16), ch["pw"].astype(BF16))
    for ch in chains:
        beta = ch["beta"]
        eg = jnp.exp(ch["gc"])
        vb = ch["v_ref"][0, ch["hh"], ch["rs"], :].astype(F32) * beta
        kbg = ch["k"].astype(F32) * (beta * eg)
        uw = _dot(ch["tmat"].astype(BF16), jnp.concatenate([vb, kbg], axis=-1).astype(BF16))
        ch["u"] = uw[:, :dh]
        ch["w"] = uw[:, dh:].astype(BF16)
        ch["eg"] = eg
        ch["attn"] = jnp.where(ch["tril"], ch["qk"] * ch["decay"], 0.0).astype(BF16)
    nchunk = max(ch["c"] for ch in chains) + 1
    for step in range(nchunk):
        cur = [ch for ch in chains if ch["c"] == (nchunk - 1 - step if ch["reverse"] else step)]
        for ch in cur:
            state = ch["s_sc"][ch["hh"]]
            sb = state.astype(BF16)
            ch["state"] = state
            ch["ws"] = _dot(ch["w"], sb)
            ch["qs"] = _dot(ch["q"], sb)
        for ch in cur:
            v_new = ch["u"] - ch["ws"]
            ch["kgv"] = (v_new * jnp.exp(ch["glast"] - ch["gc"])).astype(BF16)
            ch["o"] = ch["eg"] * ch["qs"] + _dot(ch["attn"], v_new.astype(BF16))
        for ch in cur:
            ch["s_sc"][ch["hh"]] = ch["state"] * jnp.exp(ch["glast"]) + _dot_tn(ch["k"], ch["kgv"])
            ch["o_ref"][0, ch["rs"], ch["hh"] * dh:(ch["hh"] + 1) * dh] = ch["o"].astype(BF16)


def _gdn_scan_kernel(qf, kf, vf, gcf, grf, qb, kb, vb, gcb, grb, s0f, s0b, *rest, nblk, aliased):
    if aliased:
        rest = rest[2:]
    of_ref, ob_ref, sff, sfb, sf_sc, sb_sc = rest
    j = pl.program_id(2)

    @pl.when(j == 0)
    def _():
        sf_sc[...] = s0f[0]
        sb_sc[...] = s0b[0]

    _gdn_block([(qf, kf, vf, gcf, grf, of_ref, sf_sc, False),
                (qb, kb, vb, gcb, grb, ob_ref, sb_sc, True)])

    @pl.when(j == nblk - 1)
    def _():
        sff[0] = sf_sc[...]
        sfb[0] = sb_sc[...]


def gdn_scan(qn, kn, vv, gcol, grow, s0f, s0b, *, nblk, off, prev=None):
    b, _, s, dh = qn.shape
    ct = SCAN_TILE
    fwd = lambda j: j + off
    bwd = lambda j: nblk - 1 - j + off
    def specs(pos):
        return [
            pl.BlockSpec((1, 1, ct, dh), lambda bi, h, j: (bi, h, pos(j), 0)),
            pl.BlockSpec((1, 1, ct, dh), lambda bi, h, j: (bi, h, pos(j), 0)),
            pl.BlockSpec((1, 2, ct, dh), lambda bi, h, j: (bi, h, pos(j), 0)),
            pl.BlockSpec((1, 1, ct, 8), lambda bi, h, j: (bi, h, pos(j), 0)),
            pl.BlockSpec((1, 1, 8, ct), lambda bi, h, j: (bi, h, 0, pos(j))),
        ]
    st_spec = pl.BlockSpec((1, 2, dh, dh), lambda bi, h, j: (bi, h, 0, 0))
    in_specs = specs(fwd) + specs(bwd) + [st_spec, st_spec]
    args = [qn, kn, vv, gcol, grow] * 2 + [s0f, s0b]
    aliases = {}
    if prev is not None:
        in_specs += [pl.BlockSpec(memory_space=pl.ANY)] * 2
        args += list(prev)
        aliases = {12: 0, 13: 1}
    o_shape = jax.ShapeDtypeStruct((b, s, GDN_VAL_DIM), BF16)
    st_shape = jax.ShapeDtypeStruct((b, GDN_V_HEADS, dh, dh), F32)
    kern = functools.partial(_gdn_scan_kernel, nblk=nblk, aliased=prev is not None)
    return pl.pallas_call(
        kern,
        out_shape=[o_shape, o_shape, st_shape, st_shape],
        grid=(b, GDN_K_HEADS, nblk),
        in_specs=in_specs,
        out_specs=[pl.BlockSpec((1, ct, 2 * dh), lambda bi, h, j: (bi, fwd(j), h)),
                   pl.BlockSpec((1, ct, 2 * dh), lambda bi, h, j: (bi, bwd(j), h)),
                   st_spec, st_spec],
        scratch_shapes=[pltpu.VMEM((2, dh, dh), F32), pltpu.VMEM((2, dh, dh), F32)],
        input_output_aliases=aliases,
        compiler_params=_cparams(("arbitrary", "arbitrary", "arbitrary")),
        name="gdn_scan",
    )(*args)


def _gla_direction(q_ref, k_ref, v_ref, gr_ref, wg_ref, gb_ref, o_ref, st_sc, z):
    reverse = z == 1
    ct = q_ref.shape[1]
    nchunk = ct // CHUNK
    logit = _dot3(gr_ref[0], wg_ref[z]) + gb_ref[z:z + 1, :]
    glog = -_softplus(-logit) / GLA_TAU
    m_col, _ = _block_cumsum_mats(ct, reverse)
    bcum = _cumsum_cols(m_col, glog)
    tril, _ = _chunk_masks(CHUNK, reverse)
    order = range(nchunk - 1, -1, -1) if reverse else range(nchunk)
    for c in order:
        rs = slice(c * CHUNK, (c + 1) * CHUNK)
        last = c * CHUNK if reverse else (c + 1) * CHUNK - 1
        bc = bcum[rs, :]
        bl = bcum[last:last + 1, :]
        qf = q_ref[0, rs, :].astype(F32) * (GLA_DK ** -0.5)
        kf = k_ref[0, rs, :].astype(F32)
        v = v_ref[0, rs, :]
        qe = (qf * jnp.exp(bc)).astype(BF16)
        ke = (kf * jnp.exp(-bc)).astype(BF16)
        kg = (kf * jnp.exp(bl - bc)).astype(BF16)
        attn = jnp.where(tril, _dot_nt(qe, ke), 0.0)
        st = st_sc[...]
        o = _dot(attn.astype(BF16), v) + _dot_nt(qe, st.astype(BF16))
        st_sc[...] = st * jnp.exp(bl) + _dot_tn(v, kg)
        o_ref[0, rs, :] = o.astype(BF16)


def _gla_scan_kernel(qf, kf, vf, grf, qb, kb, vb, grb, wg, gb, s0f, s0b, *rest, nblk, aliased):
    if aliased:
        rest = rest[2:]
    of_ref, ob_ref, sff, sfb, sf_sc, sb_sc = rest
    j = pl.program_id(2)

    @pl.when(j == 0)
    def _():
        sf_sc[...] = s0f[0, 0]
        sb_sc[...] = s0b[0, 0]

    _gla_direction(qf, kf, vf, grf, wg, gb, of_ref, sf_sc, 0)
    _gla_direction(qb, kb, vb, grb, wg, gb, ob_ref, sb_sc, 1)

    @pl.when(j == nblk - 1)
    def _():
        sff[0, 0] = sf_sc[...]
        sfb[0, 0] = sb_sc[...]


def gla_scan(p_main, gr, wg_pad, gate_b, s0f, s0b, *, nblk, off, prev=None):
    b, s, _ = p_main.shape
    ct = SCAN_TILE
    nh, dk, dv = GLA_HEADS, GLA_DK, GLA_DV
    fwd = lambda j: j + off
    bwd = lambda j: nblk - 1 - j + off
    def specs(pos):
        return [
            pl.BlockSpec((1, ct, dk), lambda bi, h, j: (bi, pos(j), h)),
            pl.BlockSpec((1, ct, dk), lambda bi, h, j: (bi, pos(j), nh + h)),
            pl.BlockSpec((1, ct, dv), lambda bi, h, j: (bi, pos(j), nh + h)),
            pl.BlockSpec((1, ct, LANES), lambda bi, h, j: (bi, pos(j), 0)),
        ]
    st_spec = pl.BlockSpec((1, 1, dv, dk), lambda bi, h, j: (bi, h, 0, 0))
    in_specs = specs(fwd) + specs(bwd) + [
        pl.BlockSpec((2, LANES, dk), lambda bi, h, j: (0, 0, h)),
        pl.BlockSpec((2, dk), lambda bi, h, j: (0, h)),
        st_spec, st_spec]
    args = [p_main, p_main, p_main, gr] * 2 + [wg_pad, gate_b, s0f, s0b]
    aliases = {}
    if prev is not None:
        in_specs += [pl.BlockSpec(memory_space=pl.ANY)] * 2
        args += list(prev)
        aliases = {12: 0, 13: 1}
    o_shape = jax.ShapeDtypeStruct((b, s, GLA_VAL_DIM), BF16)
    st_shape = jax.ShapeDtypeStruct((b, nh, dv, dk), F32)
    kern = functools.partial(_gla_scan_kernel, nblk=nblk, aliased=prev is not None)
    return pl.pallas_call(
        kern,
        out_shape=[o_shape, o_shape, st_shape, st_shape],
        grid=(b, nh, nblk),
        in_specs=in_specs,
        out_specs=[pl.BlockSpec((1, ct, dv), lambda bi, h, j: (bi, fwd(j), h)),
                   pl.BlockSpec((1, ct, dv), lambda bi, h, j: (bi, bwd(j), h)),
                   st_spec, st_spec],
        scratch_shapes=[pltpu.VMEM((dv, dk), F32), pltpu.VMEM((dv, dk), F32)],
        input_output_aliases=aliases,
        compiler_params=_cparams(("arbitrary", "arbitrary", "arbitrary")),
        name="gla_scan",
    )(*args)


def _router_kernel(x_ref, sc_ref, sh_ref, rw_ref, rb_ref, h_ref, wd_ref):
    h = x_ref[0] * (1.0 + sc_ref[0, 0]) + sh_ref[0, 0]
    h_ref[0] = h.astype(BF16)
    scores = jax.nn.sigmoid(_dot3(h, rw_ref[...]))
    lane = lax.broadcasted_iota(jnp.int32, scores.shape, 1)
    neg = jnp.float32(-jnp.inf)
    sel = jnp.where(lane < N_EXPERTS, scores + rb_ref[...], neg)
    chosen = jnp.zeros(scores.shape, jnp.bool_)
    for _ in range(TOP_K):
        mx = jnp.max(sel, axis=-1, keepdims=True)
        first = jnp.min(jnp.where(sel == mx, lane, LANES), axis=-1, keepdims=True)
        pick = lane == first
        chosen = jnp.logical_or(chosen, pick)
        sel = jnp.where(pick, neg, sel)
    picked = jnp.where(chosen, scores, 0.0)
    wt = picked / jnp.sum(picked, axis=-1, keepdims=True) * ROUTE_SCALE
    wd_ref[0] = jnp.where(lane == N_EXPERTS, 1.0, wt)


def moe_route(x, sc, sh, rw_pad, rb_pad, n_lat_tiles):
    b, s, d = x.shape
    tm = ROW_TILE
    mod_spec = pl.BlockSpec((1, 1, 1, d), lambda bi, i: (bi, i // n_lat_tiles, 0, 0))
    return pl.pallas_call(
        _router_kernel,
        out_shape=[jax.ShapeDtypeStruct((b, s, d), BF16), jax.ShapeDtypeStruct((b, s, LANES), F32)],
        grid=(b, s // tm),
        in_specs=[pl.BlockSpec((1, tm, d), lambda bi, i: (bi, i, 0)), mod_spec, mod_spec,
                  pl.BlockSpec((d, LANES), lambda bi, i: (0, 0)),
                  pl.BlockSpec((1, LANES), lambda bi, i: (0, 0))],
        out_specs=[pl.BlockSpec((1, tm, d), lambda bi, i: (bi, i, 0)),
                   pl.BlockSpec((1, tm, LANES), lambda bi, i: (bi, i, 0))],
        compiler_params=_cparams(("arbitrary", "arbitrary")),
        name="moe_router",
    )(x, sc, sh, rw_pad, rb_pad)


def _moe_kernel(h_ref, wd_ref, wgu_ref, wdn_ref, x_ref, gate_ref, lg_ref, lb_ref, out_ref, acc, *,
                n_exp, n_lat_rows):
    e = pl.program_id(2)

    @pl.when(e == 0)
    def _():
        acc[...] = jnp.zeros(acc.shape, F32)

    gu = _dot(h_ref[0], wgu_ref[0])
    act = _silu(gu[:, :EXPERT_FF]) * gu[:, EXPERT_FF:]
    wd = wd_ref[0]
    lane = lax.broadcasted_iota(jnp.int32, wd.shape, 1)
    wcol = jnp.sum(jnp.where(lane == e, wd, 0.0), axis=-1, keepdims=True)
    acc[...] += wcol * _dot(act.astype(BF16), wdn_ref[0])

    @pl.when(e == n_exp - 1)
    def _():
        tm = acc.shape[0]
        row = pl.program_id(1) * tm + lax.broadcasted_iota(jnp.int32, (tm, 1), 0)
        gate = jnp.where(row >= n_lat_rows, gate_ref[0, 1], gate_ref[0, 0])
        out_ref[0] = _post_norm(x_ref[0], acc[...], gate, lg_ref[...], lb_ref[...])


def moe_experts_post_norm(h, wd, wgu, wdn, x, gate, lg, lb, n_lat_rows, tm):
    b, s, d = x.shape
    n_exp = wgu.shape[0]
    row = lambda width: pl.BlockSpec((1, tm, width), lambda bi, i, e: (bi, i, 0))
    kern = functools.partial(_moe_kernel, n_exp=n_exp, n_lat_rows=n_lat_rows)
    return pl.pallas_call(
        kern,
        out_shape=jax.ShapeDtypeStruct((b, s, d), F32),
        grid=(b, s // tm, n_exp),
        in_specs=[row(d), row(LANES),
                  pl.BlockSpec((1, d, 2 * EXPERT_FF), lambda bi, i, e: (e, 0, 0)),
                  pl.BlockSpec((1, EXPERT_FF, d), lambda bi, i, e: (e, 0, 0)),
                  row(d),
                  pl.BlockSpec((1, 2, 1, d), lambda bi, i, e: (bi, 0, 0, 0)),
                  pl.BlockSpec((1, d), lambda bi, i, e: (0, 0)),
                  pl.BlockSpec((1, d), lambda bi, i, e: (0, 0))],
        out_specs=row(d),
        scratch_shapes=[pltpu.VMEM((tm, d), F32)],
        compiler_params=_cparams(("arbitrary", "arbitrary", "arbitrary")),
        name="moe_experts_post_norm",
    )(h, wd, wgu, wdn, x, gate, lg.reshape(1, d), lb.reshape(1, d))


def _rope_tables(n_lat, n_ctx):
    rows = n_lat // GRID_W
    rowp = jnp.repeat(jnp.arange(rows), GRID_W).astype(F32)
    colp = jnp.tile(jnp.arange(GRID_W), rows).astype(F32)
    n_freq = DA_HEAD_DIM // 4
    inv = 1.0 / (ROPE_BASE ** (jnp.arange(n_freq, dtype=F32) / n_freq))
    ang = jnp.concatenate([rowp[:, None] * inv, colp[:, None] * inv], -1)
    cos, sin = jnp.cos(ang), jnp.sin(ang)
    cos_t = jnp.tile(cos, (1, 4))
    sin_t = jnp.tile(jnp.concatenate([-sin, sin], -1), (1, 2))
    cos_t = jnp.concatenate([cos_t, jnp.ones((n_ctx, LANES), F32)], 0)
    sin_t = jnp.concatenate([sin_t, jnp.zeros((n_ctx, LANES), F32)], 0)
    return cos_t, sin_t


def _pick_tile(total, cands):
    for t in cands:
        if total % t == 0:
            return t
    raise ValueError(f"no tile for {total}")


def _pad_cols(w, n):
    return jnp.pad(w, ((0, 0), (0, n - w.shape[1])))


def _flash_both(p_all, lam_vec, subln, lam_init, n_lat, n_ctx):
    s = n_lat + n_ctx
    tq = _pick_tile(n_lat, (512, 256))
    tk = _pick_tile(s, (1280, 1024, 512, 256))
    o = diff_flash_attention(p_all, lam_vec, subln, lam_init, tq=tq, tk=tk, nq=n_lat // tq, nk=s // tk,
                             q_off=0, k_off=0)
    return diff_flash_attention(p_all, lam_vec, subln, lam_init, tq=n_ctx, tk=n_ctx, nq=1, nk=1,
                                q_off=n_lat // n_ctx, k_off=n_lat // n_ctx, prev=o)


def kernel(x, c, ctx, c_ctx, ada_w, ada_b, ln_g, ln_b, da_w_in, da_w_o, da_lambda, da_subln, gdn_w_in, gdn_conv, gdn_a_log, gdn_dt_bias, gdn_norm, gdn_w_o, gla_w_in, gla_w_gate, gla_gate_b, gla_norm, gla_w_o, moe_router, moe_router_b, moe_w_gu, moe_w_dn, moe_ws_gu, moe_ws_dn):
    b, n, d = x.shape
    lc = ctx.shape[1]
    assert lc == ROW_TILE and n % SCAN_TILE == 0 and d == D_MODEL
    s = n + lc
    n_lat_tiles = n // ROW_TILE
    depth = ada_w.shape[0]

    xall = jnp.concatenate([x, ctx], axis=1)
    c8 = jnp.concatenate([c, c_ctx[None], jnp.zeros((8 - b - 1, d), F32)], 0)
    mods = ada_modulation(c8, ada_w, ada_b)
    rope = _rope_tables(n, lc)
    moe_tm = _pick_tile(s, (1280, 1024, 512, 256))

    for i in range(depth):
        kind, j = i % N_MIXERS, i // N_MIXERS
        m = mods[i].reshape(8, ADA_CHUNKS, d)
        mod = jnp.stack([m[:b], jnp.broadcast_to(m[b], (b, ADA_CHUNKS, d))], axis=1)
        mod = [mod[:, :, k][:, :, None, :] for k in range(ADA_CHUNKS)]

        if kind == 0:
            lam_init = 0.8 - 0.6 * math.exp(-0.3 * i)
            w = da_w_in[j]
            w = jnp.concatenate([w[:, :d] * (DA_HEAD_DIM ** -0.5 * math.log2(math.e)), w[:, d:]], 1).astype(BF16)
            p_all = modulated_projection(xall, mod[1], mod[0], w, rope=rope, n_rope=2 * d,
                                         n_lat_tiles=n_lat_tiles)
            o = _flash_both(p_all, da_lambda[j], da_subln[j], lam_init, n, lc)
            xall = out_projection_post_norm([o], da_w_o[j].astype(BF16), xall, mod[2],
                                            ln_g[i, 0], ln_b[i, 0], n_lat_tiles)
        elif kind == 1:
            w = gdn_w_in[j]
            n_main = 2 * GDN_KEY_DIM + 2 * GDN_VAL_DIM
            p_main, gates_raw = modulated_projection(
                xall, mod[1], mod[0], w[:, :n_main].astype(BF16),
                w_small=_pad_cols(w[:, n_main:], LANES).astype(BF16), n_lat_tiles=n_lat_tiles)
            alog_row = _pad_cols(gdn_a_log[j].reshape(1, -1), LANES)
            dtb_row = _pad_cols(gdn_dt_bias[j].reshape(1, -1), LANES)
            qn, kn, vv, gates = gdn_prep(p_main, gates_raw, gdn_conv[j], alog_row, dtb_row, n_lat_tiles)
            hv, hk = GDN_V_HEADS, GDN_K_HEADS
            def per_khead(t):
                t = t.reshape(b, s, 2, hk, 2)
                return jnp.transpose(t, (0, 3, 1, 2, 4)).reshape(b, hk, s, 4)
            gcol = jnp.concatenate([per_khead(gates[..., :2 * hv]), per_khead(gates[..., 2 * hv:4 * hv])], -1)
            grow = jnp.swapaxes(gcol, 2, 3)
            zeros = jnp.zeros((b, hv, GDN_HEAD_DIM, GDN_HEAD_DIM), F32)
            of, ob, scf, scb = gdn_scan(qn, kn, vv, gcol, grow, zeros, zeros, nblk=lc // SCAN_TILE,
                                        off=n // SCAN_TILE)
            of, ob, _, _ = gdn_scan(qn, kn, vv, gcol, grow, scf, scb, nblk=n // SCAN_TILE, off=0,
                                    prev=(of, ob))
            xall = out_projection_post_norm([of, ob, p_main, gdn_norm[j]], gdn_w_o[j].astype(BF16), xall,
                                            mod[2], ln_g[i, 0], ln_b[i, 0], n_lat_tiles,
                                            gated=(GDN_V_HEADS, GDN_HEAD_DIM, 2))
        else:
            w = gla_w_in[j]
            n_main = 2 * GLA_KEY_DIM + 2 * GLA_VAL_DIM
            p_main, gr = modulated_projection(
                xall, mod[1], mod[0], w[:, :n_main].astype(BF16),
                w_small=_pad_cols(w[:, n_main:], LANES).astype(BF16), n_lat_tiles=n_lat_tiles)
            wg = gla_w_gate[j]
            wg_pad = jnp.zeros((2, LANES, GLA_KEY_DIM), F32)
            wg_pad = wg_pad.at[0, :GLA_GATE_RANK].set(wg[0]).at[1, GLA_GATE_RANK:2 * GLA_GATE_RANK].set(wg[1])
            zeros = jnp.zeros((b, GLA_HEADS, GLA_DV, GLA_DK), F32)
            of, ob, scf, scb = gla_scan(p_main, gr, wg_pad, gla_gate_b[j], zeros, zeros,
                                        nblk=lc // SCAN_TILE, off=n // SCAN_TILE)
            of, ob, _, _ = gla_scan(p_main, gr, wg_pad, gla_gate_b[j], scf, scb, nblk=n // SCAN_TILE, off=0,
                                    prev=(of, ob))
            xall = out_projection_post_norm([of, ob, p_main, gla_norm[j]], gla_w_o[j].astype(BF16), xall,
                                            mod[2], ln_g[i, 0], ln_b[i, 0], n_lat_tiles,
                                            gated=(GLA_HEADS, GLA_DV, 2))

        rw_pad = _pad_cols(moe_router[i], LANES)
        rb_pad = _pad_cols(moe_router_b[i].reshape(1, -1), LANES)
        h, wd = moe_route(xall, mod[4], mod[3], rw_pad, rb_pad, n_lat_tiles)
        wgu = jnp.concatenate([moe_w_gu[i], moe_ws_gu[i][None]], 0).astype(BF16)
        wdn = jnp.concatenate([moe_w_dn[i], moe_ws_dn[i][None]], 0).astype(BF16)
        xall = moe_experts_post_norm(h, wd, wgu, wdn, xall, mod[5], ln_g[i, 1], ln_b[i, 1], n, moe_tm)

    return xall[:, :n]
```

```python
import functools
import math

import jax
import jax.numpy as jnp
from jax import lax
from jax.experimental import pallas as pl
from jax.experimental.pallas import tpu as pltpu

F32 = jnp.float32
BF16 = jnp.bfloat16

D_MODEL = 1024
DEPTH = 4
GRID_W = 64
N_MIXERS = 3
DN_ALPHA = (2 * DEPTH) ** 0.25
LN_EPS = 1e-5
NORM_EPS = 1e-6
ADA_CHUNKS = 6

DA_HEADS = 8
DA_HEAD_DIM = 64
ROPE_BASE = 10000.0

GDN_K_HEADS = 8
GDN_V_HEADS = 16
GDN_HEAD_DIM = 128
GDN_KEY_DIM = 1024
GDN_VAL_DIM = 2048
GDN_CONV_W = 5
CHUNK = 64

GLA_HEADS = 4
GLA_KEY_DIM = 512
GLA_VAL_DIM = 1024
GLA_DK = 128
GLA_DV = 256
GLA_GATE_RANK = 16
GLA_TAU = 16.0

N_EXPERTS = 64
TOP_K = 8
EXPERT_FF = 256
ROUTE_SCALE = 2.5

LANES = 128
ROW_TILE = 256
SCAN_TILE = 256
HALO = 16
VMEM_LIMIT = 56 * 1024 * 1024


def _cparams(sem, flags=None):
    return pltpu.CompilerParams(dimension_semantics=sem, vmem_limit_bytes=VMEM_LIMIT, flags=flags)


def _dot(a, b):
    return jnp.dot(a, b, preferred_element_type=F32)


def _dot_nt(a, b):
    return lax.dot_general(a, b, (((1,), (1,)), ((), ())), preferred_element_type=F32)


def _dot_tn(a, b):
    return lax.dot_general(a, b, (((0,), (0,)), ((), ())), preferred_element_type=F32)


def _split2(a):
    hi = a.astype(BF16)
    lo = (a - hi.astype(F32)).astype(BF16)
    return hi, lo


def _split3(a):
    a0 = a.astype(BF16)
    r = a - a0.astype(F32)
    a1 = r.astype(BF16)
    a2 = (r - a1.astype(F32)).astype(BF16)
    return a0, a1, a2


def _dot3(a, b):
    a0, a1 = _split2(a)
    b0, b1 = _split2(b)
    return _dot(a0, b0) + (_dot(a0, b1) + _dot(a1, b0))


def _silu(x):
    return x * jax.nn.sigmoid(x)


def _softplus(x):
    return jnp.maximum(x, 0.0) + jnp.log(1.0 + jnp.exp(-jnp.abs(x)))


def _ada_kernel(c_ref, w_ref, b_ref, o_ref):
    s = _silu(c_ref[...])
    o_ref[0] = _dot3(s, w_ref[0]) + b_ref[0]


def ada_modulation(c8, ada_w, ada_b):
    depth, d, n = ada_w.shape
    tn = 1536
    return pl.pallas_call(
        _ada_kernel,
        out_shape=jax.ShapeDtypeStruct((depth, 8, n), F32),
        grid=(depth, n // tn),
        in_specs=[
            pl.BlockSpec((8, d), lambda i, j: (0, 0)),
            pl.BlockSpec((1, d, tn), lambda i, j: (i, 0, j)),
            pl.BlockSpec((1, 1, tn), lambda i, j: (i, 0, j)),
        ],
        out_specs=pl.BlockSpec((1, 8, tn), lambda i, j: (i, 0, j)),
        compiler_params=_cparams(("arbitrary", "arbitrary")),
        name="ada_modulation",
    )(c8, ada_w, ada_b.reshape(depth, 1, n))


def _proj_kernel(*refs, n_main, n_rope, has_small, cn):
    it = iter(refs)
    x_ref, sc_ref, sh_ref, w_ref = next(it), next(it), next(it), next(it)
    ws_ref = next(it) if has_small else None
    cos_ref = next(it) if n_rope else None
    sin_ref = next(it) if n_rope else None
    o_ref = next(it)
    os_ref = next(it) if has_small else None

    h = x_ref[0] * (1.0 + sc_ref[0, 0]) + sh_ref[0, 0]
    hb = h.astype(BF16)
    tm = hb.shape[0]
    if n_rope:
        cos = cos_ref[...]
        sin = sin_ref[...]
        lane = lax.broadcasted_iota(jnp.int32, (tm, LANES), 1)
        low_half = (lane & 32) == 0
    for j in range(n_main // cn):
        p = _dot(hb, w_ref[:, j * cn:(j + 1) * cn])
        for g in range(cn // LANES):
            pg = p[:, g * LANES:(g + 1) * LANES]
            col = j * cn + g * LANES
            if col < n_rope:
                partner = jnp.where(low_half, pltpu.roll(pg, LANES - 32, 1), pltpu.roll(pg, 32, 1))
                pg = pg * cos + partner * sin
            o_ref[0, :, col:col + LANES] = pg.astype(BF16)
    if has_small:
        os_ref[0] = _dot(hb, ws_ref[...])


def modulated_projection(x, sc, sh, w, w_small=None, rope=None, n_rope=0, n_lat_tiles=1):
    b, s, d = x.shape
    n_main = w.shape[1]
    tm = ROW_TILE
    grid = (b, s // tm)
    mod_spec = pl.BlockSpec((1, 1, 1, d), lambda bi, i: (bi, i // n_lat_tiles, 0, 0))
    in_specs = [pl.BlockSpec((1, tm, d), lambda bi, i: (bi, i, 0)), mod_spec, mod_spec,
                pl.BlockSpec((d, n_main), lambda bi, i: (0, 0))]
    args = [x, sc, sh, w]
    out_shape = [jax.ShapeDtypeStruct((b, s, n_main), BF16)]
    out_specs = [pl.BlockSpec((1, tm, n_main), lambda bi, i: (bi, i, 0))]
    if w_small is not None:
        in_specs.append(pl.BlockSpec((d, LANES), lambda bi, i: (0, 0)))
        args.append(w_small)
        out_shape.append(jax.ShapeDtypeStruct((b, s, LANES), F32))
        out_specs.append(pl.BlockSpec((1, tm, LANES), lambda bi, i: (bi, i, 0)))
    if n_rope:
        tab = pl.BlockSpec((tm, LANES), lambda bi, i: (i, 0))
        in_specs += [tab, tab]
        args += [rope[0], rope[1]]
    kern = functools.partial(_proj_kernel, n_main=n_main, n_rope=n_rope,
                             has_small=w_small is not None, cn=512)
    out = pl.pallas_call(
        kern, out_shape=out_shape, grid=grid, in_specs=in_specs, out_specs=out_specs,
        compiler_params=_cparams(("arbitrary", "arbitrary")), name="modulated_projection",
    )(*args)
    return out if w_small is not None else out[0]


NEG_INIT = -1e30


def _flash_kernel(lam_ref, q_ref, k_ref, v_ref, sub_ref, *rest, lam_init, nk, aliased):
    if aliased:
        rest = rest[1:]
    o_ref, s_a, s_b, mt_a, mt_b, m_sc, l_sc, acc_sc = rest
    ki = pl.program_id(3)
    tq, tk = s_a.shape[1], s_a.shape[2]

    @pl.when(ki == 0)
    def _():
        m_sc[...] = jnp.full(m_sc.shape, NEG_INIT, F32)
        l_sc[...] = jnp.zeros(l_sc.shape, F32)
        acc_sc[...] = jnp.zeros(acc_sc.shape, F32)
        s_b[...] = jnp.full(s_b.shape, NEG_INIT, F32)
        mt_b[...] = jnp.full(mt_b.shape, NEG_INIT, F32)

    def step(s_w, mt_w, s_r, mt_r):
        q = q_ref[0]
        k = k_ref[0]
        v = v_ref[0]
        hd = DA_HEAD_DIM
        for c in range(2):
            s = _dot_nt(q[:, c * hd:(c + 1) * hd], k[:, c * hd:(c + 1) * hd])
            s_w[c] = s
            mt_w[c] = jnp.broadcast_to(jnp.max(s, axis=-1, keepdims=True), (tq, LANES))
            m_prev = m_sc[c]
            m_new = jnp.maximum(m_prev, mt_r[c])
            alpha = jnp.exp2(m_prev - m_new)
            lsum = None
            pv = None
            kc = 2 * LANES
            for j in range(tk // kc):
                pj = jnp.exp2(s_r[c, :, j * kc:(j + 1) * kc] - jnp.concatenate([m_new, m_new], axis=-1))
                lj = pj[:, :LANES] + pj[:, LANES:]
                lsum = lj if lsum is None else lsum + lj
                t = _dot(pj.astype(BF16), v[j * kc:(j + 1) * kc, :])
                pv = t if pv is None else pv + t
            l_sc[c] = alpha * l_sc[c] + lsum
            acc_sc[c] = alpha * acc_sc[c] + pv
            m_sc[c] = m_new

    @pl.when(ki % 2 == 0)
    def _():
        step(s_a, mt_a, s_b, mt_b)

    @pl.when(ki % 2 == 1)
    def _():
        step(s_b, mt_b, s_a, mt_a)

    @pl.when(ki == nk)
    def _():
        lv = lam_ref[...]
        lam = (jnp.exp(jnp.sum(lv[0:1] * lv[1:2], axis=-1, keepdims=True))
               - jnp.exp(jnp.sum(lv[2:3] * lv[3:4], axis=-1, keepdims=True)) + lam_init)
        l0 = jnp.sum(l_sc[0], axis=-1, keepdims=True)
        l1 = jnp.sum(l_sc[1], axis=-1, keepdims=True)
        o = acc_sc[0] / l0 - lam * (acc_sc[1] / l1)
        ms = jnp.mean(o * o, axis=-1, keepdims=True)
        o = o * lax.rsqrt(ms + NORM_EPS) * sub_ref[...] * (1.0 - lam_init)
        o_ref[0] = o.astype(BF16)


def diff_flash_attention(p_all, lam_vec, subln, lam_init, *, tq, tk, nq, nk, q_off, k_off, prev=None):
    b, s, _ = p_all.shape
    hh = DA_HEADS
    in_specs = [
        pl.BlockSpec((4, DA_HEAD_DIM), lambda bi, h, qi, ki: (0, 0)),
        pl.BlockSpec((1, tq, LANES), lambda bi, h, qi, ki: (bi, qi + q_off, h)),
        pl.BlockSpec((1, tk, LANES), lambda bi, h, qi, ki: (bi, jnp.minimum(ki, nk - 1) + k_off, hh + h)),
        pl.BlockSpec((1, tk, LANES), lambda bi, h, qi, ki: (bi, jnp.maximum(ki - 1, 0) + k_off, 2 * hh + h)),
        pl.BlockSpec((1, LANES), lambda bi, h, qi, ki: (0, 0)),
    ]
    args = [lam_vec, p_all, p_all, p_all, subln.reshape(1, LANES)]
    aliases = {}
    if prev is not None:
        in_specs.append(pl.BlockSpec(memory_space=pl.ANY))
        args.append(prev)
        aliases = {5: 0}
    kern = functools.partial(_flash_kernel, lam_init=lam_init, nk=nk, aliased=prev is not None)
    return pl.pallas_call(
        kern,
        out_shape=jax.ShapeDtypeStruct((b, s, hh * LANES), BF16),
        grid=(b, hh, nq, nk + 1),
        in_specs=in_specs,
        out_specs=pl.BlockSpec((1, tq, LANES), lambda bi, h, qi, ki: (bi, qi + q_off, h)),
        scratch_shapes=[pltpu.VMEM((2, tq, tk), F32), pltpu.VMEM((2, tq, tk), F32),
                        pltpu.VMEM((2, tq, LANES), F32), pltpu.VMEM((2, tq, LANES), F32),
                        pltpu.VMEM((2, tq, LANES), F32), pltpu.VMEM((2, tq, LANES), F32),
                        pltpu.VMEM((2, tq, LANES), F32)],
        input_output_aliases=aliases,
        compiler_params=_cparams(("arbitrary", "arbitrary", "arbitrary", "arbitrary")),
        name="diff_flash_attention",
    )(*args)


def _post_norm(x, y, gate, lg, lb):
    r = DN_ALPHA * x + gate * y
    mu = jnp.mean(r, axis=-1, keepdims=True)
    rc = r - mu
    var = jnp.mean(rc * rc, axis=-1, keepdims=True)
    return rc * lax.rsqrt(var + LN_EPS) * lg + lb


def _outproj_kernel(*refs, gated, n_heads, dh):
    if gated:
        of_ref, ob_ref, z_ref, ng_ref, w_ref, x_ref, gate_ref, lg_ref, lb_ref, out_ref = refs
        y = None
        for h in range(n_heads):
            sl = slice(h * dh, (h + 1) * dh)
            o = of_ref[0, :, sl].astype(F32) + ob_ref[0, :, sl].astype(F32)
            ms = jnp.mean(o * o, axis=-1, keepdims=True)
            o = o * lax.rsqrt(ms + NORM_EPS) * ng_ref[...] * _silu(z_ref[0, :, sl].astype(F32))
            t = _dot(o.astype(BF16), w_ref[sl, :])
            y = t if y is None else y + t
    else:
        o_ref, w_ref, x_ref, gate_ref, lg_ref, lb_ref, out_ref = refs
        y = _dot(o_ref[0], w_ref[...])
    out_ref[0] = _post_norm(x_ref[0], y, gate_ref[0, 0], lg_ref[...], lb_ref[...])


def out_projection_post_norm(o_args, w_o, x, gate, lg, lb, n_lat_tiles, gated=None):
    b, s, d = x.shape
    kdim = w_o.shape[0]
    tm = ROW_TILE
    row = lambda width, cb=0: pl.BlockSpec((1, tm, width), lambda bi, i: (bi, i, cb))
    full = lambda shape: pl.BlockSpec(shape, lambda bi, i: (0,) * len(shape))
    if gated is None:
        in_specs = [row(kdim)]
        args = list(o_args)
        kern = functools.partial(_outproj_kernel, gated=False, n_heads=0, dh=0)
    else:
        n_heads, dh, z_cb = gated
        of, ob, z, ng = o_args
        in_specs = [row(kdim), row(kdim), row(kdim, z_cb), full((1, dh))]
        args = [of, ob, z, ng.reshape(1, dh)]
        kern = functools.partial(_outproj_kernel, gated=True, n_heads=n_heads, dh=dh)
    in_specs += [full((kdim, d)), row(d),
                 pl.BlockSpec((1, 1, 1, d), lambda bi, i: (bi, i // n_lat_tiles, 0, 0)),
                 full((1, d)), full((1, d))]
    args += [w_o, x, gate, lg.reshape(1, d), lb.reshape(1, d)]
    return pl.pallas_call(
        kern, out_shape=jax.ShapeDtypeStruct((b, s, d), F32), grid=(b, s // tm),
        in_specs=in_specs, out_specs=row(d),
        compiler_params=_cparams(("arbitrary", "arbitrary")), name="out_projection_post_norm",
    )(*args)


def _gdn_prep_kernel(cur_ref, prev_ref, next_ref, gate_ref, cw_ref, alog_ref, dtb_ref,
                     q_ref, k_ref, v_ref, g_ref, ext, *, n_lat_tiles, n_tiles, cn):
    i = pl.program_id(1)
    tm = cur_ref.shape[1]
    first = jnp.logical_or(i == 0, i == n_lat_tiles)
    last = jnp.logical_or(i == n_lat_tiles - 1, i == n_tiles - 1)
    pmask = jnp.where(first, 0.0, 1.0)
    nmask = jnp.where(last, 0.0, 1.0)
    pad = GDN_CONV_W // 2
    dh = GDN_HEAD_DIM
    n_qk = 2 * GDN_KEY_DIM
    for cc in range(cur_ref.shape[2] // cn):
        cs = slice(cc * cn, (cc + 1) * cn)
        ext[0:HALO, :] = prev_ref[0, :, cs].astype(F32) * pmask
        ext[HALO:HALO + tm, :] = cur_ref[0, :, cs].astype(F32)
        ext[HALO + tm:2 * HALO + tm, :] = next_ref[0, :, cs].astype(F32) * nmask
        acc = None
        for j in range(GDN_CONV_W):
            t = ext[pl.ds(HALO - pad + j, tm), :] * cw_ref[j:j + 1, cs]
            acc = t if acc is None else acc + t
        y = _silu(acc)
        for g in range(cn // dh):
            col = cc * cn + g * dh
            yg = y[:, g * dh:(g + 1) * dh]
            if col < n_qk:
                yg = yg * lax.rsqrt(jnp.sum(yg * yg, axis=-1, keepdims=True) + NORM_EPS)
                if col < GDN_KEY_DIM:
                    q_ref[0, col // dh] = (yg * (dh ** -0.5)).astype(BF16)
                else:
                    k_ref[0, (col - GDN_KEY_DIM) // dh] = yg.astype(BF16)
            else:
                v_ref[0, (col - n_qk) // dh] = yg.astype(BF16)
    a = gate_ref[0]
    lane = lax.broadcasted_iota(jnp.int32, a.shape, 1)
    gdec = -jnp.exp(alog_ref[...]) * _softplus(a + dtb_ref[...])
    g_ref[0] = jnp.where(lane < 2 * GDN_V_HEADS, gdec, jax.nn.sigmoid(a))


def gdn_prep(p_main, gates_raw, conv_w, alog_row, dtb_row, n_lat_tiles):
    b, s, _ = p_main.shape
    tm = ROW_TILE
    nt = s // tm
    nch = 2 * GDN_KEY_DIM + GDN_VAL_DIM
    hpt = tm // HALO
    nh = s // HALO
    kern = functools.partial(_gdn_prep_kernel, n_lat_tiles=n_lat_tiles, n_tiles=nt, cn=512)
    head_out = lambda nheads: pl.BlockSpec((1, nheads, tm, GDN_HEAD_DIM), lambda bi, i: (bi, 0, i, 0))
    return pl.pallas_call(
        kern,
        out_shape=[jax.ShapeDtypeStruct((b, GDN_K_HEADS, s, GDN_HEAD_DIM), BF16),
                   jax.ShapeDtypeStruct((b, GDN_K_HEADS, s, GDN_HEAD_DIM), BF16),
                   jax.ShapeDtypeStruct((b, GDN_V_HEADS, s, GDN_HEAD_DIM), BF16),
                   jax.ShapeDtypeStruct((b, s, LANES), F32)],
        grid=(b, nt),
        in_specs=[
            pl.BlockSpec((1, tm, nch), lambda bi, i: (bi, i, 0)),
            pl.BlockSpec((1, HALO, nch), lambda bi, i: (bi, jnp.maximum(i * hpt - 1, 0), 0)),
            pl.BlockSpec((1, HALO, nch), lambda bi, i: (bi, jnp.minimum((i + 1) * hpt, nh - 1), 0)),
            pl.BlockSpec((1, tm, LANES), lambda bi, i: (bi, i, 0)),
            pl.BlockSpec((GDN_CONV_W, nch), lambda bi, i: (0, 0)),
            pl.BlockSpec((1, LANES), lambda bi, i: (0, 0)),
            pl.BlockSpec((1, LANES), lambda bi, i: (0, 0)),
        ],
        out_specs=[head_out(GDN_K_HEADS), head_out(GDN_K_HEADS), head_out(GDN_V_HEADS),
                   pl.BlockSpec((1, tm, LANES), lambda bi, i: (bi, i, 0))],
        scratch_shapes=[pltpu.VMEM((tm + 2 * HALO, 512), F32)],
        compiler_params=_cparams(("arbitrary", "arbitrary")),
        name="gdn_prep",
    )(p_main, p_main, p_main, gates_raw, conv_w, alog_row, dtb_row)


def _chunk_masks(n, reverse):
    r = lax.broadcasted_iota(jnp.int32, (n, n), 0)
    c = lax.broadcasted_iota(jnp.int32, (n, n), 1)
    if reverse:
        return r <= c, r < c
    return r >= c, r > c


def _block_cumsum_mats(ct, reverse):
    r = lax.broadcasted_iota(jnp.int32, (ct, ct), 0)
    c = lax.broadcasted_iota(jnp.int32, (ct, ct), 1)
    same = (r // CHUNK) == (c // CHUNK)
    lower = jnp.logical_and(same, c <= r)
    upper = jnp.logical_and(same, c >= r)
    lo = jnp.where(lower, 1.0, 0.0).astype(BF16)
    up = jnp.where(upper, 1.0, 0.0).astype(BF16)
    return (up, lo) if reverse else (lo, up)


def _cumsum_cols(mat, x):
    x0, x1, x2 = _split3(x)
    return _dot(mat, x0) + (_dot(mat, x1) + _dot(mat, x2))


def _cumsum_rows(x, mat):
    x0, x1, x2 = _split3(x)
    return _dot(x0, mat) + (_dot(x1, mat) + _dot(x2, mat))


def _gdn_block(dirs):
    dh = GDN_HEAD_DIM
    eye = jnp.where(lax.broadcasted_iota(jnp.int32, (CHUNK, CHUNK), 0)
                    == lax.broadcasted_iota(jnp.int32, (CHUNK, CHUNK), 1), 1.0, 0.0).astype(F32)
    chains = []
    for q_ref, k_ref, v_ref, gc_ref, gr_ref, o_ref, s_sc, reverse in dirs:
        ct = q_ref.shape[2]
        goff = 2 if reverse else 0
        gcol = gc_ref[0, 0]
        grow = gr_ref[0, 0]
        m_col, m_row = _block_cumsum_mats(ct, reverse)
        gcum_c = _cumsum_cols(m_col, gcol)
        gcum_r = _cumsum_rows(grow, m_row)
        tril, strict = _chunk_masks(CHUNK, reverse)
        for c in range(ct // CHUNK):
            rs = slice(c * CHUNK, (c + 1) * CHUNK)
            last = c * CHUNK if reverse else (c + 1) * CHUNK - 1
            q = q_ref[0, 0, rs, :]
            k = k_ref[0, 0, rs, :]
            kk = _dot_nt(k, k)
            qk = _dot_nt(q, k)
            for hh in range(2):
                gi = goff + hh
                chains.append(dict(
                    q=q, k=k, kk=kk, qk=qk, rs=rs, c=c, hh=hh, reverse=reverse, tril=tril, strict=strict,
                    v_ref=v_ref, o_ref=o_ref, s_sc=s_sc,
                    gc=gcum_c[rs, gi:gi + 1], gr=gcum_r[gi:gi + 1, rs],
                    glast=gcum_c[last:last + 1, gi:gi + 1], beta=gcol[rs, 4 + gi:5 + gi]))
    for ch in chains:
        tril = ch["tril"]
        ch["decay"] = jnp.where(tril, jnp.exp(jnp.where(tril, ch["gc"] - ch["gr"], 0.0)), 0.0)
        a = -jnp.where(ch["strict"], ch["kk"] * ch["beta"] * ch["decay"], 0.0)
        ch["tmat"] = eye + a
        ch["pw"] = a
    for _ in range(5):
        for ch in chains:
            pwb = ch["pw"].astype(BF16)
            ch["pw"] = _dot(pwb, pwb)
        for ch in chains:
            ch["tmat"] = ch["tmat"] + _dot(ch["tmat"].astype(BF16), ch["pw"].astype(BF16))
    for ch in chains:
        beta = ch["beta"]
        eg = jnp.exp(ch["gc"])
        vb = ch["v_ref"][0, ch["hh"], ch["rs"], :].astype(F32) * beta
        kbg = ch["k"].astype(F32) * (beta * eg)
        uw = _dot(ch["tmat"].astype(BF16), jnp.concatenate([vb, kbg], axis=-1).astype(BF16))
        ch["u"] = uw[:, :dh]
        ch["w"] = uw[:, dh:].astype(BF16)
        ch["eg"] = eg
        ch["attn"] = jnp.where(ch["tril"], ch["qk"] * ch["decay"], 0.0).astype(BF16)
    nchunk = max(ch["c"] for ch in chains) + 1
    for step in range(nchunk):
        cur = [ch for ch in chains if ch["c"] == (nchunk - 1 - step if ch["reverse"] else step)]
        for ch in cur:
            state = ch["s_sc"][ch["hh"]]
            sb = state.astype(BF16)
            ch["state"] = state
            ch["ws"] = _dot(ch["w"], sb)
            ch["qs"] = _dot(ch["q"], sb)
        for ch in cur:
            v_new = ch["u"] - ch["ws"]
            ch["kgv"] = (v_new * jnp.exp(ch["glast"] - ch["gc"])).astype(BF16)
            ch["o"] = ch["eg"] * ch["qs"] + _dot(ch["attn"], v_new.astype(BF16))
        for ch in cur:
            ch["s_sc"][ch["hh"]] = ch["state"] * jnp.exp(ch["glast"]) + _dot_tn(ch["k"], ch["kgv"])
            ch["o_ref"][0, ch["rs"], ch["hh"] * dh:(ch["hh"] + 1) * dh] = ch["o"].astype(BF16)


def _gdn_scan_kernel(qf, kf, vf, gcf, grf, qb, kb, vb, gcb, grb, s0f, s0b, *rest, nblk, aliased):
    if aliased:
        rest = rest[2:]
    of_ref, ob_ref, sff, sfb, sf_sc, sb_sc = rest
    j = pl.program_id(2)

    @pl.when(j == 0)
    def _():
        sf_sc[...] = s0f[0]
        sb_sc[...] = s0b[0]

    _gdn_block([(qf, kf, vf, gcf, grf, of_ref, sf_sc, False),
                (qb, kb, vb, gcb, grb, ob_ref, sb_sc, True)])

    @pl.when(j == nblk - 1)
    def _():
        sff[0] = sf_sc[...]
        sfb[0] = sb_sc[...]


def gdn_scan(qn, kn, vv, gcol, grow, s0f, s0b, *, nblk, off, prev=None):
    b, _, s, dh = qn.shape
    ct = SCAN_TILE
    fwd = lambda j: j + off
    bwd = lambda j: nblk - 1 - j + off
    def specs(pos):
        return [
            pl.BlockSpec((1, 1, ct, dh), lambda bi, h, j: (bi, h, pos(j), 0)),
            pl.BlockSpec((1, 1, ct, dh), lambda bi, h, j: (bi, h, pos(j), 0)),
            pl.BlockSpec((1, 2, ct, dh), lambda bi, h, j: (bi, h, pos(j), 0)),
            pl.BlockSpec((1, 1, ct, 8), lambda bi, h, j: (bi, h, pos(j), 0)),
            pl.BlockSpec((1, 1, 8, ct), lambda bi, h, j: (bi, h, 0, pos(j))),
        ]
    st_spec = pl.BlockSpec((1, 2, dh, dh), lambda bi, h, j: (bi, h, 0, 0))
    in_specs = specs(fwd) + specs(bwd) + [st_spec, st_spec]
    args = [qn, kn, vv, gcol, grow] * 2 + [s0f, s0b]
    aliases = {}
    if prev is not None:
        in_specs += [pl.BlockSpec(memory_space=pl.ANY)] * 2
        args += list(prev)
        aliases = {12: 0, 13: 1}
    o_shape = jax.ShapeDtypeStruct((b, s, GDN_VAL_DIM), BF16)
    st_shape = jax.ShapeDtypeStruct((b, GDN_V_HEADS, dh, dh), F32)
    kern = functools.partial(_gdn_scan_kernel, nblk=nblk, aliased=prev is not None)
    return pl.pallas_call(
        kern,
        out_shape=[o_shape, o_shape, st_shape, st_shape],
        grid=(b, GDN_K_HEADS, nblk),
        in_specs=in_specs,
        out_specs=[pl.BlockSpec((1, ct, 2 * dh), lambda bi, h, j: (bi, fwd(j), h)),
                   pl.BlockSpec((1, ct, 2 * dh), lambda bi, h, j: (bi, bwd(j), h)),
                   st_spec, st_spec],
        scratch_shapes=[pltpu.VMEM((2, dh, dh), F32), pltpu.VMEM((2, dh, dh), F32)],
        input_output_aliases=aliases,
        compiler_params=_cparams(("arbitrary", "arbitrary", "arbitrary")),
        name="gdn_scan",
    )(*args)


def _gla_block(dirs, wg_ref, gb_ref):
    chains = []
    pre = []
    for q_ref, k_ref, v_ref, gr_ref, o_ref, st_sc, z in dirs:
        pre.append(_dot3(gr_ref[0], wg_ref[z]) + gb_ref[z:z + 1, :])
    for (q_ref, k_ref, v_ref, gr_ref, o_ref, st_sc, z), logit in zip(dirs, pre):
        reverse = z == 1
        ct = q_ref.shape[1]
        glog = -_softplus(-logit) / GLA_TAU
        m_col, _ = _block_cumsum_mats(ct, reverse)
        bcum = _cumsum_cols(m_col, glog)
        tril, _ = _chunk_masks(CHUNK, reverse)
        for c in range(ct // CHUNK):
            rs = slice(c * CHUNK, (c + 1) * CHUNK)
            last = c * CHUNK if reverse else (c + 1) * CHUNK - 1
            bc = bcum[rs, :]
            bl = bcum[last:last + 1, :]
            qf = q_ref[0, rs, :].astype(F32) * (GLA_DK ** -0.5)
            kf = k_ref[0, rs, :].astype(F32)
            chains.append(dict(
                c=c, rs=rs, reverse=reverse, tril=tril, o_ref=o_ref, st_sc=st_sc, v=v_ref[0, rs, :],
                qe=(qf * jnp.exp(bc)).astype(BF16), ke=(kf * jnp.exp(-bc)).astype(BF16),
                kg=(kf * jnp.exp(bl - bc)).astype(BF16), gl=jnp.exp(bl)))
    for ch in chains:
        ch["attn"] = jnp.where(ch["tril"], _dot_nt(ch["qe"], ch["ke"]), 0.0).astype(BF16)
    for ch in chains:
        ch["o"] = _dot(ch["attn"], ch["v"])
        ch["kv"] = _dot_tn(ch["v"], ch["kg"])
    nchunk = max(ch["c"] for ch in chains) + 1
    for step in range(nchunk):
        cur = [ch for ch in chains if ch["c"] == (nchunk - 1 - step if ch["reverse"] else step)]
        for ch in cur:
            st = ch["st_sc"][...]
            ch["o"] = ch["o"] + _dot_nt(ch["qe"], st.astype(BF16))
            ch["st_sc"][...] = st * ch["gl"] + ch["kv"]
        for ch in cur:
            ch["o_ref"][0, ch["rs"], :] = ch["o"].astype(BF16)


def _gla_scan_kernel(qf, kf, vf, grf, qb, kb, vb, grb, wg, gb, s0f, s0b, *rest, nblk, aliased):
    if aliased:
        rest = rest[2:]
    of_ref, ob_ref, sff, sfb, sf_sc, sb_sc = rest
    j = pl.program_id(2)

    @pl.when(j == 0)
    def _():
        sf_sc[...] = s0f[0, 0]
        sb_sc[...] = s0b[0, 0]

    _gla_block([(qf, kf, vf, grf, of_ref, sf_sc, 0), (qb, kb, vb, grb, ob_ref, sb_sc, 1)], wg, gb)

    @pl.when(j == nblk - 1)
    def _():
        sff[0, 0] = sf_sc[...]
        sfb[0, 0] = sb_sc[...]


def gla_scan(p_main, gr, wg_pad, gate_b, s0f, s0b, *, nblk, off, prev=None):
    b, s, _ = p_main.shape
    ct = SCAN_TILE
    nh, dk, dv = GLA_HEADS, GLA_DK, GLA_DV
    fwd = lambda j: j + off
    bwd = lambda j: nblk - 1 - j + off
    def specs(pos):
        return [
            pl.BlockSpec((1, ct, dk), lambda bi, h, j: (bi, pos(j), h)),
            pl.BlockSpec((1, ct, dk), lambda bi, h, j: (bi, pos(j), nh + h)),
            pl.BlockSpec((1, ct, dv), lambda bi, h, j: (bi, pos(j), nh + h)),
            pl.BlockSpec((1, ct, LANES), lambda bi, h, j: (bi, pos(j), 0)),
        ]
    st_spec = pl.BlockSpec((1, 1, dv, dk), lambda bi, h, j: (bi, h, 0, 0))
    in_specs = specs(fwd) + specs(bwd) + [
        pl.BlockSpec((2, LANES, dk), lambda bi, h, j: (0, 0, h)),
        pl.BlockSpec((2, dk), lambda bi, h, j: (0, h)),
        st_spec, st_spec]
    args = [p_main, p_main, p_main, gr] * 2 + [wg_pad, gate_b, s0f, s0b]
    aliases = {}
    if prev is not None:
        in_specs += [pl.BlockSpec(memory_space=pl.ANY)] * 2
        args += list(prev)
        aliases = {12: 0, 13: 1}
    o_shape = jax.ShapeDtypeStruct((b, s, GLA_VAL_DIM), BF16)
    st_shape = jax.ShapeDtypeStruct((b, nh, dv, dk), F32)
    kern = functools.partial(_gla_scan_kernel, nblk=nblk, aliased=prev is not None)
    return pl.pallas_call(
        kern,
        out_shape=[o_shape, o_shape, st_shape, st_shape],
        grid=(b, nh, nblk),
        in_specs=in_specs,
        out_specs=[pl.BlockSpec((1, ct, dv), lambda bi, h, j: (bi, fwd(j), h)),
                   pl.BlockSpec((1, ct, dv), lambda bi, h, j: (bi, bwd(j), h)),
                   st_spec, st_spec],
        scratch_shapes=[pltpu.VMEM((dv, dk), F32), pltpu.VMEM((dv, dk), F32)],
        input_output_aliases=aliases,
        compiler_params=_cparams(("arbitrary", "arbitrary", "arbitrary")),
        name="gla_scan",
    )(*args)


def _router_kernel(x_ref, sc_ref, sh_ref, rw_ref, rb_ref, h_ref, wd_ref):
    h = x_ref[0] * (1.0 + sc_ref[0, 0]) + sh_ref[0, 0]
    h_ref[0] = h.astype(BF16)
    scores = jax.nn.sigmoid(_dot3(h, rw_ref[...]))
    lane = lax.broadcasted_iota(jnp.int32, scores.shape, 1)
    neg = jnp.float32(-jnp.inf)
    sel = jnp.where(lane < N_EXPERTS, scores + rb_ref[...], neg)
    chosen = jnp.zeros(scores.shape, jnp.bool_)
    for _ in range(TOP_K):
        mx = jnp.max(sel, axis=-1, keepdims=True)
        first = jnp.min(jnp.where(sel == mx, lane, LANES), axis=-1, keepdims=True)
        pick = lane == first
        chosen = jnp.logical_or(chosen, pick)
        sel = jnp.where(pick, neg, sel)
    picked = jnp.where(chosen, scores, 0.0)
    wt = picked / jnp.sum(picked, axis=-1, keepdims=True) * ROUTE_SCALE
    wd_ref[0] = jnp.where(lane == N_EXPERTS, 1.0, wt)


def moe_route(x, sc, sh, rw_pad, rb_pad, n_lat_tiles):
    b, s, d = x.shape
    tm = ROW_TILE
    mod_spec = pl.BlockSpec((1, 1, 1, d), lambda bi, i: (bi, i // n_lat_tiles, 0, 0))
    return pl.pallas_call(
        _router_kernel,
        out_shape=[jax.ShapeDtypeStruct((b, s, d), BF16), jax.ShapeDtypeStruct((b, s, LANES), F32)],
        grid=(b, s // tm),
        in_specs=[pl.BlockSpec((1, tm, d), lambda bi, i: (bi, i, 0)), mod_spec, mod_spec,
                  pl.BlockSpec((d, LANES), lambda bi, i: (0, 0)),
                  pl.BlockSpec((1, LANES), lambda bi, i: (0, 0))],
        out_specs=[pl.BlockSpec((1, tm, d), lambda bi, i: (bi, i, 0)),
                   pl.BlockSpec((1, tm, LANES), lambda bi, i: (bi, i, 0))],
        compiler_params=_cparams(("arbitrary", "arbitrary")),
        name="moe_router",
    )(x, sc, sh, rw_pad, rb_pad)


def _moe_kernel(h_ref, wd_ref, wgu_ref, wdn_ref, x_ref, gate_ref, lg_ref, lb_ref, out_ref, acc, *,
                n_exp, n_lat_rows):
    e = pl.program_id(2)

    @pl.when(e == 0)
    def _():
        acc[...] = jnp.zeros(acc.shape, F32)

    gu = _dot(h_ref[0], wgu_ref[0])
    act = _silu(gu[:, :EXPERT_FF]) * gu[:, EXPERT_FF:]
    wd = wd_ref[0]
    lane = lax.broadcasted_iota(jnp.int32, wd.shape, 1)
    wcol = jnp.sum(jnp.where(lane == e, wd, 0.0), axis=-1, keepdims=True)
    acc[...] += wcol * _dot(act.astype(BF16), wdn_ref[0])

    @pl.when(e == n_exp - 1)
    def _():
        tm = acc.shape[0]
        row = pl.program_id(1) * tm + lax.broadcasted_iota(jnp.int32, (tm, 1), 0)
        gate = jnp.where(row >= n_lat_rows, gate_ref[0, 1], gate_ref[0, 0])
        out_ref[0] = _post_norm(x_ref[0], acc[...], gate, lg_ref[...], lb_ref[...])


def moe_experts_post_norm(h, wd, wgu, wdn, x, gate, lg, lb, n_lat_rows, tm):
    b, s, d = x.shape
    n_exp = wgu.shape[0]
    row = lambda width: pl.BlockSpec((1, tm, width), lambda bi, i, e: (bi, i, 0))
    kern = functools.partial(_moe_kernel, n_exp=n_exp, n_lat_rows=n_lat_rows)
    return pl.pallas_call(
        kern,
        out_shape=jax.ShapeDtypeStruct((b, s, d), F32),
        grid=(b, s // tm, n_exp),
        in_specs=[row(d), row(LANES),
                  pl.BlockSpec((1, d, 2 * EXPERT_FF), lambda bi, i, e: (e, 0, 0)),
                  pl.BlockSpec((1, EXPERT_FF, d), lambda bi, i, e: (e, 0, 0)),
                  row(d),
                  pl.BlockSpec((1, 2, 1, d), lambda bi, i, e: (bi, 0, 0, 0)),
                  pl.BlockSpec((1, d), lambda bi, i, e: (0, 0)),
                  pl.BlockSpec((1, d), lambda bi, i, e: (0, 0))],
        out_specs=row(d),
        scratch_shapes=[pltpu.VMEM((tm, d), F32)],
        compiler_params=_cparams(("arbitrary", "arbitrary", "arbitrary")),
        name="moe_experts_post_norm",
    )(h, wd, wgu, wdn, x, gate, lg.reshape(1, d), lb.reshape(1, d))


MOE_SUB = 256
MOE_CAP = 48
MOE_GRP = 4


def _moe_grouped_kernel(h_ref, wd_ref, wgu_ref, wdn_ref, wsgu_ref, wsdn_ref, f_ref,
                        acc, rcm, rrm, wsplit, cmax):
    g = pl.program_id(2)
    n_grp = pl.num_programs(2)
    tm = acc.shape[0]
    nsub = tm // MOE_SUB
    sub, cap, nslot = MOE_SUB, MOE_CAP, MOE_GRP * MOE_CAP
    lane = lax.broadcasted_iota(jnp.int32, (1, LANES), 1)

    @pl.when(g == 0)
    def _():
        r = lax.broadcasted_iota(jnp.int32, (sub, sub), 0)
        c = lax.broadcasted_iota(jnp.int32, (sub, sub), 1)
        before = jnp.where(c < r, 1.0, 0.0).astype(BF16)
        after = jnp.where(r < c, 1.0, 0.0).astype(BF16)
        ident = jnp.where(r == c, 1.0, 0.0).astype(BF16)
        cm = jnp.zeros((1, LANES), F32)
        for u in range(nsub):
            rs = slice(u * sub, (u + 1) * sub)
            hu = h_ref[0, rs, :]
            gu = _dot(hu, wsgu_ref[...])
            act = _silu(gu[:, :EXPERT_FF]) * gu[:, EXPERT_FF:]
            acc[rs, :] = _dot(act.astype(BF16), wsdn_ref[...])
            wd = wd_ref[0, rs, :]
            active = jnp.logical_and(wd != 0.0, lane < N_EXPERTS)
            a = jnp.where(active, 1.0, 0.0)
            ab = a.astype(BF16)
            rank_c = _dot(before, ab)
            rank_r = _dot_tn(ab, after)
            a_r = _dot_tn(ab, ident)
            rcm[u] = jnp.where(active, rank_c, -1.0).astype(BF16)
            rrm[u] = jnp.where(a_r > 0.5, rank_r, -1.0).astype(BF16)
            hi = wd.astype(BF16)
            wsplit[0, rs, :] = hi
            wsplit[1, rs, :] = (wd - hi.astype(F32)).astype(BF16)
            cm = jnp.maximum(cm, jnp.sum(a, axis=0, keepdims=True))
        cmax[...] = jnp.broadcast_to(cm, cmax.shape)

    e0 = g * MOE_GRP
    in_group = jnp.logical_and(lane >= e0, lane < e0 + MOE_GRP)
    n_max = jnp.max(jnp.where(in_group, cmax[0:1, :], 0.0))
    n_pass = (n_max.astype(jnp.int32) + (cap - 1)) // cap

    def slot_expert(l):
        return sum(((l >= k * cap).astype(jnp.int32) for k in range(1, MOE_GRP)), jnp.zeros_like(l))

    e_i = lax.broadcasted_iota(jnp.int32, (LANES, sub), 0)
    l_i = lax.broadcasted_iota(jnp.int32, (LANES, sub), 1)
    expand = jnp.where(jnp.logical_and(e_i == e0 + slot_expert(l_i), l_i < nslot), 1.0, 0.0).astype(BF16)
    l_t = lax.broadcasted_iota(jnp.int32, (sub, LANES), 0)
    e_t = lax.broadcasted_iota(jnp.int32, (sub, LANES), 1)
    expand_t = jnp.where(jnp.logical_and(e_t == e0 + slot_expert(l_t), l_t < nslot), 1.0, 0.0).astype(BF16)
    l_row = lax.broadcasted_iota(jnp.int32, (1, sub), 1)
    j_row = jnp.where(l_row < nslot, l_row - cap * slot_expert(l_row), -1000).astype(F32)
    l_col = lax.broadcasted_iota(jnp.int32, (sub, 1), 0)
    j_col = jnp.where(l_col < nslot, l_col - cap * slot_expert(l_col), -1000).astype(F32)

    def one_pass(p, carry):
        base = (p * cap).astype(F32)
        xg, wg = [], []
        for u in range(nsub):
            rs = slice(u * sub, (u + 1) * sub)
            rank_of_slot = _dot(expand_t, rrm[u])
            gather = jnp.where(rank_of_slot == j_col + base, 1.0, 0.0).astype(BF16)
            xg.append(_dot(gather, h_ref[0, rs, :]).astype(BF16))
            wg.append(_dot(gather, wsplit[0, rs, :]) + _dot(gather, wsplit[1, rs, :]))
        ys = []
        for ei in range(MOE_GRP):
            es = slice(ei * cap, (ei + 1) * cap)
            x_e = jnp.concatenate([xg[u][es] for u in range(nsub)], axis=0)
            gu = _dot(x_e, wgu_ref[ei])
            act = _silu(gu[:, :EXPERT_FF]) * gu[:, EXPERT_FF:]
            y = _dot(act.astype(BF16), wdn_ref[ei])
            w_e = jnp.concatenate([wg[u][es] for u in range(nsub)], axis=0)
            wcol = jnp.sum(jnp.where(lane == e0 + ei, w_e, 0.0), axis=-1, keepdims=True)
            ys.append((y * wcol).astype(BF16))
        pad = jnp.zeros((sub - nslot, ys[0].shape[1]), BF16)
        for u in range(nsub):
            rs = slice(u * sub, (u + 1) * sub)
            y_u = jnp.concatenate([ys[ei][u * cap:(u + 1) * cap] for ei in range(MOE_GRP)] + [pad], axis=0)
            slot_of_row = _dot(rcm[u], expand)
            scatter = jnp.where(slot_of_row == j_row + base, 1.0, 0.0).astype(BF16)
            acc[rs, :] += _dot(scatter, y_u)
        return carry

    lax.fori_loop(0, n_pass, one_pass, 0)

    @pl.when(g == n_grp - 1)
    def _():
        f_ref[0] = acc[...]


def moe_grouped_experts(h, wd, wgu, wdn, wsgu, wsdn, tm):
    b, s, d = h.shape
    n_grp = wgu.shape[0] // MOE_GRP
    row = lambda width: pl.BlockSpec((1, tm, width), lambda bi, i, g: (bi, i, 0))
    return pl.pallas_call(
        _moe_grouped_kernel,
        out_shape=jax.ShapeDtypeStruct((b, s, d), F32),
        grid=(b, s // tm, n_grp),
        in_specs=[row(d), row(LANES),
                  pl.BlockSpec((MOE_GRP, d, 2 * EXPERT_FF), lambda bi, i, g: (g, 0, 0)),
                  pl.BlockSpec((MOE_GRP, EXPERT_FF, d), lambda bi, i, g: (g, 0, 0)),
                  pl.BlockSpec((d, 2 * EXPERT_FF), lambda bi, i, g: (0, 0)),
                  pl.BlockSpec((EXPERT_FF, d), lambda bi, i, g: (0, 0))],
        out_specs=row(d),
        scratch_shapes=[pltpu.VMEM((tm, d), F32),
                        pltpu.VMEM((tm // MOE_SUB, MOE_SUB, LANES), BF16),
                        pltpu.VMEM((tm // MOE_SUB, LANES, MOE_SUB), BF16),
                        pltpu.VMEM((2, tm, LANES), BF16),
                        pltpu.VMEM((8, LANES), F32)],
        compiler_params=_cparams(("arbitrary", "arbitrary", "arbitrary")),
        name="moe_grouped_experts",
    )(h, wd, wgu, wdn, wsgu, wsdn)


def _post_norm_kernel(x_ref, f_ref, gate_ref, lg_ref, lb_ref, out_ref):
    out_ref[0] = _post_norm(x_ref[0], f_ref[0].astype(F32), gate_ref[0, 0], lg_ref[...], lb_ref[...])


def post_norm_rows(x, f, gate, lg, lb, n_lat_tiles):
    b, s, d = x.shape
    tm = ROW_TILE
    row = pl.BlockSpec((1, tm, d), lambda bi, i: (bi, i, 0))
    vec = pl.BlockSpec((1, d), lambda bi, i: (0, 0))
    return pl.pallas_call(
        _post_norm_kernel, out_shape=jax.ShapeDtypeStruct((b, s, d), F32), grid=(b, s // tm),
        in_specs=[row, row, pl.BlockSpec((1, 1, 1, d), lambda bi, i: (bi, i // n_lat_tiles, 0, 0)), vec, vec],
        out_specs=row, compiler_params=_cparams(("arbitrary", "arbitrary")), name="post_norm_rows",
    )(x, f, gate, lg.reshape(1, d), lb.reshape(1, d))


def _rope_tables(n_lat, n_ctx):
    rows = n_lat // GRID_W
    rowp = jnp.repeat(jnp.arange(rows), GRID_W).astype(F32)
    colp = jnp.tile(jnp.arange(GRID_W), rows).astype(F32)
    n_freq = DA_HEAD_DIM // 4
    inv = 1.0 / (ROPE_BASE ** (jnp.arange(n_freq, dtype=F32) / n_freq))
    ang = jnp.concatenate([rowp[:, None] * inv, colp[:, None] * inv], -1)
    cos, sin = jnp.cos(ang), jnp.sin(ang)
    cos_t = jnp.tile(cos, (1, 4))
    sin_t = jnp.tile(jnp.concatenate([-sin, sin], -1), (1, 2))
    cos_t = jnp.concatenate([cos_t, jnp.ones((n_ctx, LANES), F32)], 0)
    sin_t = jnp.concatenate([sin_t, jnp.zeros((n_ctx, LANES), F32)], 0)
    return cos_t, sin_t


def _pick_tile(total, cands):
    for t in cands:
        if total % t == 0:
            return t
    raise ValueError(f"no tile for {total}")


def _pad_cols(w, n):
    return jnp.pad(w, ((0, 0), (0, n - w.shape[1])))


def _flash_both(p_all, lam_vec, subln, lam_init, n_lat, n_ctx):
    s = n_lat + n_ctx
    tq = _pick_tile(n_lat, (512, 256))
    tk = _pick_tile(s, (1280, 1024, 512, 256))
    o = diff_flash_attention(p_all, lam_vec, subln, lam_init, tq=tq, tk=tk, nq=n_lat // tq, nk=s // tk,
                             q_off=0, k_off=0)
    return diff_flash_attention(p_all, lam_vec, subln, lam_init, tq=n_ctx, tk=n_ctx, nq=1, nk=1,
                                q_off=n_lat // n_ctx, k_off=n_lat // n_ctx, prev=o)


def kernel(x, c, ctx, c_ctx, ada_w, ada_b, ln_g, ln_b, da_w_in, da_w_o, da_lambda, da_subln, gdn_w_in, gdn_conv, gdn_a_log, gdn_dt_bias, gdn_norm, gdn_w_o, gla_w_in, gla_w_gate, gla_gate_b, gla_norm, gla_w_o, moe_router, moe_router_b, moe_w_gu, moe_w_dn, moe_ws_gu, moe_ws_dn):
    b, n, d = x.shape
    lc = ctx.shape[1]
    assert lc == ROW_TILE and n % SCAN_TILE == 0 and d == D_MODEL
    s = n + lc
    n_lat_tiles = n // ROW_TILE
    depth = ada_w.shape[0]

    xall = jnp.concatenate([x, ctx], axis=1)
    c8 = jnp.concatenate([c, c_ctx[None], jnp.zeros((8 - b - 1, d), F32)], 0)
    mods = ada_modulation(c8, ada_w, ada_b)
    rope = _rope_tables(n, lc)
    moe_tm = _pick_tile(s, (1280, 1024, 512, 256))

    for i in range(depth):
        kind, j = i % N_MIXERS, i // N_MIXERS
        m = mods[i].reshape(8, ADA_CHUNKS, d)
        mod = jnp.stack([m[:b], jnp.broadcast_to(m[b], (b, ADA_CHUNKS, d))], axis=1)
        mod = [mod[:, :, k][:, :, None, :] for k in range(ADA_CHUNKS)]

        if kind == 0:
            lam_init = 0.8 - 0.6 * math.exp(-0.3 * i)
            w = da_w_in[j]
            w = jnp.concatenate([w[:, :d] * (DA_HEAD_DIM ** -0.5 * math.log2(math.e)), w[:, d:]], 1).astype(BF16)
            p_all = modulated_projection(xall, mod[1], mod[0], w, rope=rope, n_rope=2 * d,
                                         n_lat_tiles=n_lat_tiles)
            o = _flash_both(p_all, da_lambda[j], da_subln[j], lam_init, n, lc)
            xall = out_projection_post_norm([o], da_w_o[j].astype(BF16), xall, mod[2],
                                            ln_g[i, 0], ln_b[i, 0], n_lat_tiles)
        elif kind == 1:
            w = gdn_w_in[j]
            n_main = 2 * GDN_KEY_DIM + 2 * GDN_VAL_DIM
            p_main, gates_raw = modulated_projection(
                xall, mod[1], mod[0], w[:, :n_main].astype(BF16),
                w_small=_pad_cols(w[:, n_main:], LANES).astype(BF16), n_lat_tiles=n_lat_tiles)
            alog_row = _pad_cols(gdn_a_log[j].reshape(1, -1), LANES)
            dtb_row = _pad_cols(gdn_dt_bias[j].reshape(1, -1), LANES)
            qn, kn, vv, gates = gdn_prep(p_main, gates_raw, gdn_conv[j], alog_row, dtb_row, n_lat_tiles)
            hv, hk = GDN_V_HEADS, GDN_K_HEADS
            def per_khead(t):
                t = t.reshape(b, s, 2, hk, 2)
                return jnp.transpose(t, (0, 3, 1, 2, 4)).reshape(b, hk, s, 4)
            gcol = jnp.concatenate([per_khead(gates[..., :2 * hv]), per_khead(gates[..., 2 * hv:4 * hv])], -1)
            grow = jnp.swapaxes(gcol, 2, 3)
            zeros = jnp.zeros((b, hv, GDN_HEAD_DIM, GDN_HEAD_DIM), F32)
            of, ob, scf, scb = gdn_scan(qn, kn, vv, gcol, grow, zeros, zeros, nblk=lc // SCAN_TILE,
                                        off=n // SCAN_TILE)
            of, ob, _, _ = gdn_scan(qn, kn, vv, gcol, grow, scf, scb, nblk=n // SCAN_TILE, off=0,
                                    prev=(of, ob))
            xall = out_projection_post_norm([of, ob, p_main, gdn_norm[j]], gdn_w_o[j].astype(BF16), xall,
                                            mod[2], ln_g[i, 0], ln_b[i, 0], n_lat_tiles,
                                            gated=(GDN_V_HEADS, GDN_HEAD_DIM, 2))
        else:
            w = gla_w_in[j]
            n_main = 2 * GLA_KEY_DIM + 2 * GLA_VAL_DIM
            p_main, gr = modulated_projection(
                xall, mod[1], mod[0], w[:, :n_main].astype(BF16),
                w_small=_pad_cols(w[:, n_main:], LANES).astype(BF16), n_lat_tiles=n_lat_tiles)
            wg = gla_w_gate[j]
            wg_pad = jnp.zeros((2, LANES, GLA_KEY_DIM), F32)
            wg_pad = wg_pad.at[0, :GLA_GATE_RANK].set(wg[0]).at[1, GLA_GATE_RANK:2 * GLA_GATE_RANK].set(wg[1])
            zeros = jnp.zeros((b, GLA_HEADS, GLA_DV, GLA_DK), F32)
            of, ob, scf, scb = gla_scan(p_main, gr, wg_pad, gla_gate_b[j], zeros, zeros,
                                        nblk=lc // SCAN_TILE, off=n // SCAN_TILE)
            of, ob, _, _ = gla_scan(p_main, gr, wg_pad, gla_gate_b[j], scf, scb, nblk=n // SCAN_TILE, off=0,
                                    prev=(of, ob))
            xall = out_projection_post_norm([of, ob, p_main, gla_norm[j]], gla_w_o[j].astype(BF16), xall,
                                            mod[2], ln_g[i, 0], ln_b[i, 0], n_lat_tiles,
                                            gated=(GLA_HEADS, GLA_DV, 2))

        rw_pad = _pad_cols(moe_router[i], LANES)
        rb_pad = _pad_cols(moe_router_b[i].reshape(1, -1), LANES)
        h, wd = moe_route(xall, mod[4], mod[3], rw_pad, rb_pad, n_lat_tiles)
        f = moe_grouped_experts(h, wd, moe_w_gu[i].astype(BF16), moe_w_dn[i].astype(BF16),
                                moe_ws_gu[i].astype(BF16), moe_ws_dn[i].astype(BF16), moe_tm)
        xall = post_norm_rows(xall, f, mod[5], ln_g[i, 1], ln_b[i, 1], n_lat_tiles)

    return xall[:, :n]
```

```python
import functools
import math

import jax
import jax.numpy as jnp
from jax import lax
from jax.experimental import pallas as pl
from jax.experimental.pallas import tpu as pltpu

F32 = jnp.float32
BF16 = jnp.bfloat16

D_MODEL = 1024
DEPTH = 4
GRID_W = 64
N_MIXERS = 3
DN_ALPHA = (2 * DEPTH) ** 0.25
LN_EPS = 1e-5
NORM_EPS = 1e-6
ADA_CHUNKS = 6

DA_HEADS = 8
DA_HEAD_DIM = 64
ROPE_BASE = 10000.0

GDN_K_HEADS = 8
GDN_V_HEADS = 16
GDN_HEAD_DIM = 128
GDN_KEY_DIM = 1024
GDN_VAL_DIM = 2048
GDN_CONV_W = 5
CHUNK = 64

GLA_HEADS = 4
GLA_KEY_DIM = 512
GLA_VAL_DIM = 1024
GLA_DK = 128
GLA_DV = 256
GLA_GATE_RANK = 16
GLA_TAU = 16.0

N_EXPERTS = 64
TOP_K = 8
EXPERT_FF = 256
ROUTE_SCALE = 2.5

LANES = 128
ROW_TILE = 256
SCAN_TILE = 256
HALO = 16
VMEM_LIMIT = 56 * 1024 * 1024


def _cparams(sem, flags=None):
    return pltpu.CompilerParams(dimension_semantics=sem, vmem_limit_bytes=VMEM_LIMIT, flags=flags)


def _dot(a, b):
    return jnp.dot(a, b, preferred_element_type=F32)


def _dot_nt(a, b):
    return lax.dot_general(a, b, (((1,), (1,)), ((), ())), preferred_element_type=F32)


def _dot_tn(a, b):
    return lax.dot_general(a, b, (((0,), (0,)), ((), ())), preferred_element_type=F32)


def _split2(a):
    hi = a.astype(BF16)
    lo = (a - hi.astype(F32)).astype(BF16)
    return hi, lo


def _split3(a):
    a0 = a.astype(BF16)
    r = a - a0.astype(F32)
    a1 = r.astype(BF16)
    a2 = (r - a1.astype(F32)).astype(BF16)
    return a0, a1, a2


def _dot3(a, b):
    a0, a1 = _split2(a)
    b0, b1 = _split2(b)
    return _dot(a0, b0) + (_dot(a0, b1) + _dot(a1, b0))


def _silu(x):
    return x * jax.nn.sigmoid(x)


def _softplus(x):
    return jnp.maximum(x, 0.0) + jnp.log(1.0 + jnp.exp(-jnp.abs(x)))


def _ada_kernel(c_ref, w_ref, b_ref, o_ref):
    s = _silu(c_ref[...])
    o_ref[0] = _dot3(s, w_ref[0]) + b_ref[0]


def ada_modulation(c8, ada_w, ada_b):
    depth, d, n = ada_w.shape
    tn = 1536
    return pl.pallas_call(
        _ada_kernel,
        out_shape=jax.ShapeDtypeStruct((depth, 8, n), F32),
        grid=(depth, n // tn),
        in_specs=[
            pl.BlockSpec((8, d), lambda i, j: (0, 0)),
            pl.BlockSpec((1, d, tn), lambda i, j: (i, 0, j)),
            pl.BlockSpec((1, 1, tn), lambda i, j: (i, 0, j)),
        ],
        out_specs=pl.BlockSpec((1, 8, tn), lambda i, j: (i, 0, j)),
        compiler_params=_cparams(("arbitrary", "arbitrary")),
        name="ada_modulation",
    )(c8, ada_w, ada_b.reshape(depth, 1, n))


def _proj_kernel(*refs, n_main, n_rope, has_small, cn):
    it = iter(refs)
    x_ref, sc_ref, sh_ref, w_ref = next(it), next(it), next(it), next(it)
    ws_ref = next(it) if has_small else None
    cos_ref = next(it) if n_rope else None
    sin_ref = next(it) if n_rope else None
    o_ref = next(it)
    os_ref = next(it) if has_small else None

    h = x_ref[0] * (1.0 + sc_ref[0, 0]) + sh_ref[0, 0]
    hb = h.astype(BF16)
    tm = hb.shape[0]
    if n_rope:
        cos = cos_ref[...]
        sin = sin_ref[...]
        lane = lax.broadcasted_iota(jnp.int32, (tm, LANES), 1)
        low_half = (lane & 32) == 0
    for j in range(n_main // cn):
        p = _dot(hb, w_ref[:, j * cn:(j + 1) * cn])
        for g in range(cn // LANES):
            pg = p[:, g * LANES:(g + 1) * LANES]
            col = j * cn + g * LANES
            if col < n_rope:
                partner = jnp.where(low_half, pltpu.roll(pg, LANES - 32, 1), pltpu.roll(pg, 32, 1))
                pg = pg * cos + partner * sin
            o_ref[0, :, col:col + LANES] = pg.astype(BF16)
    if has_small:
        os_ref[0] = _dot(hb, ws_ref[...])


def modulated_projection(x, sc, sh, w, w_small=None, rope=None, n_rope=0, n_lat_tiles=1):
    b, s, d = x.shape
    n_main = w.shape[1]
    tm = ROW_TILE
    grid = (b, s // tm)
    mod_spec = pl.BlockSpec((1, 1, 1, d), lambda bi, i: (bi, i // n_lat_tiles, 0, 0))
    in_specs = [pl.BlockSpec((1, tm, d), lambda bi, i: (bi, i, 0)), mod_spec, mod_spec,
                pl.BlockSpec((d, n_main), lambda bi, i: (0, 0))]
    args = [x, sc, sh, w]
    out_shape = [jax.ShapeDtypeStruct((b, s, n_main), BF16)]
    out_specs = [pl.BlockSpec((1, tm, n_main), lambda bi, i: (bi, i, 0))]
    if w_small is not None:
        in_specs.append(pl.BlockSpec((d, LANES), lambda bi, i: (0, 0)))
        args.append(w_small)
        out_shape.append(jax.ShapeDtypeStruct((b, s, LANES), F32))
        out_specs.append(pl.BlockSpec((1, tm, LANES), lambda bi, i: (bi, i, 0)))
    if n_rope:
        tab = pl.BlockSpec((tm, LANES), lambda bi, i: (i, 0))
        in_specs += [tab, tab]
        args += [rope[0], rope[1]]
    kern = functools.partial(_proj_kernel, n_main=n_main, n_rope=n_rope,
                             has_small=w_small is not None, cn=512)
    out = pl.pallas_call(
        kern, out_shape=out_shape, grid=grid, in_specs=in_specs, out_specs=out_specs,
        compiler_params=_cparams(("arbitrary", "arbitrary")), name="modulated_projection",
    )(*args)
    return out if w_small is not None else out[0]


NEG_INIT = -1e30


def _flash_kernel(lam_ref, q_ref, k_ref, v_ref, sub_ref, *rest, lam_init, nk, aliased):
    if aliased:
        rest = rest[1:]
    o_ref, s_a, s_b, mt_a, mt_b, m_sc, l_sc, acc_sc = rest
    ki = pl.program_id(3)
    tq, tk = s_a.shape[1], s_a.shape[2]

    @pl.when(ki == 0)
    def _():
        m_sc[...] = jnp.full(m_sc.shape, NEG_INIT, F32)
        l_sc[...] = jnp.zeros(l_sc.shape, F32)
        acc_sc[...] = jnp.zeros(acc_sc.shape, F32)
        s_b[...] = jnp.full(s_b.shape, NEG_INIT, F32)
        mt_b[...] = jnp.full(mt_b.shape, NEG_INIT, F32)

    def step(s_w, mt_w, s_r, mt_r):
        q = q_ref[0]
        k = k_ref[0]
        v = v_ref[0]
        hd = DA_HEAD_DIM
        for c in range(2):
            s = _dot_nt(q[:, c * hd:(c + 1) * hd], k[:, c * hd:(c + 1) * hd])
            s_w[c] = s
            mt_w[c] = jnp.broadcast_to(jnp.max(s, axis=-1, keepdims=True), (tq, LANES))
            m_prev = m_sc[c]
            m_new = jnp.maximum(m_prev, mt_r[c])
            alpha = jnp.exp2(m_prev - m_new)
            lsum = None
            pv = None
            kc = 2 * LANES
            for j in range(tk // kc):
                pj = jnp.exp2(s_r[c, :, j * kc:(j + 1) * kc] - jnp.concatenate([m_new, m_new], axis=-1))
                lj = pj[:, :LANES] + pj[:, LANES:]
                lsum = lj if lsum is None else lsum + lj
                t = _dot(pj.astype(BF16), v[j * kc:(j + 1) * kc, :])
                pv = t if pv is None else pv + t
            l_sc[c] = alpha * l_sc[c] + lsum
            acc_sc[c] = alpha * acc_sc[c] + pv
            m_sc[c] = m_new

    @pl.when(ki % 2 == 0)
    def _():
        step(s_a, mt_a, s_b, mt_b)

    @pl.when(ki % 2 == 1)
    def _():
        step(s_b, mt_b, s_a, mt_a)

    @pl.when(ki == nk)
    def _():
        lv = lam_ref[...]
        lam = (jnp.exp(jnp.sum(lv[0:1] * lv[1:2], axis=-1, keepdims=True))
               - jnp.exp(jnp.sum(lv[2:3] * lv[3:4], axis=-1, keepdims=True)) + lam_init)
        l0 = jnp.sum(l_sc[0], axis=-1, keepdims=True)
        l1 = jnp.sum(l_sc[1], axis=-1, keepdims=True)
        o = acc_sc[0] / l0 - lam * (acc_sc[1] / l1)
        ms = jnp.mean(o * o, axis=-1, keepdims=True)
        o = o * lax.rsqrt(ms + NORM_EPS) * sub_ref[...] * (1.0 - lam_init)
        o_ref[0] = o.astype(BF16)


def diff_flash_attention(p_all, lam_vec, subln, lam_init, *, tq, tk, nq, nk, q_off, k_off, prev=None):
    b, s, _ = p_all.shape
    hh = DA_HEADS
    in_specs = [
        pl.BlockSpec((4, DA_HEAD_DIM), lambda bi, h, qi, ki: (0, 0)),
        pl.BlockSpec((1, tq, LANES), lambda bi, h, qi, ki: (bi, qi + q_off, h)),
        pl.BlockSpec((1, tk, LANES), lambda bi, h, qi, ki: (bi, jnp.minimum(ki, nk - 1) + k_off, hh + h)),
        pl.BlockSpec((1, tk, LANES), lambda bi, h, qi, ki: (bi, jnp.maximum(ki - 1, 0) + k_off, 2 * hh + h)),
        pl.BlockSpec((1, LANES), lambda bi, h, qi, ki: (0, 0)),
    ]
    args = [lam_vec, p_all, p_all, p_all, subln.reshape(1, LANES)]
    aliases = {}
    if prev is not None:
        in_specs.append(pl.BlockSpec(memory_space=pl.ANY))
        args.append(prev)
        aliases = {5: 0}
    kern = functools.partial(_flash_kernel, lam_init=lam_init, nk=nk, aliased=prev is not None)
    return pl.pallas_call(
        kern,
        out_shape=jax.ShapeDtypeStruct((b, s, hh * LANES), BF16),
        grid=(b, hh, nq, nk + 1),
        in_specs=in_specs,
        out_specs=pl.BlockSpec((1, tq, LANES), lambda bi, h, qi, ki: (bi, qi + q_off, h)),
        scratch_shapes=[pltpu.VMEM((2, tq, tk), F32), pltpu.VMEM((2, tq, tk), F32),
                        pltpu.VMEM((2, tq, LANES), F32), pltpu.VMEM((2, tq, LANES), F32),
                        pltpu.VMEM((2, tq, LANES), F32), pltpu.VMEM((2, tq, LANES), F32),
                        pltpu.VMEM((2, tq, LANES), F32)],
        input_output_aliases=aliases,
        compiler_params=_cparams(("arbitrary", "arbitrary", "arbitrary", "arbitrary")),
        name="diff_flash_attention",
    )(*args)


def _post_norm(x, y, gate, lg, lb):
    r = DN_ALPHA * x + gate * y
    mu = jnp.mean(r, axis=-1, keepdims=True)
    rc = r - mu
    var = jnp.mean(rc * rc, axis=-1, keepdims=True)
    return rc * lax.rsqrt(var + LN_EPS) * lg + lb


def _outproj_kernel(*refs, gated, n_heads, dh):
    if gated:
        of_ref, ob_ref, z_ref, ng_ref, w_ref, x_ref, gate_ref, lg_ref, lb_ref, out_ref = refs
        y = None
        for h in range(n_heads):
            sl = slice(h * dh, (h + 1) * dh)
            o = of_ref[0, :, sl].astype(F32) + ob_ref[0, :, sl].astype(F32)
            ms = jnp.mean(o * o, axis=-1, keepdims=True)
            o = o * lax.rsqrt(ms + NORM_EPS) * ng_ref[...] * _silu(z_ref[0, :, sl].astype(F32))
            t = _dot(o.astype(BF16), w_ref[sl, :])
            y = t if y is None else y + t
    else:
        o_ref, w_ref, x_ref, gate_ref, lg_ref, lb_ref, out_ref = refs
        y = _dot(o_ref[0], w_ref[...])
    out_ref[0] = _post_norm(x_ref[0], y, gate_ref[0, 0], lg_ref[...], lb_ref[...])


def out_projection_post_norm(o_args, w_o, x, gate, lg, lb, n_lat_tiles, gated=None):
    b, s, d = x.shape
    kdim = w_o.shape[0]
    tm = ROW_TILE
    row = lambda width, cb=0: pl.BlockSpec((1, tm, width), lambda bi, i: (bi, i, cb))
    full = lambda shape: pl.BlockSpec(shape, lambda bi, i: (0,) * len(shape))
    if gated is None:
        in_specs = [row(kdim)]
        args = list(o_args)
        kern = functools.partial(_outproj_kernel, gated=False, n_heads=0, dh=0)
    else:
        n_heads, dh, z_cb = gated
        of, ob, z, ng = o_args
        in_specs = [row(kdim), row(kdim), row(kdim, z_cb), full((1, dh))]
        args = [of, ob, z, ng.reshape(1, dh)]
        kern = functools.partial(_outproj_kernel, gated=True, n_heads=n_heads, dh=dh)
    in_specs += [full((kdim, d)), row(d),
                 pl.BlockSpec((1, 1, 1, d), lambda bi, i: (bi, i // n_lat_tiles, 0, 0)),
                 full((1, d)), full((1, d))]
    args += [w_o, x, gate, lg.reshape(1, d), lb.reshape(1, d)]
    return pl.pallas_call(
        kern, out_shape=jax.ShapeDtypeStruct((b, s, d), F32), grid=(b, s // tm),
        in_specs=in_specs, out_specs=row(d),
        compiler_params=_cparams(("arbitrary", "arbitrary")), name="out_projection_post_norm",
    )(*args)


def _gdn_prep_kernel(cur_ref, prev_ref, next_ref, gate_ref, cw_ref, alog_ref, dtb_ref,
                     q_ref, k_ref, v_ref, g_ref, ext, *, n_lat_tiles, n_tiles, cn):
    i = pl.program_id(1)
    tm = cur_ref.shape[1]
    first = jnp.logical_or(i == 0, i == n_lat_tiles)
    last = jnp.logical_or(i == n_lat_tiles - 1, i == n_tiles - 1)
    pmask = jnp.where(first, 0.0, 1.0)
    nmask = jnp.where(last, 0.0, 1.0)
    pad = GDN_CONV_W // 2
    dh = GDN_HEAD_DIM
    n_qk = 2 * GDN_KEY_DIM
    for cc in range(cur_ref.shape[2] // cn):
        cs = slice(cc * cn, (cc + 1) * cn)
        ext[0:HALO, :] = prev_ref[0, :, cs].astype(F32) * pmask
        ext[HALO:HALO + tm, :] = cur_ref[0, :, cs].astype(F32)
        ext[HALO + tm:2 * HALO + tm, :] = next_ref[0, :, cs].astype(F32) * nmask
        acc = None
        for j in range(GDN_CONV_W):
            t = ext[pl.ds(HALO - pad + j, tm), :] * cw_ref[j:j + 1, cs]
            acc = t if acc is None else acc + t
        y = _silu(acc)
        for g in range(cn // dh):
            col = cc * cn + g * dh
            yg = y[:, g * dh:(g + 1) * dh]
            if col < n_qk:
                yg = yg * lax.rsqrt(jnp.sum(yg * yg, axis=-1, keepdims=True) + NORM_EPS)
                if col < GDN_KEY_DIM:
                    q_ref[0, col // dh] = (yg * (dh ** -0.5)).astype(BF16)
                else:
                    k_ref[0, (col - GDN_KEY_DIM) // dh] = yg.astype(BF16)
            else:
                v_ref[0, (col - n_qk) // dh] = yg.astype(BF16)
    a = gate_ref[0]
    lane = lax.broadcasted_iota(jnp.int32, a.shape, 1)
    gdec = -jnp.exp(alog_ref[...]) * _softplus(a + dtb_ref[...])
    g_ref[0] = jnp.where(lane < 2 * GDN_V_HEADS, gdec, jax.nn.sigmoid(a))


def gdn_prep(p_main, gates_raw, conv_w, alog_row, dtb_row, n_lat_tiles):
    b, s, _ = p_main.shape
    tm = ROW_TILE
    nt = s // tm
    nch = 2 * GDN_KEY_DIM + GDN_VAL_DIM
    hpt = tm // HALO
    nh = s // HALO
    kern = functools.partial(_gdn_prep_kernel, n_lat_tiles=n_lat_tiles, n_tiles=nt, cn=512)
    head_out = lambda nheads: pl.BlockSpec((1, nheads, tm, GDN_HEAD_DIM), lambda bi, i: (bi, 0, i, 0))
    return pl.pallas_call(
        kern,
        out_shape=[jax.ShapeDtypeStruct((b, GDN_K_HEADS, s, GDN_HEAD_DIM), BF16),
                   jax.ShapeDtypeStruct((b, GDN_K_HEADS, s, GDN_HEAD_DIM), BF16),
                   jax.ShapeDtypeStruct((b, GDN_V_HEADS, s, GDN_HEAD_DIM), BF16),
                   jax.ShapeDtypeStruct((b, s, LANES), F32)],
        grid=(b, nt),
        in_specs=[
            pl.BlockSpec((1, tm, nch), lambda bi, i: (bi, i, 0)),
            pl.BlockSpec((1, HALO, nch), lambda bi, i: (bi, jnp.maximum(i * hpt - 1, 0), 0)),
            pl.BlockSpec((1, HALO, nch), lambda bi, i: (bi, jnp.minimum((i + 1) * hpt, nh - 1), 0)),
            pl.BlockSpec((1, tm, LANES), lambda bi, i: (bi, i, 0)),
            pl.BlockSpec((GDN_CONV_W, nch), lambda bi, i: (0, 0)),
            pl.BlockSpec((1, LANES), lambda bi, i: (0, 0)),
            pl.BlockSpec((1, LANES), lambda bi, i: (0, 0)),
        ],
        out_specs=[head_out(GDN_K_HEADS), head_out(GDN_K_HEADS), head_out(GDN_V_HEADS),
                   pl.BlockSpec((1, tm, LANES), lambda bi, i: (bi, i, 0))],
        scratch_shapes=[pltpu.VMEM((tm + 2 * HALO, 512), F32)],
        compiler_params=_cparams(("arbitrary", "arbitrary")),
        name="gdn_prep",
    )(p_main, p_main, p_main, gates_raw, conv_w, alog_row, dtb_row)


def _chunk_masks(n, reverse):
    r = lax.broadcasted_iota(jnp.int32, (n, n), 0)
    c = lax.broadcasted_iota(jnp.int32, (n, n), 1)
    if reverse:
        return r <= c, r < c
    return r >= c, r > c


def _block_cumsum_mats(ct, reverse):
    r = lax.broadcasted_iota(jnp.int32, (ct, ct), 0)
    c = lax.broadcasted_iota(jnp.int32, (ct, ct), 1)
    same = (r // CHUNK) == (c // CHUNK)
    lower = jnp.logical_and(same, c <= r)
    upper = jnp.logical_and(same, c >= r)
    lo = jnp.where(lower, 1.0, 0.0).astype(BF16)
    up = jnp.where(upper, 1.0, 0.0).astype(BF16)
    return (up, lo) if reverse else (lo, up)


def _cumsum_cols(mat, x):
    x0, x1, x2 = _split3(x)
    return _dot(mat, x0) + (_dot(mat, x1) + _dot(mat, x2))


def _cumsum_rows(x, mat):
    x0, x1, x2 = _split3(x)
    return _dot(x0, mat) + (_dot(x1, mat) + _dot(x2, mat))


def _gdn_block(dirs):
    dh = GDN_HEAD_DIM
    eye = jnp.where(lax.broadcasted_iota(jnp.int32, (CHUNK, CHUNK), 0)
                    == lax.broadcasted_iota(jnp.int32, (CHUNK, CHUNK), 1), 1.0, 0.0).astype(F32)
    chains = []
    for q_ref, k_ref, v_ref, gc_ref, gr_ref, o_ref, s_sc, reverse in dirs:
        ct = q_ref.shape[2]
        goff = 2 if reverse else 0
        gcol = gc_ref[0, 0]
        grow = gr_ref[0, 0]
        m_col, m_row = _block_cumsum_mats(ct, reverse)
        gcum_c = _cumsum_cols(m_col, gcol)
        gcum_r = _cumsum_rows(grow, m_row)
        tril, strict = _chunk_masks(CHUNK, reverse)
        for c in range(ct // CHUNK):
            rs = slice(c * CHUNK, (c + 1) * CHUNK)
            last = c * CHUNK if reverse else (c + 1) * CHUNK - 1
            q = q_ref[0, 0, rs, :]
            k = k_ref[0, 0, rs, :]
            kk = _dot_nt(k, k)
            qk = _dot_nt(q, k)
            for hh in range(2):
                gi = goff + hh
                chains.append(dict(
                    q=q, k=k, kk=kk, qk=qk, rs=rs, c=c, hh=hh, reverse=reverse, tril=tril, strict=strict,
                    v_ref=v_ref, o_ref=o_ref, s_sc=s_sc,
                    gc=gcum_c[rs, gi:gi + 1], gr=gcum_r[gi:gi + 1, rs],
                    glast=gcum_c[last:last + 1, gi:gi + 1], beta=gcol[rs, 4 + gi:5 + gi]))
    for ch in chains:
        tril = ch["tril"]
        ch["decay"] = jnp.where(tril, jnp.exp(jnp.where(tril, ch["gc"] - ch["gr"], 0.0)), 0.0)
        a = -jnp.where(ch["strict"], ch["kk"] * ch["beta"] * ch["decay"], 0.0)
        ch["tmat"] = eye + a
        ch["pw"] = a
    for _ in range(5):
        for ch in chains:
            pwb = ch["pw"].astype(BF16)
            ch["pw"] = _dot(pwb, pwb)
        for ch in chains:
            ch["tmat"] = ch["tmat"] + _dot(ch["tmat"].astype(BF16), ch["pw"].astype(BF16))
    for ch in chains:
        beta = ch["beta"]
        eg = jnp.exp(ch["gc"])
        vb = ch["v_ref"][0, ch["hh"], ch["rs"], :].astype(F32) * beta
        kbg = ch["k"].astype(F32) * (beta * eg)
        uw = _dot(ch["tmat"].astype(BF16), jnp.concatenate([vb, kbg], axis=-1).astype(BF16))
        ch["u"] = uw[:, :dh]
        ch["w"] = uw[:, dh:].astype(BF16)
        ch["eg"] = eg
        ch["attn"] = jnp.where(ch["tril"], ch["qk"] * ch["decay"], 0.0).astype(BF16)
    nchunk = max(ch["c"] for ch in chains) + 1
    for step in range(nchunk):
        cur = [ch for ch in chains if ch["c"] == (nchunk - 1 - step if ch["reverse"] else step)]
        for ch in cur:
            state = ch["s_sc"][ch["hh"]]
            sb = state.astype(BF16)
            ch["state"] = state
            ch["ws"] = _dot(ch["w"], sb)
            ch["qs"] = _dot(ch["q"], sb)
        for ch in cur:
            v_new = ch["u"] - ch["ws"]
            ch["kgv"] = (v_new * jnp.exp(ch["glast"] - ch["gc"])).astype(BF16)
            ch["o"] = ch["eg"] * ch["qs"] + _dot(ch["attn"], v_new.astype(BF16))
        for ch in cur:
            ch["s_sc"][ch["hh"]] = ch["state"] * jnp.exp(ch["glast"]) + _dot_tn(ch["k"], ch["kgv"])
            ch["o_ref"][0, ch["rs"], ch["hh"] * dh:(ch["hh"] + 1) * dh] = ch["o"].astype(BF16)


def _gdn_scan_kernel(qf, kf, vf, gcf, grf, qb, kb, vb, gcb, grb, s0f, s0b, *rest, nblk, aliased):
    if aliased:
        rest = rest[2:]
    of_ref, ob_ref, sff, sfb, sf_sc, sb_sc = rest
    j = pl.program_id(2)

    @pl.when(j == 0)
    def _():
        sf_sc[...] = s0f[0]
        sb_sc[...] = s0b[0]

    _gdn_block([(qf, kf, vf, gcf, grf, of_ref, sf_sc, False),
                (qb, kb, vb, gcb, grb, ob_ref, sb_sc, True)])

    @pl.when(j == nblk - 1)
    def _():
        sff[0] = sf_sc[...]
        sfb[0] = sb_sc[...]


def gdn_scan(qn, kn, vv, gcol, grow, s0f, s0b, *, nblk, off, prev=None):
    b, _, s, dh = qn.shape
    ct = SCAN_TILE
    fwd = lambda j: j + off
    bwd = lambda j: nblk - 1 - j + off
    def specs(pos):
        return [
            pl.BlockSpec((1, 1, ct, dh), lambda bi, h, j: (bi, h, pos(j), 0)),
            pl.BlockSpec((1, 1, ct, dh), lambda bi, h, j: (bi, h, pos(j), 0)),
            pl.BlockSpec((1, 2, ct, dh), lambda bi, h, j: (bi, h, pos(j), 0)),
            pl.BlockSpec((1, 1, ct, 8), lambda bi, h, j: (bi, h, pos(j), 0)),
            pl.BlockSpec((1, 1, 8, ct), lambda bi, h, j: (bi, h, 0, pos(j))),
        ]
    st_spec = pl.BlockSpec((1, 2, dh, dh), lambda bi, h, j: (bi, h, 0, 0))
    in_specs = specs(fwd) + specs(bwd) + [st_spec, st_spec]
    args = [qn, kn, vv, gcol, grow] * 2 + [s0f, s0b]
    aliases = {}
    if prev is not None:
        in_specs += [pl.BlockSpec(memory_space=pl.ANY)] * 2
        args += list(prev)
        aliases = {12: 0, 13: 1}
    o_shape = jax.ShapeDtypeStruct((b, s, GDN_VAL_DIM), BF16)
    st_shape = jax.ShapeDtypeStruct((b, GDN_V_HEADS, dh, dh), F32)
    kern = functools.partial(_gdn_scan_kernel, nblk=nblk, aliased=prev is not None)
    return pl.pallas_call(
        kern,
        out_shape=[o_shape, o_shape, st_shape, st_shape],
        grid=(b, GDN_K_HEADS, nblk),
        in_specs=in_specs,
        out_specs=[pl.BlockSpec((1, ct, 2 * dh), lambda bi, h, j: (bi, fwd(j), h)),
                   pl.BlockSpec((1, ct, 2 * dh), lambda bi, h, j: (bi, bwd(j), h)),
                   st_spec, st_spec],
        scratch_shapes=[pltpu.VMEM((2, dh, dh), F32), pltpu.VMEM((2, dh, dh), F32)],
        input_output_aliases=aliases,
        compiler_params=_cparams(("arbitrary", "arbitrary", "arbitrary")),
        name="gdn_scan",
    )(*args)


def _gla_block(dirs, wg_ref, gb_ref):
    chains = []
    pre = []
    for q_ref, k_ref, v_ref, gr_ref, o_ref, st_sc, z in dirs:
        pre.append(_dot3(gr_ref[0], wg_ref[z]) + gb_ref[z:z + 1, :])
    for (q_ref, k_ref, v_ref, gr_ref, o_ref, st_sc, z), logit in zip(dirs, pre):
        reverse = z == 1
        ct = q_ref.shape[1]
        glog = -_softplus(-logit) / GLA_TAU
        m_col, _ = _block_cumsum_mats(ct, reverse)
        bcum = _cumsum_cols(m_col, glog)
        tril, _ = _chunk_masks(CHUNK, reverse)
        for c in range(ct // CHUNK):
            rs = slice(c * CHUNK, (c + 1) * CHUNK)
            last = c * CHUNK if reverse else (c + 1) * CHUNK - 1
            bc = bcum[rs, :]
            bl = bcum[last:last + 1, :]
            qf = q_ref[0, rs, :].astype(F32) * (GLA_DK ** -0.5)
            kf = k_ref[0, rs, :].astype(F32)
            chains.append(dict(
                c=c, rs=rs, reverse=reverse, tril=tril, o_ref=o_ref, st_sc=st_sc, v=v_ref[0, rs, :],
                qe=(qf * jnp.exp(bc)).astype(BF16), ke=(kf * jnp.exp(-bc)).astype(BF16),
                kg=(kf * jnp.exp(bl - bc)).astype(BF16), gl=jnp.exp(bl)))
    for ch in chains:
        ch["attn"] = jnp.where(ch["tril"], _dot_nt(ch["qe"], ch["ke"]), 0.0).astype(BF16)
    for ch in chains:
        ch["o"] = _dot(ch["attn"], ch["v"])
        ch["kv"] = _dot_tn(ch["v"], ch["kg"])
    nchunk = max(ch["c"] for ch in chains) + 1
    for step in range(nchunk):
        cur = [ch for ch in chains if ch["c"] == (nchunk - 1 - step if ch["reverse"] else step)]
        for ch in cur:
            st = ch["st_sc"][...]
            ch["o"] = ch["o"] + _dot_nt(ch["qe"], st.astype(BF16))
            ch["st_sc"][...] = st * ch["gl"] + ch["kv"]
        for ch in cur:
            ch["o_ref"][0, ch["rs"], :] = ch["o"].astype(BF16)


def _gla_scan_kernel(qf, kf, vf, grf, qb, kb, vb, grb, wg, gb, s0f, s0b, *rest, nblk, aliased):
    if aliased:
        rest = rest[2:]
    of_ref, ob_ref, sff, sfb, sf_sc, sb_sc = rest
    j = pl.program_id(2)

    @pl.when(j == 0)
    def _():
        sf_sc[...] = s0f[0, 0]
        sb_sc[...] = s0b[0, 0]

    _gla_block([(qf, kf, vf, grf, of_ref, sf_sc, 0), (qb, kb, vb, grb, ob_ref, sb_sc, 1)], wg, gb)

    @pl.when(j == nblk - 1)
    def _():
        sff[0, 0] = sf_sc[...]
        sfb[0, 0] = sb_sc[...]


def gla_scan(p_main, gr, wg_pad, gate_b, s0f, s0b, *, nblk, off, prev=None):
    b, s, _ = p_main.shape
    ct = SCAN_TILE
    nh, dk, dv = GLA_HEADS, GLA_DK, GLA_DV
    fwd = lambda j: j + off
    bwd = lambda j: nblk - 1 - j + off
    def specs(pos):
        return [
            pl.BlockSpec((1, ct, dk), lambda bi, h, j: (bi, pos(j), h)),
            pl.BlockSpec((1, ct, dk), lambda bi, h, j: (bi, pos(j), nh + h)),
            pl.BlockSpec((1, ct, dv), lambda bi, h, j: (bi, pos(j), nh + h)),
            pl.BlockSpec((1, ct, LANES), lambda bi, h, j: (bi, pos(j), 0)),
        ]
    st_spec = pl.BlockSpec((1, 1, dv, dk), lambda bi, h, j: (bi, h, 0, 0))
    in_specs = specs(fwd) + specs(bwd) + [
        pl.BlockSpec((2, LANES, dk), lambda bi, h, j: (0, 0, h)),
        pl.BlockSpec((2, dk), lambda bi, h, j: (0, h)),
        st_spec, st_spec]
    args = [p_main, p_main, p_main, gr] * 2 + [wg_pad, gate_b, s0f, s0b]
    aliases = {}
    if prev is not None:
        in_specs += [pl.BlockSpec(memory_space=pl.ANY)] * 2
        args += list(prev)
        aliases = {12: 0, 13: 1}
    o_shape = jax.ShapeDtypeStruct((b, s, GLA_VAL_DIM), BF16)
    st_shape = jax.ShapeDtypeStruct((b, nh, dv, dk), F32)
    kern = functools.partial(_gla_scan_kernel, nblk=nblk, aliased=prev is not None)
    return pl.pallas_call(
        kern,
        out_shape=[o_shape, o_shape, st_shape, st_shape],
        grid=(b, nh, nblk),
        in_specs=in_specs,
        out_specs=[pl.BlockSpec((1, ct, dv), lambda bi, h, j: (bi, fwd(j), h)),
                   pl.BlockSpec((1, ct, dv), lambda bi, h, j: (bi, bwd(j), h)),
                   st_spec, st_spec],
        scratch_shapes=[pltpu.VMEM((dv, dk), F32), pltpu.VMEM((dv, dk), F32)],
        input_output_aliases=aliases,
        compiler_params=_cparams(("arbitrary", "arbitrary", "arbitrary")),
        name="gla_scan",
    )(*args)


def _router_kernel(x_ref, sc_ref, sh_ref, rw_ref, rb_ref, h_ref, wd_ref):
    h = x_ref[0] * (1.0 + sc_ref[0, 0]) + sh_ref[0, 0]
    h_ref[0] = h.astype(BF16)
    scores = jax.nn.sigmoid(_dot3(h, rw_ref[...]))
    lane = lax.broadcasted_iota(jnp.int32, scores.shape, 1)
    neg = jnp.float32(-jnp.inf)
    sel = jnp.where(lane < N_EXPERTS, scores + rb_ref[...], neg)
    chosen = jnp.zeros(scores.shape, jnp.bool_)
    for _ in range(TOP_K):
        mx = jnp.max(sel, axis=-1, keepdims=True)
        first = jnp.min(jnp.where(sel == mx, lane, LANES), axis=-1, keepdims=True)
        pick = lane == first
        chosen = jnp.logical_or(chosen, pick)
        sel = jnp.where(pick, neg, sel)
    picked = jnp.where(chosen, scores, 0.0)
    wt = picked / jnp.sum(picked, axis=-1, keepdims=True) * ROUTE_SCALE
    wd_ref[0] = jnp.where(lane == N_EXPERTS, 1.0, wt)


def moe_route(x, sc, sh, rw_pad, rb_pad, n_lat_tiles):
    b, s, d = x.shape
    tm = ROW_TILE
    mod_spec = pl.BlockSpec((1, 1, 1, d), lambda bi, i: (bi, i // n_lat_tiles, 0, 0))
    return pl.pallas_call(
        _router_kernel,
        out_shape=[jax.ShapeDtypeStruct((b, s, d), BF16), jax.ShapeDtypeStruct((b, s, LANES), F32)],
        grid=(b, s // tm),
        in_specs=[pl.BlockSpec((1, tm, d), lambda bi, i: (bi, i, 0)), mod_spec, mod_spec,
                  pl.BlockSpec((d, LANES), lambda bi, i: (0, 0)),
                  pl.BlockSpec((1, LANES), lambda bi, i: (0, 0))],
        out_specs=[pl.BlockSpec((1, tm, d), lambda bi, i: (bi, i, 0)),
                   pl.BlockSpec((1, tm, LANES), lambda bi, i: (bi, i, 0))],
        compiler_params=_cparams(("arbitrary", "arbitrary")),
        name="moe_router",
    )(x, sc, sh, rw_pad, rb_pad)


def _moe_kernel(h_ref, wd_ref, wgu_ref, wdn_ref, x_ref, gate_ref, lg_ref, lb_ref, out_ref, acc, *,
                n_exp, n_lat_rows):
    e = pl.program_id(2)

    @pl.when(e == 0)
    def _():
        acc[...] = jnp.zeros(acc.shape, F32)

    gu = _dot(h_ref[0], wgu_ref[0])
    act = _silu(gu[:, :EXPERT_FF]) * gu[:, EXPERT_FF:]
    wd = wd_ref[0]
    lane = lax.broadcasted_iota(jnp.int32, wd.shape, 1)
    wcol = jnp.sum(jnp.where(lane == e, wd, 0.0), axis=-1, keepdims=True)
    acc[...] += wcol * _dot(act.astype(BF16), wdn_ref[0])

    @pl.when(e == n_exp - 1)
    def _():
        tm = acc.shape[0]
        row = pl.program_id(1) * tm + lax.broadcasted_iota(jnp.int32, (tm, 1), 0)
        gate = jnp.where(row >= n_lat_rows, gate_ref[0, 1], gate_ref[0, 0])
        out_ref[0] = _post_norm(x_ref[0], acc[...], gate, lg_ref[...], lb_ref[...])


def moe_experts_post_norm(h, wd, wgu, wdn, x, gate, lg, lb, n_lat_rows, tm):
    b, s, d = x.shape
    n_exp = wgu.shape[0]
    row = lambda width: pl.BlockSpec((1, tm, width), lambda bi, i, e: (bi, i, 0))
    kern = functools.partial(_moe_kernel, n_exp=n_exp, n_lat_rows=n_lat_rows)
    return pl.pallas_call(
        kern,
        out_shape=jax.ShapeDtypeStruct((b, s, d), F32),
        grid=(b, s // tm, n_exp),
        in_specs=[row(d), row(LANES),
                  pl.BlockSpec((1, d, 2 * EXPERT_FF), lambda bi, i, e: (e, 0, 0)),
                  pl.BlockSpec((1, EXPERT_FF, d), lambda bi, i, e: (e, 0, 0)),
                  row(d),
                  pl.BlockSpec((1, 2, 1, d), lambda bi, i, e: (bi, 0, 0, 0)),
                  pl.BlockSpec((1, d), lambda bi, i, e: (0, 0)),
                  pl.BlockSpec((1, d), lambda bi, i, e: (0, 0))],
        out_specs=row(d),
        scratch_shapes=[pltpu.VMEM((tm, d), F32)],
        compiler_params=_cparams(("arbitrary", "arbitrary", "arbitrary")),
        name="moe_experts_post_norm",
    )(h, wd, wgu, wdn, x, gate, lg.reshape(1, d), lb.reshape(1, d))


MOE_SUB = 256
MOE_CAP = 64
MOE_GRP = 4


def _moe_grouped_kernel(order_ref, h_ref, wd_ref, *rest):
    wgu_refs = rest[:MOE_GRP]
    wdn_refs = rest[MOE_GRP:2 * MOE_GRP]
    wsgu_ref, wsdn_ref, f_ref, acc, rcm, rrm, wbf, cmax = rest[2 * MOE_GRP:]
    g = pl.program_id(2)
    n_grp = pl.num_programs(2)
    tm = acc.shape[0]
    nsub = tm // MOE_SUB
    sub, cap, nslot = MOE_SUB, MOE_CAP, MOE_GRP * MOE_CAP
    lane = lax.broadcasted_iota(jnp.int32, (1, LANES), 1)

    @pl.when(g == 0)
    def _():
        r = lax.broadcasted_iota(jnp.int32, (sub, sub), 0)
        c = lax.broadcasted_iota(jnp.int32, (sub, sub), 1)
        before = jnp.where(c < r, 1.0, 0.0).astype(BF16)
        after = jnp.where(r < c, 1.0, 0.0).astype(BF16)
        ident = jnp.where(r == c, 1.0, 0.0).astype(BF16)
        cm = jnp.zeros((1, LANES), F32)
        for u in range(nsub):
            rs = slice(u * sub, (u + 1) * sub)
            hu = h_ref[0, rs, :]
            gu = _dot(hu, wsgu_ref[...])
            act = _silu(gu[:, :EXPERT_FF]) * gu[:, EXPERT_FF:]
            acc[rs, :] = _dot(act.astype(BF16), wsdn_ref[...])
            wd = wd_ref[0, rs, :]
            active = jnp.logical_and(wd != 0.0, lane < N_EXPERTS)
            a = jnp.where(active, 1.0, 0.0)
            ab = a.astype(BF16)
            rank_c = _dot(before, ab)
            rank_r = _dot_tn(ab, after)
            a_r = _dot_tn(ab, ident)
            rcm[u] = jnp.where(active, rank_c, -1.0).astype(BF16)
            rrm[u] = jnp.where(a_r > 0.5, rank_r, -1.0).astype(BF16)
            wbf[rs, :] = wd.astype(BF16)
            cm = jnp.maximum(cm, jnp.sum(a, axis=0, keepdims=True))
        cmax[...] = jnp.broadcast_to(cm, cmax.shape)

    experts = [order_ref[g * MOE_GRP + k] for k in range(MOE_GRP)]
    in_group = functools.reduce(jnp.logical_or, [lane == e for e in experts])
    n_max = jnp.max(jnp.where(in_group, cmax[0:1, :], 0.0))
    n_pass = (n_max.astype(jnp.int32) + (cap - 1)) // cap

    def slot_expert(l):
        return sum(((l >= k * cap).astype(jnp.int32) for k in range(1, MOE_GRP)), jnp.zeros_like(l))

    def expert_of_slot(l):
        pos = slot_expert(l)
        e = jnp.full(l.shape, -1, jnp.int32)
        for k in range(MOE_GRP):
            e = jnp.where(pos == k, experts[k], e)
        return jnp.where(l < nslot, e, -1)

    e_i = lax.broadcasted_iota(jnp.int32, (LANES, sub), 0)
    l_i = lax.broadcasted_iota(jnp.int32, (LANES, sub), 1)
    expand = jnp.where(e_i == expert_of_slot(l_i), 1.0, 0.0).astype(BF16)
    l_t = lax.broadcasted_iota(jnp.int32, (sub, LANES), 0)
    e_t = lax.broadcasted_iota(jnp.int32, (sub, LANES), 1)
    expand_t = jnp.where(e_t == expert_of_slot(l_t), 1.0, 0.0).astype(BF16)
    l_row = lax.broadcasted_iota(jnp.int32, (1, sub), 1)
    j_row = jnp.where(l_row < nslot, l_row - cap * slot_expert(l_row), -1000).astype(F32)
    l_col = lax.broadcasted_iota(jnp.int32, (sub, 1), 0)
    j_col = jnp.where(l_col < nslot, l_col - cap * slot_expert(l_col), -1000).astype(F32)

    def one_pass(p, carry):
        base = (p * cap).astype(F32)
        xg = []
        for u in range(nsub):
            rs = slice(u * sub, (u + 1) * sub)
            rank_of_slot = _dot(expand_t, rrm[u])
            gather = jnp.where(rank_of_slot == j_col + base, 1.0, 0.0).astype(BF16)
            xg.append(_dot(gather, h_ref[0, rs, :]).astype(BF16))
        ys = []
        for k in range(MOE_GRP):
            es = slice(k * cap, (k + 1) * cap)
            x_e = jnp.concatenate([xg[u][es] for u in range(nsub)], axis=0)
            gu = _dot(x_e, wgu_refs[k][0])
            act = _silu(gu[:, :EXPERT_FF]) * gu[:, EXPERT_FF:]
            ys.append(_dot(act.astype(BF16), wdn_refs[k][0]).astype(BF16))
        for u in range(nsub):
            rs = slice(u * sub, (u + 1) * sub)
            parts = [ys[k][u * cap:(u + 1) * cap] for k in range(MOE_GRP)]
            if nslot < sub:
                parts.append(jnp.zeros((sub - nslot, parts[0].shape[1]), BF16))
            y_u = jnp.concatenate(parts, axis=0)
            slot_of_row = _dot(rcm[u], expand)
            weight = _dot(wbf[rs, :], expand)
            scatter = jnp.where(slot_of_row == j_row + base, weight, 0.0).astype(BF16)
            acc[rs, :] += _dot(scatter, y_u)
        return carry

    lax.fori_loop(0, n_pass, one_pass, 0)

    @pl.when(g == n_grp - 1)
    def _():
        f_ref[0] = acc[...]


def moe_grouped_experts(order, h, wd, wgu, wdn, wsgu, wsdn, tm):
    b, s, d = h.shape
    n_grp = wgu.shape[0] // MOE_GRP
    row = lambda width: pl.BlockSpec((1, tm, width), lambda bi, i, g, o: (bi, i, 0))

    def expert(shape, k):
        return pl.BlockSpec((1,) + shape, lambda bi, i, g, o: (o[g * MOE_GRP + k], 0, 0))

    in_specs = ([row(d), row(LANES)]
                + [expert((d, 2 * EXPERT_FF), k) for k in range(MOE_GRP)]
                + [expert((EXPERT_FF, d), k) for k in range(MOE_GRP)]
                + [pl.BlockSpec((d, 2 * EXPERT_FF), lambda bi, i, g, o: (0, 0)),
                   pl.BlockSpec((EXPERT_FF, d), lambda bi, i, g, o: (0, 0))])
    grid_spec = pltpu.PrefetchScalarGridSpec(
        num_scalar_prefetch=1, grid=(b, s // tm, n_grp), in_specs=in_specs, out_specs=row(d),
        scratch_shapes=[pltpu.VMEM((tm, d), F32),
                        pltpu.VMEM((tm // MOE_SUB, MOE_SUB, LANES), BF16),
                        pltpu.VMEM((tm // MOE_SUB, LANES, MOE_SUB), BF16),
                        pltpu.VMEM((tm, LANES), BF16),
                        pltpu.VMEM((8, LANES), F32)])
    return pl.pallas_call(
        _moe_grouped_kernel,
        out_shape=jax.ShapeDtypeStruct((b, s, d), F32),
        grid_spec=grid_spec,
        compiler_params=_cparams(("arbitrary", "arbitrary", "arbitrary")),
        name="moe_grouped_experts",
    )(order, h, wd, *([wgu] * MOE_GRP), *([wdn] * MOE_GRP), wsgu, wsdn)


def _post_norm_kernel(x_ref, f_ref, gate_ref, lg_ref, lb_ref, out_ref):
    out_ref[0] = _post_norm(x_ref[0], f_ref[0].astype(F32), gate_ref[0, 0], lg_ref[...], lb_ref[...])


def post_norm_rows(x, f, gate, lg, lb, n_lat_tiles):
    b, s, d = x.shape
    tm = ROW_TILE
    row = pl.BlockSpec((1, tm, d), lambda bi, i: (bi, i, 0))
    vec = pl.BlockSpec((1, d), lambda bi, i: (0, 0))
    return pl.pallas_call(
        _post_norm_kernel, out_shape=jax.ShapeDtypeStruct((b, s, d), F32), grid=(b, s // tm),
        in_specs=[row, row, pl.BlockSpec((1, 1, 1, d), lambda bi, i: (bi, i // n_lat_tiles, 0, 0)), vec, vec],
        out_specs=row, compiler_params=_cparams(("arbitrary", "arbitrary")), name="post_norm_rows",
    )(x, f, gate, lg.reshape(1, d), lb.reshape(1, d))


def _rope_tables(n_lat, n_ctx):
    rows = n_lat // GRID_W
    rowp = jnp.repeat(jnp.arange(rows), GRID_W).astype(F32)
    colp = jnp.tile(jnp.arange(GRID_W), rows).astype(F32)
    n_freq = DA_HEAD_DIM // 4
    inv = 1.0 / (ROPE_BASE ** (jnp.arange(n_freq, dtype=F32) / n_freq))
    ang = jnp.concatenate([rowp[:, None] * inv, colp[:, None] * inv], -1)
    cos, sin = jnp.cos(ang), jnp.sin(ang)
    cos_t = jnp.tile(cos, (1, 4))
    sin_t = jnp.tile(jnp.concatenate([-sin, sin], -1), (1, 2))
    cos_t = jnp.concatenate([cos_t, jnp.ones((n_ctx, LANES), F32)], 0)
    sin_t = jnp.concatenate([sin_t, jnp.zeros((n_ctx, LANES), F32)], 0)
    return cos_t, sin_t


def _pick_tile(total, cands):
    for t in cands:
        if total % t == 0:
            return t
    raise ValueError(f"no tile for {total}")


def _pad_cols(w, n):
    return jnp.pad(w, ((0, 0), (0, n - w.shape[1])))


def _flash_both(p_all, lam_vec, subln, lam_init, n_lat, n_ctx):
    s = n_lat + n_ctx
    tq = _pick_tile(n_lat, (512, 256))
    tk = _pick_tile(s, (1280, 1024, 512, 256))
    o = diff_flash_attention(p_all, lam_vec, subln, lam_init, tq=tq, tk=tk, nq=n_lat // tq, nk=s // tk,
                             q_off=0, k_off=0)
    return diff_flash_attention(p_all, lam_vec, subln, lam_init, tq=n_ctx, tk=n_ctx, nq=1, nk=1,
                                q_off=n_lat // n_ctx, k_off=n_lat // n_ctx, prev=o)


def kernel(x, c, ctx, c_ctx, ada_w, ada_b, ln_g, ln_b, da_w_in, da_w_o, da_lambda, da_subln, gdn_w_in, gdn_conv, gdn_a_log, gdn_dt_bias, gdn_norm, gdn_w_o, gla_w_in, gla_w_gate, gla_gate_b, gla_norm, gla_w_o, moe_router, moe_router_b, moe_w_gu, moe_w_dn, moe_ws_gu, moe_ws_dn):
    b, n, d = x.shape
    lc = ctx.shape[1]
    assert lc == ROW_TILE and n % SCAN_TILE == 0 and d == D_MODEL
    s = n + lc
    n_lat_tiles = n // ROW_TILE
    depth = ada_w.shape[0]

    xall = jnp.concatenate([x, ctx], axis=1)
    c8 = jnp.concatenate([c, c_ctx[None], jnp.zeros((8 - b - 1, d), F32)], 0)
    mods = ada_modulation(c8, ada_w, ada_b)
    rope = _rope_tables(n, lc)
    moe_tm = _pick_tile(s, (1280, 1024, 512, 256))

    for i in range(depth):
        kind, j = i % N_MIXERS, i // N_MIXERS
        m = mods[i].reshape(8, ADA_CHUNKS, d)
        mod = jnp.stack([m[:b], jnp.broadcast_to(m[b], (b, ADA_CHUNKS, d))], axis=1)
        mod = [mod[:, :, k][:, :, None, :] for k in range(ADA_CHUNKS)]

        if kind == 0:
            lam_init = 0.8 - 0.6 * math.exp(-0.3 * i)
            w = da_w_in[j]
            w = jnp.concatenate([w[:, :d] * (DA_HEAD_DIM ** -0.5 * math.log2(math.e)), w[:, d:]], 1).astype(BF16)
            p_all = modulated_projection(xall, mod[1], mod[0], w, rope=rope, n_rope=2 * d,
                                         n_lat_tiles=n_lat_tiles)
            o = _flash_both(p_all, da_lambda[j], da_subln[j], lam_init, n, lc)
            xall = out_projection_post_norm([o], da_w_o[j].astype(BF16), xall, mod[2],
                                            ln_g[i, 0], ln_b[i, 0], n_lat_tiles)
        elif kind == 1:
            w = gdn_w_in[j]
            n_main = 2 * GDN_KEY_DIM + 2 * GDN_VAL_DIM
            p_main, gates_raw = modulated_projection(
                xall, mod[1], mod[0], w[:, :n_main].astype(BF16),
                w_small=_pad_cols(w[:, n_main:], LANES).astype(BF16), n_lat_tiles=n_lat_tiles)
            alog_row = _pad_cols(gdn_a_log[j].reshape(1, -1), LANES)
            dtb_row = _pad_cols(gdn_dt_bias[j].reshape(1, -1), LANES)
            qn, kn, vv, gates = gdn_prep(p_main, gates_raw, gdn_conv[j], alog_row, dtb_row, n_lat_tiles)
            hv, hk = GDN_V_HEADS, GDN_K_HEADS
            def per_khead(t):
                t = t.reshape(b, s, 2, hk, 2)
                return jnp.transpose(t, (0, 3, 1, 2, 4)).reshape(b, hk, s, 4)
            gcol = jnp.concatenate([per_khead(gates[..., :2 * hv]), per_khead(gates[..., 2 * hv:4 * hv])], -1)
            grow = jnp.swapaxes(gcol, 2, 3)
            zeros = jnp.zeros((b, hv, GDN_HEAD_DIM, GDN_HEAD_DIM), F32)
            of, ob, scf, scb = gdn_scan(qn, kn, vv, gcol, grow, zeros, zeros, nblk=lc // SCAN_TILE,
                                        off=n // SCAN_TILE)
            of, ob, _, _ = gdn_scan(qn, kn, vv, gcol, grow, scf, scb, nblk=n // SCAN_TILE, off=0,
                                    prev=(of, ob))
            xall = out_projection_post_norm([of, ob, p_main, gdn_norm[j]], gdn_w_o[j].astype(BF16), xall,
                                            mod[2], ln_g[i, 0], ln_b[i, 0], n_lat_tiles,
                                            gated=(GDN_V_HEADS, GDN_HEAD_DIM, 2))
        else:
            w = gla_w_in[j]
            n_main = 2 * GLA_KEY_DIM + 2 * GLA_VAL_DIM
            p_main, gr = modulated_projection(
                xall, mod[1], mod[0], w[:, :n_main].astype(BF16),
                w_small=_pad_cols(w[:, n_main:], LANES).astype(BF16), n_lat_tiles=n_lat_tiles)
            wg = gla_w_gate[j]
            wg_pad = jnp.zeros((2, LANES, GLA_KEY_DIM), F32)
            wg_pad = wg_pad.at[0, :GLA_GATE_RANK].set(wg[0]).at[1, GLA_GATE_RANK:2 * GLA_GATE_RANK].set(wg[1])
            zeros = jnp.zeros((b, GLA_HEADS, GLA_DV, GLA_DK), F32)
            of, ob, scf, scb = gla_scan(p_main, gr, wg_pad, gla_gate_b[j], zeros, zeros,
                                        nblk=lc // SCAN_TILE, off=n // SCAN_TILE)
            of, ob, _, _ = gla_scan(p_main, gr, wg_pad, gla_gate_b[j], scf, scb, nblk=n // SCAN_TILE, off=0,
                                    prev=(of, ob))
            xall = out_projection_post_norm([of, ob, p_main, gla_norm[j]], gla_w_o[j].astype(BF16), xall,
                                            mod[2], ln_g[i, 0], ln_b[i, 0], n_lat_tiles,
                                            gated=(GLA_HEADS, GLA_DV, 2))

        rw_pad = _pad_cols(moe_router[i], LANES)
        rb_pad = _pad_cols(moe_router_b[i].reshape(1, -1), LANES)
        h, wd = moe_route(xall, mod[4], mod[3], rw_pad, rb_pad, n_lat_tiles)
        popularity = jnp.sum((wd[..., :N_EXPERTS] != 0.0).astype(jnp.int32), axis=(0, 1))
        order = jnp.argsort(popularity).astype(jnp.int32)
        f = moe_grouped_experts(order, h, wd, moe_w_gu[i].astype(BF16), moe_w_dn[i].astype(BF16),
                                moe_ws_gu[i].astype(BF16), moe_ws_dn[i].astype(BF16), moe_tm)
        xall = post_norm_rows(xall, f, mod[5], ln_g[i, 1], ln_b[i, 1], n_lat_tiles)

    return xall[:, :n]
```

```python
import functools
import math

import jax
import jax.numpy as jnp
from jax import lax
from jax.experimental import pallas as pl
from jax.experimental.pallas import tpu as pltpu

F32 = jnp.float32
BF16 = jnp.bfloat16

D_MODEL = 1024
DEPTH = 4
GRID_W = 64
N_MIXERS = 3
DN_ALPHA = (2 * DEPTH) ** 0.25
LN_EPS = 1e-5
NORM_EPS = 1e-6
ADA_CHUNKS = 6

DA_HEADS = 8
DA_HEAD_DIM = 64
ROPE_BASE = 10000.0

GDN_K_HEADS = 8
GDN_V_HEADS = 16
GDN_HEAD_DIM = 128
GDN_KEY_DIM = 1024
GDN_VAL_DIM = 2048
GDN_CONV_W = 5
CHUNK = 64

GLA_HEADS = 4
GLA_KEY_DIM = 512
GLA_VAL_DIM = 1024
GLA_DK = 128
GLA_DV = 256
GLA_GATE_RANK = 16
GLA_TAU = 16.0

N_EXPERTS = 64
TOP_K = 8
EXPERT_FF = 256
ROUTE_SCALE = 2.5

LANES = 128
ROW_TILE = 256
SCAN_TILE = 256
HALO = 16
VMEM_LIMIT = 56 * 1024 * 1024


def _cparams(sem, flags=None):
    return pltpu.CompilerParams(dimension_semantics=sem, vmem_limit_bytes=VMEM_LIMIT, flags=flags)


def _dot(a, b):
    return jnp.dot(a, b, preferred_element_type=F32)


def _dot_nt(a, b):
    return lax.dot_general(a, b, (((1,), (1,)), ((), ())), preferred_element_type=F32)


def _dot_tn(a, b):
    return lax.dot_general(a, b, (((0,), (0,)), ((), ())), preferred_element_type=F32)


def _split2(a):
    hi = a.astype(BF16)
    lo = (a - hi.astype(F32)).astype(BF16)
    return hi, lo


def _split3(a):
    a0 = a.astype(BF16)
    r = a - a0.astype(F32)
    a1 = r.astype(BF16)
    a2 = (r - a1.astype(F32)).astype(BF16)
    return a0, a1, a2


def _dot3(a, b):
    a0, a1 = _split2(a)
    b0, b1 = _split2(b)
    return _dot(a0, b0) + (_dot(a0, b1) + _dot(a1, b0))


def _silu(x):
    return x * jax.nn.sigmoid(x)


def _softplus(x):
    return jnp.maximum(x, 0.0) + jnp.log(1.0 + jnp.exp(-jnp.abs(x)))


def _ada_kernel(c_ref, w_ref, b_ref, o_ref):
    s = _silu(c_ref[...])
    o_ref[0] = _dot3(s, w_ref[0]) + b_ref[0]


def ada_modulation(c8, ada_w, ada_b):
    depth, d, n = ada_w.shape
    tn = 1536
    return pl.pallas_call(
        _ada_kernel,
        out_shape=jax.ShapeDtypeStruct((depth, 8, n), F32),
        grid=(depth, n // tn),
        in_specs=[
            pl.BlockSpec((8, d), lambda i, j: (0, 0)),
            pl.BlockSpec((1, d, tn), lambda i, j: (i, 0, j)),
            pl.BlockSpec((1, 1, tn), lambda i, j: (i, 0, j)),
        ],
        out_specs=pl.BlockSpec((1, 8, tn), lambda i, j: (i, 0, j)),
        compiler_params=_cparams(("arbitrary", "arbitrary")),
        name="ada_modulation",
    )(c8, ada_w, ada_b.reshape(depth, 1, n))


def _proj_kernel(*refs, n_main, n_rope, has_small, cn):
    it = iter(refs)
    x_ref, sc_ref, sh_ref, w_ref = next(it), next(it), next(it), next(it)
    ws_ref = next(it) if has_small else None
    cos_ref = next(it) if n_rope else None
    sin_ref = next(it) if n_rope else None
    o_ref = next(it)
    os_ref = next(it) if has_small else None

    h = x_ref[0] * (1.0 + sc_ref[0, 0]) + sh_ref[0, 0]
    hb = h.astype(BF16)
    tm = hb.shape[0]
    if n_rope:
        cos = cos_ref[...]
        sin = sin_ref[...]
        lane = lax.broadcasted_iota(jnp.int32, (tm, LANES), 1)
        low_half = (lane & 32) == 0
    for j in range(n_main // cn):
        p = _dot(hb, w_ref[:, j * cn:(j + 1) * cn])
        for g in range(cn // LANES):
            pg = p[:, g * LANES:(g + 1) * LANES]
            col = j * cn + g * LANES
            if col < n_rope:
                partner = jnp.where(low_half, pltpu.roll(pg, LANES - 32, 1), pltpu.roll(pg, 32, 1))
                pg = pg * cos + partner * sin
            o_ref[0, :, col:col + LANES] = pg.astype(BF16)
    if has_small:
        os_ref[0] = _dot(hb, ws_ref[...])


def modulated_projection(x, sc, sh, w, w_small=None, rope=None, n_rope=0, n_lat_tiles=1):
    b, s, d = x.shape
    n_main = w.shape[1]
    tm = ROW_TILE
    grid = (b, s // tm)
    mod_spec = pl.BlockSpec((1, 1, 1, d), lambda bi, i: (bi, i // n_lat_tiles, 0, 0))
    in_specs = [pl.BlockSpec((1, tm, d), lambda bi, i: (bi, i, 0)), mod_spec, mod_spec,
                pl.BlockSpec((d, n_main), lambda bi, i: (0, 0))]
    args = [x, sc, sh, w]
    out_shape = [jax.ShapeDtypeStruct((b, s, n_main), BF16)]
    out_specs = [pl.BlockSpec((1, tm, n_main), lambda bi, i: (bi, i, 0))]
    if w_small is not None:
        in_specs.append(pl.BlockSpec((d, LANES), lambda bi, i: (0, 0)))
        args.append(w_small)
        out_shape.append(jax.ShapeDtypeStruct((b, s, LANES), F32))
        out_specs.append(pl.BlockSpec((1, tm, LANES), lambda bi, i: (bi, i, 0)))
    if n_rope:
        tab = pl.BlockSpec((tm, LANES), lambda bi, i: (i, 0))
        in_specs += [tab, tab]
        args += [rope[0], rope[1]]
    kern = functools.partial(_proj_kernel, n_main=n_main, n_rope=n_rope,
                             has_small=w_small is not None, cn=512)
    out = pl.pallas_call(
        kern, out_shape=out_shape, grid=grid, in_specs=in_specs, out_specs=out_specs,
        compiler_params=_cparams(("arbitrary", "arbitrary")), name="modulated_projection",
    )(*args)
    return out if w_small is not None else out[0]


NEG_INIT = -1e30
FLASH_ROW_BLOCK = 128


def _flash_kernel(lam_ref, q_ref, k_ref, v_ref, sub_ref, *rest, lam_init, nk, aliased):
    if aliased:
        rest = rest[1:]
    o_ref, s_a, s_b, mt_a, mt_b, m_sc, l_sc, acc_sc = rest
    ki = pl.program_id(3)
    tq, tk = s_a.shape[1], s_a.shape[2]

    @pl.when(ki == 0)
    def _():
        m_sc[...] = jnp.full(m_sc.shape, NEG_INIT, F32)
        l_sc[...] = jnp.zeros(l_sc.shape, F32)
        acc_sc[...] = jnp.zeros(acc_sc.shape, F32)
        s_b[...] = jnp.full(s_b.shape, NEG_INIT, F32)
        mt_b[...] = jnp.full(mt_b.shape, NEG_INIT, F32)

    def step(s_w, mt_w, s_r, mt_r):
        q = q_ref[0]
        k = k_ref[0]
        v = v_ref[0]
        hd = DA_HEAD_DIM
        rb = min(tq, FLASH_ROW_BLOCK)
        kc = 2 * LANES
        for c in range(2):
            kcm = k[:, c * hd:(c + 1) * hd]
            for r0 in range(0, tq, rb):
                rows = slice(r0, r0 + rb)
                s = _dot_nt(q[rows, c * hd:(c + 1) * hd], kcm)
                s_w[c, rows, :] = s
                mt_w[c, rows, :] = jnp.broadcast_to(jnp.max(s, axis=-1, keepdims=True), (rb, LANES))
                m_prev = m_sc[c, rows, :]
                m_new = jnp.maximum(m_prev, mt_r[c, rows, :])
                alpha = jnp.exp2(m_prev - m_new)
                m2 = jnp.concatenate([m_new, m_new], axis=-1)
                lsum = None
                pv = None
                for j in range(tk // kc):
                    pj = jnp.exp2(s_r[c, rows, j * kc:(j + 1) * kc] - m2)
                    lj = pj[:, :LANES] + pj[:, LANES:]
                    lsum = lj if lsum is None else lsum + lj
                    t = _dot(pj.astype(BF16), v[j * kc:(j + 1) * kc, :])
                    pv = t if pv is None else pv + t
                l_sc[c, rows, :] = alpha * l_sc[c, rows, :] + lsum
                acc_sc[c, rows, :] = alpha * acc_sc[c, rows, :] + pv
                m_sc[c, rows, :] = m_new

    @pl.when(ki % 2 == 0)
    def _():
        step(s_a, mt_a, s_b, mt_b)

    @pl.when(ki % 2 == 1)
    def _():
        step(s_b, mt_b, s_a, mt_a)

    @pl.when(ki == nk)
    def _():
        lv = lam_ref[...]
        lam = (jnp.exp(jnp.sum(lv[0:1] * lv[1:2], axis=-1, keepdims=True))
               - jnp.exp(jnp.sum(lv[2:3] * lv[3:4], axis=-1, keepdims=True)) + lam_init)
        l0 = jnp.sum(l_sc[0], axis=-1, keepdims=True)
        l1 = jnp.sum(l_sc[1], axis=-1, keepdims=True)
        o = acc_sc[0] / l0 - lam * (acc_sc[1] / l1)
        ms = jnp.mean(o * o, axis=-1, keepdims=True)
        o = o * lax.rsqrt(ms + NORM_EPS) * sub_ref[...] * (1.0 - lam_init)
        o_ref[0] = o.astype(BF16)


def diff_flash_attention(p_all, lam_vec, subln, lam_init, *, tq, tk, nq, nk, q_off, k_off, prev=None):
    b, s, _ = p_all.shape
    hh = DA_HEADS
    in_specs = [
        pl.BlockSpec((4, DA_HEAD_DIM), lambda bi, h, qi, ki: (0, 0)),
        pl.BlockSpec((1, tq, LANES), lambda bi, h, qi, ki: (bi, qi + q_off, h)),
        pl.BlockSpec((1, tk, LANES), lambda bi, h, qi, ki: (bi, jnp.minimum(ki, nk - 1) + k_off, hh + h)),
        pl.BlockSpec((1, tk, LANES), lambda bi, h, qi, ki: (bi, jnp.maximum(ki - 1, 0) + k_off, 2 * hh + h)),
        pl.BlockSpec((1, LANES), lambda bi, h, qi, ki: (0, 0)),
    ]
    args = [lam_vec, p_all, p_all, p_all, subln.reshape(1, LANES)]
    aliases = {}
    if prev is not None:
        in_specs.append(pl.BlockSpec(memory_space=pl.ANY))
        args.append(prev)
        aliases = {5: 0}
    kern = functools.partial(_flash_kernel, lam_init=lam_init, nk=nk, aliased=prev is not None)
    return pl.pallas_call(
        kern,
        out_shape=jax.ShapeDtypeStruct((b, s, hh * LANES), BF16),
        grid=(b, hh, nq, nk + 1),
        in_specs=in_specs,
        out_specs=pl.BlockSpec((1, tq, LANES), lambda bi, h, qi, ki: (bi, qi + q_off, h)),
        scratch_shapes=[pltpu.VMEM((2, tq, tk), F32), pltpu.VMEM((2, tq, tk), F32),
                        pltpu.VMEM((2, tq, LANES), F32), pltpu.VMEM((2, tq, LANES), F32),
                        pltpu.VMEM((2, tq, LANES), F32), pltpu.VMEM((2, tq, LANES), F32),
                        pltpu.VMEM((2, tq, LANES), F32)],
        input_output_aliases=aliases,
        compiler_params=_cparams(("arbitrary", "arbitrary", "arbitrary", "arbitrary")),
        name="diff_flash_attention",
    )(*args)


def _post_norm(x, y, gate, lg, lb):
    r = DN_ALPHA * x + gate * y
    mu = jnp.mean(r, axis=-1, keepdims=True)
    rc = r - mu
    var = jnp.mean(rc * rc, axis=-1, keepdims=True)
    return rc * lax.rsqrt(var + LN_EPS) * lg + lb


def _outproj_kernel(*refs, gated, n_heads, dh):
    if gated:
        of_ref, ob_ref, z_ref, ng_ref, w_ref, x_ref, gate_ref, lg_ref, lb_ref, out_ref = refs
        y = None
        for h in range(n_heads):
            sl = slice(h * dh, (h + 1) * dh)
            o = of_ref[0, :, sl].astype(F32) + ob_ref[0, :, sl].astype(F32)
            ms = jnp.mean(o * o, axis=-1, keepdims=True)
            o = o * lax.rsqrt(ms + NORM_EPS) * ng_ref[...] * _silu(z_ref[0, :, sl].astype(F32))
            t = _dot(o.astype(BF16), w_ref[sl, :])
            y = t if y is None else y + t
    else:
        o_ref, w_ref, x_ref, gate_ref, lg_ref, lb_ref, out_ref = refs
        y = _dot(o_ref[0], w_ref[...])
    out_ref[0] = _post_norm(x_ref[0], y, gate_ref[0, 0], lg_ref[...], lb_ref[...])


def out_projection_post_norm(o_args, w_o, x, gate, lg, lb, n_lat_tiles, gated=None):
    b, s, d = x.shape
    kdim = w_o.shape[0]
    tm = ROW_TILE
    row = lambda width, cb=0: pl.BlockSpec((1, tm, width), lambda bi, i: (bi, i, cb))
    full = lambda shape: pl.BlockSpec(shape, lambda bi, i: (0,) * len(shape))
    if gated is None:
        in_specs = [row(kdim)]
        args = list(o_args)
        kern = functools.partial(_outproj_kernel, gated=False, n_heads=0, dh=0)
    else:
        n_heads, dh, z_cb = gated
        of, ob, z, ng = o_args
        in_specs = [row(kdim), row(kdim), row(kdim, z_cb), full((1, dh))]
        args = [of, ob, z, ng.reshape(1, dh)]
        kern = functools.partial(_outproj_kernel, gated=True, n_heads=n_heads, dh=dh)
    in_specs += [full((kdim, d)), row(d),
                 pl.BlockSpec((1, 1, 1, d), lambda bi, i: (bi, i // n_lat_tiles, 0, 0)),
                 full((1, d)), full((1, d))]
    args += [w_o, x, gate, lg.reshape(1, d), lb.reshape(1, d)]
    return pl.pallas_call(
        kern, out_shape=jax.ShapeDtypeStruct((b, s, d), F32), grid=(b, s // tm),
        in_specs=in_specs, out_specs=row(d),
        compiler_params=_cparams(("arbitrary", "arbitrary")), name="out_projection_post_norm",
    )(*args)


def _gdn_prep_kernel(cur_ref, prev_ref, next_ref, gate_ref, cw_ref, alog_ref, dtb_ref,
                     q_ref, k_ref, v_ref, g_ref, ext, *, n_lat_tiles, n_tiles, cn):
    i = pl.program_id(1)
    tm = cur_ref.shape[1]
    first = jnp.logical_or(i == 0, i == n_lat_tiles)
    last = jnp.logical_or(i == n_lat_tiles - 1, i == n_tiles - 1)
    pmask = jnp.where(first, 0.0, 1.0)
    nmask = jnp.where(last, 0.0, 1.0)
    pad = GDN_CONV_W // 2
    dh = GDN_HEAD_DIM
    n_qk = 2 * GDN_KEY_DIM
    for cc in range(cur_ref.shape[2] // cn):
        cs = slice(cc * cn, (cc + 1) * cn)
        ext[0:HALO, :] = prev_ref[0, :, cs].astype(F32) * pmask
        ext[HALO:HALO + tm, :] = cur_ref[0, :, cs].astype(F32)
        ext[HALO + tm:2 * HALO + tm, :] = next_ref[0, :, cs].astype(F32) * nmask
        acc = None
        for j in range(GDN_CONV_W):
            t = ext[pl.ds(HALO - pad + j, tm), :] * cw_ref[j:j + 1, cs]
            acc = t if acc is None else acc + t
        y = _silu(acc)
        for g in range(cn // dh):
            col = cc * cn + g * dh
            yg = y[:, g * dh:(g + 1) * dh]
            if col < n_qk:
                yg = yg * lax.rsqrt(jnp.sum(yg * yg, axis=-1, keepdims=True) + NORM_EPS)
                if col < GDN_KEY_DIM:
                    q_ref[0, col // dh] = (yg * (dh ** -0.5)).astype(BF16)
                else:
                    k_ref[0, (col - GDN_KEY_DIM) // dh] = yg.astype(BF16)
            else:
                v_ref[0, (col - n_qk) // dh] = yg.astype(BF16)
    a = gate_ref[0]
    lane = lax.broadcasted_iota(jnp.int32, a.shape, 1)
    gdec = -jnp.exp(alog_ref[...]) * _softplus(a + dtb_ref[...])
    g_ref[0] = jnp.where(lane < 2 * GDN_V_HEADS, gdec, jax.nn.sigmoid(a))


def gdn_prep(p_main, gates_raw, conv_w, alog_row, dtb_row, n_lat_tiles):
    b, s, _ = p_main.shape
    tm = ROW_TILE
    nt = s // tm
    nch = 2 * GDN_KEY_DIM + GDN_VAL_DIM
    hpt = tm // HALO
    nh = s // HALO
    kern = functools.partial(_gdn_prep_kernel, n_lat_tiles=n_lat_tiles, n_tiles=nt, cn=512)
    head_out = lambda nheads: pl.BlockSpec((1, nheads, tm, GDN_HEAD_DIM), lambda bi, i: (bi, 0, i, 0))
    return pl.pallas_call(
        kern,
        out_shape=[jax.ShapeDtypeStruct((b, GDN_K_HEADS, s, GDN_HEAD_DIM), BF16),
                   jax.ShapeDtypeStruct((b, GDN_K_HEADS, s, GDN_HEAD_DIM), BF16),
                   jax.ShapeDtypeStruct((b, GDN_V_HEADS, s, GDN_HEAD_DIM), BF16),
                   jax.ShapeDtypeStruct((b, s, LANES), F32)],
        grid=(b, nt),
        in_specs=[
            pl.BlockSpec((1, tm, nch), lambda bi, i: (bi, i, 0)),
            pl.BlockSpec((1, HALO, nch), lambda bi, i: (bi, jnp.maximum(i * hpt - 1, 0), 0)),
            pl.BlockSpec((1, HALO, nch), lambda bi, i: (bi, jnp.minimum((i + 1) * hpt, nh - 1), 0)),
            pl.BlockSpec((1, tm, LANES), lambda bi, i: (bi, i, 0)),
            pl.BlockSpec((GDN_CONV_W, nch), lambda bi, i: (0, 0)),
            pl.BlockSpec((1, LANES), lambda bi, i: (0, 0)),
            pl.BlockSpec((1, LANES), lambda bi, i: (0, 0)),
        ],
        out_specs=[head_out(GDN_K_HEADS), head_out(GDN_K_HEADS), head_out(GDN_V_HEADS),
                   pl.BlockSpec((1, tm, LANES), lambda bi, i: (bi, i, 0))],
        scratch_shapes=[pltpu.VMEM((tm + 2 * HALO, 512), F32)],
        compiler_params=_cparams(("arbitrary", "arbitrary")),
        name="gdn_prep",
    )(p_main, p_main, p_main, gates_raw, conv_w, alog_row, dtb_row)


def _chunk_masks(n, reverse):
    r = lax.broadcasted_iota(jnp.int32, (n, n), 0)
    c = lax.broadcasted_iota(jnp.int32, (n, n), 1)
    if reverse:
        return r <= c, r < c
    return r >= c, r > c


def _block_cumsum_mats(ct, reverse):
    r = lax.broadcasted_iota(jnp.int32, (ct, ct), 0)
    c = lax.broadcasted_iota(jnp.int32, (ct, ct), 1)
    same = (r // CHUNK) == (c // CHUNK)
    lower = jnp.logical_and(same, c <= r)
    upper = jnp.logical_and(same, c >= r)
    lo = jnp.where(lower, 1.0, 0.0).astype(BF16)
    up = jnp.where(upper, 1.0, 0.0).astype(BF16)
    return (up, lo) if reverse else (lo, up)


def _cumsum_cols(mat, x):
    x0, x1, x2 = _split3(x)
    return _dot(mat, x0) + (_dot(mat, x1) + _dot(mat, x2))


def _cumsum_rows(x, mat):
    x0, x1, x2 = _split3(x)
    return _dot(x0, mat) + (_dot(x1, mat) + _dot(x2, mat))


def _gdn_block(dirs):
    dh = GDN_HEAD_DIM
    eye = jnp.where(lax.broadcasted_iota(jnp.int32, (CHUNK, CHUNK), 0)
                    == lax.broadcasted_iota(jnp.int32, (CHUNK, CHUNK), 1), 1.0, 0.0).astype(F32)
    chains = []
    for q_ref, k_ref, v_ref, gc_ref, gr_ref, o_ref, s_sc, reverse in dirs:
        ct = q_ref.shape[2]
        goff = 2 if reverse else 0
        gcol = gc_ref[0, 0]
        grow = gr_ref[0, 0]
        m_col, m_row = _block_cumsum_mats(ct, reverse)
        gcum_c = _cumsum_cols(m_col, gcol)
        gcum_r = _cumsum_rows(grow, m_row)
        tril, strict = _chunk_masks(CHUNK, reverse)
        for c in range(ct // CHUNK):
            rs = slice(c * CHUNK, (c + 1) * CHUNK)
            last = c * CHUNK if reverse else (c + 1) * CHUNK - 1
            q = q_ref[0, 0, rs, :]
            k = k_ref[0, 0, rs, :]
            kk = _dot_nt(k, k)
            qk = _dot_nt(q, k)
            for hh in range(2):
                gi = goff + hh
                chains.append(dict(
                    q=q, k=k, kk=kk, qk=qk, rs=rs, c=c, hh=hh, reverse=reverse, tril=tril, strict=strict,
                    v_ref=v_ref, o_ref=o_ref, s_sc=s_sc,
                    gc=gcum_c[rs, gi:gi + 1], gr=gcum_r[gi:gi + 1, rs],
                    glast=gcum_c[last:last + 1, gi:gi + 1], beta=gcol[rs, 4 + gi:5 + gi]))
    for ch in chains:
        tril = ch["tril"]
        ch["decay"] = jnp.where(tril, jnp.exp(jnp.where(tril, ch["gc"] - ch["gr"], 0.0)), 0.0)
        a = -jnp.where(ch["strict"], ch["kk"] * ch["beta"] * ch["decay"], 0.0)
        ch["tmat"] = eye + a
        ch["pw"] = a
    for _ in range(5):
        for ch in chains:
            pwb = ch["pw"].astype(BF16)
            ch["pw"] = _dot(pwb, pwb)
        for ch in chains:
            ch["tmat"] = ch["tmat"] + _dot(ch["tmat"].astype(BF16), ch["pw"].astype(BF16))
    for ch in chains:
        beta = ch["beta"]
        eg = jnp.exp(ch["gc"])
        vb = ch["v_ref"][0, ch["hh"], ch["rs"], :].astype(F32) * beta
        kbg = ch["k"].astype(F32) * (beta * eg)
        uw = _dot(ch["tmat"].astype(BF16), jnp.concatenate([vb, kbg], axis=-1).astype(BF16))
        ch["u"] = uw[:, :dh]
        ch["w"] = uw[:, dh:].astype(BF16)
        ch["eg"] = eg
        ch["attn"] = jnp.where(ch["tril"], ch["qk"] * ch["decay"], 0.0).astype(BF16)
    nchunk = max(ch["c"] for ch in chains) + 1
    for step in range(nchunk):
        cur = [ch for ch in chains if ch["c"] == (nchunk - 1 - step if ch["reverse"] else step)]
        for ch in cur:
            state = ch["s_sc"][ch["hh"]]
            sb = state.astype(BF16)
            ch["state"] = state
            ch["ws"] = _dot(ch["w"], sb)
            ch["qs"] = _dot(ch["q"], sb)
        for ch in cur:
            v_new = ch["u"] - ch["ws"]
            ch["kgv"] = (v_new * jnp.exp(ch["glast"] - ch["gc"])).astype(BF16)
            ch["o"] = ch["eg"] * ch["qs"] + _dot(ch["attn"], v_new.astype(BF16))
        for ch in cur:
            ch["s_sc"][ch["hh"]] = ch["state"] * jnp.exp(ch["glast"]) + _dot_tn(ch["k"], ch["kgv"])
            ch["o_ref"][0, ch["rs"], ch["hh"] * dh:(ch["hh"] + 1) * dh] = ch["o"].astype(BF16)


def _gdn_scan_kernel(qf, kf, vf, gcf, grf, qb, kb, vb, gcb, grb, s0f, s0b, *rest, nblk, aliased):
    if aliased:
        rest = rest[2:]
    of_ref, ob_ref, sff, sfb, sf_sc, sb_sc = rest
    j = pl.program_id(2)

    @pl.when(j == 0)
    def _():
        sf_sc[...] = s0f[0]
        sb_sc[...] = s0b[0]

    _gdn_block([(qf, kf, vf, gcf, grf, of_ref, sf_sc, False),
                (qb, kb, vb, gcb, grb, ob_ref, sb_sc, True)])

    @pl.when(j == nblk - 1)
    def _():
        sff[0] = sf_sc[...]
        sfb[0] = sb_sc[...]


def gdn_scan(qn, kn, vv, gcol, grow, s0f, s0b, *, nblk, off, prev=None):
    b, _, s, dh = qn.shape
    ct = SCAN_TILE
    fwd = lambda j: j + off
    bwd = lambda j: nblk - 1 - j + off
    def specs(pos):
        return [
            pl.BlockSpec((1, 1, ct, dh), lambda bi, h, j: (bi, h, pos(j), 0)),
            pl.BlockSpec((1, 1, ct, dh), lambda bi, h, j: (bi, h, pos(j), 0)),
            pl.BlockSpec((1, 2, ct, dh), lambda bi, h, j: (bi, h, pos(j), 0)),
            pl.BlockSpec((1, 1, ct, 8), lambda bi, h, j: (bi, h, pos(j), 0)),
            pl.BlockSpec((1, 1, 8, ct), lambda bi, h, j: (bi, h, 0, pos(j))),
        ]
    st_spec = pl.BlockSpec((1, 2, dh, dh), lambda bi, h, j: (bi, h, 0, 0))
    in_specs = specs(fwd) + specs(bwd) + [st_spec, st_spec]
    args = [qn, kn, vv, gcol, grow] * 2 + [s0f, s0b]
    aliases = {}
    if prev is not None:
        in_specs += [pl.BlockSpec(memory_space=pl.ANY)] * 2
        args += list(prev)
        aliases = {12: 0, 13: 1}
    o_shape = jax.ShapeDtypeStruct((b, s, GDN_VAL_DIM), BF16)
    st_shape = jax.ShapeDtypeStruct((b, GDN_V_HEADS, dh, dh), F32)
    kern = functools.partial(_gdn_scan_kernel, nblk=nblk, aliased=prev is not None)
    return pl.pallas_call(
        kern,
        out_shape=[o_shape, o_shape, st_shape, st_shape],
        grid=(b, GDN_K_HEADS, nblk),
        in_specs=in_specs,
        out_specs=[pl.BlockSpec((1, ct, 2 * dh), lambda bi, h, j: (bi, fwd(j), h)),
                   pl.BlockSpec((1, ct, 2 * dh), lambda bi, h, j: (bi, bwd(j), h)),
                   st_spec, st_spec],
        scratch_shapes=[pltpu.VMEM((2, dh, dh), F32), pltpu.VMEM((2, dh, dh), F32)],
        input_output_aliases=aliases,
        compiler_params=_cparams(("arbitrary", "arbitrary", "arbitrary")),
        name="gdn_scan",
    )(*args)


def _gla_block(dirs, wg_ref, gb_ref):
    chains = []
    pre = []
    for q_ref, k_ref, v_ref, gr_ref, o_ref, st_sc, z in dirs:
        pre.append(_dot3(gr_ref[0], wg_ref[z]) + gb_ref[z:z + 1, :])
    for (q_ref, k_ref, v_ref, gr_ref, o_ref, st_sc, z), logit in zip(dirs, pre):
        reverse = z == 1
        ct = q_ref.shape[1]
        glog = -_softplus(-logit) / GLA_TAU
        m_col, _ = _block_cumsum_mats(ct, reverse)
        bcum = _cumsum_cols(m_col, glog)
        tril, _ = _chunk_masks(CHUNK, reverse)
        for c in range(ct // CHUNK):
            rs = slice(c * CHUNK, (c + 1) * CHUNK)
            last = c * CHUNK if reverse else (c + 1) * CHUNK - 1
            bc = bcum[rs, :]
            bl = bcum[last:last + 1, :]
            qf = q_ref[0, rs, :].astype(F32) * (GLA_DK ** -0.5)
            kf = k_ref[0, rs, :].astype(F32)
            chains.append(dict(
                c=c, rs=rs, reverse=reverse, tril=tril, o_ref=o_ref, st_sc=st_sc, v=v_ref[0, rs, :],
                qe=(qf * jnp.exp(bc)).astype(BF16), ke=(kf * jnp.exp(-bc)).astype(BF16),
                kg=(kf * jnp.exp(bl - bc)).astype(BF16), gl=jnp.exp(bl)))
    for ch in chains:
        ch["attn"] = jnp.where(ch["tril"], _dot_nt(ch["qe"], ch["ke"]), 0.0).astype(BF16)
    for ch in chains:
        ch["o"] = _dot(ch["attn"], ch["v"])
        ch["kv"] = _dot_tn(ch["v"], ch["kg"])
    nchunk = max(ch["c"] for ch in chains) + 1
    for step in range(nchunk):
        cur = [ch for ch in chains if ch["c"] == (nchunk - 1 - step if ch["reverse"] else step)]
        for ch in cur:
            st = ch["st_sc"][...]
            ch["o"] = ch["o"] + _dot_nt(ch["qe"], st.astype(BF16))
            ch["st_sc"][...] = st * ch["gl"] + ch["kv"]
        for ch in cur:
            ch["o_ref"][0, ch["rs"], :] = ch["o"].astype(BF16)


def _gla_scan_kernel(qf, kf, vf, grf, qb, kb, vb, grb, wg, gb, s0f, s0b, *rest, nblk, aliased):
    if aliased:
        rest = rest[2:]
    of_ref, ob_ref, sff, sfb, sf_sc, sb_sc = rest
    j = pl.program_id(2)

    @pl.when(j == 0)
    def _():
        sf_sc[...] = s0f[0, 0]
        sb_sc[...] = s0b[0, 0]

    _gla_block([(qf, kf, vf, grf, of_ref, sf_sc, 0), (qb, kb, vb, grb, ob_ref, sb_sc, 1)], wg, gb)

    @pl.when(j == nblk - 1)
    def _():
        sff[0, 0] = sf_sc[...]
        sfb[0, 0] = sb_sc[...]


def gla_scan(p_main, gr, wg_pad, gate_b, s0f, s0b, *, nblk, off, prev=None):
    b, s, _ = p_main.shape
    ct = SCAN_TILE
    nh, dk, dv = GLA_HEADS, GLA_DK, GLA_DV
    fwd = lambda j: j + off
    bwd = lambda j: nblk - 1 - j + off
    def specs(pos):
        return [
            pl.BlockSpec((1, ct, dk), lambda bi, h, j: (bi, pos(j), h)),
            pl.BlockSpec((1, ct, dk), lambda bi, h, j: (bi, pos(j), nh + h)),
            pl.BlockSpec((1, ct, dv), lambda bi, h, j: (bi, pos(j), nh + h)),
            pl.BlockSpec((1, ct, LANES), lambda bi, h, j: (bi, pos(j), 0)),
        ]
    st_spec = pl.BlockSpec((1, 1, dv, dk), lambda bi, h, j: (bi, h, 0, 0))
    in_specs = specs(fwd) + specs(bwd) + [
        pl.BlockSpec((2, LANES, dk), lambda bi, h, j: (0, 0, h)),
        pl.BlockSpec((2, dk), lambda bi, h, j: (0, h)),
        st_spec, st_spec]
    args = [p_main, p_main, p_main, gr] * 2 + [wg_pad, gate_b, s0f, s0b]
    aliases = {}
    if prev is not None:
        in_specs += [pl.BlockSpec(memory_space=pl.ANY)] * 2
        args += list(prev)
        aliases = {12: 0, 13: 1}
    o_shape = jax.ShapeDtypeStruct((b, s, GLA_VAL_DIM), BF16)
    st_shape = jax.ShapeDtypeStruct((b, nh, dv, dk), F32)
    kern = functools.partial(_gla_scan_kernel, nblk=nblk, aliased=prev is not None)
    return pl.pallas_call(
        kern,
        out_shape=[o_shape, o_shape, st_shape, st_shape],
        grid=(b, nh, nblk),
        in_specs=in_specs,
        out_specs=[pl.BlockSpec((1, ct, dv), lambda bi, h, j: (bi, fwd(j), h)),
                   pl.BlockSpec((1, ct, dv), lambda bi, h, j: (bi, bwd(j), h)),
                   st_spec, st_spec],
        scratch_shapes=[pltpu.VMEM((dv, dk), F32), pltpu.VMEM((dv, dk), F32)],
        input_output_aliases=aliases,
        compiler_params=_cparams(("arbitrary", "arbitrary", "arbitrary")),
        name="gla_scan",
    )(*args)


def _router_kernel(x_ref, sc_ref, sh_ref, rwt_ref, rb_ref, h_ref, wd_ref):
    h = x_ref[0] * (1.0 + sc_ref[0, 0]) + sh_ref[0, 0]
    h_ref[0] = h.astype(BF16)
    w0, w1 = _split2(rwt_ref[...])
    h0, h1 = _split2(h)
    scores = jax.nn.sigmoid(_dot_nt(w0, h0) + (_dot_nt(w0, h1) + _dot_nt(w1, h0)))
    ne = scores.shape[0]
    row = lax.broadcasted_iota(jnp.int32, scores.shape, 0)
    neg = jnp.float32(-jnp.inf)
    sel = scores + rb_ref[...]
    chosen = jnp.zeros(scores.shape, jnp.bool_)
    for _ in range(TOP_K):
        mx = jnp.max(sel, axis=0, keepdims=True)
        first = jnp.min(jnp.where(sel == mx, row, ne), axis=0, keepdims=True)
        pick = row == first
        chosen = jnp.logical_or(chosen, pick)
        sel = jnp.where(pick, neg, sel)
    picked = jnp.where(chosen, scores, 0.0)
    wt = picked / jnp.sum(picked, axis=0, keepdims=True) * ROUTE_SCALE
    eye = jnp.where(lax.broadcasted_iota(jnp.int32, (ne, LANES), 0)
                    == lax.broadcasted_iota(jnp.int32, (ne, LANES), 1), 1.0, 0.0).astype(BF16)
    t0, t1, t2 = _split3(wt)
    wd_ref[0] = _dot_tn(t0, eye) + (_dot_tn(t1, eye) + _dot_tn(t2, eye))


def moe_route(x, sc, sh, rw_t, rb_col, n_lat_tiles):
    b, s, d = x.shape
    tm = ROW_TILE
    mod_spec = pl.BlockSpec((1, 1, 1, d), lambda bi, i: (bi, i // n_lat_tiles, 0, 0))
    return pl.pallas_call(
        _router_kernel,
        out_shape=[jax.ShapeDtypeStruct((b, s, d), BF16), jax.ShapeDtypeStruct((b, s, LANES), F32)],
        grid=(b, s // tm),
        in_specs=[pl.BlockSpec((1, tm, d), lambda bi, i: (bi, i, 0)), mod_spec, mod_spec,
                  pl.BlockSpec((N_EXPERTS, d), lambda bi, i: (0, 0)),
                  pl.BlockSpec((N_EXPERTS, 1), lambda bi, i: (0, 0))],
        out_specs=[pl.BlockSpec((1, tm, d), lambda bi, i: (bi, i, 0)),
                   pl.BlockSpec((1, tm, LANES), lambda bi, i: (bi, i, 0))],
        compiler_params=_cparams(("arbitrary", "arbitrary")),
        name="moe_router",
    )(x, sc, sh, rw_t, rb_col)


def _moe_kernel(h_ref, wd_ref, wgu_ref, wdn_ref, x_ref, gate_ref, lg_ref, lb_ref, out_ref, acc, *,
                n_exp, n_lat_rows):
    e = pl.program_id(2)

    @pl.when(e == 0)
    def _():
        acc[...] = jnp.zeros(acc.shape, F32)

    gu = _dot(h_ref[0], wgu_ref[0])
    act = _silu(gu[:, :EXPERT_FF]) * gu[:, EXPERT_FF:]
    wd = wd_ref[0]
    lane = lax.broadcasted_iota(jnp.int32, wd.shape, 1)
    wcol = jnp.sum(jnp.where(lane == e, wd, 0.0), axis=-1, keepdims=True)
    acc[...] += wcol * _dot(act.astype(BF16), wdn_ref[0])

    @pl.when(e == n_exp - 1)
    def _():
        tm = acc.shape[0]
        row = pl.program_id(1) * tm + lax.broadcasted_iota(jnp.int32, (tm, 1), 0)
        gate = jnp.where(row >= n_lat_rows, gate_ref[0, 1], gate_ref[0, 0])
        out_ref[0] = _post_norm(x_ref[0], acc[...], gate, lg_ref[...], lb_ref[...])


def moe_experts_post_norm(h, wd, wgu, wdn, x, gate, lg, lb, n_lat_rows, tm):
    b, s, d = x.shape
    n_exp = wgu.shape[0]
    row = lambda width: pl.BlockSpec((1, tm, width), lambda bi, i, e: (bi, i, 0))
    kern = functools.partial(_moe_kernel, n_exp=n_exp, n_lat_rows=n_lat_rows)
    return pl.pallas_call(
        kern,
        out_shape=jax.ShapeDtypeStruct((b, s, d), F32),
        grid=(b, s // tm, n_exp),
        in_specs=[row(d), row(LANES),
                  pl.BlockSpec((1, d, 2 * EXPERT_FF), lambda bi, i, e: (e, 0, 0)),
                  pl.BlockSpec((1, EXPERT_FF, d), lambda bi, i, e: (e, 0, 0)),
                  row(d),
                  pl.BlockSpec((1, 2, 1, d), lambda bi, i, e: (bi, 0, 0, 0)),
                  pl.BlockSpec((1, d), lambda bi, i, e: (0, 0)),
                  pl.BlockSpec((1, d), lambda bi, i, e: (0, 0))],
        out_specs=row(d),
        scratch_shapes=[pltpu.VMEM((tm, d), F32)],
        compiler_params=_cparams(("arbitrary", "arbitrary", "arbitrary")),
        name="moe_experts_post_norm",
    )(h, wd, wgu, wdn, x, gate, lg.reshape(1, d), lb.reshape(1, d))


MOE_SUB = 256
MOE_CAP = 64
MOE_GRP = 4


def _moe_grouped_kernel(order_ref, h_ref, wd_ref, *rest):
    wgu_refs = rest[:MOE_GRP]
    wdn_refs = rest[MOE_GRP:2 * MOE_GRP]
    wsgu_ref, wsdn_ref, f_ref, acc, rcm, rrm, wbf, cmax = rest[2 * MOE_GRP:]
    g = pl.program_id(2)
    n_grp = pl.num_programs(2)
    tm = acc.shape[0]
    nsub = tm // MOE_SUB
    sub, cap, nslot = MOE_SUB, MOE_CAP, MOE_GRP * MOE_CAP
    lane = lax.broadcasted_iota(jnp.int32, (1, LANES), 1)

    @pl.when(g == 0)
    def _():
        r = lax.broadcasted_iota(jnp.int32, (sub, sub), 0)
        c = lax.broadcasted_iota(jnp.int32, (sub, sub), 1)
        before = jnp.where(c < r, 1.0, 0.0).astype(BF16)
        after = jnp.where(r < c, 1.0, 0.0).astype(BF16)
        ident = jnp.where(r == c, 1.0, 0.0).astype(BF16)
        cm = jnp.zeros((1, LANES), F32)
        for u in range(nsub):
            rs = slice(u * sub, (u + 1) * sub)
            hu = h_ref[0, rs, :]
            gu = _dot(hu, wsgu_ref[...])
            act = _silu(gu[:, :EXPERT_FF]) * gu[:, EXPERT_FF:]
            acc[rs, :] = _dot(act.astype(BF16), wsdn_ref[...])
            wd = wd_ref[0, rs, :]
            active = jnp.logical_and(wd != 0.0, lane < N_EXPERTS)
            a = jnp.where(active, 1.0, 0.0)
            ab = a.astype(BF16)
            rank_c = _dot(before, ab)
            rank_r = _dot_tn(ab, after)
            a_r = _dot_tn(ab, ident)
            rcm[u] = jnp.where(active, rank_c, -1.0).astype(BF16)
            rrm[u] = jnp.where(a_r > 0.5, rank_r, -1.0).astype(BF16)
            wbf[rs, :] = wd.astype(BF16)
            cm = jnp.maximum(cm, jnp.sum(a, axis=0, keepdims=True))
        cmax[...] = jnp.broadcast_to(cm, cmax.shape)

    experts = [order_ref[g * MOE_GRP + k] for k in range(MOE_GRP)]
    in_group = functools.reduce(jnp.logical_or, [lane == e for e in experts])
    n_max = jnp.max(jnp.where(in_group, cmax[0:1, :], 0.0))
    n_pass = (n_max.astype(jnp.int32) + (cap - 1)) // cap

    def slot_expert(l):
        return sum(((l >= k * cap).astype(jnp.int32) for k in range(1, MOE_GRP)), jnp.zeros_like(l))

    def expert_of_slot(l):
        pos = slot_expert(l)
        e = jnp.full(l.shape, -1, jnp.int32)
        for k in range(MOE_GRP):
            e = jnp.where(pos == k, experts[k], e)
        return jnp.where(l < nslot, e, -1)

    e_i = lax.broadcasted_iota(jnp.int32, (LANES, sub), 0)
    l_i = lax.broadcasted_iota(jnp.int32, (LANES, sub), 1)
    expand = jnp.where(e_i == expert_of_slot(l_i), 1.0, 0.0).astype(BF16)
    l_t = lax.broadcasted_iota(jnp.int32, (sub, LANES), 0)
    e_t = lax.broadcasted_iota(jnp.int32, (sub, LANES), 1)
    expand_t = jnp.where(e_t == expert_of_slot(l_t), 1.0, 0.0).astype(BF16)
    l_row = lax.broadcasted_iota(jnp.int32, (1, sub), 1)
    j_row = jnp.where(l_row < nslot, l_row - cap * slot_expert(l_row), -1000).astype(F32)
    l_col = lax.broadcasted_iota(jnp.int32, (sub, 1), 0)
    j_col = jnp.where(l_col < nslot, l_col - cap * slot_expert(l_col), -1000).astype(F32)

    def one_pass(p, carry):
        base = (p * cap).astype(F32)
        xg = []
        for u in range(nsub):
            rs = slice(u * sub, (u + 1) * sub)
            rank_of_slot = _dot(expand_t, rrm[u])
            gather = jnp.where(rank_of_slot == j_col + base, 1.0, 0.0).astype(BF16)
            xg.append(_dot(gather, h_ref[0, rs, :]).astype(BF16))
        ys = []
        for k in range(MOE_GRP):
            es = slice(k * cap, (k + 1) * cap)
            x_e = jnp.concatenate([xg[u][es] for u in range(nsub)], axis=0)
            gu = _dot(x_e, wgu_refs[k][0])
            act = _silu(gu[:, :EXPERT_FF]) * gu[:, EXPERT_FF:]
            ys.append(_dot(act.astype(BF16), wdn_refs[k][0]).astype(BF16))
        for u in range(nsub):
            rs = slice(u * sub, (u + 1) * sub)
            parts = [ys[k][u * cap:(u + 1) * cap] for k in range(MOE_GRP)]
            if nslot < sub:
                parts.append(jnp.zeros((sub - nslot, parts[0].shape[1]), BF16))
            y_u = jnp.concatenate(parts, axis=0)
            slot_of_row = _dot(rcm[u], expand)
            weight = _dot(wbf[rs, :], expand)
            scatter = jnp.where(slot_of_row == j_row + base, weight, 0.0).astype(BF16)
            acc[rs, :] += _dot(scatter, y_u)
        return carry

    lax.fori_loop(0, n_pass, one_pass, 0)

    @pl.when(g == n_grp - 1)
    def _():
        f_ref[0] = acc[...]


def moe_grouped_experts(order, h, wd, wgu, wdn, wsgu, wsdn, tm):
    b, s, d = h.shape
    n_grp = wgu.shape[0] // MOE_GRP
    row = lambda width: pl.BlockSpec((1, tm, width), lambda bi, i, g, o: (bi, i, 0))

    def expert(shape, k):
        return pl.BlockSpec((1,) + shape, lambda bi, i, g, o: (o[g * MOE_GRP + k], 0, 0))

    in_specs = ([row(d), row(LANES)]
                + [expert((d, 2 * EXPERT_FF), k) for k in range(MOE_GRP)]
                + [expert((EXPERT_FF, d), k) for k in range(MOE_GRP)]
                + [pl.BlockSpec((d, 2 * EXPERT_FF), lambda bi, i, g, o: (0, 0)),
                   pl.BlockSpec((EXPERT_FF, d), lambda bi, i, g, o: (0, 0))])
    grid_spec = pltpu.PrefetchScalarGridSpec(
        num_scalar_prefetch=1, grid=(b, s // tm, n_grp), in_specs=in_specs, out_specs=row(d),
        scratch_shapes=[pltpu.VMEM((tm, d), F32),
                        pltpu.VMEM((tm // MOE_SUB, MOE_SUB, LANES), BF16),
                        pltpu.VMEM((tm // MOE_SUB, LANES, MOE_SUB), BF16),
                        pltpu.VMEM((tm, LANES), BF16),
                        pltpu.VMEM((8, LANES), F32)])
    return pl.pallas_call(
        _moe_grouped_kernel,
        out_shape=jax.ShapeDtypeStruct((b, s, d), F32),
        grid_spec=grid_spec,
        compiler_params=_cparams(("arbitrary", "arbitrary", "arbitrary")),
        name="moe_grouped_experts",
    )(order, h, wd, *([wgu] * MOE_GRP), *([wdn] * MOE_GRP), wsgu, wsdn)


def _post_norm_kernel(x_ref, f_ref, gate_ref, lg_ref, lb_ref, out_ref):
    out_ref[0] = _post_norm(x_ref[0], f_ref[0].astype(F32), gate_ref[0, 0], lg_ref[...], lb_ref[...])


def post_norm_rows(x, f, gate, lg, lb, n_lat_tiles, rows_out=None):
    b, s, d = x.shape
    tm = ROW_TILE
    rows_out = s if rows_out is None else rows_out
    row = pl.BlockSpec((1, tm, d), lambda bi, i: (bi, i, 0))
    vec = pl.BlockSpec((1, d), lambda bi, i: (0, 0))
    return pl.pallas_call(
        _post_norm_kernel, out_shape=jax.ShapeDtypeStruct((b, rows_out, d), F32), grid=(b, rows_out // tm),
        in_specs=[row, row, pl.BlockSpec((1, 1, 1, d), lambda bi, i: (bi, i // n_lat_tiles, 0, 0)), vec, vec],
        out_specs=row, compiler_params=_cparams(("arbitrary", "arbitrary")), name="post_norm_rows",
    )(x, f, gate, lg.reshape(1, d), lb.reshape(1, d))


def _rope_tables(n_lat, n_ctx):
    rows = n_lat // GRID_W
    rowp = jnp.repeat(jnp.arange(rows), GRID_W).astype(F32)
    colp = jnp.tile(jnp.arange(GRID_W), rows).astype(F32)
    n_freq = DA_HEAD_DIM // 4
    inv = 1.0 / (ROPE_BASE ** (jnp.arange(n_freq, dtype=F32) / n_freq))
    ang = jnp.concatenate([rowp[:, None] * inv, colp[:, None] * inv], -1)
    cos, sin = jnp.cos(ang), jnp.sin(ang)
    cos_t = jnp.tile(cos, (1, 4))
    sin_t = jnp.tile(jnp.concatenate([-sin, sin], -1), (1, 2))
    cos_t = jnp.concatenate([cos_t, jnp.ones((n_ctx, LANES), F32)], 0)
    sin_t = jnp.concatenate([sin_t, jnp.zeros((n_ctx, LANES), F32)], 0)
    return cos_t, sin_t


def _pick_tile(total, cands):
    for t in cands:
        if total % t == 0:
            return t
    raise ValueError(f"no tile for {total}")


def _pad_cols(w, n):
    return jnp.pad(w, ((0, 0), (0, n - w.shape[1])))


def _flash_both(p_all, lam_vec, subln, lam_init, n_lat, n_ctx):
    s = n_lat + n_ctx
    tq = _pick_tile(n_lat, (1024, 512, 256))
    tk = _pick_tile(s, (1280, 1024, 512, 256))
    o = diff_flash_attention(p_all, lam_vec, subln, lam_init, tq=tq, tk=tk, nq=n_lat // tq, nk=s // tk,
                             q_off=0, k_off=0)
    return diff_flash_attention(p_all, lam_vec, subln, lam_init, tq=n_ctx, tk=n_ctx, nq=1, nk=1,
                                q_off=n_lat // n_ctx, k_off=n_lat // n_ctx, prev=o)


def kernel(x, c, ctx, c_ctx, ada_w, ada_b, ln_g, ln_b, da_w_in, da_w_o, da_lambda, da_subln, gdn_w_in, gdn_conv, gdn_a_log, gdn_dt_bias, gdn_norm, gdn_w_o, gla_w_in, gla_w_gate, gla_gate_b, gla_norm, gla_w_o, moe_router, moe_router_b, moe_w_gu, moe_w_dn, moe_ws_gu, moe_ws_dn):
    b, n, d = x.shape
    lc = ctx.shape[1]
    assert lc == ROW_TILE and n % SCAN_TILE == 0 and d == D_MODEL
    s = n + lc
    n_lat_tiles = n // ROW_TILE
    depth = ada_w.shape[0]

    xall = jnp.concatenate([x, ctx], axis=1)
    c8 = jnp.concatenate([c, c_ctx[None], jnp.zeros((8 - b - 1, d), F32)], 0)
    mods = ada_modulation(c8, ada_w, ada_b)
    rope = _rope_tables(n, lc)
    moe_tm = _pick_tile(s, (1280, 1024, 512, 256))

    for i in range(depth):
        kind, j = i % N_MIXERS, i // N_MIXERS
        m = mods[i].reshape(8, ADA_CHUNKS, d)
        mod = jnp.stack([m[:b], jnp.broadcast_to(m[b], (b, ADA_CHUNKS, d))], axis=1)
        mod = [mod[:, :, k][:, :, None, :] for k in range(ADA_CHUNKS)]

        if kind == 0:
            lam_init = 0.8 - 0.6 * math.exp(-0.3 * i)
            w = da_w_in[j]
            w = jnp.concatenate([w[:, :d] * (DA_HEAD_DIM ** -0.5 * math.log2(math.e)), w[:, d:]], 1).astype(BF16)
            p_all = modulated_projection(xall, mod[1], mod[0], w, rope=rope, n_rope=2 * d,
                                         n_lat_tiles=n_lat_tiles)
            o = _flash_both(p_all, da_lambda[j], da_subln[j], lam_init, n, lc)
            xall = out_projection_post_norm([o], da_w_o[j].astype(BF16), xall, mod[2],
                                            ln_g[i, 0], ln_b[i, 0], n_lat_tiles)
        elif kind == 1:
            w = gdn_w_in[j]
            n_main = 2 * GDN_KEY_DIM + 2 * GDN_VAL_DIM
            p_main, gates_raw = modulated_projection(
                xall, mod[1], mod[0], w[:, :n_main].astype(BF16),
                w_small=_pad_cols(w[:, n_main:], LANES).astype(BF16), n_lat_tiles=n_lat_tiles)
            alog_row = _pad_cols(gdn_a_log[j].reshape(1, -1), LANES)
            dtb_row = _pad_cols(gdn_dt_bias[j].reshape(1, -1), LANES)
            qn, kn, vv, gates = gdn_prep(p_main, gates_raw, gdn_conv[j], alog_row, dtb_row, n_lat_tiles)
            hv, hk = GDN_V_HEADS, GDN_K_HEADS
            def per_khead(t):
                t = t.reshape(b, s, 2, hk, 2)
                return jnp.transpose(t, (0, 3, 1, 2, 4)).reshape(b, hk, s, 4)
            gcol = jnp.concatenate([per_khead(gates[..., :2 * hv]), per_khead(gates[..., 2 * hv:4 * hv])], -1)
            grow = jnp.swapaxes(gcol, 2, 3)
            zeros = jnp.zeros((b, hv, GDN_HEAD_DIM, GDN_HEAD_DIM), F32)
            of, ob, scf, scb = gdn_scan(qn, kn, vv, gcol, grow, zeros, zeros, nblk=lc // SCAN_TILE,
                                        off=n // SCAN_TILE)
            of, ob, _, _ = gdn_scan(qn, kn, vv, gcol, grow, scf, scb, nblk=n // SCAN_TILE, off=0,
                                    prev=(of, ob))
            xall = out_projection_post_norm([of, ob, p_main, gdn_norm[j]], gdn_w_o[j].astype(BF16), xall,
                                            mod[2], ln_g[i, 0], ln_b[i, 0], n_lat_tiles,
                                            gated=(GDN_V_HEADS, GDN_HEAD_DIM, 2))
        else:
            w = gla_w_in[j]
            n_main = 2 * GLA_KEY_DIM + 2 * GLA_VAL_DIM
            p_main, gr = modulated_projection(
                xall, mod[1], mod[0], w[:, :n_main].astype(BF16),
                w_small=_pad_cols(w[:, n_main:], LANES).astype(BF16), n_lat_tiles=n_lat_tiles)
            wg = gla_w_gate[j]
            wg_pad = jnp.zeros((2, LANES, GLA_KEY_DIM), F32)
            wg_pad = wg_pad.at[0, :GLA_GATE_RANK].set(wg[0]).at[1, GLA_GATE_RANK:2 * GLA_GATE_RANK].set(wg[1])
            zeros = jnp.zeros((b, GLA_HEADS, GLA_DV, GLA_DK), F32)
            of, ob, scf, scb = gla_scan(p_main, gr, wg_pad, gla_gate_b[j], zeros, zeros,
                                        nblk=lc // SCAN_TILE, off=n // SCAN_TILE)
            of, ob, _, _ = gla_scan(p_main, gr, wg_pad, gla_gate_b[j], scf, scb, nblk=n // SCAN_TILE, off=0,
                                    prev=(of, ob))
            xall = out_projection_post_norm([of, ob, p_main, gla_norm[j]], gla_w_o[j].astype(BF16), xall,
                                            mod[2], ln_g[i, 0], ln_b[i, 0], n_lat_tiles,
                                            gated=(GLA_HEADS, GLA_DV, 2))

        h, wd = moe_route(xall, mod[4], mod[3], moe_router[i].T, moe_router_b[i].reshape(-1, 1), n_lat_tiles)
        popularity = jnp.sum((wd[..., :N_EXPERTS] != 0.0).astype(jnp.int32), axis=(0, 1))
        order = jnp.argsort(popularity).astype(jnp.int32)
        f = moe_grouped_experts(order, h, wd, moe_w_gu[i].astype(BF16), moe_w_dn[i].astype(BF16),
                                moe_ws_gu[i].astype(BF16), moe_ws_dn[i].astype(BF16), moe_tm)
        xall = post_norm_rows(xall, f, mod[5], ln_g[i, 1], ln_b[i, 1], n_lat_tiles,
                              rows_out=n if i == depth - 1 else None)

    return xall
```

```python
import functools
import math

import jax
import jax.numpy as jnp
from jax import lax
from jax.experimental import pallas as pl
from jax.experimental.pallas import tpu as pltpu

F32 = jnp.float32
BF16 = jnp.bfloat16

D_MODEL = 1024
DEPTH = 4
GRID_W = 64
N_MIXERS = 3
DN_ALPHA = (2 * DEPTH) ** 0.25
LN_EPS = 1e-5
NORM_EPS = 1e-6
ADA_CHUNKS = 6

DA_HEADS = 8
DA_HEAD_DIM = 64
ROPE_BASE = 10000.0

GDN_K_HEADS = 8
GDN_V_HEADS = 16
GDN_HEAD_DIM = 128
GDN_KEY_DIM = 1024
GDN_VAL_DIM = 2048
GDN_CONV_W = 5
CHUNK = 64

GLA_HEADS = 4
GLA_KEY_DIM = 512
GLA_VAL_DIM = 1024
GLA_DK = 128
GLA_DV = 256
GLA_GATE_RANK = 16
GLA_TAU = 16.0

N_EXPERTS = 64
TOP_K = 8
EXPERT_FF = 256
ROUTE_SCALE = 2.5

LANES = 128
ROW_TILE = 256
SCAN_TILE = 256
GDN_HEADS_PER_STEP = 4
HALO = 16
VMEM_LIMIT = 56 * 1024 * 1024


def _cparams(sem, flags=None):
    return pltpu.CompilerParams(dimension_semantics=sem, vmem_limit_bytes=VMEM_LIMIT, flags=flags)


def _dot(a, b):
    return jnp.dot(a, b, preferred_element_type=F32)


def _dot_nt(a, b):
    return lax.dot_general(a, b, (((1,), (1,)), ((), ())), preferred_element_type=F32)


def _dot_tn(a, b):
    return lax.dot_general(a, b, (((0,), (0,)), ((), ())), preferred_element_type=F32)


def _split2(a):
    hi = a.astype(BF16)
    lo = (a - hi.astype(F32)).astype(BF16)
    return hi, lo


def _split3(a):
    a0 = a.astype(BF16)
    r = a - a0.astype(F32)
    a1 = r.astype(BF16)
    a2 = (r - a1.astype(F32)).astype(BF16)
    return a0, a1, a2


def _dot3(a, b):
    a0, a1 = _split2(a)
    b0, b1 = _split2(b)
    return _dot(a0, b0) + (_dot(a0, b1) + _dot(a1, b0))


def _silu(x):
    return x * jax.nn.sigmoid(x)


def _softplus(x):
    return jnp.maximum(x, 0.0) + jnp.log(1.0 + jnp.exp(-jnp.abs(x)))


def _ada_kernel(c_ref, w_ref, b_ref, o_ref):
    s = _silu(c_ref[...])
    o_ref[0] = _dot3(s, w_ref[0]) + b_ref[0]


def ada_modulation(c8, ada_w, ada_b):
    depth, d, n = ada_w.shape
    tn = 1536
    return pl.pallas_call(
        _ada_kernel,
        out_shape=jax.ShapeDtypeStruct((depth, 8, n), F32),
        grid=(depth, n // tn),
        in_specs=[
            pl.BlockSpec((8, d), lambda i, j: (0, 0)),
            pl.BlockSpec((1, d, tn), lambda i, j: (i, 0, j)),
            pl.BlockSpec((1, 1, tn), lambda i, j: (i, 0, j)),
        ],
        out_specs=pl.BlockSpec((1, 8, tn), lambda i, j: (i, 0, j)),
        compiler_params=_cparams(("arbitrary", "arbitrary")),
        name="ada_modulation",
    )(c8, ada_w, ada_b.reshape(depth, 1, n))


def _proj_kernel(*refs, n_main, n_rope, has_small, cn, head_major):
    it = iter(refs)
    x_ref, sc_ref, sh_ref, w_ref = next(it), next(it), next(it), next(it)
    ws_ref = next(it) if has_small else None
    cos_ref = next(it) if n_rope else None
    sin_ref = next(it) if n_rope else None
    o_ref = next(it)
    os_ref = next(it) if has_small else None

    h = x_ref[0] * (1.0 + sc_ref[0, 0]) + sh_ref[0, 0]
    hb = h.astype(BF16)
    tm = hb.shape[0]
    if n_rope:
        cos = cos_ref[...]
        sin = sin_ref[...]
        lane = lax.broadcasted_iota(jnp.int32, (tm, LANES), 1)
        low_half = (lane & 32) == 0
    for j in range(n_main // cn):
        p = _dot(hb, w_ref[:, j * cn:(j + 1) * cn])
        for g in range(cn // LANES):
            pg = p[:, g * LANES:(g + 1) * LANES]
            col = j * cn + g * LANES
            if col < n_rope:
                partner = jnp.where(low_half, pltpu.roll(pg, LANES - 32, 1), pltpu.roll(pg, 32, 1))
                pg = pg * cos + partner * sin
            if head_major:
                o_ref[0, col // LANES] = pg.astype(BF16)
            else:
                o_ref[0, :, col:col + LANES] = pg.astype(BF16)
    if has_small:
        os_ref[0] = _dot(hb, ws_ref[...])


def modulated_projection(x, sc, sh, w, w_small=None, rope=None, n_rope=0, n_lat_tiles=1, head_major=False):
    b, s, d = x.shape
    n_main = w.shape[1]
    tm = ROW_TILE
    grid = (b, s // tm)
    mod_spec = pl.BlockSpec((1, 1, 1, d), lambda bi, i: (bi, i // n_lat_tiles, 0, 0))
    in_specs = [pl.BlockSpec((1, tm, d), lambda bi, i: (bi, i, 0)), mod_spec, mod_spec,
                pl.BlockSpec((d, n_main), lambda bi, i: (0, 0))]
    args = [x, sc, sh, w]
    if head_major:
        out_shape = [jax.ShapeDtypeStruct((b, n_main // LANES, s, LANES), BF16)]
        out_specs = [pl.BlockSpec((1, n_main // LANES, tm, LANES), lambda bi, i: (bi, 0, i, 0))]
    else:
        out_shape = [jax.ShapeDtypeStruct((b, s, n_main), BF16)]
        out_specs = [pl.BlockSpec((1, tm, n_main), lambda bi, i: (bi, i, 0))]
    if w_small is not None:
        in_specs.append(pl.BlockSpec((d, LANES), lambda bi, i: (0, 0)))
        args.append(w_small)
        out_shape.append(jax.ShapeDtypeStruct((b, s, LANES), F32))
        out_specs.append(pl.BlockSpec((1, tm, LANES), lambda bi, i: (bi, i, 0)))
    if n_rope:
        tab = pl.BlockSpec((tm, LANES), lambda bi, i: (i, 0))
        in_specs += [tab, tab]
        args += [rope[0], rope[1]]
    kern = functools.partial(_proj_kernel, n_main=n_main, n_rope=n_rope,
                             has_small=w_small is not None, cn=512, head_major=head_major)
    out = pl.pallas_call(
        kern, out_shape=out_shape, grid=grid, in_specs=in_specs, out_specs=out_specs,
        compiler_params=_cparams(("arbitrary", "arbitrary")), name="modulated_projection",
    )(*args)
    return out if w_small is not None else out[0]


NEG_INIT = -1e30
FLASH_ROW_BLOCK = 128


def _flash_kernel(lam_ref, q_ref, k_ref, v_ref, sub_ref, *rest, lam_init, nk, aliased):
    if aliased:
        rest = rest[1:]
    o_ref, s_a, s_b, mt_a, mt_b, m_sc, l_sc, acc_sc = rest
    ki = pl.program_id(3)
    tq, tk = s_a.shape[1], s_a.shape[2]

    @pl.when(ki == 0)
    def _():
        m_sc[...] = jnp.full(m_sc.shape, NEG_INIT, F32)
        l_sc[...] = jnp.zeros(l_sc.shape, F32)
        acc_sc[...] = jnp.zeros(acc_sc.shape, F32)
        s_b[...] = jnp.full(s_b.shape, NEG_INIT, F32)
        mt_b[...] = jnp.full(mt_b.shape, NEG_INIT, F32)

    def step(s_w, mt_w, s_r, mt_r):
        q = q_ref[0, 0]
        k = k_ref[0, 0]
        v = v_ref[0, 0]
        hd = DA_HEAD_DIM
        rb = min(tq, FLASH_ROW_BLOCK)
        kc = 2 * LANES
        for c in range(2):
            kcm = k[:, c * hd:(c + 1) * hd]
            for r0 in range(0, tq, rb):
                rows = slice(r0, r0 + rb)
                s = _dot_nt(q[rows, c * hd:(c + 1) * hd], kcm)
                s_w[c, rows, :] = s
                mt_w[c, rows, :] = jnp.broadcast_to(jnp.max(s, axis=-1, keepdims=True), (rb, LANES))
                m_prev = m_sc[c, rows, :]
                m_new = jnp.maximum(m_prev, mt_r[c, rows, :])
                alpha = jnp.exp2(m_prev - m_new)
                m2 = jnp.concatenate([m_new, m_new], axis=-1)
                lsum = None
                pv = None
                for j in range(tk // kc):
                    pj = jnp.exp2(s_r[c, rows, j * kc:(j + 1) * kc] - m2)
                    lj = pj[:, :LANES] + pj[:, LANES:]
                    lsum = lj if lsum is None else lsum + lj
                    t = _dot(pj.astype(BF16), v[j * kc:(j + 1) * kc, :])
                    pv = t if pv is None else pv + t
                l_sc[c, rows, :] = alpha * l_sc[c, rows, :] + lsum
                acc_sc[c, rows, :] = alpha * acc_sc[c, rows, :] + pv
                m_sc[c, rows, :] = m_new

    @pl.when(ki % 2 == 0)
    def _():
        step(s_a, mt_a, s_b, mt_b)

    @pl.when(ki % 2 == 1)
    def _():
        step(s_b, mt_b, s_a, mt_a)

    @pl.when(ki == nk)
    def _():
        lv = lam_ref[...]
        lam = (jnp.exp(jnp.sum(lv[0:1] * lv[1:2], axis=-1, keepdims=True))
               - jnp.exp(jnp.sum(lv[2:3] * lv[3:4], axis=-1, keepdims=True)) + lam_init)
        l0 = jnp.sum(l_sc[0], axis=-1, keepdims=True)
        l1 = jnp.sum(l_sc[1], axis=-1, keepdims=True)
        o = acc_sc[0] / l0 - lam * (acc_sc[1] / l1)
        ms = jnp.mean(o * o, axis=-1, keepdims=True)
        o = o * lax.rsqrt(ms + NORM_EPS) * sub_ref[...] * (1.0 - lam_init)
        o_ref[0] = o.astype(BF16)


def diff_flash_attention(p_all, lam_vec, subln, lam_init, *, tq, tk, nq, nk, q_off, k_off, prev=None):
    b, _, s, _ = p_all.shape
    hh = DA_HEADS
    in_specs = [
        pl.BlockSpec((4, DA_HEAD_DIM), lambda bi, h, qi, ki: (0, 0)),
        pl.BlockSpec((1, 1, tq, LANES), lambda bi, h, qi, ki: (bi, h, qi + q_off, 0)),
        pl.BlockSpec((1, 1, tk, LANES), lambda bi, h, qi, ki: (bi, hh + h, jnp.minimum(ki, nk - 1) + k_off, 0)),
        pl.BlockSpec((1, 1, tk, LANES), lambda bi, h, qi, ki: (bi, 2 * hh + h, jnp.maximum(ki - 1, 0) + k_off, 0)),
        pl.BlockSpec((1, LANES), lambda bi, h, qi, ki: (0, 0)),
    ]
    args = [lam_vec, p_all, p_all, p_all, subln.reshape(1, LANES)]
    aliases = {}
    if prev is not None:
        in_specs.append(pl.BlockSpec(memory_space=pl.ANY))
        args.append(prev)
        aliases = {5: 0}
    kern = functools.partial(_flash_kernel, lam_init=lam_init, nk=nk, aliased=prev is not None)
    return pl.pallas_call(
        kern,
        out_shape=jax.ShapeDtypeStruct((b, s, hh * LANES), BF16),
        grid=(b, hh, nq, nk + 1),
        in_specs=in_specs,
        out_specs=pl.BlockSpec((1, tq, LANES), lambda bi, h, qi, ki: (bi, qi + q_off, h)),
        scratch_shapes=[pltpu.VMEM((2, tq, tk), F32), pltpu.VMEM((2, tq, tk), F32),
                        pltpu.VMEM((2, tq, LANES), F32), pltpu.VMEM((2, tq, LANES), F32),
                        pltpu.VMEM((2, tq, LANES), F32), pltpu.VMEM((2, tq, LANES), F32),
                        pltpu.VMEM((2, tq, LANES), F32)],
        input_output_aliases=aliases,
        compiler_params=_cparams(("arbitrary", "arbitrary", "arbitrary", "arbitrary")),
        name="diff_flash_attention",
    )(*args)


def _post_norm(x, y, gate, lg, lb):
    r = DN_ALPHA * x + gate * y
    mu = jnp.mean(r, axis=-1, keepdims=True)
    rc = r - mu
    var = jnp.mean(rc * rc, axis=-1, keepdims=True)
    return rc * lax.rsqrt(var + LN_EPS) * lg + lb


def _outproj_kernel(*refs, gated, n_heads, dh):
    if gated:
        of_ref, ob_ref, z_ref, ng_ref, w_ref, x_ref, gate_ref, lg_ref, lb_ref, out_ref = refs
        y = None
        for h in range(n_heads):
            sl = slice(h * dh, (h + 1) * dh)
            o = of_ref[0, :, sl].astype(F32) + ob_ref[0, :, sl].astype(F32)
            ms = jnp.mean(o * o, axis=-1, keepdims=True)
            o = o * lax.rsqrt(ms + NORM_EPS) * ng_ref[...] * _silu(z_ref[0, :, sl].astype(F32))
            t = _dot(o.astype(BF16), w_ref[sl, :])
            y = t if y is None else y + t
    else:
        o_ref, w_ref, x_ref, gate_ref, lg_ref, lb_ref, out_ref = refs
        y = _dot(o_ref[0], w_ref[...])
    out_ref[0] = _post_norm(x_ref[0], y, gate_ref[0, 0], lg_ref[...], lb_ref[...])


def out_projection_post_norm(o_args, w_o, x, gate, lg, lb, n_lat_tiles, gated=None):
    b, s, d = x.shape
    kdim = w_o.shape[0]
    tm = ROW_TILE
    row = lambda width, cb=0: pl.BlockSpec((1, tm, width), lambda bi, i: (bi, i, cb))
    full = lambda shape: pl.BlockSpec(shape, lambda bi, i: (0,) * len(shape))
    if gated is None:
        in_specs = [row(kdim)]
        args = list(o_args)
        kern = functools.partial(_outproj_kernel, gated=False, n_heads=0, dh=0)
    else:
        n_heads, dh, z_cb = gated
        of, ob, z, ng = o_args
        in_specs = [row(kdim), row(kdim), row(kdim, z_cb), full((1, dh))]
        args = [of, ob, z, ng.reshape(1, dh)]
        kern = functools.partial(_outproj_kernel, gated=True, n_heads=n_heads, dh=dh)
    in_specs += [full((kdim, d)), row(d),
                 pl.BlockSpec((1, 1, 1, d), lambda bi, i: (bi, i // n_lat_tiles, 0, 0)),
                 full((1, d)), full((1, d))]
    args += [w_o, x, gate, lg.reshape(1, d), lb.reshape(1, d)]
    return pl.pallas_call(
        kern, out_shape=jax.ShapeDtypeStruct((b, s, d), F32), grid=(b, s // tm),
        in_specs=in_specs, out_specs=row(d),
        compiler_params=_cparams(("arbitrary", "arbitrary")), name="out_projection_post_norm",
    )(*args)


def _gdn_prep_kernel(cur_ref, prev_ref, next_ref, gate_ref, cw_ref, alog_ref, dtb_ref,
                     q_ref, k_ref, v_ref, g_ref, ext, *, n_lat_tiles, n_tiles, cn):
    i = pl.program_id(1)
    tm = cur_ref.shape[1]
    first = jnp.logical_or(i == 0, i == n_lat_tiles)
    last = jnp.logical_or(i == n_lat_tiles - 1, i == n_tiles - 1)
    pmask = jnp.where(first, 0.0, 1.0)
    nmask = jnp.where(last, 0.0, 1.0)
    pad = GDN_CONV_W // 2
    dh = GDN_HEAD_DIM
    n_qk = 2 * GDN_KEY_DIM
    for cc in range(cur_ref.shape[2] // cn):
        cs = slice(cc * cn, (cc + 1) * cn)
        ext[0:HALO, :] = prev_ref[0, :, cs].astype(F32) * pmask
        ext[HALO:HALO + tm, :] = cur_ref[0, :, cs].astype(F32)
        ext[HALO + tm:2 * HALO + tm, :] = next_ref[0, :, cs].astype(F32) * nmask
        acc = None
        for j in range(GDN_CONV_W):
            t = ext[pl.ds(HALO - pad + j, tm), :] * cw_ref[j:j + 1, cs]
            acc = t if acc is None else acc + t
        y = _silu(acc)
        for g in range(cn // dh):
            col = cc * cn + g * dh
            yg = y[:, g * dh:(g + 1) * dh]
            if col < n_qk:
                yg = yg * lax.rsqrt(jnp.sum(yg * yg, axis=-1, keepdims=True) + NORM_EPS)
                if col < GDN_KEY_DIM:
                    q_ref[0, col // dh] = (yg * (dh ** -0.5)).astype(BF16)
                else:
                    k_ref[0, (col - GDN_KEY_DIM) // dh] = yg.astype(BF16)
            else:
                v_ref[0, (col - n_qk) // dh] = yg.astype(BF16)
    a = gate_ref[0]
    lane = lax.broadcasted_iota(jnp.int32, a.shape, 1)
    gdec = -jnp.exp(alog_ref[...]) * _softplus(a + dtb_ref[...])
    g_ref[0] = jnp.where(lane < 2 * GDN_V_HEADS, gdec, jax.nn.sigmoid(a))


def gdn_prep(p_main, gates_raw, conv_w, alog_row, dtb_row, n_lat_tiles):
    b, s, _ = p_main.shape
    tm = ROW_TILE
    nt = s // tm
    nch = 2 * GDN_KEY_DIM + GDN_VAL_DIM
    hpt = tm // HALO
    nh = s // HALO
    kern = functools.partial(_gdn_prep_kernel, n_lat_tiles=n_lat_tiles, n_tiles=nt, cn=512)
    head_out = lambda nheads: pl.BlockSpec((1, nheads, tm, GDN_HEAD_DIM), lambda bi, i: (bi, 0, i, 0))
    return pl.pallas_call(
        kern,
        out_shape=[jax.ShapeDtypeStruct((b, GDN_K_HEADS, s, GDN_HEAD_DIM), BF16),
                   jax.ShapeDtypeStruct((b, GDN_K_HEADS, s, GDN_HEAD_DIM), BF16),
                   jax.ShapeDtypeStruct((b, GDN_V_HEADS, s, GDN_HEAD_DIM), BF16),
                   jax.ShapeDtypeStruct((b, s, LANES), F32)],
        grid=(b, nt),
        in_specs=[
            pl.BlockSpec((1, tm, nch), lambda bi, i: (bi, i, 0)),
            pl.BlockSpec((1, HALO, nch), lambda bi, i: (bi, jnp.maximum(i * hpt - 1, 0), 0)),
            pl.BlockSpec((1, HALO, nch), lambda bi, i: (bi, jnp.minimum((i + 1) * hpt, nh - 1), 0)),
            pl.BlockSpec((1, tm, LANES), lambda bi, i: (bi, i, 0)),
            pl.BlockSpec((GDN_CONV_W, nch), lambda bi, i: (0, 0)),
            pl.BlockSpec((1, LANES), lambda bi, i: (0, 0)),
            pl.BlockSpec((1, LANES), lambda bi, i: (0, 0)),
        ],
        out_specs=[head_out(GDN_K_HEADS), head_out(GDN_K_HEADS), head_out(GDN_V_HEADS),
                   pl.BlockSpec((1, tm, LANES), lambda bi, i: (bi, i, 0))],
        scratch_shapes=[pltpu.VMEM((tm + 2 * HALO, 512), F32)],
        compiler_params=_cparams(("arbitrary", "arbitrary")),
        name="gdn_prep",
    )(p_main, p_main, p_main, gates_raw, conv_w, alog_row, dtb_row)


def _chunk_masks(n, reverse):
    r = lax.broadcasted_iota(jnp.int32, (n, n), 0)
    c = lax.broadcasted_iota(jnp.int32, (n, n), 1)
    if reverse:
        return r <= c, r < c
    return r >= c, r > c


def _block_cumsum_mats(ct, reverse):
    r = lax.broadcasted_iota(jnp.int32, (ct, ct), 0)
    c = lax.broadcasted_iota(jnp.int32, (ct, ct), 1)
    same = (r // CHUNK) == (c // CHUNK)
    lower = jnp.logical_and(same, c <= r)
    upper = jnp.logical_and(same, c >= r)
    lo = jnp.where(lower, 1.0, 0.0).astype(BF16)
    up = jnp.where(upper, 1.0, 0.0).astype(BF16)
    return (up, lo) if reverse else (lo, up)


def _cumsum_cols(mat, x):
    x0, x1, x2 = _split3(x)
    return _dot(mat, x0) + (_dot(mat, x1) + _dot(mat, x2))


def _cumsum_rows(x, mat):
    x0, x1, x2 = _split3(x)
    return _dot(x0, mat) + (_dot(x1, mat) + _dot(x2, mat))


def _gdn_block(dirs):
    dh = GDN_HEAD_DIM
    eye = jnp.where(lax.broadcasted_iota(jnp.int32, (CHUNK, CHUNK), 0)
                    == lax.broadcasted_iota(jnp.int32, (CHUNK, CHUNK), 1), 1.0, 0.0).astype(F32)
    chains = []
    for q_ref, k_ref, v_ref, gc_ref, gr_ref, o_ref, s_sc, reverse in dirs:
        ct = q_ref.shape[2]
        goff = 2 if reverse else 0
        m_col, m_row = _block_cumsum_mats(ct, reverse)
        tril, strict = _chunk_masks(CHUNK, reverse)
        for kh in range(q_ref.shape[1]):
            gcol = gc_ref[0, kh]
            grow = gr_ref[0, kh]
            gcum_c = _cumsum_cols(m_col, gcol)
            gcum_r = _cumsum_rows(grow, m_row)
            for c in range(ct // CHUNK):
                rs = slice(c * CHUNK, (c + 1) * CHUNK)
                last = c * CHUNK if reverse else (c + 1) * CHUNK - 1
                q = q_ref[0, kh, rs, :]
                k = k_ref[0, kh, rs, :]
                kk = _dot_nt(k, k)
                qk = _dot_nt(q, k)
                for hh in range(2):
                    gi = goff + hh
                    chains.append(dict(
                        q=q, k=k, kk=kk, qk=qk, rs=rs, c=c, hh=2 * kh + hh, reverse=reverse, tril=tril,
                        strict=strict, v_ref=v_ref, o_ref=o_ref, s_sc=s_sc,
                        gc=gcum_c[rs, gi:gi + 1], gr=gcum_r[gi:gi + 1, rs],
                        glast=gcum_c[last:last + 1, gi:gi + 1], beta=gcol[rs, 4 + gi:5 + gi]))
    for ch in chains:
        tril = ch["tril"]
        ch["decay"] = jnp.where(tril, jnp.exp(jnp.where(tril, ch["gc"] - ch["gr"], 0.0)), 0.0)
        a = -jnp.where(ch["strict"], ch["kk"] * ch["beta"] * ch["decay"], 0.0)
        ch["tmat"] = eye + a
        ch["pw"] = a
    for _ in range(5):
        for ch in chains:
            pwb = ch["pw"].astype(BF16)
            ch["pw"] = _dot(pwb, pwb)
        for ch in chains:
            ch["tmat"] = ch["tmat"] + _dot(ch["tmat"].astype(BF16), ch["pw"].astype(BF16))
    for ch in chains:
        beta = ch["beta"]
        eg = jnp.exp(ch["gc"])
        vb = ch["v_ref"][0, ch["hh"], ch["rs"], :].astype(F32) * beta
        kbg = ch["k"].astype(F32) * (beta * eg)
        uw = _dot(ch["tmat"].astype(BF16), jnp.concatenate([vb, kbg], axis=-1).astype(BF16))
        ch["u"] = uw[:, :dh]
        ch["w"] = uw[:, dh:].astype(BF16)
        ch["eg"] = eg
        ch["attn"] = jnp.where(ch["tril"], ch["qk"] * ch["decay"], 0.0).astype(BF16)
    nchunk = max(ch["c"] for ch in chains) + 1
    for step in range(nchunk):
        cur = [ch for ch in chains if ch["c"] == (nchunk - 1 - step if ch["reverse"] else step)]
        for ch in cur:
            state = ch["s_sc"][ch["hh"]]
            sb = state.astype(BF16)
            ch["state"] = state
            ch["ws"] = _dot(ch["w"], sb)
            ch["qs"] = _dot(ch["q"], sb)
        for ch in cur:
            v_new = ch["u"] - ch["ws"]
            ch["kgv"] = (v_new * jnp.exp(ch["glast"] - ch["gc"])).astype(BF16)
            ch["o"] = ch["eg"] * ch["qs"] + _dot(ch["attn"], v_new.astype(BF16))
        for ch in cur:
            ch["s_sc"][ch["hh"]] = ch["state"] * jnp.exp(ch["glast"]) + _dot_tn(ch["k"], ch["kgv"])
            ch["o_ref"][0, ch["rs"], ch["hh"] * dh:(ch["hh"] + 1) * dh] = ch["o"].astype(BF16)


def _gdn_scan_kernel(qf, kf, vf, gcf, grf, qb, kb, vb, gcb, grb, s0f, s0b, *rest, nblk, aliased):
    if aliased:
        rest = rest[2:]
    of_ref, ob_ref, sff, sfb, sf_sc, sb_sc = rest
    j = pl.program_id(2)

    @pl.when(j == 0)
    def _():
        sf_sc[...] = s0f[0]
        sb_sc[...] = s0b[0]

    _gdn_block([(qf, kf, vf, gcf, grf, of_ref, sf_sc, False),
                (qb, kb, vb, gcb, grb, ob_ref, sb_sc, True)])

    @pl.when(j == nblk - 1)
    def _():
        sff[0] = sf_sc[...]
        sfb[0] = sb_sc[...]


def gdn_scan(qn, kn, vv, gcol, grow, s0f, s0b, *, nblk, off, prev=None):
    b, _, s, dh = qn.shape
    ct = SCAN_TILE
    g = GDN_HEADS_PER_STEP
    fwd = lambda j: j + off
    bwd = lambda j: nblk - 1 - j + off
    def specs(pos):
        return [
            pl.BlockSpec((1, g, ct, dh), lambda bi, h, j: (bi, h, pos(j), 0)),
            pl.BlockSpec((1, g, ct, dh), lambda bi, h, j: (bi, h, pos(j), 0)),
            pl.BlockSpec((1, 2 * g, ct, dh), lambda bi, h, j: (bi, h, pos(j), 0)),
            pl.BlockSpec((1, g, ct, 8), lambda bi, h, j: (bi, h, pos(j), 0)),
            pl.BlockSpec((1, g, 8, ct), lambda bi, h, j: (bi, h, 0, pos(j))),
        ]
    st_spec = pl.BlockSpec((1, 2 * g, dh, dh), lambda bi, h, j: (bi, h, 0, 0))
    in_specs = specs(fwd) + specs(bwd) + [st_spec, st_spec]
    args = [qn, kn, vv, gcol, grow] * 2 + [s0f, s0b]
    aliases = {}
    if prev is not None:
        in_specs += [pl.BlockSpec(memory_space=pl.ANY)] * 2
        args += list(prev)
        aliases = {12: 0, 13: 1}
    o_shape = jax.ShapeDtypeStruct((b, s, GDN_VAL_DIM), BF16)
    st_shape = jax.ShapeDtypeStruct((b, GDN_V_HEADS, dh, dh), F32)
    kern = functools.partial(_gdn_scan_kernel, nblk=nblk, aliased=prev is not None)
    return pl.pallas_call(
        kern,
        out_shape=[o_shape, o_shape, st_shape, st_shape],
        grid=(b, GDN_K_HEADS // g, nblk),
        in_specs=in_specs,
        out_specs=[pl.BlockSpec((1, ct, 2 * g * dh), lambda bi, h, j: (bi, fwd(j), h)),
                   pl.BlockSpec((1, ct, 2 * g * dh), lambda bi, h, j: (bi, bwd(j), h)),
                   st_spec, st_spec],
        scratch_shapes=[pltpu.VMEM((2 * g, dh, dh), F32), pltpu.VMEM((2 * g, dh, dh), F32)],
        input_output_aliases=aliases,
        compiler_params=_cparams(("arbitrary", "arbitrary", "arbitrary")),
        name="gdn_scan",
    )(*args)


def _gla_block(dirs, wg_ref, gb_ref):
    chains = []
    pre = []
    for q_ref, k_ref, v_ref, gr_ref, o_ref, st_sc, z in dirs:
        pre.append(_dot3(gr_ref[0], wg_ref[z]) + gb_ref[z:z + 1, :])
    for (q_ref, k_ref, v_ref, gr_ref, o_ref, st_sc, z), logit in zip(dirs, pre):
        reverse = z == 1
        ct = q_ref.shape[1]
        glog = -_softplus(-logit) / GLA_TAU
        m_col, _ = _block_cumsum_mats(ct, reverse)
        bcum = _cumsum_cols(m_col, glog)
        tril, _ = _chunk_masks(CHUNK, reverse)
        for c in range(ct // CHUNK):
            rs = slice(c * CHUNK, (c + 1) * CHUNK)
            last = c * CHUNK if reverse else (c + 1) * CHUNK - 1
            bc = bcum[rs, :]
            bl = bcum[last:last + 1, :]
            qf = q_ref[0, rs, :].astype(F32) * (GLA_DK ** -0.5)
            kf = k_ref[0, rs, :].astype(F32)
            chains.append(dict(
                c=c, rs=rs, reverse=reverse, tril=tril, o_ref=o_ref, st_sc=st_sc, v=v_ref[0, rs, :],
                qe=(qf * jnp.exp(bc)).astype(BF16), ke=(kf * jnp.exp(-bc)).astype(BF16),
                kg=(kf * jnp.exp(bl - bc)).astype(BF16), gl=jnp.exp(bl)))
    for ch in chains:
        ch["attn"] = jnp.where(ch["tril"], _dot_nt(ch["qe"], ch["ke"]), 0.0).astype(BF16)
    for ch in chains:
        ch["o"] = _dot(ch["attn"], ch["v"])
        ch["kv"] = _dot_tn(ch["v"], ch["kg"])
    nchunk = max(ch["c"] for ch in chains) + 1
    for step in range(nchunk):
        cur = [ch for ch in chains if ch["c"] == (nchunk - 1 - step if ch["reverse"] else step)]
        for ch in cur:
            st = ch["st_sc"][...]
            ch["o"] = ch["o"] + _dot_nt(ch["qe"], st.astype(BF16))
            ch["st_sc"][...] = st * ch["gl"] + ch["kv"]
        for ch in cur:
            ch["o_ref"][0, ch["rs"], :] = ch["o"].astype(BF16)


def _gla_scan_kernel(qf, kf, vf, grf, qb, kb, vb, grb, wg, gb, s0f, s0b, *rest, nblk, aliased):
    if aliased:
        rest = rest[2:]
    of_ref, ob_ref, sff, sfb, sf_sc, sb_sc = rest
    j = pl.program_id(2)

    @pl.when(j == 0)
    def _():
        sf_sc[...] = s0f[0, 0]
        sb_sc[...] = s0b[0, 0]

    _gla_block([(qf, kf, vf, grf, of_ref, sf_sc, 0), (qb, kb, vb, grb, ob_ref, sb_sc, 1)], wg, gb)

    @pl.when(j == nblk - 1)
    def _():
        sff[0, 0] = sf_sc[...]
        sfb[0, 0] = sb_sc[...]


def gla_scan(p_main, gr, wg_pad, gate_b, s0f, s0b, *, nblk, off, prev=None):
    b, s, _ = p_main.shape
    ct = SCAN_TILE
    nh, dk, dv = GLA_HEADS, GLA_DK, GLA_DV
    fwd = lambda j: j + off
    bwd = lambda j: nblk - 1 - j + off
    def specs(pos):
        return [
            pl.BlockSpec((1, ct, dk), lambda bi, h, j: (bi, pos(j), h)),
            pl.BlockSpec((1, ct, dk), lambda bi, h, j: (bi, pos(j), nh + h)),
            pl.BlockSpec((1, ct, dv), lambda bi, h, j: (bi, pos(j), nh + h)),
            pl.BlockSpec((1, ct, LANES), lambda bi, h, j: (bi, pos(j), 0)),
        ]
    st_spec = pl.BlockSpec((1, 1, dv, dk), lambda bi, h, j: (bi, h, 0, 0))
    in_specs = specs(fwd) + specs(bwd) + [
        pl.BlockSpec((2, LANES, dk), lambda bi, h, j: (0, 0, h)),
        pl.BlockSpec((2, dk), lambda bi, h, j: (0, h)),
        st_spec, st_spec]
    args = [p_main, p_main, p_main, gr] * 2 + [wg_pad, gate_b, s0f, s0b]
    aliases = {}
    if prev is not None:
        in_specs += [pl.BlockSpec(memory_space=pl.ANY)] * 2
        args += list(prev)
        aliases = {12: 0, 13: 1}
    o_shape = jax.ShapeDtypeStruct((b, s, GLA_VAL_DIM), BF16)
    st_shape = jax.ShapeDtypeStruct((b, nh, dv, dk), F32)
    kern = functools.partial(_gla_scan_kernel, nblk=nblk, aliased=prev is not None)
    return pl.pallas_call(
        kern,
        out_shape=[o_shape, o_shape, st_shape, st_shape],
        grid=(b, nh, nblk),
        in_specs=in_specs,
        out_specs=[pl.BlockSpec((1, ct, dv), lambda bi, h, j: (bi, fwd(j), h)),
                   pl.BlockSpec((1, ct, dv), lambda bi, h, j: (bi, bwd(j), h)),
                   st_spec, st_spec],
        scratch_shapes=[pltpu.VMEM((dv, dk), F32), pltpu.VMEM((dv, dk), F32)],
        input_output_aliases=aliases,
        compiler_params=_cparams(("arbitrary", "arbitrary", "arbitrary")),
        name="gla_scan",
    )(*args)


def _router_kernel(x_ref, sc_ref, sh_ref, rwt_ref, rb_ref, h_ref, wd_ref):
    h = x_ref[0] * (1.0 + sc_ref[0, 0]) + sh_ref[0, 0]
    h_ref[0] = h.astype(BF16)
    w0, w1 = _split2(rwt_ref[...])
    h0, h1 = _split2(h)
    scores = jax.nn.sigmoid(_dot_nt(w0, h0) + (_dot_nt(w0, h1) + _dot_nt(w1, h0)))
    ne = scores.shape[0]
    row = lax.broadcasted_iota(jnp.int32, scores.shape, 0)
    neg = jnp.float32(-jnp.inf)
    sel = scores + rb_ref[...]
    chosen = jnp.zeros(scores.shape, jnp.bool_)
    for _ in range(TOP_K):
        mx = jnp.max(sel, axis=0, keepdims=True)
        first = jnp.min(jnp.where(sel == mx, row, ne), axis=0, keepdims=True)
        pick = row == first
        chosen = jnp.logical_or(chosen, pick)
        sel = jnp.where(pick, neg, sel)
    picked = jnp.where(chosen, scores, 0.0)
    wt = picked / jnp.sum(picked, axis=0, keepdims=True) * ROUTE_SCALE
    eye = jnp.where(lax.broadcasted_iota(jnp.int32, (ne, LANES), 0)
                    == lax.broadcasted_iota(jnp.int32, (ne, LANES), 1), 1.0, 0.0).astype(BF16)
    t0, t1, t2 = _split3(wt)
    wd_ref[0] = _dot_tn(t0, eye) + (_dot_tn(t1, eye) + _dot_tn(t2, eye))


def moe_route(x, sc, sh, rw_t, rb_col, n_lat_tiles):
    b, s, d = x.shape
    tm = ROW_TILE
    mod_spec = pl.BlockSpec((1, 1, 1, d), lambda bi, i: (bi, i // n_lat_tiles, 0, 0))
    return pl.pallas_call(
        _router_kernel,
        out_shape=[jax.ShapeDtypeStruct((b, s, d), BF16), jax.ShapeDtypeStruct((b, s, LANES), F32)],
        grid=(b, s // tm),
        in_specs=[pl.BlockSpec((1, tm, d), lambda bi, i: (bi, i, 0)), mod_spec, mod_spec,
                  pl.BlockSpec((N_EXPERTS, d), lambda bi, i: (0, 0)),
                  pl.BlockSpec((N_EXPERTS, 1), lambda bi, i: (0, 0))],
        out_specs=[pl.BlockSpec((1, tm, d), lambda bi, i: (bi, i, 0)),
                   pl.BlockSpec((1, tm, LANES), lambda bi, i: (bi, i, 0))],
        compiler_params=_cparams(("arbitrary", "arbitrary")),
        name="moe_router",
    )(x, sc, sh, rw_t, rb_col)


def _moe_kernel(h_ref, wd_ref, wgu_ref, wdn_ref, x_ref, gate_ref, lg_ref, lb_ref, out_ref, acc, *,
                n_exp, n_lat_rows):
    e = pl.program_id(2)

    @pl.when(e == 0)
    def _():
        acc[...] = jnp.zeros(acc.shape, F32)

    gu = _dot(h_ref[0], wgu_ref[0])
    act = _silu(gu[:, :EXPERT_FF]) * gu[:, EXPERT_FF:]
    wd = wd_ref[0]
    lane = lax.broadcasted_iota(jnp.int32, wd.shape, 1)
    wcol = jnp.sum(jnp.where(lane == e, wd, 0.0), axis=-1, keepdims=True)
    acc[...] += wcol * _dot(act.astype(BF16), wdn_ref[0])

    @pl.when(e == n_exp - 1)
    def _():
        tm = acc.shape[0]
        row = pl.program_id(1) * tm + lax.broadcasted_iota(jnp.int32, (tm, 1), 0)
        gate = jnp.where(row >= n_lat_rows, gate_ref[0, 1], gate_ref[0, 0])
        out_ref[0] = _post_norm(x_ref[0], acc[...], gate, lg_ref[...], lb_ref[...])


def moe_experts_post_norm(h, wd, wgu, wdn, x, gate, lg, lb, n_lat_rows, tm):
    b, s, d = x.shape
    n_exp = wgu.shape[0]
    row = lambda width: pl.BlockSpec((1, tm, width), lambda bi, i, e: (bi, i, 0))
    kern = functools.partial(_moe_kernel, n_exp=n_exp, n_lat_rows=n_lat_rows)
    return pl.pallas_call(
        kern,
        out_shape=jax.ShapeDtypeStruct((b, s, d), F32),
        grid=(b, s // tm, n_exp),
        in_specs=[row(d), row(LANES),
                  pl.BlockSpec((1, d, 2 * EXPERT_FF), lambda bi, i, e: (e, 0, 0)),
                  pl.BlockSpec((1, EXPERT_FF, d), lambda bi, i, e: (e, 0, 0)),
                  row(d),
                  pl.BlockSpec((1, 2, 1, d), lambda bi, i, e: (bi, 0, 0, 0)),
                  pl.BlockSpec((1, d), lambda bi, i, e: (0, 0)),
                  pl.BlockSpec((1, d), lambda bi, i, e: (0, 0))],
        out_specs=row(d),
        scratch_shapes=[pltpu.VMEM((tm, d), F32)],
        compiler_params=_cparams(("arbitrary", "arbitrary", "arbitrary")),
        name="moe_experts_post_norm",
    )(h, wd, wgu, wdn, x, gate, lg.reshape(1, d), lb.reshape(1, d))


MOE_SUB = 256
MOE_CAP = 64
MOE_GRP = 4


def _moe_grouped_kernel(order_ref, h_ref, wd_ref, *rest):
    wgu_refs = rest[:MOE_GRP]
    wdn_refs = rest[MOE_GRP:2 * MOE_GRP]
    wsgu_ref, wsdn_ref, f_ref, acc, rcm, rrm, wbf, cmax = rest[2 * MOE_GRP:]
    g = pl.program_id(2)
    n_grp = pl.num_programs(2)
    tm = acc.shape[0]
    nsub = tm // MOE_SUB
    sub, cap, nslot = MOE_SUB, MOE_CAP, MOE_GRP * MOE_CAP
    lane = lax.broadcasted_iota(jnp.int32, (1, LANES), 1)

    @pl.when(g == 0)
    def _():
        r = lax.broadcasted_iota(jnp.int32, (sub, sub), 0)
        c = lax.broadcasted_iota(jnp.int32, (sub, sub), 1)
        before = jnp.where(c < r, 1.0, 0.0).astype(BF16)
        after = jnp.where(r < c, 1.0, 0.0).astype(BF16)
        ident = jnp.where(r == c, 1.0, 0.0).astype(BF16)
        cm = jnp.zeros((1, LANES), F32)
        for u in range(nsub):
            rs = slice(u * sub, (u + 1) * sub)
            hu = h_ref[0, rs, :]
            gu = _dot(hu, wsgu_ref[...])
            act = _silu(gu[:, :EXPERT_FF]) * gu[:, EXPERT_FF:]
            acc[rs, :] = _dot(act.astype(BF16), wsdn_ref[...])
            wd = wd_ref[0, rs, :]
            active = jnp.logical_and(wd != 0.0, lane < N_EXPERTS)
            a = jnp.where(active, 1.0, 0.0)
            ab = a.astype(BF16)
            rank_c = _dot(before, ab)
            rank_r = _dot_tn(ab, after)
            a_r = _dot_tn(ab, ident)
            rcm[u] = jnp.where(active, rank_c, -1.0).astype(BF16)
            rrm[u] = jnp.where(a_r > 0.5, rank_r, -1.0).astype(BF16)
            wbf[rs, :] = wd.astype(BF16)
            cm = jnp.maximum(cm, jnp.sum(a, axis=0, keepdims=True))
        cmax[...] = jnp.broadcast_to(cm, cmax.shape)

    experts = [order_ref[g * MOE_GRP + k] for k in range(MOE_GRP)]
    in_group = functools.reduce(jnp.logical_or, [lane == e for e in experts])
    n_max = jnp.max(jnp.where(in_group, cmax[0:1, :], 0.0))
    n_pass = (n_max.astype(jnp.int32) + (cap - 1)) // cap

    def slot_expert(l):
        return sum(((l >= k * cap).astype(jnp.int32) for k in range(1, MOE_GRP)), jnp.zeros_like(l))

    def expert_of_slot(l):
        pos = slot_expert(l)
        e = jnp.full(l.shape, -1, jnp.int32)
        for k in range(MOE_GRP):
            e = jnp.where(pos == k, experts[k], e)
        return jnp.where(l < nslot, e, -1)

    e_i = lax.broadcasted_iota(jnp.int32, (LANES, sub), 0)
    l_i = lax.broadcasted_iota(jnp.int32, (LANES, sub), 1)
    expand = jnp.where(e_i == expert_of_slot(l_i), 1.0, 0.0).astype(BF16)
    l_t = lax.broadcasted_iota(jnp.int32, (sub, LANES), 0)
    e_t = lax.broadcasted_iota(jnp.int32, (sub, LANES), 1)
    expand_t = jnp.where(e_t == expert_of_slot(l_t), 1.0, 0.0).astype(BF16)
    l_row = lax.broadcasted_iota(jnp.int32, (1, sub), 1)
    j_row = jnp.where(l_row < nslot, l_row - cap * slot_expert(l_row), -1000).astype(F32)
    l_col = lax.broadcasted_iota(jnp.int32, (sub, 1), 0)
    j_col = jnp.where(l_col < nslot, l_col - cap * slot_expert(l_col), -1000).astype(F32)

    def one_pass(p, carry):
        base = (p * cap).astype(F32)
        xg = []
        for u in range(nsub):
            rs = slice(u * sub, (u + 1) * sub)
            rank_of_slot = _dot(expand_t, rrm[u])
            gather = jnp.where(rank_of_slot == j_col + base, 1.0, 0.0).astype(BF16)
            xg.append(_dot(gather, h_ref[0, rs, :]).astype(BF16))
        ys = []
        for k in range(MOE_GRP):
            es = slice(k * cap, (k + 1) * cap)
            x_e = jnp.concatenate([xg[u][es] for u in range(nsub)], axis=0)
            gu = _dot(x_e, wgu_refs[k][0])
            act = _silu(gu[:, :EXPERT_FF]) * gu[:, EXPERT_FF:]
            ys.append(_dot(act.astype(BF16), wdn_refs[k][0]).astype(BF16))
        for u in range(nsub):
            rs = slice(u * sub, (u + 1) * sub)
            parts = [ys[k][u * cap:(u + 1) * cap] for k in range(MOE_GRP)]
            if nslot < sub:
                parts.append(jnp.zeros((sub - nslot, parts[0].shape[1]), BF16))
            y_u = jnp.concatenate(parts, axis=0)
            slot_of_row = _dot(rcm[u], expand)
            weight = _dot(wbf[rs, :], expand)
            scatter = jnp.where(slot_of_row == j_row + base, weight, 0.0).astype(BF16)
            acc[rs, :] += _dot(scatter, y_u)
        return carry

    lax.fori_loop(0, n_pass, one_pass, 0)

    @pl.when(g == n_grp - 1)
    def _():
        f_ref[0] = acc[...]


def moe_grouped_experts(order, h, wd, wgu, wdn, wsgu, wsdn, tm):
    b, s, d = h.shape
    n_grp = wgu.shape[0] // MOE_GRP
    row = lambda width: pl.BlockSpec((1, tm, width), lambda bi, i, g, o: (bi, i, 0))

    def expert(shape, k):
        return pl.BlockSpec((1,) + shape, lambda bi, i, g, o: (o[g * MOE_GRP + k], 0, 0))

    in_specs = ([row(d), row(LANES)]
                + [expert((d, 2 * EXPERT_FF), k) for k in range(MOE_GRP)]
                + [expert((EXPERT_FF, d), k) for k in range(MOE_GRP)]
                + [pl.BlockSpec((d, 2 * EXPERT_FF), lambda bi, i, g, o: (0, 0)),
                   pl.BlockSpec((EXPERT_FF, d), lambda bi, i, g, o: (0, 0))])
    grid_spec = pltpu.PrefetchScalarGridSpec(
        num_scalar_prefetch=1, grid=(b, s // tm, n_grp), in_specs=in_specs, out_specs=row(d),
        scratch_shapes=[pltpu.VMEM((tm, d), F32),
                        pltpu.VMEM((tm // MOE_SUB, MOE_SUB, LANES), BF16),
                        pltpu.VMEM((tm // MOE_SUB, LANES, MOE_SUB), BF16),
                        pltpu.VMEM((tm, LANES), BF16),
                        pltpu.VMEM((8, LANES), F32)])
    return pl.pallas_call(
        _moe_grouped_kernel,
        out_shape=jax.ShapeDtypeStruct((b, s, d), F32),
        grid_spec=grid_spec,
        compiler_params=_cparams(("arbitrary", "arbitrary", "arbitrary")),
        name="moe_grouped_experts",
    )(order, h, wd, *([wgu] * MOE_GRP), *([wdn] * MOE_GRP), wsgu, wsdn)


def _post_norm_kernel(x_ref, f_ref, gate_ref, lg_ref, lb_ref, out_ref):
    out_ref[0] = _post_norm(x_ref[0], f_ref[0].astype(F32), gate_ref[0, 0], lg_ref[...], lb_ref[...])


def post_norm_rows(x, f, gate, lg, lb, n_lat_tiles, rows_out=None):
    b, s, d = x.shape
    tm = ROW_TILE
    rows_out = s if rows_out is None else rows_out
    row = pl.BlockSpec((1, tm, d), lambda bi, i: (bi, i, 0))
    vec = pl.BlockSpec((1, d), lambda bi, i: (0, 0))
    return pl.pallas_call(
        _post_norm_kernel, out_shape=jax.ShapeDtypeStruct((b, rows_out, d), F32), grid=(b, rows_out // tm),
        in_specs=[row, row, pl.BlockSpec((1, 1, 1, d), lambda bi, i: (bi, i // n_lat_tiles, 0, 0)), vec, vec],
        out_specs=row, compiler_params=_cparams(("arbitrary", "arbitrary")), name="post_norm_rows",
    )(x, f, gate, lg.reshape(1, d), lb.reshape(1, d))


def _rope_tables(n_lat, n_ctx):
    rows = n_lat // GRID_W
    rowp = jnp.repeat(jnp.arange(rows), GRID_W).astype(F32)
    colp = jnp.tile(jnp.arange(GRID_W), rows).astype(F32)
    n_freq = DA_HEAD_DIM // 4
    inv = 1.0 / (ROPE_BASE ** (jnp.arange(n_freq, dtype=F32) / n_freq))
    ang = jnp.concatenate([rowp[:, None] * inv, colp[:, None] * inv], -1)
    cos, sin = jnp.cos(ang), jnp.sin(ang)
    cos_t = jnp.tile(cos, (1, 4))
    sin_t = jnp.tile(jnp.concatenate([-sin, sin], -1), (1, 2))
    cos_t = jnp.concatenate([cos_t, jnp.ones((n_ctx, LANES), F32)], 0)
    sin_t = jnp.concatenate([sin_t, jnp.zeros((n_ctx, LANES), F32)], 0)
    return cos_t, sin_t


def _pick_tile(total, cands):
    for t in cands:
        if total % t == 0:
            return t
    raise ValueError(f"no tile for {total}")


def _pad_cols(w, n):
    return jnp.pad(w, ((0, 0), (0, n - w.shape[1])))


def _flash_both(p_all, lam_vec, subln, lam_init, n_lat, n_ctx):
    s = n_lat + n_ctx
    tq = _pick_tile(n_lat, (1024, 512, 256))
    tk = _pick_tile(s, (1280, 1024, 512, 256))
    o = diff_flash_attention(p_all, lam_vec, subln, lam_init, tq=tq, tk=tk, nq=n_lat // tq, nk=s // tk,
                             q_off=0, k_off=0)
    return diff_flash_attention(p_all, lam_vec, subln, lam_init, tq=n_ctx, tk=n_ctx, nq=1, nk=1,
                                q_off=n_lat // n_ctx, k_off=n_lat // n_ctx, prev=o)


def kernel(x, c, ctx, c_ctx, ada_w, ada_b, ln_g, ln_b, da_w_in, da_w_o, da_lambda, da_subln, gdn_w_in, gdn_conv, gdn_a_log, gdn_dt_bias, gdn_norm, gdn_w_o, gla_w_in, gla_w_gate, gla_gate_b, gla_norm, gla_w_o, moe_router, moe_router_b, moe_w_gu, moe_w_dn, moe_ws_gu, moe_ws_dn):
    b, n, d = x.shape
    lc = ctx.shape[1]
    assert lc == ROW_TILE and n % SCAN_TILE == 0 and d == D_MODEL
    s = n + lc
    n_lat_tiles = n // ROW_TILE
    depth = ada_w.shape[0]

    xall = jnp.concatenate([x, ctx], axis=1)
    c8 = jnp.concatenate([c, c_ctx[None], jnp.zeros((8 - b - 1, d), F32)], 0)
    mods = ada_modulation(c8, ada_w, ada_b)
    rope = _rope_tables(n, lc)
    moe_tm = _pick_tile(s, (1280, 1024, 512, 256))

    for i in range(depth):
        kind, j = i % N_MIXERS, i // N_MIXERS
        m = mods[i].reshape(8, ADA_CHUNKS, d)
        mod = jnp.stack([m[:b], jnp.broadcast_to(m[b], (b, ADA_CHUNKS, d))], axis=1)
        mod = [mod[:, :, k][:, :, None, :] for k in range(ADA_CHUNKS)]

        if kind == 0:
            lam_init = 0.8 - 0.6 * math.exp(-0.3 * i)
            w = da_w_in[j]
            w = jnp.concatenate([w[:, :d] * (DA_HEAD_DIM ** -0.5 * math.log2(math.e)), w[:, d:]], 1).astype(BF16)
            p_all = modulated_projection(xall, mod[1], mod[0], w, rope=rope, n_rope=2 * d,
                                         n_lat_tiles=n_lat_tiles, head_major=True)
            o = _flash_both(p_all, da_lambda[j], da_subln[j], lam_init, n, lc)
            xall = out_projection_post_norm([o], da_w_o[j].astype(BF16), xall, mod[2],
                                            ln_g[i, 0], ln_b[i, 0], n_lat_tiles)
        elif kind == 1:
            w = gdn_w_in[j]
            n_main = 2 * GDN_KEY_DIM + 2 * GDN_VAL_DIM
            p_main, gates_raw = modulated_projection(
                xall, mod[1], mod[0], w[:, :n_main].astype(BF16),
                w_small=_pad_cols(w[:, n_main:], LANES).astype(BF16), n_lat_tiles=n_lat_tiles)
            alog_row = _pad_cols(gdn_a_log[j].reshape(1, -1), LANES)
            dtb_row = _pad_cols(gdn_dt_bias[j].reshape(1, -1), LANES)
            qn, kn, vv, gates = gdn_prep(p_main, gates_raw, gdn_conv[j], alog_row, dtb_row, n_lat_tiles)
            hv, hk = GDN_V_HEADS, GDN_K_HEADS
            def per_khead(t):
                t = t.reshape(b, s, 2, hk, 2)
                return jnp.transpose(t, (0, 3, 1, 2, 4)).reshape(b, hk, s, 4)
            gcol = jnp.concatenate([per_khead(gates[..., :2 * hv]), per_khead(gates[..., 2 * hv:4 * hv])], -1)
            grow = jnp.swapaxes(gcol, 2, 3)
            zeros = jnp.zeros((b, hv, GDN_HEAD_DIM, GDN_HEAD_DIM), F32)
            of, ob, scf, scb = gdn_scan(qn, kn, vv, gcol, grow, zeros, zeros, nblk=lc // SCAN_TILE,
                                        off=n // SCAN_TILE)
            of, ob, _, _ = gdn_scan(qn, kn, vv, gcol, grow, scf, scb, nblk=n // SCAN_TILE, off=0,
                                    prev=(of, ob))
            xall = out_projection_post_norm([of, ob, p_main, gdn_norm[j]], gdn_w_o[j].astype(BF16), xall,
                                            mod[2], ln_g[i, 0], ln_b[i, 0], n_lat_tiles,
                                            gated=(GDN_V_HEADS, GDN_HEAD_DIM, 2))
        else:
            w = gla_w_in[j]
            n_main = 2 * GLA_KEY_DIM + 2 * GLA_VAL_DIM
            p_main, gr = modulated_projection(
                xall, mod[1], mod[0], w[:, :n_main].astype(BF16),
                w_small=_pad_cols(w[:, n_main:], LANES).astype(BF16), n_lat_tiles=n_lat_tiles)
            wg = gla_w_gate[j]
            wg_pad = jnp.zeros((2, LANES, GLA_KEY_DIM), F32)
            wg_pad = wg_pad.at[0, :GLA_GATE_RANK].set(wg[0]).at[1, GLA_GATE_RANK:2 * GLA_GATE_RANK].set(wg[1])
            zeros = jnp.zeros((b, GLA_HEADS, GLA_DV, GLA_DK), F32)
            of, ob, scf, scb = gla_scan(p_main, gr, wg_pad, gla_gate_b[j], zeros, zeros,
                                        nblk=lc // SCAN_TILE, off=n // SCAN_TILE)
            of, ob, _, _ = gla_scan(p_main, gr, wg_pad, gla_gate_b[j], scf, scb, nblk=n // SCAN_TILE, off=0,
                                    prev=(of, ob))
            xall = out_projection_post_norm([of, ob, p_main, gla_norm[j]], gla_w_o[j].astype(BF16), xall,
                                            mod[2], ln_g[i, 0], ln_b[i, 0], n_lat_tiles,
                                            gated=(GLA_HEADS, GLA_DV, 2))

        h, wd = moe_route(xall, mod[4], mod[3], moe_router[i].T, moe_router_b[i].reshape(-1, 1), n_lat_tiles)
        popularity = jnp.sum((wd[..., :N_EXPERTS] != 0.0).astype(jnp.int32), axis=(0, 1))
        order = jnp.argsort(popularity).astype(jnp.int32)
        f = moe_grouped_experts(order, h, wd, moe_w_gu[i].astype(BF16), moe_w_dn[i].astype(BF16),
                                moe_ws_gu[i].astype(BF16), moe_ws_dn[i].astype(BF16), moe_tm)
        xall = post_norm_rows(xall, f, mod[5], ln_g[i, 1], ln_b[i, 1], n_lat_tiles,
                              rows_out=n if i == depth - 1 else None)

    return xall
```

```python
import functools
import math

import jax
import jax.numpy as jnp
from jax import lax
from jax.experimental import pallas as pl
from jax.experimental.pallas import tpu as pltpu

F32 = jnp.float32
BF16 = jnp.bfloat16

D_MODEL = 1024
DEPTH = 4
GRID_W = 64
N_MIXERS = 3
DN_ALPHA = (2 * DEPTH) ** 0.25
LN_EPS = 1e-5
NORM_EPS = 1e-6
ADA_CHUNKS = 6

DA_HEADS = 8
DA_HEAD_DIM = 64
ROPE_BASE = 10000.0

GDN_K_HEADS = 8
GDN_V_HEADS = 16
GDN_HEAD_DIM = 128
GDN_KEY_DIM = 1024
GDN_VAL_DIM = 2048
GDN_CONV_W = 5
CHUNK = 64

GLA_HEADS = 4
GLA_KEY_DIM = 512
GLA_VAL_DIM = 1024
GLA_DK = 128
GLA_DV = 256
GLA_GATE_RANK = 16
GLA_TAU = 16.0

N_EXPERTS = 64
TOP_K = 8
EXPERT_FF = 256
ROUTE_SCALE = 2.5

LANES = 128
ROW_TILE = 256
SCAN_TILE = 256
GDN_HEADS_PER_STEP = 4
HALO = 16
VMEM_LIMIT = 56 * 1024 * 1024


def _cparams(sem, flags=None):
    return pltpu.CompilerParams(dimension_semantics=sem, vmem_limit_bytes=VMEM_LIMIT, flags=flags)


def _dot(a, b):
    return jnp.dot(a, b, preferred_element_type=F32)


def _dot_nt(a, b):
    return lax.dot_general(a, b, (((1,), (1,)), ((), ())), preferred_element_type=F32)


def _dot_tn(a, b):
    return lax.dot_general(a, b, (((0,), (0,)), ((), ())), preferred_element_type=F32)


def _split2(a):
    hi = a.astype(BF16)
    lo = (a - hi.astype(F32)).astype(BF16)
    return hi, lo


def _split3(a):
    a0 = a.astype(BF16)
    r = a - a0.astype(F32)
    a1 = r.astype(BF16)
    a2 = (r - a1.astype(F32)).astype(BF16)
    return a0, a1, a2


def _dot3(a, b):
    a0, a1 = _split2(a)
    b0, b1 = _split2(b)
    return _dot(a0, b0) + (_dot(a0, b1) + _dot(a1, b0))


def _silu(x):
    return x * jax.nn.sigmoid(x)


def _softplus(x):
    return jnp.maximum(x, 0.0) + jnp.log(1.0 + jnp.exp(-jnp.abs(x)))


def _ada_kernel(c_ref, w_ref, b_ref, o_ref):
    s = _silu(c_ref[...])
    o_ref[0] = _dot3(s, w_ref[0]) + b_ref[0]


def ada_modulation(c8, ada_w, ada_b):
    depth, d, n = ada_w.shape
    tn = 1536
    return pl.pallas_call(
        _ada_kernel,
        out_shape=jax.ShapeDtypeStruct((depth, 8, n), F32),
        grid=(depth, n // tn),
        in_specs=[
            pl.BlockSpec((8, d), lambda i, j: (0, 0)),
            pl.BlockSpec((1, d, tn), lambda i, j: (i, 0, j)),
            pl.BlockSpec((1, 1, tn), lambda i, j: (i, 0, j)),
        ],
        out_specs=pl.BlockSpec((1, 8, tn), lambda i, j: (i, 0, j)),
        compiler_params=_cparams(("arbitrary", "arbitrary")),
        name="ada_modulation",
    )(c8, ada_w, ada_b.reshape(depth, 1, n))


def _proj_kernel(*refs, n_main, n_rope, has_small, cn, head_major):
    it = iter(refs)
    x_ref, sc_ref, sh_ref, w_ref = next(it), next(it), next(it), next(it)
    ws_ref = next(it) if has_small else None
    cos_ref = next(it) if n_rope else None
    sin_ref = next(it) if n_rope else None
    o_ref = next(it)
    os_ref = next(it) if has_small else None

    h = x_ref[0] * (1.0 + sc_ref[0, 0]) + sh_ref[0, 0]
    hb = h.astype(BF16)
    tm = hb.shape[0]
    if n_rope:
        cos = cos_ref[...]
        sin = sin_ref[...]
        lane = lax.broadcasted_iota(jnp.int32, (tm, LANES), 1)
        low_half = (lane & 32) == 0
    for j in range(n_main // cn):
        p = _dot(hb, w_ref[:, j * cn:(j + 1) * cn])
        for g in range(cn // LANES):
            pg = p[:, g * LANES:(g + 1) * LANES]
            col = j * cn + g * LANES
            if col < n_rope:
                partner = jnp.where(low_half, pltpu.roll(pg, LANES - 32, 1), pltpu.roll(pg, 32, 1))
                pg = pg * cos + partner * sin
            if head_major:
                o_ref[0, col // LANES] = pg.astype(BF16)
            else:
                o_ref[0, :, col:col + LANES] = pg.astype(BF16)
    if has_small:
        os_ref[0] = _dot(hb, ws_ref[...])


def modulated_projection(x, sc, sh, w, w_small=None, rope=None, n_rope=0, n_lat_tiles=1, head_major=False):
    b, s, d = x.shape
    n_main = w.shape[1]
    tm = ROW_TILE
    grid = (b, s // tm)
    mod_spec = pl.BlockSpec((1, 1, 1, d), lambda bi, i: (bi, i // n_lat_tiles, 0, 0))
    in_specs = [pl.BlockSpec((1, tm, d), lambda bi, i: (bi, i, 0)), mod_spec, mod_spec,
                pl.BlockSpec((d, n_main), lambda bi, i: (0, 0))]
    args = [x, sc, sh, w]
    if head_major:
        out_shape = [jax.ShapeDtypeStruct((b, n_main // LANES, s, LANES), BF16)]
        out_specs = [pl.BlockSpec((1, n_main // LANES, tm, LANES), lambda bi, i: (bi, 0, i, 0))]
    else:
        out_shape = [jax.ShapeDtypeStruct((b, s, n_main), BF16)]
        out_specs = [pl.BlockSpec((1, tm, n_main), lambda bi, i: (bi, i, 0))]
    if w_small is not None:
        in_specs.append(pl.BlockSpec((d, LANES), lambda bi, i: (0, 0)))
        args.append(w_small)
        out_shape.append(jax.ShapeDtypeStruct((b, s, LANES), F32))
        out_specs.append(pl.BlockSpec((1, tm, LANES), lambda bi, i: (bi, i, 0)))
    if n_rope:
        tab = pl.BlockSpec((tm, LANES), lambda bi, i: (i, 0))
        in_specs += [tab, tab]
        args += [rope[0], rope[1]]
    kern = functools.partial(_proj_kernel, n_main=n_main, n_rope=n_rope,
                             has_small=w_small is not None, cn=512, head_major=head_major)
    out = pl.pallas_call(
        kern, out_shape=out_shape, grid=grid, in_specs=in_specs, out_specs=out_specs,
        compiler_params=_cparams(("arbitrary", "arbitrary")), name="modulated_projection",
    )(*args)
    return out if w_small is not None else out[0]


NEG_INIT = -1e30
FLASH_ROW_BLOCK = 512
FLASH_LOOKAHEAD = 1


def _flash_kernel(lam_ref, q_ref, k_ref, v_ref, sub_ref, *rest, lam_init, nk, aliased):
    if aliased:
        rest = rest[1:]
    o_ref, m_sc, l_sc, acc_sc, s_ring = rest
    ki = pl.program_id(3)
    tq = m_sc.shape[1]
    tk = k_ref.shape[2]
    nring = s_ring.shape[0]

    @pl.when(ki == 0)
    def _():
        m_sc[...] = jnp.full(m_sc.shape, NEG_INIT, F32)
        l_sc[...] = jnp.zeros(l_sc.shape, F32)
        acc_sc[...] = jnp.zeros(acc_sc.shape, F32)

    q = q_ref[0, 0]
    k = k_ref[0, 0]
    v = v_ref[0, 0]
    hd = DA_HEAD_DIM
    rb = min(tq, FLASH_ROW_BLOCK)
    kc = 2 * LANES
    blocks = [(c, r0) for c in range(2) for r0 in range(0, tq, rb)]
    for i in range(len(blocks) + FLASH_LOOKAHEAD):
        if i < len(blocks):
            c, r0 = blocks[i]
            s_ring[i % nring] = _dot_nt(q[r0:r0 + rb, c * hd:(c + 1) * hd], k[:, c * hd:(c + 1) * hd])
        j = i - FLASH_LOOKAHEAD
        if j < 0:
            continue
        c, r0 = blocks[j]
        rows = slice(r0, r0 + rb)
        s_blk = s_ring.at[j % nring]
        m_prev = m_sc[c, rows, :]
        m_new = jnp.maximum(m_prev, jnp.max(s_blk[...], axis=-1, keepdims=True))
        alpha = jnp.exp2(m_prev - m_new)
        m2 = jnp.concatenate([m_new, m_new], axis=-1)
        lsum = None
        pv = None
        for t in range(tk // kc):
            pj = jnp.exp2(s_blk[:, t * kc:(t + 1) * kc] - m2)
            lj = pj[:, :LANES] + pj[:, LANES:]
            lsum = lj if lsum is None else lsum + lj
            d = _dot(pj.astype(BF16), v[t * kc:(t + 1) * kc, :])
            pv = d if pv is None else pv + d
        l_sc[c, rows, :] = alpha * l_sc[c, rows, :] + lsum
        acc_sc[c, rows, :] = alpha * acc_sc[c, rows, :] + pv
        m_sc[c, rows, :] = m_new

    @pl.when(ki == nk - 1)
    def _():
        lv = lam_ref[...]
        lam = (jnp.exp(jnp.sum(lv[0:1] * lv[1:2], axis=-1, keepdims=True))
               - jnp.exp(jnp.sum(lv[2:3] * lv[3:4], axis=-1, keepdims=True)) + lam_init)
        l0 = jnp.sum(l_sc[0], axis=-1, keepdims=True)
        l1 = jnp.sum(l_sc[1], axis=-1, keepdims=True)
        o = acc_sc[0] / l0 - lam * (acc_sc[1] / l1)
        ms = jnp.mean(o * o, axis=-1, keepdims=True)
        o = o * lax.rsqrt(ms + NORM_EPS) * sub_ref[...] * (1.0 - lam_init)
        o_ref[0] = o.astype(BF16)


def diff_flash_attention(p_all, lam_vec, subln, lam_init, *, tq, tk, nq, nk, q_off, k_off, prev=None):
    b, _, s, _ = p_all.shape
    hh = DA_HEADS
    in_specs = [
        pl.BlockSpec((4, DA_HEAD_DIM), lambda bi, h, qi, ki: (0, 0)),
        pl.BlockSpec((1, 1, tq, LANES), lambda bi, h, qi, ki: (bi, h, qi + q_off, 0)),
        pl.BlockSpec((1, 1, tk, LANES), lambda bi, h, qi, ki: (bi, hh + h, ki + k_off, 0)),
        pl.BlockSpec((1, 1, tk, LANES), lambda bi, h, qi, ki: (bi, 2 * hh + h, ki + k_off, 0)),
        pl.BlockSpec((1, LANES), lambda bi, h, qi, ki: (0, 0)),
    ]
    args = [lam_vec, p_all, p_all, p_all, subln.reshape(1, LANES)]
    aliases = {}
    if prev is not None:
        in_specs.append(pl.BlockSpec(memory_space=pl.ANY))
        args.append(prev)
        aliases = {5: 0}
    kern = functools.partial(_flash_kernel, lam_init=lam_init, nk=nk, aliased=prev is not None)
    return pl.pallas_call(
        kern,
        out_shape=jax.ShapeDtypeStruct((b, s, hh * LANES), BF16),
        grid=(b, hh, nq, nk),
        in_specs=in_specs,
        out_specs=pl.BlockSpec((1, tq, LANES), lambda bi, h, qi, ki: (bi, qi + q_off, h)),
        scratch_shapes=[pltpu.VMEM((2, tq, LANES), F32), pltpu.VMEM((2, tq, LANES), F32),
                        pltpu.VMEM((2, tq, LANES), F32),
                        pltpu.VMEM((FLASH_LOOKAHEAD + 1, min(tq, FLASH_ROW_BLOCK), tk), F32)],
        input_output_aliases=aliases,
        compiler_params=_cparams(("arbitrary", "arbitrary", "arbitrary", "arbitrary")),
        name="diff_flash_attention",
    )(*args)


def _post_norm(x, y, gate, lg, lb):
    r = DN_ALPHA * x + gate * y
    mu = jnp.mean(r, axis=-1, keepdims=True)
    rc = r - mu
    var = jnp.mean(rc * rc, axis=-1, keepdims=True)
    return rc * lax.rsqrt(var + LN_EPS) * lg + lb


def _outproj_kernel(*refs, gated, n_heads, dh):
    if gated:
        of_ref, ob_ref, z_ref, ng_ref, w_ref, x_ref, gate_ref, lg_ref, lb_ref, out_ref = refs
        y = None
        for h in range(n_heads):
            sl = slice(h * dh, (h + 1) * dh)
            o = of_ref[0, :, sl].astype(F32) + ob_ref[0, :, sl].astype(F32)
            ms = jnp.mean(o * o, axis=-1, keepdims=True)
            o = o * lax.rsqrt(ms + NORM_EPS) * ng_ref[...] * _silu(z_ref[0, :, sl].astype(F32))
            t = _dot(o.astype(BF16), w_ref[sl, :])
            y = t if y is None else y + t
    else:
        o_ref, w_ref, x_ref, gate_ref, lg_ref, lb_ref, out_ref = refs
        y = _dot(o_ref[0], w_ref[...])
    out_ref[0] = _post_norm(x_ref[0], y, gate_ref[0, 0], lg_ref[...], lb_ref[...])


def out_projection_post_norm(o_args, w_o, x, gate, lg, lb, n_lat_tiles, gated=None):
    b, s, d = x.shape
    kdim = w_o.shape[0]
    tm = ROW_TILE
    row = lambda width, cb=0: pl.BlockSpec((1, tm, width), lambda bi, i: (bi, i, cb))
    full = lambda shape: pl.BlockSpec(shape, lambda bi, i: (0,) * len(shape))
    if gated is None:
        in_specs = [row(kdim)]
        args = list(o_args)
        kern = functools.partial(_outproj_kernel, gated=False, n_heads=0, dh=0)
    else:
        n_heads, dh, z_cb = gated
        of, ob, z, ng = o_args
        in_specs = [row(kdim), row(kdim), row(kdim, z_cb), full((1, dh))]
        args = [of, ob, z, ng.reshape(1, dh)]
        kern = functools.partial(_outproj_kernel, gated=True, n_heads=n_heads, dh=dh)
    in_specs += [full((kdim, d)), row(d),
                 pl.BlockSpec((1, 1, 1, d), lambda bi, i: (bi, i // n_lat_tiles, 0, 0)),
                 full((1, d)), full((1, d))]
    args += [w_o, x, gate, lg.reshape(1, d), lb.reshape(1, d)]
    return pl.pallas_call(
        kern, out_shape=jax.ShapeDtypeStruct((b, s, d), F32), grid=(b, s // tm),
        in_specs=in_specs, out_specs=row(d),
        compiler_params=_cparams(("arbitrary", "arbitrary")), name="out_projection_post_norm",
    )(*args)


def _gdn_prep_kernel(cur_ref, prev_ref, next_ref, gate_ref, cw_ref, alog_ref, dtb_ref,
                     q_ref, k_ref, v_ref, g_ref, ext, *, n_lat_tiles, n_tiles, cn):
    i = pl.program_id(1)
    tm = cur_ref.shape[1]
    first = jnp.logical_or(i == 0, i == n_lat_tiles)
    last = jnp.logical_or(i == n_lat_tiles - 1, i == n_tiles - 1)
    pmask = jnp.where(first, 0.0, 1.0)
    nmask = jnp.where(last, 0.0, 1.0)
    pad = GDN_CONV_W // 2
    dh = GDN_HEAD_DIM
    n_qk = 2 * GDN_KEY_DIM
    for cc in range(cur_ref.shape[2] // cn):
        cs = slice(cc * cn, (cc + 1) * cn)
        ext[0:HALO, :] = prev_ref[0, :, cs].astype(F32) * pmask
        ext[HALO:HALO + tm, :] = cur_ref[0, :, cs].astype(F32)
        ext[HALO + tm:2 * HALO + tm, :] = next_ref[0, :, cs].astype(F32) * nmask
        acc = None
        for j in range(GDN_CONV_W):
            t = ext[pl.ds(HALO - pad + j, tm), :] * cw_ref[j:j + 1, cs]
            acc = t if acc is None else acc + t
        y = _silu(acc)
        for g in range(cn // dh):
            col = cc * cn + g * dh
            yg = y[:, g * dh:(g + 1) * dh]
            if col < n_qk:
                yg = yg * lax.rsqrt(jnp.sum(yg * yg, axis=-1, keepdims=True) + NORM_EPS)
                if col < GDN_KEY_DIM:
                    q_ref[0, col // dh] = (yg * (dh ** -0.5)).astype(BF16)
                else:
                    k_ref[0, (col - GDN_KEY_DIM) // dh] = yg.astype(BF16)
            else:
                v_ref[0, (col - n_qk) // dh] = yg.astype(BF16)
    a = gate_ref[0]
    lane = lax.broadcasted_iota(jnp.int32, a.shape, 1)
    gdec = -jnp.exp(alog_ref[...]) * _softplus(a + dtb_ref[...])
    g_ref[0] = jnp.where(lane < 2 * GDN_V_HEADS, gdec, jax.nn.sigmoid(a))


def gdn_prep(p_main, gates_raw, conv_w, alog_row, dtb_row, n_lat_tiles):
    b, s, _ = p_main.shape
    tm = ROW_TILE
    nt = s // tm
    nch = 2 * GDN_KEY_DIM + GDN_VAL_DIM
    hpt = tm // HALO
    nh = s // HALO
    kern = functools.partial(_gdn_prep_kernel, n_lat_tiles=n_lat_tiles, n_tiles=nt, cn=512)
    head_out = lambda nheads: pl.BlockSpec((1, nheads, tm, GDN_HEAD_DIM), lambda bi, i: (bi, 0, i, 0))
    return pl.pallas_call(
        kern,
        out_shape=[jax.ShapeDtypeStruct((b, GDN_K_HEADS, s, GDN_HEAD_DIM), BF16),
                   jax.ShapeDtypeStruct((b, GDN_K_HEADS, s, GDN_HEAD_DIM), BF16),
                   jax.ShapeDtypeStruct((b, GDN_V_HEADS, s, GDN_HEAD_DIM), BF16),
                   jax.ShapeDtypeStruct((b, s, LANES), F32)],
        grid=(b, nt),
        in_specs=[
            pl.BlockSpec((1, tm, nch), lambda bi, i: (bi, i, 0)),
            pl.BlockSpec((1, HALO, nch), lambda bi, i: (bi, jnp.maximum(i * hpt - 1, 0), 0)),
            pl.BlockSpec((1, HALO, nch), lambda bi, i: (bi, jnp.minimum((i + 1) * hpt, nh - 1), 0)),
            pl.BlockSpec((1, tm, LANES), lambda bi, i: (bi, i, 0)),
            pl.BlockSpec((GDN_CONV_W, nch), lambda bi, i: (0, 0)),
            pl.BlockSpec((1, LANES), lambda bi, i: (0, 0)),
            pl.BlockSpec((1, LANES), lambda bi, i: (0, 0)),
        ],
        out_specs=[head_out(GDN_K_HEADS), head_out(GDN_K_HEADS), head_out(GDN_V_HEADS),
                   pl.BlockSpec((1, tm, LANES), lambda bi, i: (bi, i, 0))],
        scratch_shapes=[pltpu.VMEM((tm + 2 * HALO, 512), F32)],
        compiler_params=_cparams(("arbitrary", "arbitrary")),
        name="gdn_prep",
    )(p_main, p_main, p_main, gates_raw, conv_w, alog_row, dtb_row)


def _chunk_masks(n, reverse):
    r = lax.broadcasted_iota(jnp.int32, (n, n), 0)
    c = lax.broadcasted_iota(jnp.int32, (n, n), 1)
    if reverse:
        return r <= c, r < c
    return r >= c, r > c


def _block_cumsum_mats(ct, reverse):
    r = lax.broadcasted_iota(jnp.int32, (ct, ct), 0)
    c = lax.broadcasted_iota(jnp.int32, (ct, ct), 1)
    same = (r // CHUNK) == (c // CHUNK)
    lower = jnp.logical_and(same, c <= r)
    upper = jnp.logical_and(same, c >= r)
    lo = jnp.where(lower, 1.0, 0.0).astype(BF16)
    up = jnp.where(upper, 1.0, 0.0).astype(BF16)
    return (up, lo) if reverse else (lo, up)


def _cumsum_cols(mat, x):
    x0, x1, x2 = _split3(x)
    return _dot(mat, x0) + (_dot(mat, x1) + _dot(mat, x2))


def _cumsum_rows(x, mat):
    x0, x1, x2 = _split3(x)
    return _dot(x0, mat) + (_dot(x1, mat) + _dot(x2, mat))


def _gdn_block(dirs):
    dh = GDN_HEAD_DIM
    eye = jnp.where(lax.broadcasted_iota(jnp.int32, (CHUNK, CHUNK), 0)
                    == lax.broadcasted_iota(jnp.int32, (CHUNK, CHUNK), 1), 1.0, 0.0).astype(F32)
    chains = []
    for q_ref, k_ref, v_ref, gc_ref, gr_ref, o_ref, s_sc, reverse in dirs:
        ct = q_ref.shape[2]
        goff = 2 if reverse else 0
        m_col, m_row = _block_cumsum_mats(ct, reverse)
        tril, strict = _chunk_masks(CHUNK, reverse)
        for kh in range(q_ref.shape[1]):
            gcol = gc_ref[0, kh]
            grow = gr_ref[0, kh]
            gcum_c = _cumsum_cols(m_col, gcol)
            gcum_r = _cumsum_rows(grow, m_row)
            for c in range(ct // CHUNK):
                rs = slice(c * CHUNK, (c + 1) * CHUNK)
                last = c * CHUNK if reverse else (c + 1) * CHUNK - 1
                q = q_ref[0, kh, rs, :]
                k = k_ref[0, kh, rs, :]
                kk = _dot_nt(k, k)
                qk = _dot_nt(q, k)
                for hh in range(2):
                    gi = goff + hh
                    chains.append(dict(
                        q=q, k=k, kk=kk, qk=qk, rs=rs, c=c, hh=2 * kh + hh, reverse=reverse, tril=tril,
                        strict=strict, v_ref=v_ref, o_ref=o_ref, s_sc=s_sc,
                        gc=gcum_c[rs, gi:gi + 1], gr=gcum_r[gi:gi + 1, rs],
                        glast=gcum_c[last:last + 1, gi:gi + 1], beta=gcol[rs, 4 + gi:5 + gi]))
    for ch in chains:
        tril = ch["tril"]
        ch["decay"] = jnp.where(tril, jnp.exp(jnp.where(tril, ch["gc"] - ch["gr"], 0.0)), 0.0)
        a = -jnp.where(ch["strict"], ch["kk"] * ch["beta"] * ch["decay"], 0.0)
        ch["tmat"] = eye + a
        ch["pw"] = a
    for _ in range(5):
        for ch in chains:
            pwb = ch["pw"].astype(BF16)
            ch["pw"] = _dot(pwb, pwb)
        for ch in chains:
            ch["tmat"] = ch["tmat"] + _dot(ch["tmat"].astype(BF16), ch["pw"].astype(BF16))
    for ch in chains:
        beta = ch["beta"]
        eg = jnp.exp(ch["gc"])
        vb = ch["v_ref"][0, ch["hh"], ch["rs"], :].astype(F32) * beta
        kbg = ch["k"].astype(F32) * (beta * eg)
        uw = _dot(ch["tmat"].astype(BF16), jnp.concatenate([vb, kbg], axis=-1).astype(BF16))
        ch["u"] = uw[:, :dh]
        ch["w"] = uw[:, dh:].astype(BF16)
        ch["eg"] = eg
        ch["attn"] = jnp.where(ch["tril"], ch["qk"] * ch["decay"], 0.0).astype(BF16)
    nchunk = max(ch["c"] for ch in chains) + 1
    for step in range(nchunk):
        cur = [ch for ch in chains if ch["c"] == (nchunk - 1 - step if ch["reverse"] else step)]
        for ch in cur:
            state = ch["s_sc"][ch["hh"]]
            sb = state.astype(BF16)
            ch["state"] = state
            ch["ws"] = _dot(ch["w"], sb)
            ch["qs"] = _dot(ch["q"], sb)
        for ch in cur:
            v_new = ch["u"] - ch["ws"]
            ch["kgv"] = (v_new * jnp.exp(ch["glast"] - ch["gc"])).astype(BF16)
            ch["o"] = ch["eg"] * ch["qs"] + _dot(ch["attn"], v_new.astype(BF16))
        for ch in cur:
            ch["s_sc"][ch["hh"]] = ch["state"] * jnp.exp(ch["glast"]) + _dot_tn(ch["k"], ch["kgv"])
            ch["o_ref"][0, ch["rs"], ch["hh"] * dh:(ch["hh"] + 1) * dh] = ch["o"].astype(BF16)


def _gdn_scan_kernel(qf, kf, vf, gcf, grf, qb, kb, vb, gcb, grb, s0f, s0b, *rest, nblk, aliased):
    if aliased:
        rest = rest[2:]
    of_ref, ob_ref, sff, sfb, sf_sc, sb_sc = rest
    j = pl.program_id(2)

    @pl.when(j == 0)
    def _():
        sf_sc[...] = s0f[0]
        sb_sc[...] = s0b[0]

    _gdn_block([(qf, kf, vf, gcf, grf, of_ref, sf_sc, False),
                (qb, kb, vb, gcb, grb, ob_ref, sb_sc, True)])

    @pl.when(j == nblk - 1)
    def _():
        sff[0] = sf_sc[...]
        sfb[0] = sb_sc[...]


def gdn_scan(qn, kn, vv, gcol, grow, s0f, s0b, *, nblk, off, prev=None):
    b, _, s, dh = qn.shape
    ct = SCAN_TILE
    g = GDN_HEADS_PER_STEP
    fwd = lambda j: j + off
    bwd = lambda j: nblk - 1 - j + off
    def specs(pos):
        return [
            pl.BlockSpec((1, g, ct, dh), lambda bi, h, j: (bi, h, pos(j), 0)),
            pl.BlockSpec((1, g, ct, dh), lambda bi, h, j: (bi, h, pos(j), 0)),
            pl.BlockSpec((1, 2 * g, ct, dh), lambda bi, h, j: (bi, h, pos(j), 0)),
            pl.BlockSpec((1, g, ct, 8), lambda bi, h, j: (bi, h, pos(j), 0)),
            pl.BlockSpec((1, g, 8, ct), lambda bi, h, j: (bi, h, 0, pos(j))),
        ]
    st_spec = pl.BlockSpec((1, 2 * g, dh, dh), lambda bi, h, j: (bi, h, 0, 0))
    in_specs = specs(fwd) + specs(bwd) + [st_spec, st_spec]
    args = [qn, kn, vv, gcol, grow] * 2 + [s0f, s0b]
    aliases = {}
    if prev is not None:
        in_specs += [pl.BlockSpec(memory_space=pl.ANY)] * 2
        args += list(prev)
        aliases = {12: 0, 13: 1}
    o_shape = jax.ShapeDtypeStruct((b, s, GDN_VAL_DIM), BF16)
    st_shape = jax.ShapeDtypeStruct((b, GDN_V_HEADS, dh, dh), F32)
    kern = functools.partial(_gdn_scan_kernel, nblk=nblk, aliased=prev is not None)
    return pl.pallas_call(
        kern,
        out_shape=[o_shape, o_shape, st_shape, st_shape],
        grid=(b, GDN_K_HEADS // g, nblk),
        in_specs=in_specs,
        out_specs=[pl.BlockSpec((1, ct, 2 * g * dh), lambda bi, h, j: (bi, fwd(j), h)),
                   pl.BlockSpec((1, ct, 2 * g * dh), lambda bi, h, j: (bi, bwd(j), h)),
                   st_spec, st_spec],
        scratch_shapes=[pltpu.VMEM((2 * g, dh, dh), F32), pltpu.VMEM((2 * g, dh, dh), F32)],
        input_output_aliases=aliases,
        compiler_params=_cparams(("arbitrary", "arbitrary", "arbitrary")),
        name="gdn_scan",
    )(*args)


def _gla_block(dirs, wg_ref, gb_ref):
    chains = []
    pre = []
    for q_ref, k_ref, v_ref, gr_ref, o_ref, st_sc, z in dirs:
        pre.append(_dot3(gr_ref[0], wg_ref[z]) + gb_ref[z:z + 1, :])
    for (q_ref, k_ref, v_ref, gr_ref, o_ref, st_sc, z), logit in zip(dirs, pre):
        reverse = z == 1
        ct = q_ref.shape[1]
        glog = -_softplus(-logit) / GLA_TAU
        m_col, _ = _block_cumsum_mats(ct, reverse)
        bcum = _cumsum_cols(m_col, glog)
        tril, _ = _chunk_masks(CHUNK, reverse)
        for c in range(ct // CHUNK):
            rs = slice(c * CHUNK, (c + 1) * CHUNK)
            last = c * CHUNK if reverse else (c + 1) * CHUNK - 1
            bc = bcum[rs, :]
            bl = bcum[last:last + 1, :]
            qf = q_ref[0, rs, :].astype(F32) * (GLA_DK ** -0.5)
            kf = k_ref[0, rs, :].astype(F32)
            chains.append(dict(
                c=c, rs=rs, reverse=reverse, tril=tril, o_ref=o_ref, st_sc=st_sc, v=v_ref[0, rs, :],
                qe=(qf * jnp.exp(bc)).astype(BF16), ke=(kf * jnp.exp(-bc)).astype(BF16),
                kg=(kf * jnp.exp(bl - bc)).astype(BF16), gl=jnp.exp(bl)))
    for ch in chains:
        ch["attn"] = jnp.where(ch["tril"], _dot_nt(ch["qe"], ch["ke"]), 0.0).astype(BF16)
    for ch in chains:
        ch["o"] = _dot(ch["attn"], ch["v"])
        ch["kv"] = _dot_tn(ch["v"], ch["kg"])
    nchunk = max(ch["c"] for ch in chains) + 1
    for step in range(nchunk):
        cur = [ch for ch in chains if ch["c"] == (nchunk - 1 - step if ch["reverse"] else step)]
        for ch in cur:
            st = ch["st_sc"][...]
            ch["o"] = ch["o"] + _dot_nt(ch["qe"], st.astype(BF16))
            ch["st_sc"][...] = st * ch["gl"] + ch["kv"]
        for ch in cur:
            ch["o_ref"][0, ch["rs"], :] = ch["o"].astype(BF16)


def _gla_scan_kernel(qf, kf, vf, grf, qb, kb, vb, grb, wg, gb, s0f, s0b, *rest, nblk, aliased):
    if aliased:
        rest = rest[2:]
    of_ref, ob_ref, sff, sfb, sf_sc, sb_sc = rest
    j = pl.program_id(2)

    @pl.when(j == 0)
    def _():
        sf_sc[...] = s0f[0, 0]
        sb_sc[...] = s0b[0, 0]

    _gla_block([(qf, kf, vf, grf, of_ref, sf_sc, 0), (qb, kb, vb, grb, ob_ref, sb_sc, 1)], wg, gb)

    @pl.when(j == nblk - 1)
    def _():
        sff[0, 0] = sf_sc[...]
        sfb[0, 0] = sb_sc[...]


def gla_scan(p_main, gr, wg_pad, gate_b, s0f, s0b, *, nblk, off, prev=None):
    b, s, _ = p_main.shape
    ct = SCAN_TILE
    nh, dk, dv = GLA_HEADS, GLA_DK, GLA_DV
    fwd = lambda j: j + off
    bwd = lambda j: nblk - 1 - j + off
    def specs(pos):
        return [
            pl.BlockSpec((1, ct, dk), lambda bi, h, j: (bi, pos(j), h)),
            pl.BlockSpec((1, ct, dk), lambda bi, h, j: (bi, pos(j), nh + h)),
            pl.BlockSpec((1, ct, dv), lambda bi, h, j: (bi, pos(j), nh + h)),
            pl.BlockSpec((1, ct, LANES), lambda bi, h, j: (bi, pos(j), 0)),
        ]
    st_spec = pl.BlockSpec((1, 1, dv, dk), lambda bi, h, j: (bi, h, 0, 0))
    in_specs = specs(fwd) + specs(bwd) + [
        pl.BlockSpec((2, LANES, dk), lambda bi, h, j: (0, 0, h)),
        pl.BlockSpec((2, dk), lambda bi, h, j: (0, h)),
        st_spec, st_spec]
    args = [p_main, p_main, p_main, gr] * 2 + [wg_pad, gate_b, s0f, s0b]
    aliases = {}
    if prev is not None:
        in_specs += [pl.BlockSpec(memory_space=pl.ANY)] * 2
        args += list(prev)
        aliases = {12: 0, 13: 1}
    o_shape = jax.ShapeDtypeStruct((b, s, GLA_VAL_DIM), BF16)
    st_shape = jax.ShapeDtypeStruct((b, nh, dv, dk), F32)
    kern = functools.partial(_gla_scan_kernel, nblk=nblk, aliased=prev is not None)
    return pl.pallas_call(
        kern,
        out_shape=[o_shape, o_shape, st_shape, st_shape],
        grid=(b, nh, nblk),
        in_specs=in_specs,
        out_specs=[pl.BlockSpec((1, ct, dv), lambda bi, h, j: (bi, fwd(j), h)),
                   pl.BlockSpec((1, ct, dv), lambda bi, h, j: (bi, bwd(j), h)),
                   st_spec, st_spec],
        scratch_shapes=[pltpu.VMEM((dv, dk), F32), pltpu.VMEM((dv, dk), F32)],
        input_output_aliases=aliases,
        compiler_params=_cparams(("arbitrary", "arbitrary", "arbitrary")),
        name="gla_scan",
    )(*args)


def _router_kernel(x_ref, sc_ref, sh_ref, rwt_ref, rb_ref, h_ref, wd_ref):
    h = x_ref[0] * (1.0 + sc_ref[0, 0]) + sh_ref[0, 0]
    h_ref[0] = h.astype(BF16)
    w0, w1 = _split2(rwt_ref[...])
    h0, h1 = _split2(h)
    scores = jax.nn.sigmoid(_dot_nt(w0, h0) + (_dot_nt(w0, h1) + _dot_nt(w1, h0)))
    ne = scores.shape[0]
    row = lax.broadcasted_iota(jnp.int32, scores.shape, 0)
    neg = jnp.float32(-jnp.inf)
    sel = scores + rb_ref[...]
    chosen = jnp.zeros(scores.shape, jnp.bool_)
    for _ in range(TOP_K):
        mx = jnp.max(sel, axis=0, keepdims=True)
        first = jnp.min(jnp.where(sel == mx, row, ne), axis=0, keepdims=True)
        pick = row == first
        chosen = jnp.logical_or(chosen, pick)
        sel = jnp.where(pick, neg, sel)
    picked = jnp.where(chosen, scores, 0.0)
    wt = picked / jnp.sum(picked, axis=0, keepdims=True) * ROUTE_SCALE
    eye = jnp.where(lax.broadcasted_iota(jnp.int32, (ne, LANES), 0)
                    == lax.broadcasted_iota(jnp.int32, (ne, LANES), 1), 1.0, 0.0).astype(BF16)
    t0, t1, t2 = _split3(wt)
    wd_ref[0] = _dot_tn(t0, eye) + (_dot_tn(t1, eye) + _dot_tn(t2, eye))


def moe_route(x, sc, sh, rw_t, rb_col, n_lat_tiles):
    b, s, d = x.shape
    tm = ROW_TILE
    mod_spec = pl.BlockSpec((1, 1, 1, d), lambda bi, i: (bi, i // n_lat_tiles, 0, 0))
    return pl.pallas_call(
        _router_kernel,
        out_shape=[jax.ShapeDtypeStruct((b, s, d), BF16), jax.ShapeDtypeStruct((b, s, LANES), F32)],
        grid=(b, s // tm),
        in_specs=[pl.BlockSpec((1, tm, d), lambda bi, i: (bi, i, 0)), mod_spec, mod_spec,
                  pl.BlockSpec((N_EXPERTS, d), lambda bi, i: (0, 0)),
                  pl.BlockSpec((N_EXPERTS, 1), lambda bi, i: (0, 0))],
        out_specs=[pl.BlockSpec((1, tm, d), lambda bi, i: (bi, i, 0)),
                   pl.BlockSpec((1, tm, LANES), lambda bi, i: (bi, i, 0))],
        compiler_params=_cparams(("arbitrary", "arbitrary")),
        name="moe_router",
    )(x, sc, sh, rw_t, rb_col)


def _moe_kernel(h_ref, wd_ref, wgu_ref, wdn_ref, x_ref, gate_ref, lg_ref, lb_ref, out_ref, acc, *,
                n_exp, n_lat_rows):
    e = pl.program_id(2)

    @pl.when(e == 0)
    def _():
        acc[...] = jnp.zeros(acc.shape, F32)

    gu = _dot(h_ref[0], wgu_ref[0])
    act = _silu(gu[:, :EXPERT_FF]) * gu[:, EXPERT_FF:]
    wd = wd_ref[0]
    lane = lax.broadcasted_iota(jnp.int32, wd.shape, 1)
    wcol = jnp.sum(jnp.where(lane == e, wd, 0.0), axis=-1, keepdims=True)
    acc[...] += wcol * _dot(act.astype(BF16), wdn_ref[0])

    @pl.when(e == n_exp - 1)
    def _():
        tm = acc.shape[0]
        row = pl.program_id(1) * tm + lax.broadcasted_iota(jnp.int32, (tm, 1), 0)
        gate = jnp.where(row >= n_lat_rows, gate_ref[0, 1], gate_ref[0, 0])
        out_ref[0] = _post_norm(x_ref[0], acc[...], gate, lg_ref[...], lb_ref[...])


def moe_experts_post_norm(h, wd, wgu, wdn, x, gate, lg, lb, n_lat_rows, tm):
    b, s, d = x.shape
    n_exp = wgu.shape[0]
    row = lambda width: pl.BlockSpec((1, tm, width), lambda bi, i, e: (bi, i, 0))
    kern = functools.partial(_moe_kernel, n_exp=n_exp, n_lat_rows=n_lat_rows)
    return pl.pallas_call(
        kern,
        out_shape=jax.ShapeDtypeStruct((b, s, d), F32),
        grid=(b, s // tm, n_exp),
        in_specs=[row(d), row(LANES),
                  pl.BlockSpec((1, d, 2 * EXPERT_FF), lambda bi, i, e: (e, 0, 0)),
                  pl.BlockSpec((1, EXPERT_FF, d), lambda bi, i, e: (e, 0, 0)),
                  row(d),
                  pl.BlockSpec((1, 2, 1, d), lambda bi, i, e: (bi, 0, 0, 0)),
                  pl.BlockSpec((1, d), lambda bi, i, e: (0, 0)),
                  pl.BlockSpec((1, d), lambda bi, i, e: (0, 0))],
        out_specs=row(d),
        scratch_shapes=[pltpu.VMEM((tm, d), F32)],
        compiler_params=_cparams(("arbitrary", "arbitrary", "arbitrary")),
        name="moe_experts_post_norm",
    )(h, wd, wgu, wdn, x, gate, lg.reshape(1, d), lb.reshape(1, d))


MOE_SUB = 256
MOE_CAP = 64
MOE_GRP = 4


def _moe_grouped_kernel(order_ref, h_ref, wd_ref, *rest):
    wgu_refs = rest[:MOE_GRP]
    wdn_refs = rest[MOE_GRP:2 * MOE_GRP]
    wsgu_ref, wsdn_ref, f_ref, acc, rcm, rrm, wbf, cmax = rest[2 * MOE_GRP:]
    g = pl.program_id(2)
    n_grp = pl.num_programs(2)
    tm = acc.shape[0]
    nsub = tm // MOE_SUB
    sub, cap, nslot = MOE_SUB, MOE_CAP, MOE_GRP * MOE_CAP
    lane = lax.broadcasted_iota(jnp.int32, (1, LANES), 1)

    @pl.when(g == 0)
    def _():
        r = lax.broadcasted_iota(jnp.int32, (sub, sub), 0)
        c = lax.broadcasted_iota(jnp.int32, (sub, sub), 1)
        before = jnp.where(c < r, 1.0, 0.0).astype(BF16)
        after = jnp.where(r < c, 1.0, 0.0).astype(BF16)
        ident = jnp.where(r == c, 1.0, 0.0).astype(BF16)
        cm = jnp.zeros((1, LANES), F32)
        for u in range(nsub):
            rs = slice(u * sub, (u + 1) * sub)
            hu = h_ref[0, rs, :]
            gu = _dot(hu, wsgu_ref[...])
            act = _silu(gu[:, :EXPERT_FF]) * gu[:, EXPERT_FF:]
            acc[rs, :] = _dot(act.astype(BF16), wsdn_ref[...])
            wd = wd_ref[0, rs, :]
            active = jnp.logical_and(wd != 0.0, lane < N_EXPERTS)
            a = jnp.where(active, 1.0, 0.0)
            ab = a.astype(BF16)
            rank_c = _dot(before, ab)
            rank_r = _dot_tn(ab, after)
            a_r = _dot_tn(ab, ident)
            rcm[u] = jnp.where(active, rank_c, -1.0).astype(BF16)
            rrm[u] = jnp.where(a_r > 0.5, rank_r, -1.0).astype(BF16)
            wbf[rs, :] = wd.astype(BF16)
            cm = jnp.maximum(cm, jnp.sum(a, axis=0, keepdims=True))
        cmax[...] = jnp.broadcast_to(cm, cmax.shape)

    experts = [order_ref[g * MOE_GRP + k] for k in range(MOE_GRP)]
    in_group = functools.reduce(jnp.logical_or, [lane == e for e in experts])
    n_max = jnp.max(jnp.where(in_group, cmax[0:1, :], 0.0))
    n_pass = (n_max.astype(jnp.int32) + (cap - 1)) // cap

    def slot_expert(l):
        return sum(((l >= k * cap).astype(jnp.int32) for k in range(1, MOE_GRP)), jnp.zeros_like(l))

    def expert_of_slot(l):
        pos = slot_expert(l)
        e = jnp.full(l.shape, -1, jnp.int32)
        for k in range(MOE_GRP):
            e = jnp.where(pos == k, experts[k], e)
        return jnp.where(l < nslot, e, -1)

    e_i = lax.broadcasted_iota(jnp.int32, (LANES, sub), 0)
    l_i = lax.broadcasted_iota(jnp.int32, (LANES, sub), 1)
    expand = jnp.where(e_i == expert_of_slot(l_i), 1.0, 0.0).astype(BF16)
    l_t = lax.broadcasted_iota(jnp.int32, (sub, LANES), 0)
    e_t = lax.broadcasted_iota(jnp.int32, (sub, LANES), 1)
    expand_t = jnp.where(e_t == expert_of_slot(l_t), 1.0, 0.0).astype(BF16)
    l_row = lax.broadcasted_iota(jnp.int32, (1, sub), 1)
    j_row = jnp.where(l_row < nslot, l_row - cap * slot_expert(l_row), -1000).astype(F32)
    l_col = lax.broadcasted_iota(jnp.int32, (sub, 1), 0)
    j_col = jnp.where(l_col < nslot, l_col - cap * slot_expert(l_col), -1000).astype(F32)

    def one_pass(p, carry):
        base = (p * cap).astype(F32)
        xg = []
        for u in range(nsub):
            rs = slice(u * sub, (u + 1) * sub)
            rank_of_slot = _dot(expand_t, rrm[u])
            gather = jnp.where(rank_of_slot == j_col + base, 1.0, 0.0).astype(BF16)
            xg.append(_dot(gather, h_ref[0, rs, :]).astype(BF16))
        ys = []
        for k in range(MOE_GRP):
            es = slice(k * cap, (k + 1) * cap)
            x_e = jnp.concatenate([xg[u][es] for u in range(nsub)], axis=0)
            gu = _dot(x_e, wgu_refs[k][0])
            act = _silu(gu[:, :EXPERT_FF]) * gu[:, EXPERT_FF:]
            ys.append(_dot(act.astype(BF16), wdn_refs[k][0]).astype(BF16))
        for u in range(nsub):
            rs = slice(u * sub, (u + 1) * sub)
            parts = [ys[k][u * cap:(u + 1) * cap] for k in range(MOE_GRP)]
            if nslot < sub:
                parts.append(jnp.zeros((sub - nslot, parts[0].shape[1]), BF16))
            y_u = jnp.concatenate(parts, axis=0)
            slot_of_row = _dot(rcm[u], expand)
            weight = _dot(wbf[rs, :], expand)
            scatter = jnp.where(slot_of_row == j_row + base, weight, 0.0).astype(BF16)
            acc[rs, :] += _dot(scatter, y_u)
        return carry

    lax.fori_loop(0, n_pass, one_pass, 0)

    @pl.when(g == n_grp - 1)
    def _():
        f_ref[0] = acc[...]


def moe_grouped_experts(order, h, wd, wgu, wdn, wsgu, wsdn, tm):
    b, s, d = h.shape
    n_grp = wgu.shape[0] // MOE_GRP
    row = lambda width: pl.BlockSpec((1, tm, width), lambda bi, i, g, o: (bi, i, 0))

    def expert(shape, k):
        return pl.BlockSpec((1,) + shape, lambda bi, i, g, o: (o[g * MOE_GRP + k], 0, 0))

    in_specs = ([row(d), row(LANES)]
                + [expert((d, 2 * EXPERT_FF), k) for k in range(MOE_GRP)]
                + [expert((EXPERT_FF, d), k) for k in range(MOE_GRP)]
                + [pl.BlockSpec((d, 2 * EXPERT_FF), lambda bi, i, g, o: (0, 0)),
                   pl.BlockSpec((EXPERT_FF, d), lambda bi, i, g, o: (0, 0))])
    grid_spec = pltpu.PrefetchScalarGridSpec(
        num_scalar_prefetch=1, grid=(b, s // tm, n_grp), in_specs=in_specs, out_specs=row(d),
        scratch_shapes=[pltpu.VMEM((tm, d), F32),
                        pltpu.VMEM((tm // MOE_SUB, MOE_SUB, LANES), BF16),
                        pltpu.VMEM((tm // MOE_SUB, LANES, MOE_SUB), BF16),
                        pltpu.VMEM((tm, LANES), BF16),
                        pltpu.VMEM((8, LANES), F32)])
    return pl.pallas_call(
        _moe_grouped_kernel,
        out_shape=jax.ShapeDtypeStruct((b, s, d), F32),
        grid_spec=grid_spec,
        compiler_params=_cparams(("arbitrary", "arbitrary", "arbitrary")),
        name="moe_grouped_experts",
    )(order, h, wd, *([wgu] * MOE_GRP), *([wdn] * MOE_GRP), wsgu, wsdn)


def _post_norm_kernel(x_ref, f_ref, gate_ref, lg_ref, lb_ref, out_ref):
    out_ref[0] = _post_norm(x_ref[0], f_ref[0].astype(F32), gate_ref[0, 0], lg_ref[...], lb_ref[...])


def post_norm_rows(x, f, gate, lg, lb, n_lat_tiles, rows_out=None):
    b, s, d = x.shape
    tm = ROW_TILE
    rows_out = s if rows_out is None else rows_out
    row = pl.BlockSpec((1, tm, d), lambda bi, i: (bi, i, 0))
    vec = pl.BlockSpec((1, d), lambda bi, i: (0, 0))
    return pl.pallas_call(
        _post_norm_kernel, out_shape=jax.ShapeDtypeStruct((b, rows_out, d), F32), grid=(b, rows_out // tm),
        in_specs=[row, row, pl.BlockSpec((1, 1, 1, d), lambda bi, i: (bi, i // n_lat_tiles, 0, 0)), vec, vec],
        out_specs=row, compiler_params=_cparams(("arbitrary", "arbitrary")), name="post_norm_rows",
    )(x, f, gate, lg.reshape(1, d), lb.reshape(1, d))


def _rope_tables(n_lat, n_ctx):
    rows = n_lat // GRID_W
    rowp = jnp.repeat(jnp.arange(rows), GRID_W).astype(F32)
    colp = jnp.tile(jnp.arange(GRID_W), rows).astype(F32)
    n_freq = DA_HEAD_DIM // 4
    inv = 1.0 / (ROPE_BASE ** (jnp.arange(n_freq, dtype=F32) / n_freq))
    ang = jnp.concatenate([rowp[:, None] * inv, colp[:, None] * inv], -1)
    cos, sin = jnp.cos(ang), jnp.sin(ang)
    cos_t = jnp.tile(cos, (1, 4))
    sin_t = jnp.tile(jnp.concatenate([-sin, sin], -1), (1, 2))
    cos_t = jnp.concatenate([cos_t, jnp.ones((n_ctx, LANES), F32)], 0)
    sin_t = jnp.concatenate([sin_t, jnp.zeros((n_ctx, LANES), F32)], 0)
    return cos_t, sin_t


def _pick_tile(total, cands):
    for t in cands:
        if total % t == 0:
            return t
    raise ValueError(f"no tile for {total}")


def _pad_cols(w, n):
    return jnp.pad(w, ((0, 0), (0, n - w.shape[1])))


def _flash_both(p_all, lam_vec, subln, lam_init, n_lat, n_ctx):
    s = n_lat + n_ctx
    tq = _pick_tile(n_lat, (1024, 512, 256))
    tk = _pick_tile(s, (3328, 1280, 1024, 512, 256))
    o = diff_flash_attention(p_all, lam_vec, subln, lam_init, tq=tq, tk=tk, nq=n_lat // tq, nk=s // tk,
                             q_off=0, k_off=0)
    return diff_flash_attention(p_all, lam_vec, subln, lam_init, tq=n_ctx, tk=n_ctx, nq=1, nk=1,
                                q_off=n_lat // n_ctx, k_off=n_lat // n_ctx, prev=o)


def kernel(x, c, ctx, c_ctx, ada_w, ada_b, ln_g, ln_b, da_w_in, da_w_o, da_lambda, da_subln, gdn_w_in, gdn_conv, gdn_a_log, gdn_dt_bias, gdn_norm, gdn_w_o, gla_w_in, gla_w_gate, gla_gate_b, gla_norm, gla_w_o, moe_router, moe_router_b, moe_w_gu, moe_w_dn, moe_ws_gu, moe_ws_dn):
    b, n, d = x.shape
    lc = ctx.shape[1]
    assert lc == ROW_TILE and n % SCAN_TILE == 0 and d == D_MODEL
    s = n + lc
    n_lat_tiles = n // ROW_TILE
    depth = ada_w.shape[0]

    xall = jnp.concatenate([x, ctx], axis=1)
    c8 = jnp.concatenate([c, c_ctx[None], jnp.zeros((8 - b - 1, d), F32)], 0)
    mods = ada_modulation(c8, ada_w, ada_b)
    rope = _rope_tables(n, lc)
    moe_tm = _pick_tile(s, (1280, 1024, 512, 256))

    for i in range(depth):
        kind, j = i % N_MIXERS, i // N_MIXERS
        m = mods[i].reshape(8, ADA_CHUNKS, d)
        mod = jnp.stack([m[:b], jnp.broadcast_to(m[b], (b, ADA_CHUNKS, d))], axis=1)
        mod = [mod[:, :, k][:, :, None, :] for k in range(ADA_CHUNKS)]

        if kind == 0:
            lam_init = 0.8 - 0.6 * math.exp(-0.3 * i)
            w = da_w_in[j]
            w = jnp.concatenate([w[:, :d] * (DA_HEAD_DIM ** -0.5 * math.log2(math.e)), w[:, d:]], 1).astype(BF16)
            p_all = modulated_projection(xall, mod[1], mod[0], w, rope=rope, n_rope=2 * d,
                                         n_lat_tiles=n_lat_tiles, head_major=True)
            o = _flash_both(p_all, da_lambda[j], da_subln[j], lam_init, n, lc)
            xall = out_projection_post_norm([o], da_w_o[j].astype(BF16), xall, mod[2],
                                            ln_g[i, 0], ln_b[i, 0], n_lat_tiles)
        elif kind == 1:
            w = gdn_w_in[j]
            n_main = 2 * GDN_KEY_DIM + 2 * GDN_VAL_DIM
            p_main, gates_raw = modulated_projection(
                xall, mod[1], mod[0], w[:, :n_main].astype(BF16),
                w_small=_pad_cols(w[:, n_main:], LANES).astype(BF16), n_lat_tiles=n_lat_tiles)
            alog_row = _pad_cols(gdn_a_log[j].reshape(1, -1), LANES)
            dtb_row = _pad_cols(gdn_dt_bias[j].reshape(1, -1), LANES)
            qn, kn, vv, gates = gdn_prep(p_main, gates_raw, gdn_conv[j], alog_row, dtb_row, n_lat_tiles)
            hv, hk = GDN_V_HEADS, GDN_K_HEADS
            def per_khead(t):
                t = t.reshape(b, s, 2, hk, 2)
                return jnp.transpose(t, (0, 3, 1, 2, 4)).reshape(b, hk, s, 4)
            gcol = jnp.concatenate([per_khead(gates[..., :2 * hv]), per_khead(gates[..., 2 * hv:4 * hv])], -1)
            grow = jnp.swapaxes(gcol, 2, 3)
            zeros = jnp.zeros((b, hv, GDN_HEAD_DIM, GDN_HEAD_DIM), F32)
            of, ob, scf, scb = gdn_scan(qn, kn, vv, gcol, grow, zeros, zeros, nblk=lc // SCAN_TILE,
                                        off=n // SCAN_TILE)
            of, ob, _, _ = gdn_scan(qn, kn, vv, gcol, grow, scf, scb, nblk=n // SCAN_TILE, off=0,
                                    prev=(of, ob))
            xall = out_projection_post_norm([of, ob, p_main, gdn_norm[j]], gdn_w_o[j].astype(BF16), xall,
                                            mod[2], ln_g[i, 0], ln_b[i, 0], n_lat_tiles,
                                            gated=(GDN_V_HEADS, GDN_HEAD_DIM, 2))
        else:
            w = gla_w_in[j]
            n_main = 2 * GLA_KEY_DIM + 2 * GLA_VAL_DIM
            p_main, gr = modulated_projection(
                xall, mod[1], mod[0], w[:, :n_main].astype(BF16),
                w_small=_pad_cols(w[:, n_main:], LANES).astype(BF16), n_lat_tiles=n_lat_tiles)
            wg = gla_w_gate[j]
            wg_pad = jnp.zeros((2, LANES, GLA_KEY_DIM), F32)
            wg_pad = wg_pad.at[0, :GLA_GATE_RANK].set(wg[0]).at[1, GLA_GATE_RANK:2 * GLA_GATE_RANK].set(wg[1])
            zeros = jnp.zeros((b, GLA_HEADS, GLA_DV, GLA_DK), F32)
            of, ob, scf, scb = gla_scan(p_main, gr, wg_pad, gla_gate_b[j], zeros, zeros,
                                        nblk=lc // SCAN_TILE, off=n // SCAN_TILE)
            of, ob, _, _ = gla_scan(p_main, gr, wg_pad, gla_gate_b[j], scf, scb, nblk=n // SCAN_TILE, off=0,
                                    prev=(of, ob))
            xall = out_projection_post_norm([of, ob, p_main, gla_norm[j]], gla_w_o[j].astype(BF16), xall,
                                            mod[2], ln_g[i, 0], ln_b[i, 0], n_lat_tiles,
                                            gated=(GLA_HEADS, GLA_DV, 2))

        h, wd = moe_route(xall, mod[4], mod[3], moe_router[i].T, moe_router_b[i].reshape(-1, 1), n_lat_tiles)
        popularity = jnp.sum((wd[..., :N_EXPERTS] != 0.0).astype(jnp.int32), axis=(0, 1))
        order = jnp.argsort(popularity).astype(jnp.int32)
        f = moe_grouped_experts(order, h, wd, moe_w_gu[i].astype(BF16), moe_w_dn[i].astype(BF16),
                                moe_ws_gu[i].astype(BF16), moe_ws_dn[i].astype(BF16), moe_tm)
        xall = post_norm_rows(xall, f, mod[5], ln_g[i, 1], ln_b[i, 1], n_lat_tiles,
                              rows_out=n if i == depth - 1 else None)

    return xall
```

```python
import functools
import math

import jax
import jax.numpy as jnp
from jax import lax
from jax.experimental import pallas as pl
from jax.experimental.pallas import tpu as pltpu

F32 = jnp.float32
BF16 = jnp.bfloat16

D_MODEL = 1024
DEPTH = 4
GRID_W = 64
N_MIXERS = 3
DN_ALPHA = (2 * DEPTH) ** 0.25
LN_EPS = 1e-5
NORM_EPS = 1e-6
ADA_CHUNKS = 6

DA_HEADS = 8
DA_HEAD_DIM = 64
ROPE_BASE = 10000.0

GDN_K_HEADS = 8
GDN_V_HEADS = 16
GDN_HEAD_DIM = 128
GDN_KEY_DIM = 1024
GDN_VAL_DIM = 2048
GDN_CONV_W = 5
CHUNK = 64

GLA_HEADS = 4
GLA_KEY_DIM = 512
GLA_VAL_DIM = 1024
GLA_DK = 128
GLA_DV = 256
GLA_GATE_RANK = 16
GLA_TAU = 16.0

N_EXPERTS = 64
TOP_K = 8
EXPERT_FF = 256
ROUTE_SCALE = 2.5

LANES = 128
ROW_TILE = 256
SCAN_TILE = 256
GDN_HEADS_PER_STEP = 4
HALO = 16
VMEM_LIMIT = 56 * 1024 * 1024


def _cparams(sem, flags=None):
    return pltpu.CompilerParams(dimension_semantics=sem, vmem_limit_bytes=VMEM_LIMIT, flags=flags)


def _dot(a, b):
    return jnp.dot(a, b, preferred_element_type=F32)


def _dot_nt(a, b):
    return lax.dot_general(a, b, (((1,), (1,)), ((), ())), preferred_element_type=F32)


def _dot_tn(a, b):
    return lax.dot_general(a, b, (((0,), (0,)), ((), ())), preferred_element_type=F32)


def _split2(a):
    hi = a.astype(BF16)
    lo = (a - hi.astype(F32)).astype(BF16)
    return hi, lo


def _split3(a):
    a0 = a.astype(BF16)
    r = a - a0.astype(F32)
    a1 = r.astype(BF16)
    a2 = (r - a1.astype(F32)).astype(BF16)
    return a0, a1, a2


def _dot3(a, b):
    a0, a1 = _split2(a)
    b0, b1 = _split2(b)
    return _dot(a0, b0) + (_dot(a0, b1) + _dot(a1, b0))


def _silu(x):
    return x * jax.nn.sigmoid(x)


def _softplus(x):
    return jnp.maximum(x, 0.0) + jnp.log(1.0 + jnp.exp(-jnp.abs(x)))


def _ada_kernel(c_ref, w_ref, b_ref, o_ref):
    s = _silu(c_ref[...])
    o_ref[0] = _dot3(s, w_ref[0]) + b_ref[0]


def ada_modulation(c8, ada_w, ada_b):
    depth, d, n = ada_w.shape
    tn = 1536
    return pl.pallas_call(
        _ada_kernel,
        out_shape=jax.ShapeDtypeStruct((depth, 8, n), F32),
        grid=(depth, n // tn),
        in_specs=[
            pl.BlockSpec((8, d), lambda i, j: (0, 0)),
            pl.BlockSpec((1, d, tn), lambda i, j: (i, 0, j)),
            pl.BlockSpec((1, 1, tn), lambda i, j: (i, 0, j)),
        ],
        out_specs=pl.BlockSpec((1, 8, tn), lambda i, j: (i, 0, j)),
        compiler_params=_cparams(("arbitrary", "arbitrary")),
        name="ada_modulation",
    )(c8, ada_w, ada_b.reshape(depth, 1, n))


def _proj_kernel(*refs, n_main, n_rope, has_small, cn, head_major):
    it = iter(refs)
    x_ref, sc_ref, sh_ref, w_ref = next(it), next(it), next(it), next(it)
    ws_ref = next(it) if has_small else None
    cos_ref = next(it) if n_rope else None
    sin_ref = next(it) if n_rope else None
    o_ref = next(it)
    os_ref = next(it) if has_small else None

    h = x_ref[0] * (1.0 + sc_ref[0, 0]) + sh_ref[0, 0]
    hb = h.astype(BF16)
    tm = hb.shape[0]
    if n_rope:
        cos = cos_ref[...]
        sin = sin_ref[...]
        lane = lax.broadcasted_iota(jnp.int32, (tm, LANES), 1)
        low_half = (lane & 32) == 0
    for j in range(n_main // cn):
        p = _dot(hb, w_ref[:, j * cn:(j + 1) * cn])
        for g in range(cn // LANES):
            pg = p[:, g * LANES:(g + 1) * LANES]
            col = j * cn + g * LANES
            if col < n_rope:
                partner = jnp.where(low_half, pltpu.roll(pg, LANES - 32, 1), pltpu.roll(pg, 32, 1))
                pg = pg * cos + partner * sin
            if head_major:
                o_ref[0, col // LANES] = pg.astype(BF16)
            else:
                o_ref[0, :, col:col + LANES] = pg.astype(BF16)
    if has_small:
        os_ref[0] = _dot(hb, ws_ref[...])


def modulated_projection(x, sc, sh, w, w_small=None, rope=None, n_rope=0, n_lat_tiles=1, head_major=False):
    b, s, d = x.shape
    n_main = w.shape[1]
    tm = ROW_TILE
    grid = (b, s // tm)
    mod_spec = pl.BlockSpec((1, 1, 1, d), lambda bi, i: (bi, i // n_lat_tiles, 0, 0))
    in_specs = [pl.BlockSpec((1, tm, d), lambda bi, i: (bi, i, 0)), mod_spec, mod_spec,
                pl.BlockSpec((d, n_main), lambda bi, i: (0, 0))]
    args = [x, sc, sh, w]
    if head_major:
        out_shape = [jax.ShapeDtypeStruct((b, n_main // LANES, s, LANES), BF16)]
        out_specs = [pl.BlockSpec((1, n_main // LANES, tm, LANES), lambda bi, i: (bi, 0, i, 0))]
    else:
        out_shape = [jax.ShapeDtypeStruct((b, s, n_main), BF16)]
        out_specs = [pl.BlockSpec((1, tm, n_main), lambda bi, i: (bi, i, 0))]
    if w_small is not None:
        in_specs.append(pl.BlockSpec((d, LANES), lambda bi, i: (0, 0)))
        args.append(w_small)
        out_shape.append(jax.ShapeDtypeStruct((b, s, LANES), F32))
        out_specs.append(pl.BlockSpec((1, tm, LANES), lambda bi, i: (bi, i, 0)))
    if n_rope:
        tab = pl.BlockSpec((tm, LANES), lambda bi, i: (i, 0))
        in_specs += [tab, tab]
        args += [rope[0], rope[1]]
    kern = functools.partial(_proj_kernel, n_main=n_main, n_rope=n_rope,
                             has_small=w_small is not None, cn=512, head_major=head_major)
    out = pl.pallas_call(
        kern, out_shape=out_shape, grid=grid, in_specs=in_specs, out_specs=out_specs,
        compiler_params=_cparams(("arbitrary", "arbitrary")), name="modulated_projection",
    )(*args)
    return out if w_small is not None else out[0]


NEG_INIT = -1e30
FLASH_ROW_BLOCK = 512
FLASH_LOOKAHEAD = 1


def _flash_kernel(lam_ref, q_ref, k_ref, v_ref, sub_ref, *rest, lam_init, nk, aliased):
    if aliased:
        rest = rest[1:]
    o_ref, m_sc, l_sc, acc_sc, s_ring = rest
    ki = pl.program_id(3)
    tq = m_sc.shape[1]
    tk = k_ref.shape[2]
    nring = s_ring.shape[0]

    @pl.when(ki == 0)
    def _():
        m_sc[...] = jnp.full(m_sc.shape, NEG_INIT, F32)
        l_sc[...] = jnp.zeros(l_sc.shape, F32)
        acc_sc[...] = jnp.zeros(acc_sc.shape, F32)

    q = q_ref[0, 0]
    k = k_ref[0, 0]
    v = v_ref[0, 0]
    hd = DA_HEAD_DIM
    rb = min(tq, FLASH_ROW_BLOCK)
    kc = 2 * LANES
    blocks = [(c, r0) for c in range(2) for r0 in range(0, tq, rb)]
    for i in range(len(blocks) + FLASH_LOOKAHEAD):
        if i < len(blocks):
            c, r0 = blocks[i]
            s_ring[i % nring] = _dot_nt(q[r0:r0 + rb, c * hd:(c + 1) * hd], k[:, c * hd:(c + 1) * hd])
        j = i - FLASH_LOOKAHEAD
        if j < 0:
            continue
        c, r0 = blocks[j]
        rows = slice(r0, r0 + rb)
        s_blk = s_ring.at[j % nring]
        m_prev = m_sc[c, rows, :]
        m_new = jnp.maximum(m_prev, jnp.max(s_blk[...], axis=-1, keepdims=True))
        alpha = jnp.exp2(m_prev - m_new)
        m2 = jnp.concatenate([m_new, m_new], axis=-1)
        lsum = None
        pv = None
        for t in range(tk // kc):
            pj = jnp.exp2(s_blk[:, t * kc:(t + 1) * kc] - m2)
            lj = pj[:, :LANES] + pj[:, LANES:]
            lsum = lj if lsum is None else lsum + lj
            d = _dot(pj.astype(BF16), v[t * kc:(t + 1) * kc, :])
            pv = d if pv is None else pv + d
        l_sc[c, rows, :] = alpha * l_sc[c, rows, :] + lsum
        acc_sc[c, rows, :] = alpha * acc_sc[c, rows, :] + pv
        m_sc[c, rows, :] = m_new

    @pl.when(ki == nk - 1)
    def _():
        lv = lam_ref[...]
        lam = (jnp.exp(jnp.sum(lv[0:1] * lv[1:2], axis=-1, keepdims=True))
               - jnp.exp(jnp.sum(lv[2:3] * lv[3:4], axis=-1, keepdims=True)) + lam_init)
        l0 = jnp.sum(l_sc[0], axis=-1, keepdims=True)
        l1 = jnp.sum(l_sc[1], axis=-1, keepdims=True)
        o = acc_sc[0] / l0 - lam * (acc_sc[1] / l1)
        ms = jnp.mean(o * o, axis=-1, keepdims=True)
        o = o * lax.rsqrt(ms + NORM_EPS) * sub_ref[...] * (1.0 - lam_init)
        o_ref[0] = o.astype(BF16)


def diff_flash_attention(p_all, lam_vec, subln, lam_init, *, tq, tk, nq, nk, q_off, k_off, prev=None):
    b, _, s, _ = p_all.shape
    hh = DA_HEADS
    in_specs = [
        pl.BlockSpec((4, DA_HEAD_DIM), lambda bi, h, qi, ki: (0, 0)),
        pl.BlockSpec((1, 1, tq, LANES), lambda bi, h, qi, ki: (bi, h, qi + q_off, 0)),
        pl.BlockSpec((1, 1, tk, LANES), lambda bi, h, qi, ki: (bi, hh + h, ki + k_off, 0)),
        pl.BlockSpec((1, 1, tk, LANES), lambda bi, h, qi, ki: (bi, 2 * hh + h, ki + k_off, 0)),
        pl.BlockSpec((1, LANES), lambda bi, h, qi, ki: (0, 0)),
    ]
    args = [lam_vec, p_all, p_all, p_all, subln.reshape(1, LANES)]
    aliases = {}
    if prev is not None:
        in_specs.append(pl.BlockSpec(memory_space=pl.ANY))
        args.append(prev)
        aliases = {5: 0}
    kern = functools.partial(_flash_kernel, lam_init=lam_init, nk=nk, aliased=prev is not None)
    return pl.pallas_call(
        kern,
        out_shape=jax.ShapeDtypeStruct((b, s, hh * LANES), BF16),
        grid=(b, hh, nq, nk),
        in_specs=in_specs,
        out_specs=pl.BlockSpec((1, tq, LANES), lambda bi, h, qi, ki: (bi, qi + q_off, h)),
        scratch_shapes=[pltpu.VMEM((2, tq, LANES), F32), pltpu.VMEM((2, tq, LANES), F32),
                        pltpu.VMEM((2, tq, LANES), F32),
                        pltpu.VMEM((FLASH_LOOKAHEAD + 1, min(tq, FLASH_ROW_BLOCK), tk), F32)],
        input_output_aliases=aliases,
        compiler_params=_cparams(("arbitrary", "arbitrary", "arbitrary", "arbitrary")),
        name="diff_flash_attention",
    )(*args)


def _post_norm(x, y, gate, lg, lb):
    r = DN_ALPHA * x + gate * y
    mu = jnp.mean(r, axis=-1, keepdims=True)
    rc = r - mu
    var = jnp.mean(rc * rc, axis=-1, keepdims=True)
    return rc * lax.rsqrt(var + LN_EPS) * lg + lb


def _outproj_kernel(*refs, gated, n_heads, dh):
    if gated:
        of_ref, ob_ref, z_ref, ng_ref, w_ref, x_ref, gate_ref, lg_ref, lb_ref, out_ref = refs
        y = None
        for h in range(n_heads):
            sl = slice(h * dh, (h + 1) * dh)
            o = of_ref[0, :, sl].astype(F32) + ob_ref[0, :, sl].astype(F32)
            ms = jnp.mean(o * o, axis=-1, keepdims=True)
            o = o * lax.rsqrt(ms + NORM_EPS) * ng_ref[...] * _silu(z_ref[0, :, sl].astype(F32))
            t = _dot(o.astype(BF16), w_ref[sl, :])
            y = t if y is None else y + t
    else:
        o_ref, w_ref, x_ref, gate_ref, lg_ref, lb_ref, out_ref = refs
        y = _dot(o_ref[0], w_ref[...])
    out_ref[0] = _post_norm(x_ref[0], y, gate_ref[0, 0], lg_ref[...], lb_ref[...])


def out_projection_post_norm(o_args, w_o, x, gate, lg, lb, n_lat_tiles, gated=None):
    b, s, d = x.shape
    kdim = w_o.shape[0]
    tm = ROW_TILE
    row = lambda width, cb=0: pl.BlockSpec((1, tm, width), lambda bi, i: (bi, i, cb))
    full = lambda shape: pl.BlockSpec(shape, lambda bi, i: (0,) * len(shape))
    if gated is None:
        in_specs = [row(kdim)]
        args = list(o_args)
        kern = functools.partial(_outproj_kernel, gated=False, n_heads=0, dh=0)
    else:
        n_heads, dh, z_cb = gated
        of, ob, z, ng = o_args
        in_specs = [row(kdim), row(kdim), row(kdim, z_cb), full((1, dh))]
        args = [of, ob, z, ng.reshape(1, dh)]
        kern = functools.partial(_outproj_kernel, gated=True, n_heads=n_heads, dh=dh)
    in_specs += [full((kdim, d)), row(d),
                 pl.BlockSpec((1, 1, 1, d), lambda bi, i: (bi, i // n_lat_tiles, 0, 0)),
                 full((1, d)), full((1, d))]
    args += [w_o, x, gate, lg.reshape(1, d), lb.reshape(1, d)]
    return pl.pallas_call(
        kern, out_shape=jax.ShapeDtypeStruct((b, s, d), F32), grid=(b, s // tm),
        in_specs=in_specs, out_specs=row(d),
        compiler_params=_cparams(("arbitrary", "arbitrary")), name="out_projection_post_norm",
    )(*args)


def _gdn_prep_kernel(cur_ref, prev_ref, next_ref, gate_ref, cw_ref, alog_ref, dtb_ref,
                     q_ref, k_ref, v_ref, g_ref, ext, *, n_lat_tiles, n_tiles, cn):
    i = pl.program_id(1)
    tm = cur_ref.shape[1]
    first = jnp.logical_or(i == 0, i == n_lat_tiles)
    last = jnp.logical_or(i == n_lat_tiles - 1, i == n_tiles - 1)
    pmask = jnp.where(first, 0.0, 1.0)
    nmask = jnp.where(last, 0.0, 1.0)
    pad = GDN_CONV_W // 2
    dh = GDN_HEAD_DIM
    n_qk = 2 * GDN_KEY_DIM
    for cc in range(cur_ref.shape[2] // cn):
        cs = slice(cc * cn, (cc + 1) * cn)
        ext[0:HALO, :] = prev_ref[0, :, cs].astype(F32) * pmask
        ext[HALO:HALO + tm, :] = cur_ref[0, :, cs].astype(F32)
        ext[HALO + tm:2 * HALO + tm, :] = next_ref[0, :, cs].astype(F32) * nmask
        acc = None
        for j in range(GDN_CONV_W):
            t = ext[pl.ds(HALO - pad + j, tm), :] * cw_ref[j:j + 1, cs]
            acc = t if acc is None else acc + t
        y = _silu(acc)
        for g in range(cn // dh):
            col = cc * cn + g * dh
            yg = y[:, g * dh:(g + 1) * dh]
            if col < n_qk:
                yg = yg * lax.rsqrt(jnp.sum(yg * yg, axis=-1, keepdims=True) + NORM_EPS)
                if col < GDN_KEY_DIM:
                    q_ref[0, col // dh] = (yg * (dh ** -0.5)).astype(BF16)
                else:
                    k_ref[0, (col - GDN_KEY_DIM) // dh] = yg.astype(BF16)
            else:
                v_ref[0, (col - n_qk) // dh] = yg.astype(BF16)
    a = gate_ref[0]
    lane = lax.broadcasted_iota(jnp.int32, a.shape, 1)
    gdec = -jnp.exp(alog_ref[...]) * _softplus(a + dtb_ref[...])
    g_ref[0] = jnp.where(lane < 2 * GDN_V_HEADS, gdec, jax.nn.sigmoid(a))


def gdn_prep(p_main, gates_raw, conv_w, alog_row, dtb_row, n_lat_tiles):
    b, s, _ = p_main.shape
    tm = ROW_TILE
    nt = s // tm
    nch = 2 * GDN_KEY_DIM + GDN_VAL_DIM
    hpt = tm // HALO
    nh = s // HALO
    kern = functools.partial(_gdn_prep_kernel, n_lat_tiles=n_lat_tiles, n_tiles=nt, cn=512)
    head_out = lambda nheads: pl.BlockSpec((1, nheads, tm, GDN_HEAD_DIM), lambda bi, i: (bi, 0, i, 0))
    return pl.pallas_call(
        kern,
        out_shape=[jax.ShapeDtypeStruct((b, GDN_K_HEADS, s, GDN_HEAD_DIM), BF16),
                   jax.ShapeDtypeStruct((b, GDN_K_HEADS, s, GDN_HEAD_DIM), BF16),
                   jax.ShapeDtypeStruct((b, GDN_V_HEADS, s, GDN_HEAD_DIM), BF16),
                   jax.ShapeDtypeStruct((b, s, LANES), F32)],
        grid=(b, nt),
        in_specs=[
            pl.BlockSpec((1, tm, nch), lambda bi, i: (bi, i, 0)),
            pl.BlockSpec((1, HALO, nch), lambda bi, i: (bi, jnp.maximum(i * hpt - 1, 0), 0)),
            pl.BlockSpec((1, HALO, nch), lambda bi, i: (bi, jnp.minimum((i + 1) * hpt, nh - 1), 0)),
            pl.BlockSpec((1, tm, LANES), lambda bi, i: (bi, i, 0)),
            pl.BlockSpec((GDN_CONV_W, nch), lambda bi, i: (0, 0)),
            pl.BlockSpec((1, LANES), lambda bi, i: (0, 0)),
            pl.BlockSpec((1, LANES), lambda bi, i: (0, 0)),
        ],
        out_specs=[head_out(GDN_K_HEADS), head_out(GDN_K_HEADS), head_out(GDN_V_HEADS),
                   pl.BlockSpec((1, tm, LANES), lambda bi, i: (bi, i, 0))],
        scratch_shapes=[pltpu.VMEM((tm + 2 * HALO, 512), F32)],
        compiler_params=_cparams(("arbitrary", "arbitrary")),
        name="gdn_prep",
    )(p_main, p_main, p_main, gates_raw, conv_w, alog_row, dtb_row)


def _chunk_masks(n, reverse):
    r = lax.broadcasted_iota(jnp.int32, (n, n), 0)
    c = lax.broadcasted_iota(jnp.int32, (n, n), 1)
    if reverse:
        return r <= c, r < c
    return r >= c, r > c


def _block_cumsum_mats(ct, reverse):
    r = lax.broadcasted_iota(jnp.int32, (ct, ct), 0)
    c = lax.broadcasted_iota(jnp.int32, (ct, ct), 1)
    same = (r // CHUNK) == (c // CHUNK)
    lower = jnp.logical_and(same, c <= r)
    upper = jnp.logical_and(same, c >= r)
    lo = jnp.where(lower, 1.0, 0.0).astype(BF16)
    up = jnp.where(upper, 1.0, 0.0).astype(BF16)
    return (up, lo) if reverse else (lo, up)


def _cumsum_cols(mat, x):
    x0, x1, x2 = _split3(x)
    return _dot(mat, x0) + (_dot(mat, x1) + _dot(mat, x2))


def _cumsum_rows(x, mat):
    x0, x1, x2 = _split3(x)
    return _dot(x0, mat) + (_dot(x1, mat) + _dot(x2, mat))


def _gdn_block(dirs):
    dh = GDN_HEAD_DIM
    eye = jnp.where(lax.broadcasted_iota(jnp.int32, (CHUNK, CHUNK), 0)
                    == lax.broadcasted_iota(jnp.int32, (CHUNK, CHUNK), 1), 1.0, 0.0).astype(F32)
    chains = []
    for q_ref, k_ref, v_ref, gc_ref, gr_ref, o_ref, s_sc, reverse in dirs:
        ct = q_ref.shape[2]
        goff = 2 if reverse else 0
        m_col, m_row = _block_cumsum_mats(ct, reverse)
        tril, strict = _chunk_masks(CHUNK, reverse)
        for kh in range(q_ref.shape[1]):
            gcol = gc_ref[0, kh]
            grow = gr_ref[0, kh]
            gcum_c = _cumsum_cols(m_col, gcol)
            gcum_r = _cumsum_rows(grow, m_row)
            for c in range(ct // CHUNK):
                rs = slice(c * CHUNK, (c + 1) * CHUNK)
                last = c * CHUNK if reverse else (c + 1) * CHUNK - 1
                q = q_ref[0, kh, rs, :]
                k = k_ref[0, kh, rs, :]
                kk = _dot_nt(k, k)
                qk = _dot_nt(q, k)
                for hh in range(2):
                    gi = goff + hh
                    chains.append(dict(
                        q=q, k=k, kk=kk, qk=qk, rs=rs, c=c, hh=2 * kh + hh, reverse=reverse, tril=tril,
                        strict=strict, v_ref=v_ref, o_ref=o_ref, s_sc=s_sc,
                        gc=gcum_c[rs, gi:gi + 1], gr=gcum_r[gi:gi + 1, rs],
                        glast=gcum_c[last:last + 1, gi:gi + 1], beta=gcol[rs, 4 + gi:5 + gi]))
    for ch in chains:
        tril = ch["tril"]
        ch["decay"] = jnp.where(tril, jnp.exp(jnp.where(tril, ch["gc"] - ch["gr"], 0.0)), 0.0)
        a = -jnp.where(ch["strict"], ch["kk"] * ch["beta"] * ch["decay"], 0.0)
        ch["tmat"] = eye + a
        ch["pw"] = a
    for _ in range(5):
        for ch in chains:
            pwb = ch["pw"].astype(BF16)
            ch["pw"] = _dot(pwb, pwb)
        for ch in chains:
            ch["tmat"] = ch["tmat"] + _dot(ch["tmat"].astype(BF16), ch["pw"].astype(BF16))
    for ch in chains:
        beta = ch["beta"]
        eg = jnp.exp(ch["gc"])
        vb = ch["v_ref"][0, ch["hh"], ch["rs"], :].astype(F32) * beta
        kbg = ch["k"].astype(F32) * (beta * eg)
        uw = _dot(ch["tmat"].astype(BF16), jnp.concatenate([vb, kbg], axis=-1).astype(BF16))
        ch["u"] = uw[:, :dh]
        ch["w"] = uw[:, dh:].astype(BF16)
        ch["eg"] = eg
        ch["attn"] = jnp.where(ch["tril"], ch["qk"] * ch["decay"], 0.0).astype(BF16)
    nchunk = max(ch["c"] for ch in chains) + 1
    for step in range(nchunk):
        cur = [ch for ch in chains if ch["c"] == (nchunk - 1 - step if ch["reverse"] else step)]
        for ch in cur:
            state = ch["s_sc"][ch["hh"]]
            sb = state.astype(BF16)
            ch["state"] = state
            ch["ws"] = _dot(ch["w"], sb)
            ch["qs"] = _dot(ch["q"], sb)
        for ch in cur:
            v_new = ch["u"] - ch["ws"]
            ch["kgv"] = (v_new * jnp.exp(ch["glast"] - ch["gc"])).astype(BF16)
            ch["o"] = ch["eg"] * ch["qs"] + _dot(ch["attn"], v_new.astype(BF16))
        for ch in cur:
            ch["s_sc"][ch["hh"]] = ch["state"] * jnp.exp(ch["glast"]) + _dot_tn(ch["k"], ch["kgv"])
            ch["o_ref"][0, ch["rs"], ch["hh"] * dh:(ch["hh"] + 1) * dh] = ch["o"].astype(BF16)


def _gdn_scan_kernel(qf, kf, vf, gcf, grf, qb, kb, vb, gcb, grb, s0f, s0b, *rest, nblk, aliased):
    if aliased:
        rest = rest[2:]
    of_ref, ob_ref, sff, sfb, sf_sc, sb_sc = rest
    j = pl.program_id(2)

    @pl.when(j == 0)
    def _():
        sf_sc[...] = s0f[0]
        sb_sc[...] = s0b[0]

    _gdn_block([(qf, kf, vf, gcf, grf, of_ref, sf_sc, False),
                (qb, kb, vb, gcb, grb, ob_ref, sb_sc, True)])

    @pl.when(j == nblk - 1)
    def _():
        sff[0] = sf_sc[...]
        sfb[0] = sb_sc[...]


def gdn_scan(qn, kn, vv, gcol, grow, s0f, s0b, *, nblk, off, prev=None):
    b, _, s, dh = qn.shape
    ct = SCAN_TILE
    g = GDN_HEADS_PER_STEP
    fwd = lambda j: j + off
    bwd = lambda j: nblk - 1 - j + off
    def specs(pos):
        return [
            pl.BlockSpec((1, g, ct, dh), lambda bi, h, j: (bi, h, pos(j), 0)),
            pl.BlockSpec((1, g, ct, dh), lambda bi, h, j: (bi, h, pos(j), 0)),
            pl.BlockSpec((1, 2 * g, ct, dh), lambda bi, h, j: (bi, h, pos(j), 0)),
            pl.BlockSpec((1, g, ct, 8), lambda bi, h, j: (bi, h, pos(j), 0)),
            pl.BlockSpec((1, g, 8, ct), lambda bi, h, j: (bi, h, 0, pos(j))),
        ]
    st_spec = pl.BlockSpec((1, 2 * g, dh, dh), lambda bi, h, j: (bi, h, 0, 0))
    in_specs = specs(fwd) + specs(bwd) + [st_spec, st_spec]
    args = [qn, kn, vv, gcol, grow] * 2 + [s0f, s0b]
    aliases = {}
    if prev is not None:
        in_specs += [pl.BlockSpec(memory_space=pl.ANY)] * 2
        args += list(prev)
        aliases = {12: 0, 13: 1}
    o_shape = jax.ShapeDtypeStruct((b, s, GDN_VAL_DIM), BF16)
    st_shape = jax.ShapeDtypeStruct((b, GDN_V_HEADS, dh, dh), F32)
    kern = functools.partial(_gdn_scan_kernel, nblk=nblk, aliased=prev is not None)
    return pl.pallas_call(
        kern,
        out_shape=[o_shape, o_shape, st_shape, st_shape],
        grid=(b, GDN_K_HEADS // g, nblk),
        in_specs=in_specs,
        out_specs=[pl.BlockSpec((1, ct, 2 * g * dh), lambda bi, h, j: (bi, fwd(j), h)),
                   pl.BlockSpec((1, ct, 2 * g * dh), lambda bi, h, j: (bi, bwd(j), h)),
                   st_spec, st_spec],
        scratch_shapes=[pltpu.VMEM((2 * g, dh, dh), F32), pltpu.VMEM((2 * g, dh, dh), F32)],
        input_output_aliases=aliases,
        compiler_params=_cparams(("arbitrary", "arbitrary", "arbitrary")),
        name="gdn_scan",
    )(*args)


def _gla_block(dirs, wg_ref, gb_ref):
    chains = []
    pre = []
    for q_ref, k_ref, v_ref, gr_ref, o_ref, st_sc, z in dirs:
        pre.append(_dot3(gr_ref[0], wg_ref[z]) + gb_ref[z:z + 1, :])
    for (q_ref, k_ref, v_ref, gr_ref, o_ref, st_sc, z), logit in zip(dirs, pre):
        reverse = z == 1
        ct = q_ref.shape[1]
        glog = -_softplus(-logit) / GLA_TAU
        m_col, _ = _block_cumsum_mats(ct, reverse)
        bcum = _cumsum_cols(m_col, glog)
        tril, _ = _chunk_masks(CHUNK, reverse)
        for c in range(ct // CHUNK):
            rs = slice(c * CHUNK, (c + 1) * CHUNK)
            last = c * CHUNK if reverse else (c + 1) * CHUNK - 1
            bc = bcum[rs, :]
            bl = bcum[last:last + 1, :]
            qf = q_ref[0, rs, :].astype(F32) * (GLA_DK ** -0.5)
            kf = k_ref[0, rs, :].astype(F32)
            chains.append(dict(
                c=c, rs=rs, reverse=reverse, tril=tril, o_ref=o_ref, st_sc=st_sc, v=v_ref[0, rs, :],
                qe=(qf * jnp.exp(bc)).astype(BF16), ke=(kf * jnp.exp(-bc)).astype(BF16),
                kg=(kf * jnp.exp(bl - bc)).astype(BF16), gl=jnp.exp(bl)))
    for ch in chains:
        ch["attn"] = jnp.where(ch["tril"], _dot_nt(ch["qe"], ch["ke"]), 0.0).astype(BF16)
    for ch in chains:
        ch["o"] = _dot(ch["attn"], ch["v"])
        ch["kv"] = _dot_tn(ch["v"], ch["kg"])
    nchunk = max(ch["c"] for ch in chains) + 1
    for step in range(nchunk):
        cur = [ch for ch in chains if ch["c"] == (nchunk - 1 - step if ch["reverse"] else step)]
        for ch in cur:
            st = ch["st_sc"][...]
            ch["o"] = ch["o"] + _dot_nt(ch["qe"], st.astype(BF16))
            ch["st_sc"][...] = st * ch["gl"] + ch["kv"]
        for ch in cur:
            ch["o_ref"][0, ch["rs"], :] = ch["o"].astype(BF16)


def _gla_scan_kernel(qf, kf, vf, grf, qb, kb, vb, grb, wg, gb, s0f, s0b, *rest, nblk, aliased):
    if aliased:
        rest = rest[2:]
    of_ref, ob_ref, sff, sfb, sf_sc, sb_sc = rest
    j = pl.program_id(2)

    @pl.when(j == 0)
    def _():
        sf_sc[...] = s0f[0, 0]
        sb_sc[...] = s0b[0, 0]

    _gla_block([(qf, kf, vf, grf, of_ref, sf_sc, 0), (qb, kb, vb, grb, ob_ref, sb_sc, 1)], wg, gb)

    @pl.when(j == nblk - 1)
    def _():
        sff[0, 0] = sf_sc[...]
        sfb[0, 0] = sb_sc[...]


def gla_scan(p_main, gr, wg_pad, gate_b, s0f, s0b, *, nblk, off, prev=None):
    b, s, _ = p_main.shape
    ct = SCAN_TILE
    nh, dk, dv = GLA_HEADS, GLA_DK, GLA_DV
    fwd = lambda j: j + off
    bwd = lambda j: nblk - 1 - j + off
    def specs(pos):
        return [
            pl.BlockSpec((1, ct, dk), lambda bi, h, j: (bi, pos(j), h)),
            pl.BlockSpec((1, ct, dk), lambda bi, h, j: (bi, pos(j), nh + h)),
            pl.BlockSpec((1, ct, dv), lambda bi, h, j: (bi, pos(j), nh + h)),
            pl.BlockSpec((1, ct, LANES), lambda bi, h, j: (bi, pos(j), 0)),
        ]
    st_spec = pl.BlockSpec((1, 1, dv, dk), lambda bi, h, j: (bi, h, 0, 0))
    in_specs = specs(fwd) + specs(bwd) + [
        pl.BlockSpec((2, LANES, dk), lambda bi, h, j: (0, 0, h)),
        pl.BlockSpec((2, dk), lambda bi, h, j: (0, h)),
        st_spec, st_spec]
    args = [p_main, p_main, p_main, gr] * 2 + [wg_pad, gate_b, s0f, s0b]
    aliases = {}
    if prev is not None:
        in_specs += [pl.BlockSpec(memory_space=pl.ANY)] * 2
        args += list(prev)
        aliases = {12: 0, 13: 1}
    o_shape = jax.ShapeDtypeStruct((b, s, GLA_VAL_DIM), BF16)
    st_shape = jax.ShapeDtypeStruct((b, nh, dv, dk), F32)
    kern = functools.partial(_gla_scan_kernel, nblk=nblk, aliased=prev is not None)
    return pl.pallas_call(
        kern,
        out_shape=[o_shape, o_shape, st_shape, st_shape],
        grid=(b, nh, nblk),
        in_specs=in_specs,
        out_specs=[pl.BlockSpec((1, ct, dv), lambda bi, h, j: (bi, fwd(j), h)),
                   pl.BlockSpec((1, ct, dv), lambda bi, h, j: (bi, bwd(j), h)),
                   st_spec, st_spec],
        scratch_shapes=[pltpu.VMEM((dv, dk), F32), pltpu.VMEM((dv, dk), F32)],
        input_output_aliases=aliases,
        compiler_params=_cparams(("arbitrary", "arbitrary", "arbitrary")),
        name="gla_scan",
    )(*args)


def _router_kernel(x_ref, sc_ref, sh_ref, rwt_ref, rb_ref, h_ref, wd_ref):
    h = x_ref[0] * (1.0 + sc_ref[0, 0]) + sh_ref[0, 0]
    h_ref[0] = h.astype(BF16)
    w0, w1 = _split2(rwt_ref[...])
    h0, h1 = _split2(h)
    scores = jax.nn.sigmoid(_dot_nt(w0, h0) + (_dot_nt(w0, h1) + _dot_nt(w1, h0)))
    ne = scores.shape[0]
    row = lax.broadcasted_iota(jnp.int32, scores.shape, 0)
    neg = jnp.float32(-jnp.inf)
    sel = scores + rb_ref[...]
    chosen = jnp.zeros(scores.shape, jnp.bool_)
    for _ in range(TOP_K):
        mx = jnp.max(sel, axis=0, keepdims=True)
        first = jnp.min(jnp.where(sel == mx, row, ne), axis=0, keepdims=True)
        pick = row == first
        chosen = jnp.logical_or(chosen, pick)
        sel = jnp.where(pick, neg, sel)
    picked = jnp.where(chosen, scores, 0.0)
    wt = picked / jnp.sum(picked, axis=0, keepdims=True) * ROUTE_SCALE
    eye = jnp.where(lax.broadcasted_iota(jnp.int32, (ne, LANES), 0)
                    == lax.broadcasted_iota(jnp.int32, (ne, LANES), 1), 1.0, 0.0).astype(BF16)
    t0, t1, t2 = _split3(wt)
    wd_ref[0] = _dot_tn(t0, eye) + (_dot_tn(t1, eye) + _dot_tn(t2, eye))


def moe_route(x, sc, sh, rw_t, rb_col, n_lat_tiles):
    b, s, d = x.shape
    tm = ROW_TILE
    mod_spec = pl.BlockSpec((1, 1, 1, d), lambda bi, i: (bi, i // n_lat_tiles, 0, 0))
    return pl.pallas_call(
        _router_kernel,
        out_shape=[jax.ShapeDtypeStruct((b, s, d), BF16), jax.ShapeDtypeStruct((b, s, LANES), F32)],
        grid=(b, s // tm),
        in_specs=[pl.BlockSpec((1, tm, d), lambda bi, i: (bi, i, 0)), mod_spec, mod_spec,
                  pl.BlockSpec((N_EXPERTS, d), lambda bi, i: (0, 0)),
                  pl.BlockSpec((N_EXPERTS, 1), lambda bi, i: (0, 0))],
        out_specs=[pl.BlockSpec((1, tm, d), lambda bi, i: (bi, i, 0)),
                   pl.BlockSpec((1, tm, LANES), lambda bi, i: (bi, i, 0))],
        compiler_params=_cparams(("arbitrary", "arbitrary")),
        name="moe_router",
    )(x, sc, sh, rw_t, rb_col)


def _moe_kernel(h_ref, wd_ref, wgu_ref, wdn_ref, x_ref, gate_ref, lg_ref, lb_ref, out_ref, acc, *,
                n_exp, n_lat_rows):
    e = pl.program_id(2)

    @pl.when(e == 0)
    def _():
        acc[...] = jnp.zeros(acc.shape, F32)

    gu = _dot(h_ref[0], wgu_ref[0])
    act = _silu(gu[:, :EXPERT_FF]) * gu[:, EXPERT_FF:]
    wd = wd_ref[0]
    lane = lax.broadcasted_iota(jnp.int32, wd.shape, 1)
    wcol = jnp.sum(jnp.where(lane == e, wd, 0.0), axis=-1, keepdims=True)
    acc[...] += wcol * _dot(act.astype(BF16), wdn_ref[0])

    @pl.when(e == n_exp - 1)
    def _():
        tm = acc.shape[0]
        row = pl.program_id(1) * tm + lax.broadcasted_iota(jnp.int32, (tm, 1), 0)
        gate = jnp.where(row >= n_lat_rows, gate_ref[0, 1], gate_ref[0, 0])
        out_ref[0] = _post_norm(x_ref[0], acc[...], gate, lg_ref[...], lb_ref[...])


def moe_experts_post_norm(h, wd, wgu, wdn, x, gate, lg, lb, n_lat_rows, tm):
    b, s, d = x.shape
    n_exp = wgu.shape[0]
    row = lambda width: pl.BlockSpec((1, tm, width), lambda bi, i, e: (bi, i, 0))
    kern = functools.partial(_moe_kernel, n_exp=n_exp, n_lat_rows=n_lat_rows)
    return pl.pallas_call(
        kern,
        out_shape=jax.ShapeDtypeStruct((b, s, d), F32),
        grid=(b, s // tm, n_exp),
        in_specs=[row(d), row(LANES),
                  pl.BlockSpec((1, d, 2 * EXPERT_FF), lambda bi, i, e: (e, 0, 0)),
                  pl.BlockSpec((1, EXPERT_FF, d), lambda bi, i, e: (e, 0, 0)),
                  row(d),
                  pl.BlockSpec((1, 2, 1, d), lambda bi, i, e: (bi, 0, 0, 0)),
                  pl.BlockSpec((1, d), lambda bi, i, e: (0, 0)),
                  pl.BlockSpec((1, d), lambda bi, i, e: (0, 0))],
        out_specs=row(d),
        scratch_shapes=[pltpu.VMEM((tm, d), F32)],
        compiler_params=_cparams(("arbitrary", "arbitrary", "arbitrary")),
        name="moe_experts_post_norm",
    )(h, wd, wgu, wdn, x, gate, lg.reshape(1, d), lb.reshape(1, d))


MOE_SUB = 256
MOE_BLK = 32
MOE_NBLK = MOE_SUB // MOE_BLK
MOE_MAX_STEPS = N_EXPERTS * (MOE_SUB // MOE_BLK) // MOE_NBLK


def _moe_grouped_kernel(blk_e_ref, blk_r_ref, nstep_ref, h_ref, wd_ref, *rest):
    wgu_refs = rest[:MOE_NBLK]
    wdn_refs = rest[MOE_NBLK:2 * MOE_NBLK]
    wsgu_ref, wsdn_ref, f_ref, rcm, rrm, wbf = rest[2 * MOE_NBLK:]
    g = pl.program_id(2)
    tm = f_ref.shape[1]
    nsub = tm // MOE_SUB
    sub, blk = MOE_SUB, MOE_BLK
    lane = lax.broadcasted_iota(jnp.int32, (1, LANES), 1)

    @pl.when(g == 0)
    def _():
        r = lax.broadcasted_iota(jnp.int32, (sub, sub), 0)
        c = lax.broadcasted_iota(jnp.int32, (sub, sub), 1)
        before = jnp.where(c < r, 1.0, 0.0).astype(BF16)
        after = jnp.where(r < c, 1.0, 0.0).astype(BF16)
        ident = jnp.where(r == c, 1.0, 0.0).astype(BF16)
        for u in range(nsub):
            rs = slice(u * sub, (u + 1) * sub)
            hu = h_ref[0, rs, :]
            gu = _dot(hu, wsgu_ref[...])
            act = _silu(gu[:, :EXPERT_FF]) * gu[:, EXPERT_FF:]
            f_ref[0, rs, :] = _dot(act.astype(BF16), wsdn_ref[...])
            wd = wd_ref[0, rs, :]
            active = jnp.logical_and(wd != 0.0, lane < N_EXPERTS)
            ab = jnp.where(active, 1.0, 0.0).astype(BF16)
            rank_c = _dot(before, ab)
            rank_r = _dot_tn(ab, after)
            a_r = _dot_tn(ab, ident)
            rcm[u] = jnp.where(active, rank_c, -1.0).astype(BF16)
            rrm[u] = jnp.where(a_r > 0.5, rank_r, -1.0).astype(BF16)
            wbf[rs, :] = wd.astype(BF16)

    @pl.when(g < nstep_ref[0])
    def _():
        experts = [blk_e_ref[g * MOE_NBLK + k] for k in range(MOE_NBLK)]
        first_rank = [blk_r_ref[g * MOE_NBLK + k] * blk for k in range(MOE_NBLK)]

        def block_of(l):
            return sum(((l >= k * blk).astype(jnp.int32) for k in range(1, MOE_NBLK)), jnp.zeros_like(l))

        def per_block(l, values):
            pos = block_of(l)
            out = jnp.zeros(l.shape, jnp.int32)
            for k in range(MOE_NBLK):
                out = jnp.where(pos == k, values[k], out)
            return out

        def wanted_rank(l):
            return (l - blk * block_of(l) + per_block(l, first_rank)).astype(F32)

        e_i = lax.broadcasted_iota(jnp.int32, (LANES, sub), 0)
        l_i = lax.broadcasted_iota(jnp.int32, (LANES, sub), 1)
        expand = jnp.where(e_i == per_block(l_i, experts), 1.0, 0.0).astype(BF16)
        l_t = lax.broadcasted_iota(jnp.int32, (sub, LANES), 0)
        e_t = lax.broadcasted_iota(jnp.int32, (sub, LANES), 1)
        expand_t = jnp.where(e_t == per_block(l_t, experts), 1.0, 0.0).astype(BF16)
        want_row = wanted_rank(lax.broadcasted_iota(jnp.int32, (1, sub), 1))
        want_col = wanted_rank(lax.broadcasted_iota(jnp.int32, (sub, 1), 0))

        xg = []
        for u in range(nsub):
            rs = slice(u * sub, (u + 1) * sub)
            rank_of_slot = _dot(expand_t, rrm[u])
            gather = jnp.where(rank_of_slot == want_col, 1.0, 0.0).astype(BF16)
            xg.append(_dot(gather, h_ref[0, rs, :]).astype(BF16))
        ys = []
        for k in range(MOE_NBLK):
            es = slice(k * blk, (k + 1) * blk)
            x_e = jnp.concatenate([xg[u][es] for u in range(nsub)], axis=0)
            gu = _dot(x_e, wgu_refs[k][0])
            act = _silu(gu[:, :EXPERT_FF]) * gu[:, EXPERT_FF:]
            ys.append(_dot(act.astype(BF16), wdn_refs[k][0]).astype(BF16))
        for u in range(nsub):
            rs = slice(u * sub, (u + 1) * sub)
            y_u = jnp.concatenate([ys[k][u * blk:(u + 1) * blk] for k in range(MOE_NBLK)], axis=0)
            slot_of_row = _dot(rcm[u], expand)
            weight = _dot(wbf[rs, :], expand)
            scatter = jnp.where(slot_of_row == want_row, weight, 0.0).astype(BF16)
            f_ref[0, rs, :] += _dot(scatter, y_u)


def _moe_block_plan(wd):
    b, s, _ = wd.shape
    active = (wd[..., :N_EXPERTS] != 0.0).astype(jnp.int32)
    counts = active.reshape(b * (s // MOE_SUB), MOE_SUB, N_EXPERTS).sum(1)
    nblk = (counts.max(0) + MOE_BLK - 1) // MOE_BLK
    total = nblk.sum()
    cap = MOE_MAX_STEPS * MOE_NBLK
    first = jnp.cumsum(nblk) - nblk
    blk_e = jnp.repeat(jnp.arange(N_EXPERTS, dtype=jnp.int32), nblk, total_repeat_length=cap)
    idx = jnp.arange(cap, dtype=jnp.int32)
    blk_r = jnp.where(idx < total, idx - first[blk_e], MOE_SUB // MOE_BLK)
    nstep = (total + MOE_NBLK - 1) // MOE_NBLK
    return blk_e.astype(jnp.int32), blk_r.astype(jnp.int32), nstep.reshape(1).astype(jnp.int32)


def moe_grouped_experts(h, wd, wgu, wdn, wsgu, wsdn, tm):
    b, s, d = h.shape
    blk_e, blk_r, nstep = _moe_block_plan(wd)
    row = lambda width: pl.BlockSpec((1, tm, width), lambda bi, i, g, be, br, ns: (bi, i, 0))

    def expert(shape, k):
        return pl.BlockSpec((1,) + shape, lambda bi, i, g, be, br, ns: (be[g * MOE_NBLK + k], 0, 0))

    in_specs = ([row(d), row(LANES)]
                + [expert((d, 2 * EXPERT_FF), k) for k in range(MOE_NBLK)]
                + [expert((EXPERT_FF, d), k) for k in range(MOE_NBLK)]
                + [pl.BlockSpec((d, 2 * EXPERT_FF), lambda bi, i, g, be, br, ns: (0, 0)),
                   pl.BlockSpec((EXPERT_FF, d), lambda bi, i, g, be, br, ns: (0, 0))])
    grid_spec = pltpu.PrefetchScalarGridSpec(
        num_scalar_prefetch=3, grid=(b, s // tm, MOE_MAX_STEPS), in_specs=in_specs, out_specs=row(d),
        scratch_shapes=[pltpu.VMEM((tm // MOE_SUB, MOE_SUB, LANES), BF16),
                        pltpu.VMEM((tm // MOE_SUB, LANES, MOE_SUB), BF16),
                        pltpu.VMEM((tm, LANES), BF16)])
    return pl.pallas_call(
        _moe_grouped_kernel,
        out_shape=jax.ShapeDtypeStruct((b, s, d), F32),
        grid_spec=grid_spec,
        compiler_params=_cparams(("arbitrary", "arbitrary", "arbitrary")),
        name="moe_grouped_experts",
    )(blk_e, blk_r, nstep, h, wd, *([wgu] * MOE_NBLK), *([wdn] * MOE_NBLK), wsgu, wsdn)


def _post_norm_kernel(x_ref, f_ref, gate_ref, lg_ref, lb_ref, out_ref):
    out_ref[0] = _post_norm(x_ref[0], f_ref[0].astype(F32), gate_ref[0, 0], lg_ref[...], lb_ref[...])


def post_norm_rows(x, f, gate, lg, lb, n_lat_tiles, rows_out=None):
    b, s, d = x.shape
    tm = ROW_TILE
    rows_out = s if rows_out is None else rows_out
    row = pl.BlockSpec((1, tm, d), lambda bi, i: (bi, i, 0))
    vec = pl.BlockSpec((1, d), lambda bi, i: (0, 0))
    return pl.pallas_call(
        _post_norm_kernel, out_shape=jax.ShapeDtypeStruct((b, rows_out, d), F32), grid=(b, rows_out // tm),
        in_specs=[row, row, pl.BlockSpec((1, 1, 1, d), lambda bi, i: (bi, i // n_lat_tiles, 0, 0)), vec, vec],
        out_specs=row, compiler_params=_cparams(("arbitrary", "arbitrary")), name="post_norm_rows",
    )(x, f, gate, lg.reshape(1, d), lb.reshape(1, d))


def _rope_tables(n_lat, n_ctx):
    rows = n_lat // GRID_W
    rowp = jnp.repeat(jnp.arange(rows), GRID_W).astype(F32)
    colp = jnp.tile(jnp.arange(GRID_W), rows).astype(F32)
    n_freq = DA_HEAD_DIM // 4
    inv = 1.0 / (ROPE_BASE ** (jnp.arange(n_freq, dtype=F32) / n_freq))
    ang = jnp.concatenate([rowp[:, None] * inv, colp[:, None] * inv], -1)
    cos, sin = jnp.cos(ang), jnp.sin(ang)
    cos_t = jnp.tile(cos, (1, 4))
    sin_t = jnp.tile(jnp.concatenate([-sin, sin], -1), (1, 2))
    cos_t = jnp.concatenate([cos_t, jnp.ones((n_ctx, LANES), F32)], 0)
    sin_t = jnp.concatenate([sin_t, jnp.zeros((n_ctx, LANES), F32)], 0)
    return cos_t, sin_t


def _pick_tile(total, cands):
    for t in cands:
        if total % t == 0:
            return t
    raise ValueError(f"no tile for {total}")


def _pad_cols(w, n):
    return jnp.pad(w, ((0, 0), (0, n - w.shape[1])))


def _flash_both(p_all, lam_vec, subln, lam_init, n_lat, n_ctx):
    s = n_lat + n_ctx
    tq = _pick_tile(n_lat, (1024, 512, 256))
    tk = _pick_tile(s, (3328, 1280, 1024, 512, 256))
    o = diff_flash_attention(p_all, lam_vec, subln, lam_init, tq=tq, tk=tk, nq=n_lat // tq, nk=s // tk,
                             q_off=0, k_off=0)
    return diff_flash_attention(p_all, lam_vec, subln, lam_init, tq=n_ctx, tk=n_ctx, nq=1, nk=1,
                                q_off=n_lat // n_ctx, k_off=n_lat // n_ctx, prev=o)


def kernel(x, c, ctx, c_ctx, ada_w, ada_b, ln_g, ln_b, da_w_in, da_w_o, da_lambda, da_subln, gdn_w_in, gdn_conv, gdn_a_log, gdn_dt_bias, gdn_norm, gdn_w_o, gla_w_in, gla_w_gate, gla_gate_b, gla_norm, gla_w_o, moe_router, moe_router_b, moe_w_gu, moe_w_dn, moe_ws_gu, moe_ws_dn):
    b, n, d = x.shape
    lc = ctx.shape[1]
    assert lc == ROW_TILE and n % SCAN_TILE == 0 and d == D_MODEL
    s = n + lc
    n_lat_tiles = n // ROW_TILE
    depth = ada_w.shape[0]

    xall = jnp.concatenate([x, ctx], axis=1)
    c8 = jnp.concatenate([c, c_ctx[None], jnp.zeros((8 - b - 1, d), F32)], 0)
    mods = ada_modulation(c8, ada_w, ada_b)
    rope = _rope_tables(n, lc)
    moe_tm = _pick_tile(s, (1280, 1024, 512, 256))

    for i in range(depth):
        kind, j = i % N_MIXERS, i // N_MIXERS
        m = mods[i].reshape(8, ADA_CHUNKS, d)
        mod = jnp.stack([m[:b], jnp.broadcast_to(m[b], (b, ADA_CHUNKS, d))], axis=1)
        mod = [mod[:, :, k][:, :, None, :] for k in range(ADA_CHUNKS)]

        if kind == 0:
            lam_init = 0.8 - 0.6 * math.exp(-0.3 * i)
            w = da_w_in[j]
            w = jnp.concatenate([w[:, :d] * (DA_HEAD_DIM ** -0.5 * math.log2(math.e)), w[:, d:]], 1).astype(BF16)
            p_all = modulated_projection(xall, mod[1], mod[0], w, rope=rope, n_rope=2 * d,
                                         n_lat_tiles=n_lat_tiles, head_major=True)
            o = _flash_both(p_all, da_lambda[j], da_subln[j], lam_init, n, lc)
            xall = out_projection_post_norm([o], da_w_o[j].astype(BF16), xall, mod[2],
                                            ln_g[i, 0], ln_b[i, 0], n_lat_tiles)
        elif kind == 1:
            w = gdn_w_in[j]
            n_main = 2 * GDN_KEY_DIM + 2 * GDN_VAL_DIM
            p_main, gates_raw = modulated_projection(
                xall, mod[1], mod[0], w[:, :n_main].astype(BF16),
                w_small=_pad_cols(w[:, n_main:], LANES).astype(BF16), n_lat_tiles=n_lat_tiles)
            alog_row = _pad_cols(gdn_a_log[j].reshape(1, -1), LANES)
            dtb_row = _pad_cols(gdn_dt_bias[j].reshape(1, -1), LANES)
            qn, kn, vv, gates = gdn_prep(p_main, gates_raw, gdn_conv[j], alog_row, dtb_row, n_lat_tiles)
            hv, hk = GDN_V_HEADS, GDN_K_HEADS
            def per_khead(t):
                t = t.reshape(b, s, 2, hk, 2)
                return jnp.transpose(t, (0, 3, 1, 2, 4)).reshape(b, hk, s, 4)
            gcol = jnp.concatenate([per_khead(gates[..., :2 * hv]), per_khead(gates[..., 2 * hv:4 * hv])], -1)
            grow = jnp.swapaxes(gcol, 2, 3)
            zeros = jnp.zeros((b, hv, GDN_HEAD_DIM, GDN_HEAD_DIM), F32)
            of, ob, scf, scb = gdn_scan(qn, kn, vv, gcol, grow, zeros, zeros, nblk=lc // SCAN_TILE,
                                        off=n // SCAN_TILE)
            of, ob, _, _ = gdn_scan(qn, kn, vv, gcol, grow, scf, scb, nblk=n // SCAN_TILE, off=0,
                                    prev=(of, ob))
            xall = out_projection_post_norm([of, ob, p_main, gdn_norm[j]], gdn_w_o[j].astype(BF16), xall,
                                            mod[2], ln_g[i, 0], ln_b[i, 0], n_lat_tiles,
                                            gated=(GDN_V_HEADS, GDN_HEAD_DIM, 2))
        else:
            w = gla_w_in[j]
            n_main = 2 * GLA_KEY_DIM + 2 * GLA_VAL_DIM
            p_main, gr = modulated_projection(
                xall, mod[1], mod[0], w[:, :n_main].astype(BF16),
                w_small=_pad_cols(w[:, n_main:], LANES).astype(BF16), n_lat_tiles=n_lat_tiles)
            wg = gla_w_gate[j]
            wg_pad = jnp.zeros((2, LANES, GLA_KEY_DIM), F32)
            wg_pad = wg_pad.at[0, :GLA_GATE_RANK].set(wg[0]).at[1, GLA_GATE_RANK:2 * GLA_GATE_RANK].set(wg[1])
            zeros = jnp.zeros((b, GLA_HEADS, GLA_DV, GLA_DK), F32)
            of, ob, scf, scb = gla_scan(p_main, gr, wg_pad, gla_gate_b[j], zeros, zeros,
                                        nblk=lc // SCAN_TILE, off=n // SCAN_TILE)
            of, ob, _, _ = gla_scan(p_main, gr, wg_pad, gla_gate_b[j], scf, scb, nblk=n // SCAN_TILE, off=0,
                                    prev=(of, ob))
            xall = out_projection_post_norm([of, ob, p_main, gla_norm[j]], gla_w_o[j].astype(BF16), xall,
                                            mod[2], ln_g[i, 0], ln_b[i, 0], n_lat_tiles,
                                            gated=(GLA_HEADS, GLA_DV, 2))

        h, wd = moe_route(xall, mod[4], mod[3], moe_router[i].T, moe_router_b[i].reshape(-1, 1), n_lat_tiles)
        f = moe_grouped_experts(h, wd, moe_w_gu[i].astype(BF16), moe_w_dn[i].astype(BF16),
                                moe_ws_gu[i].astype(BF16), moe_ws_dn[i].astype(BF16), moe_tm)
        xall = post_norm_rows(xall, f, mod[5], ln_g[i, 1], ln_b[i, 1], n_lat_tiles,
                              rows_out=n if i == depth - 1 else None)

    return xall
```

```python
import functools
import math

import jax
import jax.numpy as jnp
from jax import lax
from jax.experimental import pallas as pl
from jax.experimental.pallas import tpu as pltpu

F32 = jnp.float32
BF16 = jnp.bfloat16

D_MODEL = 1024
DEPTH = 4
GRID_W = 64
N_MIXERS = 3
DN_ALPHA = (2 * DEPTH) ** 0.25
LN_EPS = 1e-5
NORM_EPS = 1e-6
ADA_CHUNKS = 6

DA_HEADS = 8
DA_HEAD_DIM = 64
ROPE_BASE = 10000.0

GDN_K_HEADS = 8
GDN_V_HEADS = 16
GDN_HEAD_DIM = 128
GDN_KEY_DIM = 1024
GDN_VAL_DIM = 2048
GDN_CONV_W = 5
CHUNK = 64

GLA_HEADS = 4
GLA_KEY_DIM = 512
GLA_VAL_DIM = 1024
GLA_DK = 128
GLA_DV = 256
GLA_GATE_RANK = 16
GLA_TAU = 16.0

N_EXPERTS = 64
TOP_K = 8
EXPERT_FF = 256
ROUTE_SCALE = 2.5

LANES = 128
ROW_TILE = 256
SCAN_TILE = 256
GDN_HEADS_PER_STEP = 4
GLA_HEADS_PER_STEP = 4
HALO = 16
VMEM_LIMIT = 56 * 1024 * 1024


def _cparams(sem, flags=None):
    return pltpu.CompilerParams(dimension_semantics=sem, vmem_limit_bytes=VMEM_LIMIT, flags=flags)


def _dot(a, b):
    return jnp.dot(a, b, preferred_element_type=F32)


def _dot_nt(a, b):
    return lax.dot_general(a, b, (((1,), (1,)), ((), ())), preferred_element_type=F32)


def _dot_tn(a, b):
    return lax.dot_general(a, b, (((0,), (0,)), ((), ())), preferred_element_type=F32)


def _split2(a):
    hi = a.astype(BF16)
    lo = (a - hi.astype(F32)).astype(BF16)
    return hi, lo


def _split3(a):
    a0 = a.astype(BF16)
    r = a - a0.astype(F32)
    a1 = r.astype(BF16)
    a2 = (r - a1.astype(F32)).astype(BF16)
    return a0, a1, a2


def _dot3(a, b):
    a0, a1 = _split2(a)
    b0, b1 = _split2(b)
    return _dot(a0, b0) + (_dot(a0, b1) + _dot(a1, b0))


def _silu(x):
    return x * jax.nn.sigmoid(x)


def _softplus(x):
    return jnp.maximum(x, 0.0) + jnp.log(1.0 + jnp.exp(-jnp.abs(x)))


def _ada_kernel(c_ref, w_ref, b_ref, o_ref):
    s = _silu(c_ref[...])
    o_ref[0] = _dot3(s, w_ref[0]) + b_ref[0]


def ada_modulation(c8, ada_w, ada_b):
    depth, d, n = ada_w.shape
    tn = 1536
    return pl.pallas_call(
        _ada_kernel,
        out_shape=jax.ShapeDtypeStruct((depth, 8, n), F32),
        grid=(depth, n // tn),
        in_specs=[
            pl.BlockSpec((8, d), lambda i, j: (0, 0)),
            pl.BlockSpec((1, d, tn), lambda i, j: (i, 0, j)),
            pl.BlockSpec((1, 1, tn), lambda i, j: (i, 0, j)),
        ],
        out_specs=pl.BlockSpec((1, 8, tn), lambda i, j: (i, 0, j)),
        compiler_params=_cparams(("arbitrary", "arbitrary")),
        name="ada_modulation",
    )(c8, ada_w, ada_b.reshape(depth, 1, n))


def _proj_kernel(*refs, n_main, n_rope, has_small, cn, head_major):
    it = iter(refs)
    x_ref, sc_ref, sh_ref, w_ref = next(it), next(it), next(it), next(it)
    ws_ref = next(it) if has_small else None
    cos_ref = next(it) if n_rope else None
    sin_ref = next(it) if n_rope else None
    o_ref = next(it)
    os_ref = next(it) if has_small else None

    h = x_ref[0] * (1.0 + sc_ref[0, 0]) + sh_ref[0, 0]
    hb = h.astype(BF16)
    tm = hb.shape[0]
    if n_rope:
        cos = cos_ref[...]
        sin = sin_ref[...]
        lane = lax.broadcasted_iota(jnp.int32, (tm, LANES), 1)
        low_half = (lane & 32) == 0
    for j in range(n_main // cn):
        p = _dot(hb, w_ref[:, j * cn:(j + 1) * cn])
        for g in range(cn // LANES):
            pg = p[:, g * LANES:(g + 1) * LANES]
            col = j * cn + g * LANES
            if col < n_rope:
                partner = jnp.where(low_half, pltpu.roll(pg, LANES - 32, 1), pltpu.roll(pg, 32, 1))
                pg = pg * cos + partner * sin
            if head_major:
                o_ref[0, col // LANES] = pg.astype(BF16)
            else:
                o_ref[0, :, col:col + LANES] = pg.astype(BF16)
    if has_small:
        os_ref[0] = _dot(hb, ws_ref[...])


def modulated_projection(x, sc, sh, w, w_small=None, rope=None, n_rope=0, n_lat_tiles=1, head_major=False):
    b, s, d = x.shape
    n_main = w.shape[1]
    tm = ROW_TILE
    grid = (b, s // tm)
    mod_spec = pl.BlockSpec((1, 1, 1, d), lambda bi, i: (bi, i // n_lat_tiles, 0, 0))
    in_specs = [pl.BlockSpec((1, tm, d), lambda bi, i: (bi, i, 0)), mod_spec, mod_spec,
                pl.BlockSpec((d, n_main), lambda bi, i: (0, 0))]
    args = [x, sc, sh, w]
    if head_major:
        out_shape = [jax.ShapeDtypeStruct((b, n_main // LANES, s, LANES), BF16)]
        out_specs = [pl.BlockSpec((1, n_main // LANES, tm, LANES), lambda bi, i: (bi, 0, i, 0))]
    else:
        out_shape = [jax.ShapeDtypeStruct((b, s, n_main), BF16)]
        out_specs = [pl.BlockSpec((1, tm, n_main), lambda bi, i: (bi, i, 0))]
    if w_small is not None:
        in_specs.append(pl.BlockSpec((d, LANES), lambda bi, i: (0, 0)))
        args.append(w_small)
        out_shape.append(jax.ShapeDtypeStruct((b, s, LANES), F32))
        out_specs.append(pl.BlockSpec((1, tm, LANES), lambda bi, i: (bi, i, 0)))
    if n_rope:
        tab = pl.BlockSpec((tm, LANES), lambda bi, i: (i, 0))
        in_specs += [tab, tab]
        args += [rope[0], rope[1]]
    kern = functools.partial(_proj_kernel, n_main=n_main, n_rope=n_rope,
                             has_small=w_small is not None, cn=512, head_major=head_major)
    out = pl.pallas_call(
        kern, out_shape=out_shape, grid=grid, in_specs=in_specs, out_specs=out_specs,
        compiler_params=_cparams(("arbitrary", "arbitrary")), name="modulated_projection",
    )(*args)
    return out if w_small is not None else out[0]


NEG_INIT = -1e30
FLASH_ROW_BLOCK = 512
FLASH_LOOKAHEAD = 1


def _flash_kernel(lam_ref, q_ref, k_ref, v_ref, sub_ref, *rest, lam_init, nk, aliased):
    if aliased:
        rest = rest[1:]
    o_ref, m_sc, l_sc, acc_sc, s_ring = rest
    ki = pl.program_id(3)
    tq = m_sc.shape[1]
    tk = k_ref.shape[2]
    nring = s_ring.shape[0]

    @pl.when(ki == 0)
    def _():
        m_sc[...] = jnp.full(m_sc.shape, NEG_INIT, F32)
        l_sc[...] = jnp.zeros(l_sc.shape, F32)
        acc_sc[...] = jnp.zeros(acc_sc.shape, F32)

    q = q_ref[0, 0]
    k = k_ref[0, 0]
    v = v_ref[0, 0]
    hd = DA_HEAD_DIM
    rb = min(tq, FLASH_ROW_BLOCK)
    kc = 2 * LANES
    blocks = [(c, r0) for c in range(2) for r0 in range(0, tq, rb)]
    for i in range(len(blocks) + FLASH_LOOKAHEAD):
        if i < len(blocks):
            c, r0 = blocks[i]
            s_ring[i % nring] = _dot_nt(q[r0:r0 + rb, c * hd:(c + 1) * hd], k[:, c * hd:(c + 1) * hd])
        j = i - FLASH_LOOKAHEAD
        if j < 0:
            continue
        c, r0 = blocks[j]
        rows = slice(r0, r0 + rb)
        s_blk = s_ring.at[j % nring]
        m_prev = m_sc[c, rows, :]
        m_new = jnp.maximum(m_prev, jnp.max(s_blk[...], axis=-1, keepdims=True))
        alpha = jnp.exp2(m_prev - m_new)
        m2 = jnp.concatenate([m_new, m_new], axis=-1)
        lsum = None
        pv = None
        for t in range(tk // kc):
            pj = jnp.exp2(s_blk[:, t * kc:(t + 1) * kc] - m2)
            lj = pj[:, :LANES] + pj[:, LANES:]
            lsum = lj if lsum is None else lsum + lj
            d = _dot(pj.astype(BF16), v[t * kc:(t + 1) * kc, :])
            pv = d if pv is None else pv + d
        l_sc[c, rows, :] = alpha * l_sc[c, rows, :] + lsum
        acc_sc[c, rows, :] = alpha * acc_sc[c, rows, :] + pv
        m_sc[c, rows, :] = m_new

    @pl.when(ki == nk - 1)
    def _():
        lv = lam_ref[...]
        lam = (jnp.exp(jnp.sum(lv[0:1] * lv[1:2], axis=-1, keepdims=True))
               - jnp.exp(jnp.sum(lv[2:3] * lv[3:4], axis=-1, keepdims=True)) + lam_init)
        l0 = jnp.sum(l_sc[0], axis=-1, keepdims=True)
        l1 = jnp.sum(l_sc[1], axis=-1, keepdims=True)
        o = acc_sc[0] / l0 - lam * (acc_sc[1] / l1)
        ms = jnp.mean(o * o, axis=-1, keepdims=True)
        o = o * lax.rsqrt(ms + NORM_EPS) * sub_ref[...] * (1.0 - lam_init)
        o_ref[0] = o.astype(BF16)


def diff_flash_attention(p_all, lam_vec, subln, lam_init, *, tq, tk, nq, nk, q_off, k_off, prev=None):
    b, _, s, _ = p_all.shape
    hh = DA_HEADS
    in_specs = [
        pl.BlockSpec((4, DA_HEAD_DIM), lambda bi, h, qi, ki: (0, 0)),
        pl.BlockSpec((1, 1, tq, LANES), lambda bi, h, qi, ki: (bi, h, qi + q_off, 0)),
        pl.BlockSpec((1, 1, tk, LANES), lambda bi, h, qi, ki: (bi, hh + h, ki + k_off, 0)),
        pl.BlockSpec((1, 1, tk, LANES), lambda bi, h, qi, ki: (bi, 2 * hh + h, ki + k_off, 0)),
        pl.BlockSpec((1, LANES), lambda bi, h, qi, ki: (0, 0)),
    ]
    args = [lam_vec, p_all, p_all, p_all, subln.reshape(1, LANES)]
    aliases = {}
    if prev is not None:
        in_specs.append(pl.BlockSpec(memory_space=pl.ANY))
        args.append(prev)
        aliases = {5: 0}
    kern = functools.partial(_flash_kernel, lam_init=lam_init, nk=nk, aliased=prev is not None)
    return pl.pallas_call(
        kern,
        out_shape=jax.ShapeDtypeStruct((b, s, hh * LANES), BF16),
        grid=(b, hh, nq, nk),
        in_specs=in_specs,
        out_specs=pl.BlockSpec((1, tq, LANES), lambda bi, h, qi, ki: (bi, qi + q_off, h)),
        scratch_shapes=[pltpu.VMEM((2, tq, LANES), F32), pltpu.VMEM((2, tq, LANES), F32),
                        pltpu.VMEM((2, tq, LANES), F32),
                        pltpu.VMEM((FLASH_LOOKAHEAD + 1, min(tq, FLASH_ROW_BLOCK), tk), F32)],
        input_output_aliases=aliases,
        compiler_params=_cparams(("arbitrary", "arbitrary", "arbitrary", "arbitrary")),
        name="diff_flash_attention",
    )(*args)


def _post_norm(x, y, gate, lg, lb):
    r = DN_ALPHA * x + gate * y
    mu = jnp.mean(r, axis=-1, keepdims=True)
    rc = r - mu
    var = jnp.mean(rc * rc, axis=-1, keepdims=True)
    return rc * lax.rsqrt(var + LN_EPS) * lg + lb


def _outproj_kernel(*refs, gated, n_heads, dh):
    if gated:
        of_ref, ob_ref, z_ref, ng_ref, w_ref, x_ref, gate_ref, lg_ref, lb_ref, out_ref = refs
        y = None
        for h in range(n_heads):
            sl = slice(h * dh, (h + 1) * dh)
            o = of_ref[0, :, sl].astype(F32) + ob_ref[0, :, sl].astype(F32)
            ms = jnp.mean(o * o, axis=-1, keepdims=True)
            o = o * lax.rsqrt(ms + NORM_EPS) * ng_ref[...] * _silu(z_ref[0, :, sl].astype(F32))
            t = _dot(o.astype(BF16), w_ref[sl, :])
            y = t if y is None else y + t
    else:
        o_ref, w_ref, x_ref, gate_ref, lg_ref, lb_ref, out_ref = refs
        y = _dot(o_ref[0], w_ref[...])
    out_ref[0] = _post_norm(x_ref[0], y, gate_ref[0, 0], lg_ref[...], lb_ref[...])


def out_projection_post_norm(o_args, w_o, x, gate, lg, lb, n_lat_tiles, gated=None):
    b, s, d = x.shape
    kdim = w_o.shape[0]
    tm = ROW_TILE
    row = lambda width, cb=0: pl.BlockSpec((1, tm, width), lambda bi, i: (bi, i, cb))
    full = lambda shape: pl.BlockSpec(shape, lambda bi, i: (0,) * len(shape))
    if gated is None:
        in_specs = [row(kdim)]
        args = list(o_args)
        kern = functools.partial(_outproj_kernel, gated=False, n_heads=0, dh=0)
    else:
        n_heads, dh, z_cb = gated
        of, ob, z, ng = o_args
        in_specs = [row(kdim), row(kdim), row(kdim, z_cb), full((1, dh))]
        args = [of, ob, z, ng.reshape(1, dh)]
        kern = functools.partial(_outproj_kernel, gated=True, n_heads=n_heads, dh=dh)
    in_specs += [full((kdim, d)), row(d),
                 pl.BlockSpec((1, 1, 1, d), lambda bi, i: (bi, i // n_lat_tiles, 0, 0)),
                 full((1, d)), full((1, d))]
    args += [w_o, x, gate, lg.reshape(1, d), lb.reshape(1, d)]
    return pl.pallas_call(
        kern, out_shape=jax.ShapeDtypeStruct((b, s, d), F32), grid=(b, s // tm),
        in_specs=in_specs, out_specs=row(d),
        compiler_params=_cparams(("arbitrary", "arbitrary")), name="out_projection_post_norm",
    )(*args)


def _gdn_prep_kernel(cur_ref, prev_ref, next_ref, gate_ref, cw_ref, alog_ref, dtb_ref,
                     q_ref, k_ref, v_ref, g_ref, ext, *, n_lat_tiles, n_tiles, cn):
    i = pl.program_id(1)
    tm = cur_ref.shape[1]
    first = jnp.logical_or(i == 0, i == n_lat_tiles)
    last = jnp.logical_or(i == n_lat_tiles - 1, i == n_tiles - 1)
    pmask = jnp.where(first, 0.0, 1.0)
    nmask = jnp.where(last, 0.0, 1.0)
    pad = GDN_CONV_W // 2
    dh = GDN_HEAD_DIM
    n_qk = 2 * GDN_KEY_DIM
    for cc in range(cur_ref.shape[2] // cn):
        cs = slice(cc * cn, (cc + 1) * cn)
        ext[0:HALO, :] = prev_ref[0, :, cs].astype(F32) * pmask
        ext[HALO:HALO + tm, :] = cur_ref[0, :, cs].astype(F32)
        ext[HALO + tm:2 * HALO + tm, :] = next_ref[0, :, cs].astype(F32) * nmask
        acc = None
        for j in range(GDN_CONV_W):
            t = ext[pl.ds(HALO - pad + j, tm), :] * cw_ref[j:j + 1, cs]
            acc = t if acc is None else acc + t
        y = _silu(acc)
        for g in range(cn // dh):
            col = cc * cn + g * dh
            yg = y[:, g * dh:(g + 1) * dh]
            if col < n_qk:
                yg = yg * lax.rsqrt(jnp.sum(yg * yg, axis=-1, keepdims=True) + NORM_EPS)
                if col < GDN_KEY_DIM:
                    q_ref[0, col // dh] = (yg * (dh ** -0.5)).astype(BF16)
                else:
                    k_ref[0, (col - GDN_KEY_DIM) // dh] = yg.astype(BF16)
            else:
                v_ref[0, (col - n_qk) // dh] = yg.astype(BF16)
    a = gate_ref[0]
    lane = lax.broadcasted_iota(jnp.int32, a.shape, 1)
    gdec = -jnp.exp(alog_ref[...]) * _softplus(a + dtb_ref[...])
    g_ref[0] = jnp.where(lane < 2 * GDN_V_HEADS, gdec, jax.nn.sigmoid(a))


def gdn_prep(p_main, gates_raw, conv_w, alog_row, dtb_row, n_lat_tiles):
    b, s, _ = p_main.shape
    tm = ROW_TILE
    nt = s // tm
    nch = 2 * GDN_KEY_DIM + GDN_VAL_DIM
    hpt = tm // HALO
    nh = s // HALO
    kern = functools.partial(_gdn_prep_kernel, n_lat_tiles=n_lat_tiles, n_tiles=nt, cn=512)
    head_out = lambda nheads: pl.BlockSpec((1, nheads, tm, GDN_HEAD_DIM), lambda bi, i: (bi, 0, i, 0))
    return pl.pallas_call(
        kern,
        out_shape=[jax.ShapeDtypeStruct((b, GDN_K_HEADS, s, GDN_HEAD_DIM), BF16),
                   jax.ShapeDtypeStruct((b, GDN_K_HEADS, s, GDN_HEAD_DIM), BF16),
                   jax.ShapeDtypeStruct((b, GDN_V_HEADS, s, GDN_HEAD_DIM), BF16),
                   jax.ShapeDtypeStruct((b, s, LANES), F32)],
        grid=(b, nt),
        in_specs=[
            pl.BlockSpec((1, tm, nch), lambda bi, i: (bi, i, 0)),
            pl.BlockSpec((1, HALO, nch), lambda bi, i: (bi, jnp.maximum(i * hpt - 1, 0), 0)),
            pl.BlockSpec((1, HALO, nch), lambda bi, i: (bi, jnp.minimum((i + 1) * hpt, nh - 1), 0)),
            pl.BlockSpec((1, tm, LANES), lambda bi, i: (bi, i, 0)),
            pl.BlockSpec((GDN_CONV_W, nch), lambda bi, i: (0, 0)),
            pl.BlockSpec((1, LANES), lambda bi, i: (0, 0)),
            pl.BlockSpec((1, LANES), lambda bi, i: (0, 0)),
        ],
        out_specs=[head_out(GDN_K_HEADS), head_out(GDN_K_HEADS), head_out(GDN_V_HEADS),
                   pl.BlockSpec((1, tm, LANES), lambda bi, i: (bi, i, 0))],
        scratch_shapes=[pltpu.VMEM((tm + 2 * HALO, 512), F32)],
        compiler_params=_cparams(("arbitrary", "arbitrary")),
        name="gdn_prep",
    )(p_main, p_main, p_main, gates_raw, conv_w, alog_row, dtb_row)


def _chunk_masks(n, reverse):
    r = lax.broadcasted_iota(jnp.int32, (n, n), 0)
    c = lax.broadcasted_iota(jnp.int32, (n, n), 1)
    if reverse:
        return r <= c, r < c
    return r >= c, r > c


def _block_cumsum_mats(ct, reverse):
    r = lax.broadcasted_iota(jnp.int32, (ct, ct), 0)
    c = lax.broadcasted_iota(jnp.int32, (ct, ct), 1)
    same = (r // CHUNK) == (c // CHUNK)
    lower = jnp.logical_and(same, c <= r)
    upper = jnp.logical_and(same, c >= r)
    lo = jnp.where(lower, 1.0, 0.0).astype(BF16)
    up = jnp.where(upper, 1.0, 0.0).astype(BF16)
    return (up, lo) if reverse else (lo, up)


def _cumsum_cols(mat, x):
    x0, x1, x2 = _split3(x)
    return _dot(mat, x0) + (_dot(mat, x1) + _dot(mat, x2))


def _cumsum_rows(x, mat):
    x0, x1, x2 = _split3(x)
    return _dot(x0, mat) + (_dot(x1, mat) + _dot(x2, mat))


def _gdn_block(dirs):
    dh = GDN_HEAD_DIM
    eye = jnp.where(lax.broadcasted_iota(jnp.int32, (CHUNK, CHUNK), 0)
                    == lax.broadcasted_iota(jnp.int32, (CHUNK, CHUNK), 1), 1.0, 0.0).astype(F32)
    chains = []
    for q_ref, k_ref, v_ref, gc_ref, gr_ref, o_ref, s_sc, reverse in dirs:
        ct = q_ref.shape[2]
        goff = 2 if reverse else 0
        m_col, m_row = _block_cumsum_mats(ct, reverse)
        tril, strict = _chunk_masks(CHUNK, reverse)
        for kh in range(q_ref.shape[1]):
            gcol = gc_ref[0, kh]
            grow = gr_ref[0, kh]
            gcum_c = _cumsum_cols(m_col, gcol)
            gcum_r = _cumsum_rows(grow, m_row)
            for c in range(ct // CHUNK):
                rs = slice(c * CHUNK, (c + 1) * CHUNK)
                last = c * CHUNK if reverse else (c + 1) * CHUNK - 1
                q = q_ref[0, kh, rs, :]
                k = k_ref[0, kh, rs, :]
                kk = _dot_nt(k, k)
                qk = _dot_nt(q, k)
                for hh in range(2):
                    gi = goff + hh
                    chains.append(dict(
                        q=q, k=k, kk=kk, qk=qk, rs=rs, c=c, hh=2 * kh + hh, reverse=reverse, tril=tril,
                        strict=strict, v_ref=v_ref, o_ref=o_ref, s_sc=s_sc,
                        gc=gcum_c[rs, gi:gi + 1], gr=gcum_r[gi:gi + 1, rs],
                        glast=gcum_c[last:last + 1, gi:gi + 1], beta=gcol[rs, 4 + gi:5 + gi]))
    for ch in chains:
        tril = ch["tril"]
        ch["decay"] = jnp.where(tril, jnp.exp(jnp.where(tril, ch["gc"] - ch["gr"], 0.0)), 0.0)
        a = -jnp.where(ch["strict"], ch["kk"] * ch["beta"] * ch["decay"], 0.0)
        ch["tmat"] = eye + a
        ch["pw"] = a
    for _ in range(5):
        for ch in chains:
            pwb = ch["pw"].astype(BF16)
            ch["pw"] = _dot(pwb, pwb)
        for ch in chains:
            ch["tmat"] = ch["tmat"] + _dot(ch["tmat"].astype(BF16), ch["pw"].astype(BF16))
    for ch in chains:
        beta = ch["beta"]
        eg = jnp.exp(ch["gc"])
        vb = ch["v_ref"][0, ch["hh"], ch["rs"], :].astype(F32) * beta
        kbg = ch["k"].astype(F32) * (beta * eg)
        uw = _dot(ch["tmat"].astype(BF16), jnp.concatenate([vb, kbg], axis=-1).astype(BF16))
        ch["u"] = uw[:, :dh]
        ch["w"] = uw[:, dh:].astype(BF16)
        ch["eg"] = eg
        ch["attn"] = jnp.where(ch["tril"], ch["qk"] * ch["decay"], 0.0).astype(BF16)
    nchunk = max(ch["c"] for ch in chains) + 1
    for step in range(nchunk):
        cur = [ch for ch in chains if ch["c"] == (nchunk - 1 - step if ch["reverse"] else step)]
        for ch in cur:
            state = ch["s_sc"][ch["hh"]]
            sb = state.astype(BF16)
            ch["state"] = state
            ch["ws"] = _dot(ch["w"], sb)
            ch["qs"] = _dot(ch["q"], sb)
        for ch in cur:
            v_new = ch["u"] - ch["ws"]
            ch["kgv"] = (v_new * jnp.exp(ch["glast"] - ch["gc"])).astype(BF16)
            ch["o"] = ch["eg"] * ch["qs"] + _dot(ch["attn"], v_new.astype(BF16))
        for ch in cur:
            ch["s_sc"][ch["hh"]] = ch["state"] * jnp.exp(ch["glast"]) + _dot_tn(ch["k"], ch["kgv"])
            ch["o_ref"][0, ch["rs"], ch["hh"] * dh:(ch["hh"] + 1) * dh] = ch["o"].astype(BF16)


def _gdn_scan_kernel(qf, kf, vf, gcf, grf, qb, kb, vb, gcb, grb, s0f, s0b, *rest, nblk, aliased):
    if aliased:
        rest = rest[2:]
    of_ref, ob_ref, sff, sfb, sf_sc, sb_sc = rest
    j = pl.program_id(2)

    @pl.when(j == 0)
    def _():
        sf_sc[...] = s0f[0]
        sb_sc[...] = s0b[0]

    _gdn_block([(qf, kf, vf, gcf, grf, of_ref, sf_sc, False),
                (qb, kb, vb, gcb, grb, ob_ref, sb_sc, True)])

    @pl.when(j == nblk - 1)
    def _():
        sff[0] = sf_sc[...]
        sfb[0] = sb_sc[...]


def gdn_scan(qn, kn, vv, gcol, grow, s0f, s0b, *, nblk, off, prev=None):
    b, _, s, dh = qn.shape
    ct = SCAN_TILE
    g = GDN_HEADS_PER_STEP
    fwd = lambda j: j + off
    bwd = lambda j: nblk - 1 - j + off
    def specs(pos):
        return [
            pl.BlockSpec((1, g, ct, dh), lambda bi, h, j: (bi, h, pos(j), 0)),
            pl.BlockSpec((1, g, ct, dh), lambda bi, h, j: (bi, h, pos(j), 0)),
            pl.BlockSpec((1, 2 * g, ct, dh), lambda bi, h, j: (bi, h, pos(j), 0)),
            pl.BlockSpec((1, g, ct, 8), lambda bi, h, j: (bi, h, pos(j), 0)),
            pl.BlockSpec((1, g, 8, ct), lambda bi, h, j: (bi, h, 0, pos(j))),
        ]
    st_spec = pl.BlockSpec((1, 2 * g, dh, dh), lambda bi, h, j: (bi, h, 0, 0))
    in_specs = specs(fwd) + specs(bwd) + [st_spec, st_spec]
    args = [qn, kn, vv, gcol, grow] * 2 + [s0f, s0b]
    aliases = {}
    if prev is not None:
        in_specs += [pl.BlockSpec(memory_space=pl.ANY)] * 2
        args += list(prev)
        aliases = {12: 0, 13: 1}
    o_shape = jax.ShapeDtypeStruct((b, s, GDN_VAL_DIM), BF16)
    st_shape = jax.ShapeDtypeStruct((b, GDN_V_HEADS, dh, dh), F32)
    kern = functools.partial(_gdn_scan_kernel, nblk=nblk, aliased=prev is not None)
    return pl.pallas_call(
        kern,
        out_shape=[o_shape, o_shape, st_shape, st_shape],
        grid=(b, GDN_K_HEADS // g, nblk),
        in_specs=in_specs,
        out_specs=[pl.BlockSpec((1, ct, 2 * g * dh), lambda bi, h, j: (bi, fwd(j), h)),
                   pl.BlockSpec((1, ct, 2 * g * dh), lambda bi, h, j: (bi, bwd(j), h)),
                   st_spec, st_spec],
        scratch_shapes=[pltpu.VMEM((2 * g, dh, dh), F32), pltpu.VMEM((2 * g, dh, dh), F32)],
        input_output_aliases=aliases,
        compiler_params=_cparams(("arbitrary", "arbitrary", "arbitrary")),
        name="gdn_scan",
    )(*args)


def _gla_block(dirs, wg_ref, gb_ref):
    chains = []
    pre = []
    for q_ref, k_ref, v_ref, gr_ref, o_ref, st_sc, z in dirs:
        pre.append(_dot3(gr_ref[0], wg_ref[z]) + gb_ref[z:z + 1, :])
    for (q_ref, k_ref, v_ref, gr_ref, o_ref, st_sc, z), logit in zip(dirs, pre):
        reverse = z == 1
        ct = q_ref.shape[1]
        glog = -_softplus(-logit) / GLA_TAU
        m_col, _ = _block_cumsum_mats(ct, reverse)
        bcum = _cumsum_cols(m_col, glog)
        tril, _ = _chunk_masks(CHUNK, reverse)
        for hh in range(st_sc.shape[0]):
            ks = slice(hh * GLA_DK, (hh + 1) * GLA_DK)
            vs = slice(hh * GLA_DV, (hh + 1) * GLA_DV)
            for c in range(ct // CHUNK):
                rs = slice(c * CHUNK, (c + 1) * CHUNK)
                last = c * CHUNK if reverse else (c + 1) * CHUNK - 1
                bc = bcum[rs, ks]
                bl = bcum[last:last + 1, ks]
                qf = q_ref[0, rs, ks].astype(F32) * (GLA_DK ** -0.5)
                kf = k_ref[0, rs, ks].astype(F32)
                chains.append(dict(
                    c=c, rs=rs, hh=hh, vs=vs, reverse=reverse, tril=tril, o_ref=o_ref, st_sc=st_sc,
                    v=v_ref[0, rs, vs],
                    qe=(qf * jnp.exp(bc)).astype(BF16), ke=(kf * jnp.exp(-bc)).astype(BF16),
                    kg=(kf * jnp.exp(bl - bc)).astype(BF16), gl=jnp.exp(bl)))
    for ch in chains:
        ch["attn"] = jnp.where(ch["tril"], _dot_nt(ch["qe"], ch["ke"]), 0.0).astype(BF16)
    for ch in chains:
        ch["o"] = _dot(ch["attn"], ch["v"])
        ch["kv"] = _dot_tn(ch["v"], ch["kg"])
    nchunk = max(ch["c"] for ch in chains) + 1
    for step in range(nchunk):
        cur = [ch for ch in chains if ch["c"] == (nchunk - 1 - step if ch["reverse"] else step)]
        for ch in cur:
            st = ch["st_sc"][ch["hh"]]
            ch["o"] = ch["o"] + _dot_nt(ch["qe"], st.astype(BF16))
            ch["st_sc"][ch["hh"]] = st * ch["gl"] + ch["kv"]
        for ch in cur:
            ch["o_ref"][0, ch["rs"], ch["vs"]] = ch["o"].astype(BF16)


def _gla_scan_kernel(qf, kf, vf, grf, qb, kb, vb, grb, wg, gb, s0f, s0b, *rest, nblk, aliased):
    if aliased:
        rest = rest[2:]
    of_ref, ob_ref, sff, sfb, sf_sc, sb_sc = rest
    j = pl.program_id(2)

    @pl.when(j == 0)
    def _():
        sf_sc[...] = s0f[0]
        sb_sc[...] = s0b[0]

    _gla_block([(qf, kf, vf, grf, of_ref, sf_sc, 0), (qb, kb, vb, grb, ob_ref, sb_sc, 1)], wg, gb)

    @pl.when(j == nblk - 1)
    def _():
        sff[0] = sf_sc[...]
        sfb[0] = sb_sc[...]


def gla_scan(p_main, gr, wg_pad, gate_b, s0f, s0b, *, nblk, off, prev=None):
    b, s, _ = p_main.shape
    ct = SCAN_TILE
    nh, dk, dv = GLA_HEADS, GLA_DK, GLA_DV
    g = GLA_HEADS_PER_STEP
    ng = nh // g
    fwd = lambda j: j + off
    bwd = lambda j: nblk - 1 - j + off
    def specs(pos):
        return [
            pl.BlockSpec((1, ct, g * dk), lambda bi, h, j: (bi, pos(j), h)),
            pl.BlockSpec((1, ct, g * dk), lambda bi, h, j: (bi, pos(j), ng + h)),
            pl.BlockSpec((1, ct, g * dv), lambda bi, h, j: (bi, pos(j), ng + h)),
            pl.BlockSpec((1, ct, LANES), lambda bi, h, j: (bi, pos(j), 0)),
        ]
    st_spec = pl.BlockSpec((1, g, dv, dk), lambda bi, h, j: (bi, h, 0, 0))
    in_specs = specs(fwd) + specs(bwd) + [
        pl.BlockSpec((2, LANES, g * dk), lambda bi, h, j: (0, 0, h)),
        pl.BlockSpec((2, g * dk), lambda bi, h, j: (0, h)),
        st_spec, st_spec]
    args = [p_main, p_main, p_main, gr] * 2 + [wg_pad, gate_b, s0f, s0b]
    aliases = {}
    if prev is not None:
        in_specs += [pl.BlockSpec(memory_space=pl.ANY)] * 2
        args += list(prev)
        aliases = {12: 0, 13: 1}
    o_shape = jax.ShapeDtypeStruct((b, s, GLA_VAL_DIM), BF16)
    st_shape = jax.ShapeDtypeStruct((b, nh, dv, dk), F32)
    kern = functools.partial(_gla_scan_kernel, nblk=nblk, aliased=prev is not None)
    return pl.pallas_call(
        kern,
        out_shape=[o_shape, o_shape, st_shape, st_shape],
        grid=(b, ng, nblk),
        in_specs=in_specs,
        out_specs=[pl.BlockSpec((1, ct, g * dv), lambda bi, h, j: (bi, fwd(j), h)),
                   pl.BlockSpec((1, ct, g * dv), lambda bi, h, j: (bi, bwd(j), h)),
                   st_spec, st_spec],
        scratch_shapes=[pltpu.VMEM((g, dv, dk), F32), pltpu.VMEM((g, dv, dk), F32)],
        input_output_aliases=aliases,
        compiler_params=_cparams(("arbitrary", "arbitrary", "arbitrary")),
        name="gla_scan",
    )(*args)


def _router_kernel(x_ref, sc_ref, sh_ref, rwt_ref, rb_ref, h_ref, wd_ref):
    h = x_ref[0] * (1.0 + sc_ref[0, 0]) + sh_ref[0, 0]
    h_ref[0] = h.astype(BF16)
    w0, w1 = _split2(rwt_ref[...])
    h0, h1 = _split2(h)
    scores = jax.nn.sigmoid(_dot_nt(w0, h0) + (_dot_nt(w0, h1) + _dot_nt(w1, h0)))
    ne = scores.shape[0]
    row = lax.broadcasted_iota(jnp.int32, scores.shape, 0)
    neg = jnp.float32(-jnp.inf)
    sel = scores + rb_ref[...]
    chosen = jnp.zeros(scores.shape, jnp.bool_)
    for _ in range(TOP_K):
        mx = jnp.max(sel, axis=0, keepdims=True)
        first = jnp.min(jnp.where(sel == mx, row, ne), axis=0, keepdims=True)
        pick = row == first
        chosen = jnp.logical_or(chosen, pick)
        sel = jnp.where(pick, neg, sel)
    picked = jnp.where(chosen, scores, 0.0)
    wt = picked / jnp.sum(picked, axis=0, keepdims=True) * ROUTE_SCALE
    eye = jnp.where(lax.broadcasted_iota(jnp.int32, (ne, LANES), 0)
                    == lax.broadcasted_iota(jnp.int32, (ne, LANES), 1), 1.0, 0.0).astype(BF16)
    t0, t1, t2 = _split3(wt)
    wd_ref[0] = _dot_tn(t0, eye) + (_dot_tn(t1, eye) + _dot_tn(t2, eye))


def moe_route(x, sc, sh, rw_t, rb_col, n_lat_tiles):
    b, s, d = x.shape
    tm = ROW_TILE
    mod_spec = pl.BlockSpec((1, 1, 1, d), lambda bi, i: (bi, i // n_lat_tiles, 0, 0))
    return pl.pallas_call(
        _router_kernel,
        out_shape=[jax.ShapeDtypeStruct((b, s, d), BF16), jax.ShapeDtypeStruct((b, s, LANES), F32)],
        grid=(b, s // tm),
        in_specs=[pl.BlockSpec((1, tm, d), lambda bi, i: (bi, i, 0)), mod_spec, mod_spec,
                  pl.BlockSpec((N_EXPERTS, d), lambda bi, i: (0, 0)),
                  pl.BlockSpec((N_EXPERTS, 1), lambda bi, i: (0, 0))],
        out_specs=[pl.BlockSpec((1, tm, d), lambda bi, i: (bi, i, 0)),
                   pl.BlockSpec((1, tm, LANES), lambda bi, i: (bi, i, 0))],
        compiler_params=_cparams(("arbitrary", "arbitrary")),
        name="moe_router",
    )(x, sc, sh, rw_t, rb_col)


def _moe_kernel(h_ref, wd_ref, wgu_ref, wdn_ref, x_ref, gate_ref, lg_ref, lb_ref, out_ref, acc, *,
                n_exp, n_lat_rows):
    e = pl.program_id(2)

    @pl.when(e == 0)
    def _():
        acc[...] = jnp.zeros(acc.shape, F32)

    gu = _dot(h_ref[0], wgu_ref[0])
    act = _silu(gu[:, :EXPERT_FF]) * gu[:, EXPERT_FF:]
    wd = wd_ref[0]
    lane = lax.broadcasted_iota(jnp.int32, wd.shape, 1)
    wcol = jnp.sum(jnp.where(lane == e, wd, 0.0), axis=-1, keepdims=True)
    acc[...] += wcol * _dot(act.astype(BF16), wdn_ref[0])

    @pl.when(e == n_exp - 1)
    def _():
        tm = acc.shape[0]
        row = pl.program_id(1) * tm + lax.broadcasted_iota(jnp.int32, (tm, 1), 0)
        gate = jnp.where(row >= n_lat_rows, gate_ref[0, 1], gate_ref[0, 0])
        out_ref[0] = _post_norm(x_ref[0], acc[...], gate, lg_ref[...], lb_ref[...])


def moe_experts_post_norm(h, wd, wgu, wdn, x, gate, lg, lb, n_lat_rows, tm):
    b, s, d = x.shape
    n_exp = wgu.shape[0]
    row = lambda width: pl.BlockSpec((1, tm, width), lambda bi, i, e: (bi, i, 0))
    kern = functools.partial(_moe_kernel, n_exp=n_exp, n_lat_rows=n_lat_rows)
    return pl.pallas_call(
        kern,
        out_shape=jax.ShapeDtypeStruct((b, s, d), F32),
        grid=(b, s // tm, n_exp),
        in_specs=[row(d), row(LANES),
                  pl.BlockSpec((1, d, 2 * EXPERT_FF), lambda bi, i, e: (e, 0, 0)),
                  pl.BlockSpec((1, EXPERT_FF, d), lambda bi, i, e: (e, 0, 0)),
                  row(d),
                  pl.BlockSpec((1, 2, 1, d), lambda bi, i, e: (bi, 0, 0, 0)),
                  pl.BlockSpec((1, d), lambda bi, i, e: (0, 0)),
                  pl.BlockSpec((1, d), lambda bi, i, e: (0, 0))],
        out_specs=row(d),
        scratch_shapes=[pltpu.VMEM((tm, d), F32)],
        compiler_params=_cparams(("arbitrary", "arbitrary", "arbitrary")),
        name="moe_experts_post_norm",
    )(h, wd, wgu, wdn, x, gate, lg.reshape(1, d), lb.reshape(1, d))


MOE_SUB = 256
MOE_CAP = 64
MOE_GRP = 4


def _moe_grouped_kernel(order_ref, h_ref, wd_ref, *rest):
    wgu_refs = rest[:MOE_GRP]
    wdn_refs = rest[MOE_GRP:2 * MOE_GRP]
    wsgu_ref, wsdn_ref, f_ref, acc, rrm, wrm, cmax = rest[2 * MOE_GRP:]
    g = pl.program_id(2)
    n_grp = pl.num_programs(2)
    tm = acc.shape[0]
    nsub = tm // MOE_SUB
    sub, cap, nslot = MOE_SUB, MOE_CAP, MOE_GRP * MOE_CAP
    lane = lax.broadcasted_iota(jnp.int32, (1, LANES), 1)

    @pl.when(g == 0)
    def _():
        r = lax.broadcasted_iota(jnp.int32, (sub, sub), 0)
        c = lax.broadcasted_iota(jnp.int32, (sub, sub), 1)
        after = jnp.where(r < c, 1.0, 0.0).astype(BF16)
        ident = jnp.where(r == c, 1.0, 0.0).astype(BF16)
        cm = jnp.zeros((1, LANES), F32)
        for u in range(nsub):
            rs = slice(u * sub, (u + 1) * sub)
            hu = h_ref[0, rs, :]
            gu = _dot(hu, wsgu_ref[...])
            act = _silu(gu[:, :EXPERT_FF]) * gu[:, EXPERT_FF:]
            acc[rs, :] = _dot(act.astype(BF16), wsdn_ref[...])
            wd = wd_ref[0, rs, :]
            active = jnp.logical_and(wd != 0.0, lane < N_EXPERTS)
            a = jnp.where(active, 1.0, 0.0)
            ab = a.astype(BF16)
            rank_r = _dot_tn(ab, after)
            a_r = _dot_tn(ab, ident)
            rrm[u] = jnp.where(a_r > 0.5, rank_r, -1.0)
            wrm[u] = _dot_tn(wd.astype(BF16), ident)
            cm = jnp.maximum(cm, jnp.sum(a, axis=0, keepdims=True))
        cmax[...] = jnp.broadcast_to(cm, cmax.shape)

    experts = [order_ref[g * MOE_GRP + k] for k in range(MOE_GRP)]
    in_group = functools.reduce(jnp.logical_or, [lane == e for e in experts])
    n_max = jnp.max(jnp.where(in_group, cmax[0:1, :], 0.0))
    n_pass = (n_max.astype(jnp.int32) + (cap - 1)) // cap

    def slot_expert(l):
        return sum(((l >= k * cap).astype(jnp.int32) for k in range(1, MOE_GRP)), jnp.zeros_like(l))

    l_col = lax.broadcasted_iota(jnp.int32, (sub, 1), 0)
    j_col = jnp.where(l_col < nslot, l_col - cap * slot_expert(l_col), -1000).astype(F32)

    def per_slot(table, u, fill):
        rows = [jnp.broadcast_to(table[u, pl.ds(experts[k], 1), :], (cap, sub)) for k in range(MOE_GRP)]
        if nslot < sub:
            rows.append(jnp.full((sub - nslot, sub), fill, F32))
        return jnp.concatenate(rows, axis=0)

    def one_pass(p, carry):
        base = (p * cap).astype(F32)
        xg, expanders = [], []
        for u in range(nsub):
            rs = slice(u * sub, (u + 1) * sub)
            match = per_slot(rrm, u, -1.0) == j_col + base
            gather = jnp.where(match, 1.0, 0.0).astype(BF16)
            expanders.append(jnp.where(match, per_slot(wrm, u, 0.0), 0.0).astype(BF16))
            xg.append(_dot(gather, h_ref[0, rs, :]).astype(BF16))
        ys = []
        for k in range(MOE_GRP):
            es = slice(k * cap, (k + 1) * cap)
            x_e = jnp.concatenate([xg[u][es] for u in range(nsub)], axis=0)
            gu = _dot(x_e, wgu_refs[k][0])
            act = _silu(gu[:, :EXPERT_FF]) * gu[:, EXPERT_FF:]
            ys.append(_dot(act.astype(BF16), wdn_refs[k][0]).astype(BF16))
        for u in range(nsub):
            rs = slice(u * sub, (u + 1) * sub)
            parts = [ys[k][u * cap:(u + 1) * cap] for k in range(MOE_GRP)]
            if nslot < sub:
                parts.append(jnp.zeros((sub - nslot, parts[0].shape[1]), BF16))
            y_u = jnp.concatenate(parts, axis=0)
            acc[rs, :] += _dot_tn(expanders[u], y_u)
        return carry

    lax.fori_loop(0, n_pass, one_pass, 0)

    @pl.when(g == n_grp - 1)
    def _():
        f_ref[0] = acc[...]


def moe_grouped_experts(order, h, wd, wgu, wdn, wsgu, wsdn, tm):
    b, s, d = h.shape
    n_grp = wgu.shape[0] // MOE_GRP
    row = lambda width: pl.BlockSpec((1, tm, width), lambda bi, i, g, o: (bi, i, 0))

    def expert(shape, k):
        return pl.BlockSpec((1,) + shape, lambda bi, i, g, o: (o[g * MOE_GRP + k], 0, 0))

    in_specs = ([row(d), row(LANES)]
                + [expert((d, 2 * EXPERT_FF), k) for k in range(MOE_GRP)]
                + [expert((EXPERT_FF, d), k) for k in range(MOE_GRP)]
                + [pl.BlockSpec((d, 2 * EXPERT_FF), lambda bi, i, g, o: (0, 0)),
                   pl.BlockSpec((EXPERT_FF, d), lambda bi, i, g, o: (0, 0))])
    grid_spec = pltpu.PrefetchScalarGridSpec(
        num_scalar_prefetch=1, grid=(b, s // tm, n_grp), in_specs=in_specs, out_specs=row(d),
        scratch_shapes=[pltpu.VMEM((tm, d), F32),
                        pltpu.VMEM((tm // MOE_SUB, LANES, MOE_SUB), F32),
                        pltpu.VMEM((tm // MOE_SUB, LANES, MOE_SUB), F32),
                        pltpu.VMEM((8, LANES), F32)])
    return pl.pallas_call(
        _moe_grouped_kernel,
        out_shape=jax.ShapeDtypeStruct((b, s, d), F32),
        grid_spec=grid_spec,
        compiler_params=_cparams(("arbitrary", "arbitrary", "arbitrary")),
        name="moe_grouped_experts",
    )(order, h, wd, *([wgu] * MOE_GRP), *([wdn] * MOE_GRP), wsgu, wsdn)


def _post_norm_kernel(x_ref, f_ref, gate_ref, lg_ref, lb_ref, out_ref):
    out_ref[0] = _post_norm(x_ref[0], f_ref[0].astype(F32), gate_ref[0, 0], lg_ref[...], lb_ref[...])


def post_norm_rows(x, f, gate, lg, lb, n_lat_tiles, rows_out=None):
    b, s, d = x.shape
    tm = ROW_TILE
    rows_out = s if rows_out is None else rows_out
    row = pl.BlockSpec((1, tm, d), lambda bi, i: (bi, i, 0))
    vec = pl.BlockSpec((1, d), lambda bi, i: (0, 0))
    return pl.pallas_call(
        _post_norm_kernel, out_shape=jax.ShapeDtypeStruct((b, rows_out, d), F32), grid=(b, rows_out // tm),
        in_specs=[row, row, pl.BlockSpec((1, 1, 1, d), lambda bi, i: (bi, i // n_lat_tiles, 0, 0)), vec, vec],
        out_specs=row, compiler_params=_cparams(("arbitrary", "arbitrary")), name="post_norm_rows",
    )(x, f, gate, lg.reshape(1, d), lb.reshape(1, d))


def _rope_tables(n_lat, n_ctx):
    rows = n_lat // GRID_W
    rowp = jnp.repeat(jnp.arange(rows), GRID_W).astype(F32)
    colp = jnp.tile(jnp.arange(GRID_W), rows).astype(F32)
    n_freq = DA_HEAD_DIM // 4
    inv = 1.0 / (ROPE_BASE ** (jnp.arange(n_freq, dtype=F32) / n_freq))
    ang = jnp.concatenate([rowp[:, None] * inv, colp[:, None] * inv], -1)
    cos, sin = jnp.cos(ang), jnp.sin(ang)
    cos_t = jnp.tile(cos, (1, 4))
    sin_t = jnp.tile(jnp.concatenate([-sin, sin], -1), (1, 2))
    cos_t = jnp.concatenate([cos_t, jnp.ones((n_ctx, LANES), F32)], 0)
    sin_t = jnp.concatenate([sin_t, jnp.zeros((n_ctx, LANES), F32)], 0)
    return cos_t, sin_t


def _pick_tile(total, cands):
    for t in cands:
        if total % t == 0:
            return t
    raise ValueError(f"no tile for {total}")


def _pad_cols(w, n):
    return jnp.pad(w, ((0, 0), (0, n - w.shape[1])))


def _flash_both(p_all, lam_vec, subln, lam_init, n_lat, n_ctx):
    s = n_lat + n_ctx
    tq = _pick_tile(n_lat, (1024, 512, 256))
    tk = _pick_tile(s, (3328, 1280, 1024, 512, 256))
    o = diff_flash_attention(p_all, lam_vec, subln, lam_init, tq=tq, tk=tk, nq=n_lat // tq, nk=s // tk,
                             q_off=0, k_off=0)
    return diff_flash_attention(p_all, lam_vec, subln, lam_init, tq=n_ctx, tk=n_ctx, nq=1, nk=1,
                                q_off=n_lat // n_ctx, k_off=n_lat // n_ctx, prev=o)


def kernel(x, c, ctx, c_ctx, ada_w, ada_b, ln_g, ln_b, da_w_in, da_w_o, da_lambda, da_subln, gdn_w_in, gdn_conv, gdn_a_log, gdn_dt_bias, gdn_norm, gdn_w_o, gla_w_in, gla_w_gate, gla_gate_b, gla_norm, gla_w_o, moe_router, moe_router_b, moe_w_gu, moe_w_dn, moe_ws_gu, moe_ws_dn):
    b, n, d = x.shape
    lc = ctx.shape[1]
    assert lc == ROW_TILE and n % SCAN_TILE == 0 and d == D_MODEL
    s = n + lc
    n_lat_tiles = n // ROW_TILE
    depth = ada_w.shape[0]

    xall = jnp.concatenate([x, ctx], axis=1)
    c8 = jnp.concatenate([c, c_ctx[None], jnp.zeros((8 - b - 1, d), F32)], 0)
    mods = ada_modulation(c8, ada_w, ada_b)
    rope = _rope_tables(n, lc)
    moe_tm = _pick_tile(s, (1280, 1024, 512, 256))

    for i in range(depth):
        kind, j = i % N_MIXERS, i // N_MIXERS
        m = mods[i].reshape(8, ADA_CHUNKS, d)
        mod = jnp.stack([m[:b], jnp.broadcast_to(m[b], (b, ADA_CHUNKS, d))], axis=1)
        mod = [mod[:, :, k][:, :, None, :] for k in range(ADA_CHUNKS)]

        if kind == 0:
            lam_init = 0.8 - 0.6 * math.exp(-0.3 * i)
            w = da_w_in[j]
            w = jnp.concatenate([w[:, :d] * (DA_HEAD_DIM ** -0.5 * math.log2(math.e)), w[:, d:]], 1).astype(BF16)
            p_all = modulated_projection(xall, mod[1], mod[0], w, rope=rope, n_rope=2 * d,
                                         n_lat_tiles=n_lat_tiles, head_major=True)
            o = _flash_both(p_all, da_lambda[j], da_subln[j], lam_init, n, lc)
            xall = out_projection_post_norm([o], da_w_o[j].astype(BF16), xall, mod[2],
                                            ln_g[i, 0], ln_b[i, 0], n_lat_tiles)
        elif kind == 1:
            w = gdn_w_in[j]
            n_main = 2 * GDN_KEY_DIM + 2 * GDN_VAL_DIM
            p_main, gates_raw = modulated_projection(
                xall, mod[1], mod[0], w[:, :n_main].astype(BF16),
                w_small=_pad_cols(w[:, n_main:], LANES).astype(BF16), n_lat_tiles=n_lat_tiles)
            alog_row = _pad_cols(gdn_a_log[j].reshape(1, -1), LANES)
            dtb_row = _pad_cols(gdn_dt_bias[j].reshape(1, -1), LANES)
            qn, kn, vv, gates = gdn_prep(p_main, gates_raw, gdn_conv[j], alog_row, dtb_row, n_lat_tiles)
            hv, hk = GDN_V_HEADS, GDN_K_HEADS
            def per_khead(t):
                t = t.reshape(b, s, 2, hk, 2)
                return jnp.transpose(t, (0, 3, 1, 2, 4)).reshape(b, hk, s, 4)
            gcol = jnp.concatenate([per_khead(gates[..., :2 * hv]), per_khead(gates[..., 2 * hv:4 * hv])], -1)
            grow = jnp.swapaxes(gcol, 2, 3)
            zeros = jnp.zeros((b, hv, GDN_HEAD_DIM, GDN_HEAD_DIM), F32)
            of, ob, scf, scb = gdn_scan(qn, kn, vv, gcol, grow, zeros, zeros, nblk=lc // SCAN_TILE,
                                        off=n // SCAN_TILE)
            of, ob, _, _ = gdn_scan(qn, kn, vv, gcol, grow, scf, scb, nblk=n // SCAN_TILE, off=0,
                                    prev=(of, ob))
            xall = out_projection_post_norm([of, ob, p_main, gdn_norm[j]], gdn_w_o[j].astype(BF16), xall,
                                            mod[2], ln_g[i, 0], ln_b[i, 0], n_lat_tiles,
                                            gated=(GDN_V_HEADS, GDN_HEAD_DIM, 2))
        else:
            w = gla_w_in[j]
            n_main = 2 * GLA_KEY_DIM + 2 * GLA_VAL_DIM
            p_main, gr = modulated_projection(
                xall, mod[1], mod[0], w[:, :n_main].astype(BF16),
                w_small=_pad_cols(w[:, n_main:], LANES).astype(BF16), n_lat_tiles=n_lat_tiles)
            wg = gla_w_gate[j]
            wg_pad = jnp.zeros((2, LANES, GLA_KEY_DIM), F32)
            wg_pad = wg_pad.at[0, :GLA_GATE_RANK].set(wg[0]).at[1, GLA_GATE_RANK:2 * GLA_GATE_RANK].set(wg[1])
            zeros = jnp.zeros((b, GLA_HEADS, GLA_DV, GLA_DK), F32)
            of, ob, scf, scb = gla_scan(p_main, gr, wg_pad, gla_gate_b[j], zeros, zeros,
                                        nblk=lc // SCAN_TILE, off=n // SCAN_TILE)
            of, ob, _, _ = gla_scan(p_main, gr, wg_pad, gla_gate_b[j], scf, scb, nblk=n // SCAN_TILE, off=0,
                                    prev=(of, ob))
            xall = out_projection_post_norm([of, ob, p_main, gla_norm[j]], gla_w_o[j].astype(BF16), xall,
                                            mod[2], ln_g[i, 0], ln_b[i, 0], n_lat_tiles,
                                            gated=(GLA_HEADS, GLA_DV, 2))

        h, wd = moe_route(xall, mod[4], mod[3], moe_router[i].T, moe_router_b[i].reshape(-1, 1), n_lat_tiles)
        popularity = jnp.sum((wd[..., :N_EXPERTS] != 0.0).astype(jnp.int32), axis=(0, 1))
        order = jnp.argsort(popularity).astype(jnp.int32)
        f = moe_grouped_experts(order, h, wd, moe_w_gu[i].astype(BF16), moe_w_dn[i].astype(BF16),
                                moe_ws_gu[i].astype(BF16), moe_ws_dn[i].astype(BF16), moe_tm)
        xall = post_norm_rows(xall, f, mod[5], ln_g[i, 1], ln_b[i, 1], n_lat_tiles,
                              rows_out=n if i == depth - 1 else None)

    return xall
```

```python
import functools
import math

import jax
import jax.numpy as jnp
from jax import lax
from jax.experimental import pallas as pl
from jax.experimental.pallas import tpu as pltpu

F32 = jnp.float32
BF16 = jnp.bfloat16

D_MODEL = 1024
DEPTH = 4
GRID_W = 64
N_MIXERS = 3
DN_ALPHA = (2 * DEPTH) ** 0.25
LN_EPS = 1e-5
NORM_EPS = 1e-6
ADA_CHUNKS = 6

DA_HEADS = 8
DA_HEAD_DIM = 64
ROPE_BASE = 10000.0

GDN_K_HEADS = 8
GDN_V_HEADS = 16
GDN_HEAD_DIM = 128
GDN_KEY_DIM = 1024
GDN_VAL_DIM = 2048
GDN_CONV_W = 5
CHUNK = 64

GLA_HEADS = 4
GLA_KEY_DIM = 512
GLA_VAL_DIM = 1024
GLA_DK = 128
GLA_DV = 256
GLA_GATE_RANK = 16
GLA_TAU = 16.0

N_EXPERTS = 64
TOP_K = 8
EXPERT_FF = 256
ROUTE_SCALE = 2.5

LANES = 128
ROW_TILE = 256
SCAN_TILE = 256
GDN_HEADS_PER_STEP = 4
GLA_HEADS_PER_STEP = 4
HALO = 16
VMEM_LIMIT = 56 * 1024 * 1024


def _cparams(sem, flags=None):
    return pltpu.CompilerParams(dimension_semantics=sem, vmem_limit_bytes=VMEM_LIMIT, flags=flags)


def _dot(a, b):
    return jnp.dot(a, b, preferred_element_type=F32)


def _dot_nt(a, b):
    return lax.dot_general(a, b, (((1,), (1,)), ((), ())), preferred_element_type=F32)


def _dot_tn(a, b):
    return lax.dot_general(a, b, (((0,), (0,)), ((), ())), preferred_element_type=F32)


def _split2(a):
    hi = a.astype(BF16)
    lo = (a - hi.astype(F32)).astype(BF16)
    return hi, lo


def _split3(a):
    a0 = a.astype(BF16)
    r = a - a0.astype(F32)
    a1 = r.astype(BF16)
    a2 = (r - a1.astype(F32)).astype(BF16)
    return a0, a1, a2


def _dot3(a, b):
    a0, a1 = _split2(a)
    b0, b1 = _split2(b)
    return _dot(a0, b0) + (_dot(a0, b1) + _dot(a1, b0))


def _select_mod(ref, tile_rows, n_lat_rows):
    row = pl.program_id(1) * tile_rows + lax.broadcasted_iota(jnp.int32, (tile_rows, 1), 0)
    return jnp.where(row >= n_lat_rows, ref[0, 1], ref[0, 0])


def _silu(x):
    return x * jax.nn.sigmoid(x)


def _softplus(x):
    return jnp.maximum(x, 0.0) + jnp.log(1.0 + jnp.exp(-jnp.abs(x)))


def _ada_kernel(c_ref, w_ref, b_ref, o_ref):
    s = _silu(c_ref[...])
    o_ref[0] = _dot3(s, w_ref[0]) + b_ref[0]


def ada_modulation(c8, ada_w, ada_b):
    depth, d, n = ada_w.shape
    tn = 1536
    return pl.pallas_call(
        _ada_kernel,
        out_shape=jax.ShapeDtypeStruct((depth, 8, n), F32),
        grid=(depth, n // tn),
        in_specs=[
            pl.BlockSpec((8, d), lambda i, j: (0, 0)),
            pl.BlockSpec((1, d, tn), lambda i, j: (i, 0, j)),
            pl.BlockSpec((1, 1, tn), lambda i, j: (i, 0, j)),
        ],
        out_specs=pl.BlockSpec((1, 8, tn), lambda i, j: (i, 0, j)),
        compiler_params=_cparams(("arbitrary", "arbitrary")),
        name="ada_modulation",
    )(c8, ada_w, ada_b.reshape(depth, 1, n))


def _proj_kernel(*refs, n_main, n_rope, has_small, cn, head_major, n_lat_rows):
    it = iter(refs)
    x_ref, sc_ref, sh_ref, w_ref = next(it), next(it), next(it), next(it)
    ws_ref = next(it) if has_small else None
    cos_ref = next(it) if n_rope else None
    sin_ref = next(it) if n_rope else None
    o_ref = next(it)
    os_ref = next(it) if has_small else None

    tm = x_ref.shape[1]
    h = x_ref[0] * (1.0 + _select_mod(sc_ref, tm, n_lat_rows)) + _select_mod(sh_ref, tm, n_lat_rows)
    hb = h.astype(BF16)
    if n_rope:
        cos = cos_ref[...]
        sin = sin_ref[...]
        lane = lax.broadcasted_iota(jnp.int32, (tm, LANES), 1)
        low_half = (lane & 32) == 0
    for j in range(n_main // cn):
        p = _dot(hb, w_ref[:, j * cn:(j + 1) * cn])
        for g in range(cn // LANES):
            pg = p[:, g * LANES:(g + 1) * LANES]
            col = j * cn + g * LANES
            if col < n_rope:
                partner = jnp.where(low_half, pltpu.roll(pg, LANES - 32, 1), pltpu.roll(pg, 32, 1))
                pg = pg * cos + partner * sin
            if head_major:
                o_ref[0, col // LANES] = pg.astype(BF16)
            else:
                o_ref[0, :, col:col + LANES] = pg.astype(BF16)
    if has_small:
        os_ref[0] = _dot(hb, ws_ref[...])


def modulated_projection(x, sc, sh, w, w_small=None, rope=None, n_rope=0, n_lat_rows=0, head_major=False, tm=ROW_TILE):
    b, s, d = x.shape
    n_main = w.shape[1]
    grid = (b, s // tm)
    mod_spec = pl.BlockSpec((1, 2, 1, d), lambda bi, i: (bi, 0, 0, 0))
    in_specs = [pl.BlockSpec((1, tm, d), lambda bi, i: (bi, i, 0)), mod_spec, mod_spec,
                pl.BlockSpec((d, n_main), lambda bi, i: (0, 0))]
    args = [x, sc, sh, w]
    if head_major:
        out_shape = [jax.ShapeDtypeStruct((b, n_main // LANES, s, LANES), BF16)]
        out_specs = [pl.BlockSpec((1, n_main // LANES, tm, LANES), lambda bi, i: (bi, 0, i, 0))]
    else:
        out_shape = [jax.ShapeDtypeStruct((b, s, n_main), BF16)]
        out_specs = [pl.BlockSpec((1, tm, n_main), lambda bi, i: (bi, i, 0))]
    if w_small is not None:
        in_specs.append(pl.BlockSpec((d, LANES), lambda bi, i: (0, 0)))
        args.append(w_small)
        out_shape.append(jax.ShapeDtypeStruct((b, s, LANES), F32))
        out_specs.append(pl.BlockSpec((1, tm, LANES), lambda bi, i: (bi, i, 0)))
    if n_rope:
        tab = pl.BlockSpec((tm, LANES), lambda bi, i: (i, 0))
        in_specs += [tab, tab]
        args += [rope[0], rope[1]]
    kern = functools.partial(_proj_kernel, n_main=n_main, n_rope=n_rope,
                             has_small=w_small is not None, cn=512, head_major=head_major,
                             n_lat_rows=n_lat_rows)
    out = pl.pallas_call(
        kern, out_shape=out_shape, grid=grid, in_specs=in_specs, out_specs=out_specs,
        compiler_params=_cparams(("arbitrary", "arbitrary")), name="modulated_projection",
    )(*args)
    return out if w_small is not None else out[0]


NEG_INIT = -1e30
FLASH_ROW_BLOCK = 512
FLASH_LOOKAHEAD = 1


def _flash_kernel(lam_ref, q_ref, k_ref, v_ref, sub_ref, *rest, lam_init, nk, aliased):
    if aliased:
        rest = rest[1:]
    o_ref, m_sc, l_sc, acc_sc, s_ring = rest
    ki = pl.program_id(3)
    tq = m_sc.shape[1]
    tk = k_ref.shape[2]
    nring = s_ring.shape[0]

    @pl.when(ki == 0)
    def _():
        m_sc[...] = jnp.full(m_sc.shape, NEG_INIT, F32)
        l_sc[...] = jnp.zeros(l_sc.shape, F32)
        acc_sc[...] = jnp.zeros(acc_sc.shape, F32)

    q = q_ref[0, 0]
    k = k_ref[0, 0]
    v = v_ref[0, 0]
    hd = DA_HEAD_DIM
    rb = min(tq, FLASH_ROW_BLOCK)
    kc = 2 * LANES
    blocks = [(c, r0) for c in range(2) for r0 in range(0, tq, rb)]
    for i in range(len(blocks) + FLASH_LOOKAHEAD):
        if i < len(blocks):
            c, r0 = blocks[i]
            s_ring[i % nring] = _dot_nt(q[r0:r0 + rb, c * hd:(c + 1) * hd], k[:, c * hd:(c + 1) * hd])
        j = i - FLASH_LOOKAHEAD
        if j < 0:
            continue
        c, r0 = blocks[j]
        rows = slice(r0, r0 + rb)
        s_blk = s_ring.at[j % nring]
        m_prev = m_sc[c, rows, :]
        m_new = jnp.maximum(m_prev, jnp.max(s_blk[...], axis=-1, keepdims=True))
        alpha = jnp.exp2(m_prev - m_new)
        m2 = jnp.concatenate([m_new, m_new], axis=-1)
        lsum = None
        pv = None
        for t in range(tk // kc):
            pj = jnp.exp2(s_blk[:, t * kc:(t + 1) * kc] - m2)
            lj = pj[:, :LANES] + pj[:, LANES:]
            lsum = lj if lsum is None else lsum + lj
            d = _dot(pj.astype(BF16), v[t * kc:(t + 1) * kc, :])
            pv = d if pv is None else pv + d
        l_sc[c, rows, :] = alpha * l_sc[c, rows, :] + lsum
        acc_sc[c, rows, :] = alpha * acc_sc[c, rows, :] + pv
        m_sc[c, rows, :] = m_new

    @pl.when(ki == nk - 1)
    def _():
        lv = lam_ref[...]
        lam = (jnp.exp(jnp.sum(lv[0:1] * lv[1:2], axis=-1, keepdims=True))
               - jnp.exp(jnp.sum(lv[2:3] * lv[3:4], axis=-1, keepdims=True)) + lam_init)
        l0 = jnp.sum(l_sc[0], axis=-1, keepdims=True)
        l1 = jnp.sum(l_sc[1], axis=-1, keepdims=True)
        o = acc_sc[0] / l0 - lam * (acc_sc[1] / l1)
        ms = jnp.mean(o * o, axis=-1, keepdims=True)
        o = o * lax.rsqrt(ms + NORM_EPS) * sub_ref[...] * (1.0 - lam_init)
        o_ref[0] = o.astype(BF16)


def diff_flash_attention(p_all, lam_vec, subln, lam_init, *, tq, tk, nq, nk, q_off, k_off, prev=None):
    b, _, s, _ = p_all.shape
    hh = DA_HEADS
    in_specs = [
        pl.BlockSpec((4, DA_HEAD_DIM), lambda bi, h, qi, ki: (0, 0)),
        pl.BlockSpec((1, 1, tq, LANES), lambda bi, h, qi, ki: (bi, h, qi + q_off, 0)),
        pl.BlockSpec((1, 1, tk, LANES), lambda bi, h, qi, ki: (bi, hh + h, ki + k_off, 0)),
        pl.BlockSpec((1, 1, tk, LANES), lambda bi, h, qi, ki: (bi, 2 * hh + h, ki + k_off, 0)),
        pl.BlockSpec((1, LANES), lambda bi, h, qi, ki: (0, 0)),
    ]
    args = [lam_vec, p_all, p_all, p_all, subln.reshape(1, LANES)]
    aliases = {}
    if prev is not None:
        in_specs.append(pl.BlockSpec(memory_space=pl.ANY))
        args.append(prev)
        aliases = {5: 0}
    kern = functools.partial(_flash_kernel, lam_init=lam_init, nk=nk, aliased=prev is not None)
    return pl.pallas_call(
        kern,
        out_shape=jax.ShapeDtypeStruct((b, s, hh * LANES), BF16),
        grid=(b, hh, nq, nk),
        in_specs=in_specs,
        out_specs=pl.BlockSpec((1, tq, LANES), lambda bi, h, qi, ki: (bi, qi + q_off, h)),
        scratch_shapes=[pltpu.VMEM((2, tq, LANES), F32), pltpu.VMEM((2, tq, LANES), F32),
                        pltpu.VMEM((2, tq, LANES), F32),
                        pltpu.VMEM((FLASH_LOOKAHEAD + 1, min(tq, FLASH_ROW_BLOCK), tk), F32)],
        input_output_aliases=aliases,
        compiler_params=_cparams(("arbitrary", "arbitrary", "arbitrary", "arbitrary")),
        name="diff_flash_attention",
    )(*args)


def _post_norm(x, y, gate, lg, lb):
    r = DN_ALPHA * x + gate * y
    mu = jnp.mean(r, axis=-1, keepdims=True)
    rc = r - mu
    var = jnp.mean(rc * rc, axis=-1, keepdims=True)
    return rc * lax.rsqrt(var + LN_EPS) * lg + lb


def _outproj_kernel(*refs, gated, n_heads, dh, n_lat_rows):
    if gated:
        of_ref, ob_ref, z_ref, ng_ref, w_ref, x_ref, gate_ref, lg_ref, lb_ref, out_ref = refs
        y = None
        for h in range(n_heads):
            sl = slice(h * dh, (h + 1) * dh)
            o = of_ref[0, :, sl].astype(F32) + ob_ref[0, :, sl].astype(F32)
            ms = jnp.mean(o * o, axis=-1, keepdims=True)
            o = o * lax.rsqrt(ms + NORM_EPS) * ng_ref[...] * _silu(z_ref[0, :, sl].astype(F32))
            t = _dot(o.astype(BF16), w_ref[sl, :])
            y = t if y is None else y + t
    else:
        o_ref, w_ref, x_ref, gate_ref, lg_ref, lb_ref, out_ref = refs
        y = _dot(o_ref[0], w_ref[...])
    gate = _select_mod(gate_ref, x_ref.shape[1], n_lat_rows)
    out_ref[0] = _post_norm(x_ref[0], y, gate, lg_ref[...], lb_ref[...])


def out_projection_post_norm(o_args, w_o, x, gate, lg, lb, n_lat_rows, gated=None, tm=ROW_TILE):
    b, s, d = x.shape
    kdim = w_o.shape[0]
    row = lambda width, cb=0: pl.BlockSpec((1, tm, width), lambda bi, i: (bi, i, cb))
    full = lambda shape: pl.BlockSpec(shape, lambda bi, i: (0,) * len(shape))
    if gated is None:
        in_specs = [row(kdim)]
        args = list(o_args)
        kern = functools.partial(_outproj_kernel, gated=False, n_heads=0, dh=0, n_lat_rows=n_lat_rows)
    else:
        n_heads, dh, z_cb = gated
        of, ob, z, ng = o_args
        in_specs = [row(kdim), row(kdim), row(kdim, z_cb), full((1, dh))]
        args = [of, ob, z, ng.reshape(1, dh)]
        kern = functools.partial(_outproj_kernel, gated=True, n_heads=n_heads, dh=dh, n_lat_rows=n_lat_rows)
    in_specs += [full((kdim, d)), row(d),
                 pl.BlockSpec((1, 2, 1, d), lambda bi, i: (bi, 0, 0, 0)),
                 full((1, d)), full((1, d))]
    args += [w_o, x, gate, lg.reshape(1, d), lb.reshape(1, d)]
    return pl.pallas_call(
        kern, out_shape=jax.ShapeDtypeStruct((b, s, d), F32), grid=(b, s // tm),
        in_specs=in_specs, out_specs=row(d),
        compiler_params=_cparams(("arbitrary", "arbitrary")), name="out_projection_post_norm",
    )(*args)


def _gdn_prep_kernel(cur_ref, prev_ref, next_ref, gate_ref, cw_ref, alog_ref, dtb_ref,
                     q_ref, k_ref, v_ref, g_ref, ext, *, n_lat_tiles, n_tiles, cn):
    i = pl.program_id(1)
    tm = cur_ref.shape[1]
    first = jnp.logical_or(i == 0, i == n_lat_tiles)
    last = jnp.logical_or(i == n_lat_tiles - 1, i == n_tiles - 1)
    pmask = jnp.where(first, 0.0, 1.0)
    nmask = jnp.where(last, 0.0, 1.0)
    pad = GDN_CONV_W // 2
    dh = GDN_HEAD_DIM
    n_qk = 2 * GDN_KEY_DIM
    for cc in range(cur_ref.shape[2] // cn):
        cs = slice(cc * cn, (cc + 1) * cn)
        ext[0:HALO, :] = prev_ref[0, :, cs].astype(F32) * pmask
        ext[HALO:HALO + tm, :] = cur_ref[0, :, cs].astype(F32)
        ext[HALO + tm:2 * HALO + tm, :] = next_ref[0, :, cs].astype(F32) * nmask
        acc = None
        for j in range(GDN_CONV_W):
            t = ext[pl.ds(HALO - pad + j, tm), :] * cw_ref[j:j + 1, cs]
            acc = t if acc is None else acc + t
        y = _silu(acc)
        for g in range(cn // dh):
            col = cc * cn + g * dh
            yg = y[:, g * dh:(g + 1) * dh]
            if col < n_qk:
                yg = yg * lax.rsqrt(jnp.sum(yg * yg, axis=-1, keepdims=True) + NORM_EPS)
                if col < GDN_KEY_DIM:
                    q_ref[0, col // dh] = (yg * (dh ** -0.5)).astype(BF16)
                else:
                    k_ref[0, (col - GDN_KEY_DIM) // dh] = yg.astype(BF16)
            else:
                v_ref[0, (col - n_qk) // dh] = yg.astype(BF16)
    a = gate_ref[0]
    lane = lax.broadcasted_iota(jnp.int32, a.shape, 1)
    gdec = -jnp.exp(alog_ref[...]) * _softplus(a + dtb_ref[...])
    g_ref[0] = jnp.where(lane < 2 * GDN_V_HEADS, gdec, jax.nn.sigmoid(a))


def gdn_prep(p_main, gates_raw, conv_w, alog_row, dtb_row, n_lat_tiles):
    b, s, _ = p_main.shape
    tm = ROW_TILE
    nt = s // tm
    nch = 2 * GDN_KEY_DIM + GDN_VAL_DIM
    hpt = tm // HALO
    nh = s // HALO
    kern = functools.partial(_gdn_prep_kernel, n_lat_tiles=n_lat_tiles, n_tiles=nt, cn=512)
    head_out = lambda nheads: pl.BlockSpec((1, nheads, tm, GDN_HEAD_DIM), lambda bi, i: (bi, 0, i, 0))
    return pl.pallas_call(
        kern,
        out_shape=[jax.ShapeDtypeStruct((b, GDN_K_HEADS, s, GDN_HEAD_DIM), BF16),
                   jax.ShapeDtypeStruct((b, GDN_K_HEADS, s, GDN_HEAD_DIM), BF16),
                   jax.ShapeDtypeStruct((b, GDN_V_HEADS, s, GDN_HEAD_DIM), BF16),
                   jax.ShapeDtypeStruct((b, s, LANES), F32)],
        grid=(b, nt),
        in_specs=[
            pl.BlockSpec((1, tm, nch), lambda bi, i: (bi, i, 0)),
            pl.BlockSpec((1, HALO, nch), lambda bi, i: (bi, jnp.maximum(i * hpt - 1, 0), 0)),
            pl.BlockSpec((1, HALO, nch), lambda bi, i: (bi, jnp.minimum((i + 1) * hpt, nh - 1), 0)),
            pl.BlockSpec((1, tm, LANES), lambda bi, i: (bi, i, 0)),
            pl.BlockSpec((GDN_CONV_W, nch), lambda bi, i: (0, 0)),
            pl.BlockSpec((1, LANES), lambda bi, i: (0, 0)),
            pl.BlockSpec((1, LANES), lambda bi, i: (0, 0)),
        ],
        out_specs=[head_out(GDN_K_HEADS), head_out(GDN_K_HEADS), head_out(GDN_V_HEADS),
                   pl.BlockSpec((1, tm, LANES), lambda bi, i: (bi, i, 0))],
        scratch_shapes=[pltpu.VMEM((tm + 2 * HALO, 512), F32)],
        compiler_params=_cparams(("arbitrary", "arbitrary")),
        name="gdn_prep",
    )(p_main, p_main, p_main, gates_raw, conv_w, alog_row, dtb_row)


def _chunk_masks(n, reverse):
    r = lax.broadcasted_iota(jnp.int32, (n, n), 0)
    c = lax.broadcasted_iota(jnp.int32, (n, n), 1)
    if reverse:
        return r <= c, r < c
    return r >= c, r > c


def _block_cumsum_mats(ct, reverse):
    r = lax.broadcasted_iota(jnp.int32, (ct, ct), 0)
    c = lax.broadcasted_iota(jnp.int32, (ct, ct), 1)
    same = (r // CHUNK) == (c // CHUNK)
    lower = jnp.logical_and(same, c <= r)
    upper = jnp.logical_and(same, c >= r)
    lo = jnp.where(lower, 1.0, 0.0).astype(BF16)
    up = jnp.where(upper, 1.0, 0.0).astype(BF16)
    return (up, lo) if reverse else (lo, up)


def _cumsum_cols(mat, x):
    x0, x1, x2 = _split3(x)
    return _dot(mat, x0) + (_dot(mat, x1) + _dot(mat, x2))


def _cumsum_rows(x, mat):
    x0, x1, x2 = _split3(x)
    return _dot(x0, mat) + (_dot(x1, mat) + _dot(x2, mat))


def _gdn_block(dirs):
    dh = GDN_HEAD_DIM
    eye = jnp.where(lax.broadcasted_iota(jnp.int32, (CHUNK, CHUNK), 0)
                    == lax.broadcasted_iota(jnp.int32, (CHUNK, CHUNK), 1), 1.0, 0.0).astype(F32)
    chains = []
    for q_ref, k_ref, v_ref, gc_ref, gr_ref, o_ref, s_sc, reverse in dirs:
        ct = q_ref.shape[2]
        goff = 2 if reverse else 0
        m_col, m_row = _block_cumsum_mats(ct, reverse)
        tril, strict = _chunk_masks(CHUNK, reverse)
        for kh in range(q_ref.shape[1]):
            gcol = gc_ref[0, kh]
            grow = gr_ref[0, kh]
            gcum_c = _cumsum_cols(m_col, gcol)
            gcum_r = _cumsum_rows(grow, m_row)
            for c in range(ct // CHUNK):
                rs = slice(c * CHUNK, (c + 1) * CHUNK)
                last = c * CHUNK if reverse else (c + 1) * CHUNK - 1
                q = q_ref[0, kh, rs, :]
                k = k_ref[0, kh, rs, :]
                kk = _dot_nt(k, k)
                qk = _dot_nt(q, k)
                for hh in range(2):
                    gi = goff + hh
                    chains.append(dict(
                        q=q, k=k, kk=kk, qk=qk, rs=rs, c=c, hh=2 * kh + hh, reverse=reverse, tril=tril,
                        strict=strict, v_ref=v_ref, o_ref=o_ref, s_sc=s_sc,
                        gc=gcum_c[rs, gi:gi + 1], gr=gcum_r[gi:gi + 1, rs],
                        glast=gcum_c[last:last + 1, gi:gi + 1], beta=gcol[rs, 4 + gi:5 + gi]))
    for ch in chains:
        tril = ch["tril"]
        ch["decay"] = jnp.where(tril, jnp.exp(jnp.where(tril, ch["gc"] - ch["gr"], 0.0)), 0.0)
        a = -jnp.where(ch["strict"], ch["kk"] * ch["beta"] * ch["decay"], 0.0)
        ch["tmat"] = eye + a
        ch["pw"] = a
    for _ in range(5):
        for ch in chains:
            pwb = ch["pw"].astype(BF16)
            ch["pw"] = _dot(pwb, pwb)
        for ch in chains:
            ch["tmat"] = ch["tmat"] + _dot(ch["tmat"].astype(BF16), ch["pw"].astype(BF16))
    for ch in chains:
        beta = ch["beta"]
        eg = jnp.exp(ch["gc"])
        vb = ch["v_ref"][0, ch["hh"], ch["rs"], :].astype(F32) * beta
        kbg = ch["k"].astype(F32) * (beta * eg)
        uw = _dot(ch["tmat"].astype(BF16), jnp.concatenate([vb, kbg], axis=-1).astype(BF16))
        ch["u"] = uw[:, :dh]
        ch["w"] = uw[:, dh:].astype(BF16)
        ch["eg"] = eg
        ch["attn"] = jnp.where(ch["tril"], ch["qk"] * ch["decay"], 0.0).astype(BF16)
    nchunk = max(ch["c"] for ch in chains) + 1
    for step in range(nchunk):
        cur = [ch for ch in chains if ch["c"] == (nchunk - 1 - step if ch["reverse"] else step)]
        for ch in cur:
            state = ch["s_sc"][ch["hh"]]
            sb = state.astype(BF16)
            ch["state"] = state
            ch["ws"] = _dot(ch["w"], sb)
            ch["qs"] = _dot(ch["q"], sb)
        for ch in cur:
            v_new = ch["u"] - ch["ws"]
            ch["kgv"] = (v_new * jnp.exp(ch["glast"] - ch["gc"])).astype(BF16)
            ch["o"] = ch["eg"] * ch["qs"] + _dot(ch["attn"], v_new.astype(BF16))
        for ch in cur:
            ch["s_sc"][ch["hh"]] = ch["state"] * jnp.exp(ch["glast"]) + _dot_tn(ch["k"], ch["kgv"])
            ch["o_ref"][0, ch["rs"], ch["hh"] * dh:(ch["hh"] + 1) * dh] = ch["o"].astype(BF16)


def _gdn_scan_kernel(qf, kf, vf, gcf, grf, qb, kb, vb, gcb, grb, s0f, s0b, *rest, nblk, aliased):
    if aliased:
        rest = rest[2:]
    of_ref, ob_ref, sff, sfb, sf_sc, sb_sc = rest
    j = pl.program_id(2)

    @pl.when(j == 0)
    def _():
        sf_sc[...] = s0f[0]
        sb_sc[...] = s0b[0]

    _gdn_block([(qf, kf, vf, gcf, grf, of_ref, sf_sc, False),
                (qb, kb, vb, gcb, grb, ob_ref, sb_sc, True)])

    @pl.when(j == nblk - 1)
    def _():
        sff[0] = sf_sc[...]
        sfb[0] = sb_sc[...]


def gdn_scan(qn, kn, vv, gcol, grow, s0f, s0b, *, nblk, off, prev=None):
    b, _, s, dh = qn.shape
    ct = SCAN_TILE
    g = GDN_HEADS_PER_STEP
    fwd = lambda j: j + off
    bwd = lambda j: nblk - 1 - j + off
    def specs(pos):
        return [
            pl.BlockSpec((1, g, ct, dh), lambda bi, h, j: (bi, h, pos(j), 0)),
            pl.BlockSpec((1, g, ct, dh), lambda bi, h, j: (bi, h, pos(j), 0)),
            pl.BlockSpec((1, 2 * g, ct, dh), lambda bi, h, j: (bi, h, pos(j), 0)),
            pl.BlockSpec((1, g, ct, 8), lambda bi, h, j: (bi, h, pos(j), 0)),
            pl.BlockSpec((1, g, 8, ct), lambda bi, h, j: (bi, h, 0, pos(j))),
        ]
    st_spec = pl.BlockSpec((1, 2 * g, dh, dh), lambda bi, h, j: (bi, h, 0, 0))
    in_specs = specs(fwd) + specs(bwd) + [st_spec, st_spec]
    args = [qn, kn, vv, gcol, grow] * 2 + [s0f, s0b]
    aliases = {}
    if prev is not None:
        in_specs += [pl.BlockSpec(memory_space=pl.ANY)] * 2
        args += list(prev)
        aliases = {12: 0, 13: 1}
    o_shape = jax.ShapeDtypeStruct((b, s, GDN_VAL_DIM), BF16)
    st_shape = jax.ShapeDtypeStruct((b, GDN_V_HEADS, dh, dh), F32)
    kern = functools.partial(_gdn_scan_kernel, nblk=nblk, aliased=prev is not None)
    return pl.pallas_call(
        kern,
        out_shape=[o_shape, o_shape, st_shape, st_shape],
        grid=(b, GDN_K_HEADS // g, nblk),
        in_specs=in_specs,
        out_specs=[pl.BlockSpec((1, ct, 2 * g * dh), lambda bi, h, j: (bi, fwd(j), h)),
                   pl.BlockSpec((1, ct, 2 * g * dh), lambda bi, h, j: (bi, bwd(j), h)),
                   st_spec, st_spec],
        scratch_shapes=[pltpu.VMEM((2 * g, dh, dh), F32), pltpu.VMEM((2 * g, dh, dh), F32)],
        input_output_aliases=aliases,
        compiler_params=_cparams(("arbitrary", "arbitrary", "arbitrary")),
        name="gdn_scan",
    )(*args)


def _gla_block(dirs, wg_ref, gb_ref):
    chains = []
    pre = []
    for q_ref, k_ref, v_ref, gr_ref, o_ref, st_sc, z in dirs:
        pre.append(_dot3(gr_ref[0], wg_ref[z]) + gb_ref[z:z + 1, :])
    for (q_ref, k_ref, v_ref, gr_ref, o_ref, st_sc, z), logit in zip(dirs, pre):
        reverse = z == 1
        ct = q_ref.shape[1]
        glog = -_softplus(-logit) / GLA_TAU
        m_col, _ = _block_cumsum_mats(ct, reverse)
        bcum = _cumsum_cols(m_col, glog)
        tril, _ = _chunk_masks(CHUNK, reverse)
        for hh in range(st_sc.shape[0]):
            ks = slice(hh * GLA_DK, (hh + 1) * GLA_DK)
            vs = slice(hh * GLA_DV, (hh + 1) * GLA_DV)
            for c in range(ct // CHUNK):
                rs = slice(c * CHUNK, (c + 1) * CHUNK)
                last = c * CHUNK if reverse else (c + 1) * CHUNK - 1
                bc = bcum[rs, ks]
                bl = bcum[last:last + 1, ks]
                qf = q_ref[0, rs, ks].astype(F32) * (GLA_DK ** -0.5)
                kf = k_ref[0, rs, ks].astype(F32)
                chains.append(dict(
                    c=c, rs=rs, hh=hh, vs=vs, reverse=reverse, tril=tril, o_ref=o_ref, st_sc=st_sc,
                    v=v_ref[0, rs, vs],
                    qe=(qf * jnp.exp(bc)).astype(BF16), ke=(kf * jnp.exp(-bc)).astype(BF16),
                    kg=(kf * jnp.exp(bl - bc)).astype(BF16), gl=jnp.exp(bl)))
    for ch in chains:
        ch["attn"] = jnp.where(ch["tril"], _dot_nt(ch["qe"], ch["ke"]), 0.0).astype(BF16)
    for ch in chains:
        ch["o"] = _dot(ch["attn"], ch["v"])
        ch["kv"] = _dot_tn(ch["v"], ch["kg"])
    nchunk = max(ch["c"] for ch in chains) + 1
    for step in range(nchunk):
        cur = [ch for ch in chains if ch["c"] == (nchunk - 1 - step if ch["reverse"] else step)]
        for ch in cur:
            st = ch["st_sc"][ch["hh"]]
            ch["o"] = ch["o"] + _dot_nt(ch["qe"], st.astype(BF16))
            ch["st_sc"][ch["hh"]] = st * ch["gl"] + ch["kv"]
        for ch in cur:
            ch["o_ref"][0, ch["rs"], ch["vs"]] = ch["o"].astype(BF16)


def _gla_scan_kernel(qf, kf, vf, grf, qb, kb, vb, grb, wg, gb, s0f, s0b, *rest, nblk, aliased):
    if aliased:
        rest = rest[2:]
    of_ref, ob_ref, sff, sfb, sf_sc, sb_sc = rest
    j = pl.program_id(2)

    @pl.when(j == 0)
    def _():
        sf_sc[...] = s0f[0]
        sb_sc[...] = s0b[0]

    _gla_block([(qf, kf, vf, grf, of_ref, sf_sc, 0), (qb, kb, vb, grb, ob_ref, sb_sc, 1)], wg, gb)

    @pl.when(j == nblk - 1)
    def _():
        sff[0] = sf_sc[...]
        sfb[0] = sb_sc[...]


def gla_scan(p_main, gr, wg_pad, gate_b, s0f, s0b, *, nblk, off, prev=None):
    b, s, _ = p_main.shape
    ct = SCAN_TILE
    nh, dk, dv = GLA_HEADS, GLA_DK, GLA_DV
    g = GLA_HEADS_PER_STEP
    ng = nh // g
    fwd = lambda j: j + off
    bwd = lambda j: nblk - 1 - j + off
    def specs(pos):
        return [
            pl.BlockSpec((1, ct, g * dk), lambda bi, h, j: (bi, pos(j), h)),
            pl.BlockSpec((1, ct, g * dk), lambda bi, h, j: (bi, pos(j), ng + h)),
            pl.BlockSpec((1, ct, g * dv), lambda bi, h, j: (bi, pos(j), ng + h)),
            pl.BlockSpec((1, ct, LANES), lambda bi, h, j: (bi, pos(j), 0)),
        ]
    st_spec = pl.BlockSpec((1, g, dv, dk), lambda bi, h, j: (bi, h, 0, 0))
    in_specs = specs(fwd) + specs(bwd) + [
        pl.BlockSpec((2, LANES, g * dk), lambda bi, h, j: (0, 0, h)),
        pl.BlockSpec((2, g * dk), lambda bi, h, j: (0, h)),
        st_spec, st_spec]
    args = [p_main, p_main, p_main, gr] * 2 + [wg_pad, gate_b, s0f, s0b]
    aliases = {}
    if prev is not None:
        in_specs += [pl.BlockSpec(memory_space=pl.ANY)] * 2
        args += list(prev)
        aliases = {12: 0, 13: 1}
    o_shape = jax.ShapeDtypeStruct((b, s, GLA_VAL_DIM), BF16)
    st_shape = jax.ShapeDtypeStruct((b, nh, dv, dk), F32)
    kern = functools.partial(_gla_scan_kernel, nblk=nblk, aliased=prev is not None)
    return pl.pallas_call(
        kern,
        out_shape=[o_shape, o_shape, st_shape, st_shape],
        grid=(b, ng, nblk),
        in_specs=in_specs,
        out_specs=[pl.BlockSpec((1, ct, g * dv), lambda bi, h, j: (bi, fwd(j), h)),
                   pl.BlockSpec((1, ct, g * dv), lambda bi, h, j: (bi, bwd(j), h)),
                   st_spec, st_spec],
        scratch_shapes=[pltpu.VMEM((g, dv, dk), F32), pltpu.VMEM((g, dv, dk), F32)],
        input_output_aliases=aliases,
        compiler_params=_cparams(("arbitrary", "arbitrary", "arbitrary")),
        name="gla_scan",
    )(*args)


def _router_kernel(x_ref, sc_ref, sh_ref, rwt_ref, rb_ref, h_ref, wd_ref, *, n_lat_rows):
    tm = x_ref.shape[1]
    h = x_ref[0] * (1.0 + _select_mod(sc_ref, tm, n_lat_rows)) + _select_mod(sh_ref, tm, n_lat_rows)
    h_ref[0] = h.astype(BF16)
    w0, w1 = _split2(rwt_ref[...])
    h0, h1 = _split2(h)
    scores = jax.nn.sigmoid(_dot_nt(w0, h0) + (_dot_nt(w0, h1) + _dot_nt(w1, h0)))
    ne = scores.shape[0]
    row = lax.broadcasted_iota(jnp.int32, scores.shape, 0)
    neg = jnp.float32(-jnp.inf)
    sel = scores + rb_ref[...]
    chosen = jnp.zeros(scores.shape, jnp.bool_)
    for _ in range(TOP_K):
        mx = jnp.max(sel, axis=0, keepdims=True)
        first = jnp.min(jnp.where(sel == mx, row, ne), axis=0, keepdims=True)
        pick = row == first
        chosen = jnp.logical_or(chosen, pick)
        sel = jnp.where(pick, neg, sel)
    picked = jnp.where(chosen, scores, 0.0)
    wt = picked / jnp.sum(picked, axis=0, keepdims=True) * ROUTE_SCALE
    eye = jnp.where(lax.broadcasted_iota(jnp.int32, (ne, LANES), 0)
                    == lax.broadcasted_iota(jnp.int32, (ne, LANES), 1), 1.0, 0.0).astype(BF16)
    t0, t1, t2 = _split3(wt)
    wd_ref[0] = _dot_tn(t0, eye) + (_dot_tn(t1, eye) + _dot_tn(t2, eye))


def moe_route(x, sc, sh, rw_t, rb_col, n_lat_rows, tm=ROW_TILE):
    b, s, d = x.shape
    mod_spec = pl.BlockSpec((1, 2, 1, d), lambda bi, i: (bi, 0, 0, 0))
    return pl.pallas_call(
        functools.partial(_router_kernel, n_lat_rows=n_lat_rows),
        out_shape=[jax.ShapeDtypeStruct((b, s, d), BF16), jax.ShapeDtypeStruct((b, s, LANES), F32)],
        grid=(b, s // tm),
        in_specs=[pl.BlockSpec((1, tm, d), lambda bi, i: (bi, i, 0)), mod_spec, mod_spec,
                  pl.BlockSpec((N_EXPERTS, d), lambda bi, i: (0, 0)),
                  pl.BlockSpec((N_EXPERTS, 1), lambda bi, i: (0, 0))],
        out_specs=[pl.BlockSpec((1, tm, d), lambda bi, i: (bi, i, 0)),
                   pl.BlockSpec((1, tm, LANES), lambda bi, i: (bi, i, 0))],
        compiler_params=_cparams(("arbitrary", "arbitrary")),
        name="moe_router",
    )(x, sc, sh, rw_t, rb_col)


def _moe_kernel(h_ref, wd_ref, wgu_ref, wdn_ref, x_ref, gate_ref, lg_ref, lb_ref, out_ref, acc, *,
                n_exp, n_lat_rows):
    e = pl.program_id(2)

    @pl.when(e == 0)
    def _():
        acc[...] = jnp.zeros(acc.shape, F32)

    gu = _dot(h_ref[0], wgu_ref[0])
    act = _silu(gu[:, :EXPERT_FF]) * gu[:, EXPERT_FF:]
    wd = wd_ref[0]
    lane = lax.broadcasted_iota(jnp.int32, wd.shape, 1)
    wcol = jnp.sum(jnp.where(lane == e, wd, 0.0), axis=-1, keepdims=True)
    acc[...] += wcol * _dot(act.astype(BF16), wdn_ref[0])

    @pl.when(e == n_exp - 1)
    def _():
        tm = acc.shape[0]
        row = pl.program_id(1) * tm + lax.broadcasted_iota(jnp.int32, (tm, 1), 0)
        gate = jnp.where(row >= n_lat_rows, gate_ref[0, 1], gate_ref[0, 0])
        out_ref[0] = _post_norm(x_ref[0], acc[...], gate, lg_ref[...], lb_ref[...])


def moe_experts_post_norm(h, wd, wgu, wdn, x, gate, lg, lb, n_lat_rows, tm):
    b, s, d = x.shape
    n_exp = wgu.shape[0]
    row = lambda width: pl.BlockSpec((1, tm, width), lambda bi, i, e: (bi, i, 0))
    kern = functools.partial(_moe_kernel, n_exp=n_exp, n_lat_rows=n_lat_rows)
    return pl.pallas_call(
        kern,
        out_shape=jax.ShapeDtypeStruct((b, s, d), F32),
        grid=(b, s // tm, n_exp),
        in_specs=[row(d), row(LANES),
                  pl.BlockSpec((1, d, 2 * EXPERT_FF), lambda bi, i, e: (e, 0, 0)),
                  pl.BlockSpec((1, EXPERT_FF, d), lambda bi, i, e: (e, 0, 0)),
                  row(d),
                  pl.BlockSpec((1, 2, 1, d), lambda bi, i, e: (bi, 0, 0, 0)),
                  pl.BlockSpec((1, d), lambda bi, i, e: (0, 0)),
                  pl.BlockSpec((1, d), lambda bi, i, e: (0, 0))],
        out_specs=row(d),
        scratch_shapes=[pltpu.VMEM((tm, d), F32)],
        compiler_params=_cparams(("arbitrary", "arbitrary", "arbitrary")),
        name="moe_experts_post_norm",
    )(h, wd, wgu, wdn, x, gate, lg.reshape(1, d), lb.reshape(1, d))


MOE_SUB = 256
MOE_CAP = 64
MOE_GRP = 4


def _moe_grouped_kernel(order_ref, h_ref, wd_ref, *rest):
    wgu_refs = rest[:MOE_GRP]
    wdn_refs = rest[MOE_GRP:2 * MOE_GRP]
    wsgu_ref, wsdn_ref, f_ref, acc, rrm, wrm, cmax = rest[2 * MOE_GRP:]
    g = pl.program_id(2)
    n_grp = pl.num_programs(2)
    tm = acc.shape[0]
    nsub = tm // MOE_SUB
    sub, cap, nslot = MOE_SUB, MOE_CAP, MOE_GRP * MOE_CAP
    lane = lax.broadcasted_iota(jnp.int32, (1, LANES), 1)

    @pl.when(g == 0)
    def _():
        r = lax.broadcasted_iota(jnp.int32, (sub, sub), 0)
        c = lax.broadcasted_iota(jnp.int32, (sub, sub), 1)
        after = jnp.where(r < c, 1.0, 0.0).astype(BF16)
        ident = jnp.where(r == c, 1.0, 0.0).astype(BF16)
        cm = jnp.zeros((1, LANES), F32)
        for u in range(nsub):
            rs = slice(u * sub, (u + 1) * sub)
            hu = h_ref[0, rs, :]
            gu = _dot(hu, wsgu_ref[...])
            act = _silu(gu[:, :EXPERT_FF]) * gu[:, EXPERT_FF:]
            acc[rs, :] = _dot(act.astype(BF16), wsdn_ref[...])
            wd = wd_ref[0, rs, :]
            active = jnp.logical_and(wd != 0.0, lane < N_EXPERTS)
            a = jnp.where(active, 1.0, 0.0)
            ab = a.astype(BF16)
            rank_r = _dot_tn(ab, after)
            a_r = _dot_tn(ab, ident)
            rrm[u] = jnp.where(a_r > 0.5, rank_r, -1.0)
            wrm[u] = _dot_tn(wd.astype(BF16), ident)
            cm = jnp.maximum(cm, jnp.sum(a, axis=0, keepdims=True))
        cmax[...] = jnp.broadcast_to(cm, cmax.shape)

    experts = [order_ref[g * MOE_GRP + k] for k in range(MOE_GRP)]
    in_group = functools.reduce(jnp.logical_or, [lane == e for e in experts])
    n_max = jnp.max(jnp.where(in_group, cmax[0:1, :], 0.0))
    n_pass = (n_max.astype(jnp.int32) + (cap - 1)) // cap

    def slot_expert(l):
        return sum(((l >= k * cap).astype(jnp.int32) for k in range(1, MOE_GRP)), jnp.zeros_like(l))

    l_col = lax.broadcasted_iota(jnp.int32, (sub, 1), 0)
    j_col = jnp.where(l_col < nslot, l_col - cap * slot_expert(l_col), -1000).astype(F32)

    def per_slot(table, u, fill):
        rows = [jnp.broadcast_to(table[u, pl.ds(experts[k], 1), :], (cap, sub)) for k in range(MOE_GRP)]
        if nslot < sub:
            rows.append(jnp.full((sub - nslot, sub), fill, F32))
        return jnp.concatenate(rows, axis=0)

    def one_pass(p, carry):
        base = (p * cap).astype(F32)
        xg, expanders = [], []
        for u in range(nsub):
            rs = slice(u * sub, (u + 1) * sub)
            match = per_slot(rrm, u, -1.0) == j_col + base
            gather = jnp.where(match, 1.0, 0.0).astype(BF16)
            expanders.append(jnp.where(match, per_slot(wrm, u, 0.0), 0.0).astype(BF16))
            xg.append(_dot(gather, h_ref[0, rs, :]).astype(BF16))
        ys = []
        for k in range(MOE_GRP):
            es = slice(k * cap, (k + 1) * cap)
            x_e = jnp.concatenate([xg[u][es] for u in range(nsub)], axis=0)
            gu = _dot(x_e, wgu_refs[k][0])
            act = _silu(gu[:, :EXPERT_FF]) * gu[:, EXPERT_FF:]
            ys.append(_dot(act.astype(BF16), wdn_refs[k][0]).astype(BF16))
        for u in range(nsub):
            rs = slice(u * sub, (u + 1) * sub)
            parts = [ys[k][u * cap:(u + 1) * cap] for k in range(MOE_GRP)]
            if nslot < sub:
                parts.append(jnp.zeros((sub - nslot, parts[0].shape[1]), BF16))
            y_u = jnp.concatenate(parts, axis=0)
            acc[rs, :] += _dot_tn(expanders[u], y_u)
        return carry

    lax.fori_loop(0, n_pass, one_pass, 0)

    @pl.when(g == n_grp - 1)
    def _():
        f_ref[0] = acc[...]


def moe_grouped_experts(order, h, wd, wgu, wdn, wsgu, wsdn, tm):
    b, s, d = h.shape
    n_grp = wgu.shape[0] // MOE_GRP
    row = lambda width: pl.BlockSpec((1, tm, width), lambda bi, i, g, o: (bi, i, 0))

    def expert(shape, k):
        return pl.BlockSpec((1,) + shape, lambda bi, i, g, o: (o[g * MOE_GRP + k], 0, 0))

    in_specs = ([row(d), row(LANES)]
                + [expert((d, 2 * EXPERT_FF), k) for k in range(MOE_GRP)]
                + [expert((EXPERT_FF, d), k) for k in range(MOE_GRP)]
                + [pl.BlockSpec((d, 2 * EXPERT_FF), lambda bi, i, g, o: (0, 0)),
                   pl.BlockSpec((EXPERT_FF, d), lambda bi, i, g, o: (0, 0))])
    grid_spec = pltpu.PrefetchScalarGridSpec(
        num_scalar_prefetch=1, grid=(b, s // tm, n_grp), in_specs=in_specs, out_specs=row(d),
        scratch_shapes=[pltpu.VMEM((tm, d), F32),
                        pltpu.VMEM((tm // MOE_SUB, LANES, MOE_SUB), F32),
                        pltpu.VMEM((tm // MOE_SUB, LANES, MOE_SUB), F32),
                        pltpu.VMEM((8, LANES), F32)])
    return pl.pallas_call(
        _moe_grouped_kernel,
        out_shape=jax.ShapeDtypeStruct((b, s, d), F32),
        grid_spec=grid_spec,
        compiler_params=_cparams(("arbitrary", "arbitrary", "arbitrary")),
        name="moe_grouped_experts",
    )(order, h, wd, *([wgu] * MOE_GRP), *([wdn] * MOE_GRP), wsgu, wsdn)


def _post_norm_kernel(x_ref, f_ref, gate_ref, lg_ref, lb_ref, out_ref, *, n_lat_rows):
    gate = _select_mod(gate_ref, x_ref.shape[1], n_lat_rows)
    out_ref[0] = _post_norm(x_ref[0], f_ref[0].astype(F32), gate, lg_ref[...], lb_ref[...])


def post_norm_rows(x, f, gate, lg, lb, n_lat_rows, rows_out=None):
    b, s, d = x.shape
    rows_out = s if rows_out is None else rows_out
    tm = _pick_tile(rows_out, (1280, 1024, 640, 512, 256))
    row = pl.BlockSpec((1, tm, d), lambda bi, i: (bi, i, 0))
    vec = pl.BlockSpec((1, d), lambda bi, i: (0, 0))
    return pl.pallas_call(
        functools.partial(_post_norm_kernel, n_lat_rows=n_lat_rows),
        out_shape=jax.ShapeDtypeStruct((b, rows_out, d), F32), grid=(b, rows_out // tm),
        in_specs=[row, row, pl.BlockSpec((1, 2, 1, d), lambda bi, i: (bi, 0, 0, 0)), vec, vec],
        out_specs=row, compiler_params=_cparams(("arbitrary", "arbitrary")), name="post_norm_rows",
    )(x, f, gate, lg.reshape(1, d), lb.reshape(1, d))


def _rope_tables(n_lat, n_ctx):
    rows = n_lat // GRID_W
    rowp = jnp.repeat(jnp.arange(rows), GRID_W).astype(F32)
    colp = jnp.tile(jnp.arange(GRID_W), rows).astype(F32)
    n_freq = DA_HEAD_DIM // 4
    inv = 1.0 / (ROPE_BASE ** (jnp.arange(n_freq, dtype=F32) / n_freq))
    ang = jnp.concatenate([rowp[:, None] * inv, colp[:, None] * inv], -1)
    cos, sin = jnp.cos(ang), jnp.sin(ang)
    cos_t = jnp.tile(cos, (1, 4))
    sin_t = jnp.tile(jnp.concatenate([-sin, sin], -1), (1, 2))
    cos_t = jnp.concatenate([cos_t, jnp.ones((n_ctx, LANES), F32)], 0)
    sin_t = jnp.concatenate([sin_t, jnp.zeros((n_ctx, LANES), F32)], 0)
    return cos_t, sin_t


def _pick_tile(total, cands):
    for t in cands:
        if total % t == 0:
            return t
    raise ValueError(f"no tile for {total}")


def _pad_cols(w, n):
    return jnp.pad(w, ((0, 0), (0, n - w.shape[1])))


def _flash_both(p_all, lam_vec, subln, lam_init, n_lat, n_ctx):
    s = n_lat + n_ctx
    tq = _pick_tile(n_lat, (1024, 512, 256))
    tk = _pick_tile(s, (3328, 1280, 1024, 512, 256))
    o = diff_flash_attention(p_all, lam_vec, subln, lam_init, tq=tq, tk=tk, nq=n_lat // tq, nk=s // tk,
                             q_off=0, k_off=0)
    return diff_flash_attention(p_all, lam_vec, subln, lam_init, tq=n_ctx, tk=n_ctx, nq=1, nk=1,
                                q_off=n_lat // n_ctx, k_off=n_lat // n_ctx, prev=o)


def kernel(x, c, ctx, c_ctx, ada_w, ada_b, ln_g, ln_b, da_w_in, da_w_o, da_lambda, da_subln, gdn_w_in, gdn_conv, gdn_a_log, gdn_dt_bias, gdn_norm, gdn_w_o, gla_w_in, gla_w_gate, gla_gate_b, gla_norm, gla_w_o, moe_router, moe_router_b, moe_w_gu, moe_w_dn, moe_ws_gu, moe_ws_dn):
    b, n, d = x.shape
    lc = ctx.shape[1]
    assert lc == ROW_TILE and n % SCAN_TILE == 0 and d == D_MODEL
    s = n + lc
    n_lat_tiles = n // ROW_TILE
    depth = ada_w.shape[0]

    xall = jnp.concatenate([x, ctx], axis=1)
    c8 = jnp.concatenate([c, c_ctx[None], jnp.zeros((8 - b - 1, d), F32)], 0)
    mods = ada_modulation(c8, ada_w, ada_b)
    rope = _rope_tables(n, lc)
    moe_tm = _pick_tile(s, (1280, 1024, 512, 256))
    big_tm = _pick_tile(s, (1280, 1024, 512, 256))
    mid_tm = _pick_tile(s, (640, 512, 256))

    for i in range(depth):
        kind, j = i % N_MIXERS, i // N_MIXERS
        m = mods[i].reshape(8, ADA_CHUNKS, d)
        mod = jnp.stack([m[:b], jnp.broadcast_to(m[b], (b, ADA_CHUNKS, d))], axis=1)
        mod = [mod[:, :, k][:, :, None, :] for k in range(ADA_CHUNKS)]

        if kind == 0:
            lam_init = 0.8 - 0.6 * math.exp(-0.3 * i)
            w = da_w_in[j]
            w = jnp.concatenate([w[:, :d] * (DA_HEAD_DIM ** -0.5 * math.log2(math.e)), w[:, d:]], 1).astype(BF16)
            p_all = modulated_projection(xall, mod[1], mod[0], w, rope=rope, n_rope=2 * d,
                                         n_lat_rows=n, head_major=True, tm=big_tm)
            o = _flash_both(p_all, da_lambda[j], da_subln[j], lam_init, n, lc)
            xall = out_projection_post_norm([o], da_w_o[j].astype(BF16), xall, mod[2],
                                            ln_g[i, 0], ln_b[i, 0], n, tm=big_tm)
        elif kind == 1:
            w = gdn_w_in[j]
            n_main = 2 * GDN_KEY_DIM + 2 * GDN_VAL_DIM
            p_main, gates_raw = modulated_projection(
                xall, mod[1], mod[0], w[:, :n_main].astype(BF16),
                w_small=_pad_cols(w[:, n_main:], LANES).astype(BF16), n_lat_rows=n, tm=mid_tm)
            alog_row = _pad_cols(gdn_a_log[j].reshape(1, -1), LANES)
            dtb_row = _pad_cols(gdn_dt_bias[j].reshape(1, -1), LANES)
            qn, kn, vv, gates = gdn_prep(p_main, gates_raw, gdn_conv[j], alog_row, dtb_row, n_lat_tiles)
            hv, hk = GDN_V_HEADS, GDN_K_HEADS
            def per_khead(t):
                t = t.reshape(b, s, 2, hk, 2)
                return jnp.transpose(t, (0, 3, 1, 2, 4)).reshape(b, hk, s, 4)
            gcol = jnp.concatenate([per_khead(gates[..., :2 * hv]), per_khead(gates[..., 2 * hv:4 * hv])], -1)
            grow = jnp.swapaxes(gcol, 2, 3)
            zeros = jnp.zeros((b, hv, GDN_HEAD_DIM, GDN_HEAD_DIM), F32)
            of, ob, scf, scb = gdn_scan(qn, kn, vv, gcol, grow, zeros, zeros, nblk=lc // SCAN_TILE,
                                        off=n // SCAN_TILE)
            of, ob, _, _ = gdn_scan(qn, kn, vv, gcol, grow, scf, scb, nblk=n // SCAN_TILE, off=0,
                                    prev=(of, ob))
            xall = out_projection_post_norm([of, ob, p_main, gdn_norm[j]], gdn_w_o[j].astype(BF16), xall,
                                            mod[2], ln_g[i, 0], ln_b[i, 0], n,
                                            gated=(GDN_V_HEADS, GDN_HEAD_DIM, 2), tm=mid_tm)
        else:
            w = gla_w_in[j]
            n_main = 2 * GLA_KEY_DIM + 2 * GLA_VAL_DIM
            p_main, gr = modulated_projection(
                xall, mod[1], mod[0], w[:, :n_main].astype(BF16),
                w_small=_pad_cols(w[:, n_main:], LANES).astype(BF16), n_lat_rows=n, tm=big_tm)
            wg = gla_w_gate[j]
            wg_pad = jnp.zeros((2, LANES, GLA_KEY_DIM), F32)
            wg_pad = wg_pad.at[0, :GLA_GATE_RANK].set(wg[0]).at[1, GLA_GATE_RANK:2 * GLA_GATE_RANK].set(wg[1])
            zeros = jnp.zeros((b, GLA_HEADS, GLA_DV, GLA_DK), F32)
            of, ob, scf, scb = gla_scan(p_main, gr, wg_pad, gla_gate_b[j], zeros, zeros,
                                        nblk=lc // SCAN_TILE, off=n // SCAN_TILE)
            of, ob, _, _ = gla_scan(p_main, gr, wg_pad, gla_gate_b[j], scf, scb, nblk=n // SCAN_TILE, off=0,
                                    prev=(of, ob))
            xall = out_projection_post_norm([of, ob, p_main, gla_norm[j]], gla_w_o[j].astype(BF16), xall,
                                            mod[2], ln_g[i, 0], ln_b[i, 0], n,
                                            gated=(GLA_HEADS, GLA_DV, 2), tm=mid_tm)

        h, wd = moe_route(xall, mod[4], mod[3], moe_router[i].T, moe_router_b[i].reshape(-1, 1), n, tm=big_tm)
        popularity = jnp.sum((wd[..., :N_EXPERTS] != 0.0).astype(jnp.int32), axis=(0, 1))
        order = jnp.argsort(popularity).astype(jnp.int32)
        f = moe_grouped_experts(order, h, wd, moe_w_gu[i].astype(BF16), moe_w_dn[i].astype(BF16),
                                moe_ws_gu[i].astype(BF16), moe_ws_dn[i].astype(BF16), moe_tm)
        xall = post_norm_rows(xall, f, mod[5], ln_g[i, 1], ln_b[i, 1], n,
                              rows_out=n if i == depth - 1 else None)

    return xall
```

```python
import functools
import math

import jax
import jax.numpy as jnp
from jax import lax
from jax.experimental import pallas as pl
from jax.experimental.pallas import tpu as pltpu

F32 = jnp.float32
BF16 = jnp.bfloat16

D_MODEL = 1024
DEPTH = 4
GRID_W = 64
N_MIXERS = 3
DN_ALPHA = (2 * DEPTH) ** 0.25
LN_EPS = 1e-5
NORM_EPS = 1e-6
ADA_CHUNKS = 6

DA_HEADS = 8
DA_HEAD_DIM = 64
ROPE_BASE = 10000.0

GDN_K_HEADS = 8
GDN_V_HEADS = 16
GDN_HEAD_DIM = 128
GDN_KEY_DIM = 1024
GDN_VAL_DIM = 2048
GDN_CONV_W = 5
CHUNK = 64

GLA_HEADS = 4
GLA_KEY_DIM = 512
GLA_VAL_DIM = 1024
GLA_DK = 128
GLA_DV = 256
GLA_GATE_RANK = 16
GLA_TAU = 16.0

N_EXPERTS = 64
TOP_K = 8
EXPERT_FF = 256
ROUTE_SCALE = 2.5

LANES = 128
ROW_TILE = 256
SCAN_TILE = 256
GDN_HEADS_PER_STEP = 4
GLA_HEADS_PER_STEP = 4
HALO = 16
VMEM_LIMIT = 56 * 1024 * 1024


def _cparams(sem, flags=None):
    return pltpu.CompilerParams(dimension_semantics=sem, vmem_limit_bytes=VMEM_LIMIT, flags=flags)


def _dot(a, b):
    return jnp.dot(a, b, preferred_element_type=F32)


def _dot_nt(a, b):
    return lax.dot_general(a, b, (((1,), (1,)), ((), ())), preferred_element_type=F32)


def _dot_tn(a, b):
    return lax.dot_general(a, b, (((0,), (0,)), ((), ())), preferred_element_type=F32)


def _split2(a):
    hi = a.astype(BF16)
    lo = (a - hi.astype(F32)).astype(BF16)
    return hi, lo


def _split3(a):
    a0 = a.astype(BF16)
    r = a - a0.astype(F32)
    a1 = r.astype(BF16)
    a2 = (r - a1.astype(F32)).astype(BF16)
    return a0, a1, a2


def _dot3(a, b):
    a0, a1 = _split2(a)
    b0, b1 = _split2(b)
    return _dot(a0, b0) + (_dot(a0, b1) + _dot(a1, b0))


def _select_mod(ref, tile_rows, n_lat_rows):
    row = pl.program_id(1) * tile_rows + lax.broadcasted_iota(jnp.int32, (tile_rows, 1), 0)
    return jnp.where(row >= n_lat_rows, ref[0, 1], ref[0, 0])


def _silu(x):
    return x * jax.nn.sigmoid(x)


def _softplus(x):
    return jnp.maximum(x, 0.0) + jnp.log(1.0 + jnp.exp(-jnp.abs(x)))


def _ada_kernel(c_ref, w_ref, b_ref, o_ref):
    s = _silu(c_ref[...])
    o_ref[0] = _dot3(s, w_ref[0]) + b_ref[0]


def ada_modulation(c8, ada_w, ada_b):
    depth, d, n = ada_w.shape
    tn = 1536
    return pl.pallas_call(
        _ada_kernel,
        out_shape=jax.ShapeDtypeStruct((depth, 8, n), F32),
        grid=(depth, n // tn),
        in_specs=[
            pl.BlockSpec((8, d), lambda i, j: (0, 0)),
            pl.BlockSpec((1, d, tn), lambda i, j: (i, 0, j)),
            pl.BlockSpec((1, 1, tn), lambda i, j: (i, 0, j)),
        ],
        out_specs=pl.BlockSpec((1, 8, tn), lambda i, j: (i, 0, j)),
        compiler_params=_cparams(("arbitrary", "arbitrary")),
        name="ada_modulation",
    )(c8, ada_w, ada_b.reshape(depth, 1, n))


def _proj_kernel(*refs, n_main, n_rope, has_small, cn, head_major, n_lat_rows):
    it = iter(refs)
    x_ref, sc_ref, sh_ref, w_ref = next(it), next(it), next(it), next(it)
    ws_ref = next(it) if has_small else None
    cos_ref = next(it) if n_rope else None
    sin_ref = next(it) if n_rope else None
    o_ref = next(it)
    os_ref = next(it) if has_small else None

    tm = x_ref.shape[1]
    h = x_ref[0] * (1.0 + _select_mod(sc_ref, tm, n_lat_rows)) + _select_mod(sh_ref, tm, n_lat_rows)
    hb = h.astype(BF16)
    if n_rope:
        cos = cos_ref[...]
        sin = sin_ref[...]
        lane = lax.broadcasted_iota(jnp.int32, (tm, LANES), 1)
        low_half = (lane & 32) == 0
    for j in range(n_main // cn):
        p = _dot(hb, w_ref[:, j * cn:(j + 1) * cn])
        for g in range(cn // LANES):
            pg = p[:, g * LANES:(g + 1) * LANES]
            col = j * cn + g * LANES
            if col < n_rope:
                partner = jnp.where(low_half, pltpu.roll(pg, LANES - 32, 1), pltpu.roll(pg, 32, 1))
                pg = pg * cos + partner * sin
            if head_major:
                o_ref[0, col // LANES] = pg.astype(BF16)
            else:
                o_ref[0, :, col:col + LANES] = pg.astype(BF16)
    if has_small:
        os_ref[0] = _dot(hb, ws_ref[...])


def modulated_projection(x, sc, sh, w, w_small=None, rope=None, n_rope=0, n_lat_rows=0, head_major=False, tm=ROW_TILE):
    b, s, d = x.shape
    n_main = w.shape[1]
    grid = (b, s // tm)
    mod_spec = pl.BlockSpec((1, 2, 1, d), lambda bi, i: (bi, 0, 0, 0))
    in_specs = [pl.BlockSpec((1, tm, d), lambda bi, i: (bi, i, 0)), mod_spec, mod_spec,
                pl.BlockSpec((d, n_main), lambda bi, i: (0, 0))]
    args = [x, sc, sh, w]
    if head_major:
        out_shape = [jax.ShapeDtypeStruct((b, n_main // LANES, s, LANES), BF16)]
        out_specs = [pl.BlockSpec((1, n_main // LANES, tm, LANES), lambda bi, i: (bi, 0, i, 0))]
    else:
        out_shape = [jax.ShapeDtypeStruct((b, s, n_main), BF16)]
        out_specs = [pl.BlockSpec((1, tm, n_main), lambda bi, i: (bi, i, 0))]
    if w_small is not None:
        in_specs.append(pl.BlockSpec((d, LANES), lambda bi, i: (0, 0)))
        args.append(w_small)
        out_shape.append(jax.ShapeDtypeStruct((b, s, LANES), F32))
        out_specs.append(pl.BlockSpec((1, tm, LANES), lambda bi, i: (bi, i, 0)))
    if n_rope:
        tab = pl.BlockSpec((tm, LANES), lambda bi, i: (i, 0))
        in_specs += [tab, tab]
        args += [rope[0], rope[1]]
    kern = functools.partial(_proj_kernel, n_main=n_main, n_rope=n_rope,
                             has_small=w_small is not None, cn=512, head_major=head_major,
                             n_lat_rows=n_lat_rows)
    out = pl.pallas_call(
        kern, out_shape=out_shape, grid=grid, in_specs=in_specs, out_specs=out_specs,
        compiler_params=_cparams(("arbitrary", "arbitrary")), name="modulated_projection",
    )(*args)
    return out if w_small is not None else out[0]


NEG_INIT = -1e30
FLASH_ROW_BLOCK = 512
FLASH_LOOKAHEAD = 1


def _flash_kernel(lam_ref, q_ref, k_ref, v_ref, sub_ref, *rest, lam_init, nk, aliased):
    if aliased:
        rest = rest[1:]
    o_ref, m_sc, l_sc, acc_sc, s_ring = rest
    ki = pl.program_id(3)
    tq = m_sc.shape[1]
    tk = k_ref.shape[2]
    nring = s_ring.shape[0]

    @pl.when(ki == 0)
    def _():
        m_sc[...] = jnp.full(m_sc.shape, NEG_INIT, F32)
        l_sc[...] = jnp.zeros(l_sc.shape, F32)
        acc_sc[...] = jnp.zeros(acc_sc.shape, F32)

    q = q_ref[0, 0]
    k = k_ref[0, 0]
    v = v_ref[0, 0]
    hd = DA_HEAD_DIM
    rb = min(tq, FLASH_ROW_BLOCK)
    kc = 2 * LANES
    blocks = [(c, r0) for c in range(2) for r0 in range(0, tq, rb)]
    for i in range(len(blocks) + FLASH_LOOKAHEAD):
        if i < len(blocks):
            c, r0 = blocks[i]
            s_ring[i % nring] = _dot_nt(q[r0:r0 + rb, c * hd:(c + 1) * hd], k[:, c * hd:(c + 1) * hd])
        j = i - FLASH_LOOKAHEAD
        if j < 0:
            continue
        c, r0 = blocks[j]
        rows = slice(r0, r0 + rb)
        s_blk = s_ring.at[j % nring]
        m_prev = m_sc[c, rows, :]
        m_new = jnp.maximum(m_prev, jnp.max(s_blk[...], axis=-1, keepdims=True))
        alpha = jnp.exp2(m_prev - m_new)
        m2 = jnp.concatenate([m_new, m_new], axis=-1)
        lsum = None
        pv = None
        for t in range(tk // kc):
            pj = jnp.exp2(s_blk[:, t * kc:(t + 1) * kc] - m2)
            lj = pj[:, :LANES] + pj[:, LANES:]
            lsum = lj if lsum is None else lsum + lj
            d = _dot(pj.astype(BF16), v[t * kc:(t + 1) * kc, :])
            pv = d if pv is None else pv + d
        l_sc[c, rows, :] = alpha * l_sc[c, rows, :] + lsum
        acc_sc[c, rows, :] = alpha * acc_sc[c, rows, :] + pv
        m_sc[c, rows, :] = m_new

    @pl.when(ki == nk - 1)
    def _():
        lv = lam_ref[...]
        lam = (jnp.exp(jnp.sum(lv[0:1] * lv[1:2], axis=-1, keepdims=True))
               - jnp.exp(jnp.sum(lv[2:3] * lv[3:4], axis=-1, keepdims=True)) + lam_init)
        l0 = jnp.sum(l_sc[0], axis=-1, keepdims=True)
        l1 = jnp.sum(l_sc[1], axis=-1, keepdims=True)
        o = acc_sc[0] / l0 - lam * (acc_sc[1] / l1)
        ms = jnp.mean(o * o, axis=-1, keepdims=True)
        o = o * lax.rsqrt(ms + NORM_EPS) * sub_ref[...] * (1.0 - lam_init)
        o_ref[0] = o.astype(BF16)


def diff_flash_attention(p_all, lam_vec, subln, lam_init, *, tq, tk, nq, nk, q_off, k_off, prev=None):
    b, _, s, _ = p_all.shape
    hh = DA_HEADS
    in_specs = [
        pl.BlockSpec((4, DA_HEAD_DIM), lambda bi, h, qi, ki: (0, 0)),
        pl.BlockSpec((1, 1, tq, LANES), lambda bi, h, qi, ki: (bi, h, qi + q_off, 0)),
        pl.BlockSpec((1, 1, tk, LANES), lambda bi, h, qi, ki: (bi, hh + h, ki + k_off, 0)),
        pl.BlockSpec((1, 1, tk, LANES), lambda bi, h, qi, ki: (bi, 2 * hh + h, ki + k_off, 0)),
        pl.BlockSpec((1, LANES), lambda bi, h, qi, ki: (0, 0)),
    ]
    args = [lam_vec, p_all, p_all, p_all, subln.reshape(1, LANES)]
    aliases = {}
    if prev is not None:
        in_specs.append(pl.BlockSpec(memory_space=pl.ANY))
        args.append(prev)
        aliases = {5: 0}
    kern = functools.partial(_flash_kernel, lam_init=lam_init, nk=nk, aliased=prev is not None)
    return pl.pallas_call(
        kern,
        out_shape=jax.ShapeDtypeStruct((b, s, hh * LANES), BF16),
        grid=(b, hh, nq, nk),
        in_specs=in_specs,
        out_specs=pl.BlockSpec((1, tq, LANES), lambda bi, h, qi, ki: (bi, qi + q_off, h)),
        scratch_shapes=[pltpu.VMEM((2, tq, LANES), F32), pltpu.VMEM((2, tq, LANES), F32),
                        pltpu.VMEM((2, tq, LANES), F32),
                        pltpu.VMEM((FLASH_LOOKAHEAD + 1, min(tq, FLASH_ROW_BLOCK), tk), F32)],
        input_output_aliases=aliases,
        compiler_params=_cparams(("arbitrary", "arbitrary", "arbitrary", "arbitrary")),
        name="diff_flash_attention",
    )(*args)


def _post_norm(x, y, gate, lg, lb):
    r = DN_ALPHA * x + gate * y
    mu = jnp.mean(r, axis=-1, keepdims=True)
    rc = r - mu
    var = jnp.mean(rc * rc, axis=-1, keepdims=True)
    return rc * lax.rsqrt(var + LN_EPS) * lg + lb


def _outproj_kernel(*refs, gated, n_heads, dh, n_lat_rows):
    if gated:
        of_ref, ob_ref, z_ref, ng_ref, w_ref, x_ref, gate_ref, lg_ref, lb_ref, out_ref = refs
        y = None
        for h in range(n_heads):
            sl = slice(h * dh, (h + 1) * dh)
            o = of_ref[0, :, sl].astype(F32) + ob_ref[0, :, sl].astype(F32)
            ms = jnp.mean(o * o, axis=-1, keepdims=True)
            o = o * lax.rsqrt(ms + NORM_EPS) * ng_ref[...] * _silu(z_ref[0, :, sl].astype(F32))
            t = _dot(o.astype(BF16), w_ref[sl, :])
            y = t if y is None else y + t
    else:
        o_ref, w_ref, x_ref, gate_ref, lg_ref, lb_ref, out_ref = refs
        y = _dot(o_ref[0], w_ref[...])
    gate = _select_mod(gate_ref, x_ref.shape[1], n_lat_rows)
    out_ref[0] = _post_norm(x_ref[0], y, gate, lg_ref[...], lb_ref[...])


def out_projection_post_norm(o_args, w_o, x, gate, lg, lb, n_lat_rows, gated=None, tm=ROW_TILE):
    b, s, d = x.shape
    kdim = w_o.shape[0]
    row = lambda width, cb=0: pl.BlockSpec((1, tm, width), lambda bi, i: (bi, i, cb))
    full = lambda shape: pl.BlockSpec(shape, lambda bi, i: (0,) * len(shape))
    if gated is None:
        in_specs = [row(kdim)]
        args = list(o_args)
        kern = functools.partial(_outproj_kernel, gated=False, n_heads=0, dh=0, n_lat_rows=n_lat_rows)
    else:
        n_heads, dh, z_cb = gated
        of, ob, z, ng = o_args
        in_specs = [row(kdim), row(kdim), row(kdim, z_cb), full((1, dh))]
        args = [of, ob, z, ng.reshape(1, dh)]
        kern = functools.partial(_outproj_kernel, gated=True, n_heads=n_heads, dh=dh, n_lat_rows=n_lat_rows)
    in_specs += [full((kdim, d)), row(d),
                 pl.BlockSpec((1, 2, 1, d), lambda bi, i: (bi, 0, 0, 0)),
                 full((1, d)), full((1, d))]
    args += [w_o, x, gate, lg.reshape(1, d), lb.reshape(1, d)]
    return pl.pallas_call(
        kern, out_shape=jax.ShapeDtypeStruct((b, s, d), F32), grid=(b, s // tm),
        in_specs=in_specs, out_specs=row(d),
        compiler_params=_cparams(("arbitrary", "arbitrary")), name="out_projection_post_norm",
    )(*args)


def _gdn_prep_kernel(cur_ref, prev_ref, next_ref, gate_ref, cw_ref, alog_ref, dtb_ref,
                     q_ref, k_ref, v_ref, g_ref, ext, *, n_lat_tiles, n_tiles, cn):
    i = pl.program_id(1)
    tm = cur_ref.shape[1]
    first = jnp.logical_or(i == 0, i == n_lat_tiles)
    last = jnp.logical_or(i == n_lat_tiles - 1, i == n_tiles - 1)
    pmask = jnp.where(first, 0.0, 1.0)
    nmask = jnp.where(last, 0.0, 1.0)
    pad = GDN_CONV_W // 2
    dh = GDN_HEAD_DIM
    n_qk = 2 * GDN_KEY_DIM
    for cc in range(cur_ref.shape[2] // cn):
        cs = slice(cc * cn, (cc + 1) * cn)
        ext[0:HALO, :] = prev_ref[0, :, cs].astype(F32) * pmask
        ext[HALO:HALO + tm, :] = cur_ref[0, :, cs].astype(F32)
        ext[HALO + tm:2 * HALO + tm, :] = next_ref[0, :, cs].astype(F32) * nmask
        acc = None
        for j in range(GDN_CONV_W):
            t = ext[pl.ds(HALO - pad + j, tm), :] * cw_ref[j:j + 1, cs]
            acc = t if acc is None else acc + t
        y = _silu(acc)
        for g in range(cn // dh):
            col = cc * cn + g * dh
            yg = y[:, g * dh:(g + 1) * dh]
            if col < n_qk:
                yg = yg * lax.rsqrt(jnp.sum(yg * yg, axis=-1, keepdims=True) + NORM_EPS)
                if col < GDN_KEY_DIM:
                    q_ref[0, col // dh] = (yg * (dh ** -0.5)).astype(BF16)
                else:
                    k_ref[0, (col - GDN_KEY_DIM) // dh] = yg.astype(BF16)
            else:
                v_ref[0, (col - n_qk) // dh] = yg.astype(BF16)
    a = gate_ref[0]
    lane = lax.broadcasted_iota(jnp.int32, a.shape, 1)
    gdec = -jnp.exp(alog_ref[...]) * _softplus(a + dtb_ref[...])
    g_ref[0] = jnp.where(lane < 2 * GDN_V_HEADS, gdec, jax.nn.sigmoid(a))


def gdn_prep(p_main, gates_raw, conv_w, alog_row, dtb_row, n_lat_tiles):
    b, s, _ = p_main.shape
    tm = ROW_TILE
    nt = s // tm
    nch = 2 * GDN_KEY_DIM + GDN_VAL_DIM
    hpt = tm // HALO
    nh = s // HALO
    kern = functools.partial(_gdn_prep_kernel, n_lat_tiles=n_lat_tiles, n_tiles=nt, cn=512)
    head_out = lambda nheads: pl.BlockSpec((1, nheads, tm, GDN_HEAD_DIM), lambda bi, i: (bi, 0, i, 0))
    return pl.pallas_call(
        kern,
        out_shape=[jax.ShapeDtypeStruct((b, GDN_K_HEADS, s, GDN_HEAD_DIM), BF16),
                   jax.ShapeDtypeStruct((b, GDN_K_HEADS, s, GDN_HEAD_DIM), BF16),
                   jax.ShapeDtypeStruct((b, GDN_V_HEADS, s, GDN_HEAD_DIM), BF16),
                   jax.ShapeDtypeStruct((b, s, LANES), F32)],
        grid=(b, nt),
        in_specs=[
            pl.BlockSpec((1, tm, nch), lambda bi, i: (bi, i, 0)),
            pl.BlockSpec((1, HALO, nch), lambda bi, i: (bi, jnp.maximum(i * hpt - 1, 0), 0)),
            pl.BlockSpec((1, HALO, nch), lambda bi, i: (bi, jnp.minimum((i + 1) * hpt, nh - 1), 0)),
            pl.BlockSpec((1, tm, LANES), lambda bi, i: (bi, i, 0)),
            pl.BlockSpec((GDN_CONV_W, nch), lambda bi, i: (0, 0)),
            pl.BlockSpec((1, LANES), lambda bi, i: (0, 0)),
            pl.BlockSpec((1, LANES), lambda bi, i: (0, 0)),
        ],
        out_specs=[head_out(GDN_K_HEADS), head_out(GDN_K_HEADS), head_out(GDN_V_HEADS),
                   pl.BlockSpec((1, tm, LANES), lambda bi, i: (bi, i, 0))],
        scratch_shapes=[pltpu.VMEM((tm + 2 * HALO, 512), F32)],
        compiler_params=_cparams(("arbitrary", "arbitrary")),
        name="gdn_prep",
    )(p_main, p_main, p_main, gates_raw, conv_w, alog_row, dtb_row)


def _chunk_masks(n, reverse):
    r = lax.broadcasted_iota(jnp.int32, (n, n), 0)
    c = lax.broadcasted_iota(jnp.int32, (n, n), 1)
    if reverse:
        return r <= c, r < c
    return r >= c, r > c


def _block_cumsum_mats(ct, reverse):
    r = lax.broadcasted_iota(jnp.int32, (ct, ct), 0)
    c = lax.broadcasted_iota(jnp.int32, (ct, ct), 1)
    same = (r // CHUNK) == (c // CHUNK)
    lower = jnp.logical_and(same, c <= r)
    upper = jnp.logical_and(same, c >= r)
    lo = jnp.where(lower, 1.0, 0.0).astype(BF16)
    up = jnp.where(upper, 1.0, 0.0).astype(BF16)
    return (up, lo) if reverse else (lo, up)


def _cumsum_cols(mat, x):
    x0, x1, x2 = _split3(x)
    return _dot(mat, x0) + (_dot(mat, x1) + _dot(mat, x2))


def _cumsum_rows(x, mat):
    x0, x1, x2 = _split3(x)
    return _dot(x0, mat) + (_dot(x1, mat) + _dot(x2, mat))


def _gdn_block(dirs):
    dh = GDN_HEAD_DIM
    eye = jnp.where(lax.broadcasted_iota(jnp.int32, (CHUNK, CHUNK), 0)
                    == lax.broadcasted_iota(jnp.int32, (CHUNK, CHUNK), 1), 1.0, 0.0).astype(F32)
    chains = []
    for q_ref, k_ref, v_ref, gc_ref, gr_ref, o_ref, s_sc, reverse in dirs:
        ct = q_ref.shape[2]
        goff = 2 if reverse else 0
        m_col, m_row = _block_cumsum_mats(ct, reverse)
        tril, strict = _chunk_masks(CHUNK, reverse)
        for kh in range(q_ref.shape[1]):
            gcol = gc_ref[0, kh]
            grow = gr_ref[0, kh]
            gcum_c = _cumsum_cols(m_col, gcol)
            gcum_r = _cumsum_rows(grow, m_row)
            for c in range(ct // CHUNK):
                rs = slice(c * CHUNK, (c + 1) * CHUNK)
                last = c * CHUNK if reverse else (c + 1) * CHUNK - 1
                q = q_ref[0, kh, rs, :]
                k = k_ref[0, kh, rs, :]
                kk = _dot_nt(k, k)
                qk = _dot_nt(q, k)
                for hh in range(2):
                    gi = goff + hh
                    chains.append(dict(
                        q=q, k=k, kk=kk, qk=qk, rs=rs, c=c, hh=2 * kh + hh, reverse=reverse, tril=tril,
                        strict=strict, v_ref=v_ref, o_ref=o_ref, s_sc=s_sc,
                        gc=gcum_c[rs, gi:gi + 1], gr=gcum_r[gi:gi + 1, rs],
                        glast=gcum_c[last:last + 1, gi:gi + 1], beta=gcol[rs, 4 + gi:5 + gi]))
    for ch in chains:
        tril = ch["tril"]
        ch["decay"] = jnp.where(tril, jnp.exp(jnp.where(tril, ch["gc"] - ch["gr"], 0.0)), 0.0)
        a = -jnp.where(ch["strict"], ch["kk"] * ch["beta"] * ch["decay"], 0.0)
        ch["tmat"] = eye + a
        ch["pw"] = a
    for _ in range(5):
        for ch in chains:
            pwb = ch["pw"].astype(BF16)
            ch["pw"] = _dot(pwb, pwb)
        for ch in chains:
            ch["tmat"] = ch["tmat"] + _dot(ch["tmat"].astype(BF16), ch["pw"].astype(BF16))
    for ch in chains:
        beta = ch["beta"]
        eg = jnp.exp(ch["gc"])
        vb = ch["v_ref"][0, ch["hh"], ch["rs"], :].astype(F32) * beta
        kbg = ch["k"].astype(F32) * (beta * eg)
        uw = _dot(ch["tmat"].astype(BF16), jnp.concatenate([vb, kbg], axis=-1).astype(BF16))
        ch["u"] = uw[:, :dh]
        ch["w"] = uw[:, dh:].astype(BF16)
        ch["eg"] = eg
        ch["attn"] = jnp.where(ch["tril"], ch["qk"] * ch["decay"], 0.0).astype(BF16)
    nchunk = max(ch["c"] for ch in chains) + 1
    for step in range(nchunk):
        cur = [ch for ch in chains if ch["c"] == (nchunk - 1 - step if ch["reverse"] else step)]
        for ch in cur:
            state = ch["s_sc"][ch["hh"]]
            sb = state.astype(BF16)
            ch["state"] = state
            ch["ws"] = _dot(ch["w"], sb)
            ch["qs"] = _dot(ch["q"], sb)
        for ch in cur:
            v_new = ch["u"] - ch["ws"]
            ch["kgv"] = (v_new * jnp.exp(ch["glast"] - ch["gc"])).astype(BF16)
            ch["o"] = ch["eg"] * ch["qs"] + _dot(ch["attn"], v_new.astype(BF16))
        for ch in cur:
            ch["s_sc"][ch["hh"]] = ch["state"] * jnp.exp(ch["glast"]) + _dot_tn(ch["k"], ch["kgv"])
            ch["o_ref"][0, ch["rs"], ch["hh"] * dh:(ch["hh"] + 1) * dh] = ch["o"].astype(BF16)


def _gdn_scan_kernel(qf, kf, vf, gcf, grf, qb, kb, vb, gcb, grb, s0f, s0b, *rest, nblk, aliased):
    if aliased:
        rest = rest[2:]
    of_ref, ob_ref, sff, sfb, sf_sc, sb_sc = rest
    j = pl.program_id(2)

    @pl.when(j == 0)
    def _():
        sf_sc[...] = s0f[0]
        sb_sc[...] = s0b[0]

    _gdn_block([(qf, kf, vf, gcf, grf, of_ref, sf_sc, False),
                (qb, kb, vb, gcb, grb, ob_ref, sb_sc, True)])

    @pl.when(j == nblk - 1)
    def _():
        sff[0] = sf_sc[...]
        sfb[0] = sb_sc[...]


def gdn_scan(qn, kn, vv, gcol, grow, s0f, s0b, *, nblk, off, prev=None):
    b, _, s, dh = qn.shape
    ct = SCAN_TILE
    g = GDN_HEADS_PER_STEP
    fwd = lambda j: j + off
    bwd = lambda j: nblk - 1 - j + off
    def specs(pos):
        return [
            pl.BlockSpec((1, g, ct, dh), lambda bi, h, j: (bi, h, pos(j), 0)),
            pl.BlockSpec((1, g, ct, dh), lambda bi, h, j: (bi, h, pos(j), 0)),
            pl.BlockSpec((1, 2 * g, ct, dh), lambda bi, h, j: (bi, h, pos(j), 0)),
            pl.BlockSpec((1, g, ct, 8), lambda bi, h, j: (bi, h, pos(j), 0)),
            pl.BlockSpec((1, g, 8, ct), lambda bi, h, j: (bi, h, 0, pos(j))),
        ]
    st_spec = pl.BlockSpec((1, 2 * g, dh, dh), lambda bi, h, j: (bi, h, 0, 0))
    in_specs = specs(fwd) + specs(bwd) + [st_spec, st_spec]
    args = [qn, kn, vv, gcol, grow] * 2 + [s0f, s0b]
    aliases = {}
    if prev is not None:
        in_specs += [pl.BlockSpec(memory_space=pl.ANY)] * 2
        args += list(prev)
        aliases = {12: 0, 13: 1}
    o_shape = jax.ShapeDtypeStruct((b, s, GDN_VAL_DIM), BF16)
    st_shape = jax.ShapeDtypeStruct((b, GDN_V_HEADS, dh, dh), F32)
    kern = functools.partial(_gdn_scan_kernel, nblk=nblk, aliased=prev is not None)
    return pl.pallas_call(
        kern,
        out_shape=[o_shape, o_shape, st_shape, st_shape],
        grid=(b, GDN_K_HEADS // g, nblk),
        in_specs=in_specs,
        out_specs=[pl.BlockSpec((1, ct, 2 * g * dh), lambda bi, h, j: (bi, fwd(j), h)),
                   pl.BlockSpec((1, ct, 2 * g * dh), lambda bi, h, j: (bi, bwd(j), h)),
                   st_spec, st_spec],
        scratch_shapes=[pltpu.VMEM((2 * g, dh, dh), F32), pltpu.VMEM((2 * g, dh, dh), F32)],
        input_output_aliases=aliases,
        compiler_params=_cparams(("arbitrary", "arbitrary", "arbitrary")),
        name="gdn_scan",
    )(*args)


def _gla_block(dirs, wg_ref, gb_ref):
    chains = []
    pre = []
    for q_ref, k_ref, v_ref, gr_ref, o_ref, st_sc, z in dirs:
        pre.append(_dot3(gr_ref[0], wg_ref[z]) + gb_ref[z:z + 1, :])
    for (q_ref, k_ref, v_ref, gr_ref, o_ref, st_sc, z), logit in zip(dirs, pre):
        reverse = z == 1
        ct = q_ref.shape[1]
        glog = -_softplus(-logit) / GLA_TAU
        m_col, _ = _block_cumsum_mats(ct, reverse)
        bcum = _cumsum_cols(m_col, glog)
        tril, _ = _chunk_masks(CHUNK, reverse)
        for hh in range(st_sc.shape[0]):
            ks = slice(hh * GLA_DK, (hh + 1) * GLA_DK)
            vs = slice(hh * GLA_DV, (hh + 1) * GLA_DV)
            for c in range(ct // CHUNK):
                rs = slice(c * CHUNK, (c + 1) * CHUNK)
                last = c * CHUNK if reverse else (c + 1) * CHUNK - 1
                bc = bcum[rs, ks]
                bl = bcum[last:last + 1, ks]
                qf = q_ref[0, rs, ks].astype(F32) * (GLA_DK ** -0.5)
                kf = k_ref[0, rs, ks].astype(F32)
                chains.append(dict(
                    c=c, rs=rs, hh=hh, vs=vs, reverse=reverse, tril=tril, o_ref=o_ref, st_sc=st_sc,
                    v=v_ref[0, rs, vs],
                    qe=(qf * jnp.exp(bc)).astype(BF16), ke=(kf * jnp.exp(-bc)).astype(BF16),
                    kg=(kf * jnp.exp(bl - bc)).astype(BF16), gl=jnp.exp(bl)))
    for ch in chains:
        ch["attn"] = jnp.where(ch["tril"], _dot_nt(ch["qe"], ch["ke"]), 0.0).astype(BF16)
    for ch in chains:
        ch["o"] = _dot(ch["attn"], ch["v"])
        ch["kv"] = _dot_tn(ch["v"], ch["kg"])
    nchunk = max(ch["c"] for ch in chains) + 1
    for step in range(nchunk):
        cur = [ch for ch in chains if ch["c"] == (nchunk - 1 - step if ch["reverse"] else step)]
        for ch in cur:
            st = ch["st_sc"][ch["hh"]]
            ch["o"] = ch["o"] + _dot_nt(ch["qe"], st.astype(BF16))
            ch["st_sc"][ch["hh"]] = st * ch["gl"] + ch["kv"]
        for ch in cur:
            ch["o_ref"][0, ch["rs"], ch["vs"]] = ch["o"].astype(BF16)


def _gla_scan_kernel(qf, kf, vf, grf, qb, kb, vb, grb, wg, gb, s0f, s0b, *rest, nblk, aliased):
    if aliased:
        rest = rest[2:]
    of_ref, ob_ref, sff, sfb, sf_sc, sb_sc = rest
    j = pl.program_id(2)

    @pl.when(j == 0)
    def _():
        sf_sc[...] = s0f[0]
        sb_sc[...] = s0b[0]

    _gla_block([(qf, kf, vf, grf, of_ref, sf_sc, 0), (qb, kb, vb, grb, ob_ref, sb_sc, 1)], wg, gb)

    @pl.when(j == nblk - 1)
    def _():
        sff[0] = sf_sc[...]
        sfb[0] = sb_sc[...]


def gla_scan(p_main, gr, wg_pad, gate_b, s0f, s0b, *, nblk, off, prev=None):
    b, s, _ = p_main.shape
    ct = SCAN_TILE
    nh, dk, dv = GLA_HEADS, GLA_DK, GLA_DV
    g = GLA_HEADS_PER_STEP
    ng = nh // g
    fwd = lambda j: j + off
    bwd = lambda j: nblk - 1 - j + off
    def specs(pos):
        return [
            pl.BlockSpec((1, ct, g * dk), lambda bi, h, j: (bi, pos(j), h)),
            pl.BlockSpec((1, ct, g * dk), lambda bi, h, j: (bi, pos(j), ng + h)),
            pl.BlockSpec((1, ct, g * dv), lambda bi, h, j: (bi, pos(j), ng + h)),
            pl.BlockSpec((1, ct, LANES), lambda bi, h, j: (bi, pos(j), 0)),
        ]
    st_spec = pl.BlockSpec((1, g, dv, dk), lambda bi, h, j: (bi, h, 0, 0))
    in_specs = specs(fwd) + specs(bwd) + [
        pl.BlockSpec((2, LANES, g * dk), lambda bi, h, j: (0, 0, h)),
        pl.BlockSpec((2, g * dk), lambda bi, h, j: (0, h)),
        st_spec, st_spec]
    args = [p_main, p_main, p_main, gr] * 2 + [wg_pad, gate_b, s0f, s0b]
    aliases = {}
    if prev is not None:
        in_specs += [pl.BlockSpec(memory_space=pl.ANY)] * 2
        args += list(prev)
        aliases = {12: 0, 13: 1}
    o_shape = jax.ShapeDtypeStruct((b, s, GLA_VAL_DIM), BF16)
    st_shape = jax.ShapeDtypeStruct((b, nh, dv, dk), F32)
    kern = functools.partial(_gla_scan_kernel, nblk=nblk, aliased=prev is not None)
    return pl.pallas_call(
        kern,
        out_shape=[o_shape, o_shape, st_shape, st_shape],
        grid=(b, ng, nblk),
        in_specs=in_specs,
        out_specs=[pl.BlockSpec((1, ct, g * dv), lambda bi, h, j: (bi, fwd(j), h)),
                   pl.BlockSpec((1, ct, g * dv), lambda bi, h, j: (bi, bwd(j), h)),
                   st_spec, st_spec],
        scratch_shapes=[pltpu.VMEM((g, dv, dk), F32), pltpu.VMEM((g, dv, dk), F32)],
        input_output_aliases=aliases,
        compiler_params=_cparams(("arbitrary", "arbitrary", "arbitrary")),
        name="gla_scan",
    )(*args)


def _router_kernel(x_ref, sc_ref, sh_ref, rwt_ref, rb_ref, h_ref, wd_ref, *, n_lat_rows):
    tm = x_ref.shape[1]
    h = x_ref[0] * (1.0 + _select_mod(sc_ref, tm, n_lat_rows)) + _select_mod(sh_ref, tm, n_lat_rows)
    h_ref[0] = h.astype(BF16)
    w0, w1 = _split2(rwt_ref[...])
    h0, h1 = _split2(h)
    scores = jax.nn.sigmoid(_dot_nt(w0, h0) + (_dot_nt(w0, h1) + _dot_nt(w1, h0)))
    ne = scores.shape[0]
    row = lax.broadcasted_iota(jnp.int32, scores.shape, 0)
    neg = jnp.float32(-jnp.inf)
    sel = scores + rb_ref[...]
    chosen = jnp.zeros(scores.shape, jnp.bool_)
    for _ in range(TOP_K):
        mx = jnp.max(sel, axis=0, keepdims=True)
        first = jnp.min(jnp.where(sel == mx, row, ne), axis=0, keepdims=True)
        pick = row == first
        chosen = jnp.logical_or(chosen, pick)
        sel = jnp.where(pick, neg, sel)
    picked = jnp.where(chosen, scores, 0.0)
    wt = picked / jnp.sum(picked, axis=0, keepdims=True) * ROUTE_SCALE
    eye = jnp.where(lax.broadcasted_iota(jnp.int32, (ne, LANES), 0)
                    == lax.broadcasted_iota(jnp.int32, (ne, LANES), 1), 1.0, 0.0).astype(BF16)
    t0, t1, t2 = _split3(wt)
    wd_ref[0] = _dot_tn(t0, eye) + (_dot_tn(t1, eye) + _dot_tn(t2, eye))


def moe_route(x, sc, sh, rw_t, rb_col, n_lat_rows, tm=ROW_TILE):
    b, s, d = x.shape
    mod_spec = pl.BlockSpec((1, 2, 1, d), lambda bi, i: (bi, 0, 0, 0))
    return pl.pallas_call(
        functools.partial(_router_kernel, n_lat_rows=n_lat_rows),
        out_shape=[jax.ShapeDtypeStruct((b, s, d), BF16), jax.ShapeDtypeStruct((b, s, LANES), F32)],
        grid=(b, s // tm),
        in_specs=[pl.BlockSpec((1, tm, d), lambda bi, i: (bi, i, 0)), mod_spec, mod_spec,
                  pl.BlockSpec((N_EXPERTS, d), lambda bi, i: (0, 0)),
                  pl.BlockSpec((N_EXPERTS, 1), lambda bi, i: (0, 0))],
        out_specs=[pl.BlockSpec((1, tm, d), lambda bi, i: (bi, i, 0)),
                   pl.BlockSpec((1, tm, LANES), lambda bi, i: (bi, i, 0))],
        compiler_params=_cparams(("arbitrary", "arbitrary")),
        name="moe_router",
    )(x, sc, sh, rw_t, rb_col)


def _moe_kernel(h_ref, wd_ref, wgu_ref, wdn_ref, x_ref, gate_ref, lg_ref, lb_ref, out_ref, acc, *,
                n_exp, n_lat_rows):
    e = pl.program_id(2)

    @pl.when(e == 0)
    def _():
        acc[...] = jnp.zeros(acc.shape, F32)

    gu = _dot(h_ref[0], wgu_ref[0])
    act = _silu(gu[:, :EXPERT_FF]) * gu[:, EXPERT_FF:]
    wd = wd_ref[0]
    lane = lax.broadcasted_iota(jnp.int32, wd.shape, 1)
    wcol = jnp.sum(jnp.where(lane == e, wd, 0.0), axis=-1, keepdims=True)
    acc[...] += wcol * _dot(act.astype(BF16), wdn_ref[0])

    @pl.when(e == n_exp - 1)
    def _():
        tm = acc.shape[0]
        row = pl.program_id(1) * tm + lax.broadcasted_iota(jnp.int32, (tm, 1), 0)
        gate = jnp.where(row >= n_lat_rows, gate_ref[0, 1], gate_ref[0, 0])
        out_ref[0] = _post_norm(x_ref[0], acc[...], gate, lg_ref[...], lb_ref[...])


def moe_experts_post_norm(h, wd, wgu, wdn, x, gate, lg, lb, n_lat_rows, tm):
    b, s, d = x.shape
    n_exp = wgu.shape[0]
    row = lambda width: pl.BlockSpec((1, tm, width), lambda bi, i, e: (bi, i, 0))
    kern = functools.partial(_moe_kernel, n_exp=n_exp, n_lat_rows=n_lat_rows)
    return pl.pallas_call(
        kern,
        out_shape=jax.ShapeDtypeStruct((b, s, d), F32),
        grid=(b, s // tm, n_exp),
        in_specs=[row(d), row(LANES),
                  pl.BlockSpec((1, d, 2 * EXPERT_FF), lambda bi, i, e: (e, 0, 0)),
                  pl.BlockSpec((1, EXPERT_FF, d), lambda bi, i, e: (e, 0, 0)),
                  row(d),
                  pl.BlockSpec((1, 2, 1, d), lambda bi, i, e: (bi, 0, 0, 0)),
                  pl.BlockSpec((1, d), lambda bi, i, e: (0, 0)),
                  pl.BlockSpec((1, d), lambda bi, i, e: (0, 0))],
        out_specs=row(d),
        scratch_shapes=[pltpu.VMEM((tm, d), F32)],
        compiler_params=_cparams(("arbitrary", "arbitrary", "arbitrary")),
        name="moe_experts_post_norm",
    )(h, wd, wgu, wdn, x, gate, lg.reshape(1, d), lb.reshape(1, d))


MOE_SUB = 256
MOE_CAP = 64
MOE_GRP = 4


def _moe_grouped_kernel(order_ref, h_ref, wd_ref, *rest):
    wgu_refs = rest[:MOE_GRP]
    wdn_refs = rest[MOE_GRP:2 * MOE_GRP]
    wsgu_ref, wsdn_ref, f_ref, acc, rrm, wrm, cmax = rest[2 * MOE_GRP:]
    g = pl.program_id(2)
    n_grp = pl.num_programs(2)
    tm = acc.shape[0]
    nsub = tm // MOE_SUB
    sub, cap, nslot = MOE_SUB, MOE_CAP, MOE_GRP * MOE_CAP
    lane = lax.broadcasted_iota(jnp.int32, (1, LANES), 1)

    @pl.when(g == 0)
    def _():
        r = lax.broadcasted_iota(jnp.int32, (sub, sub), 0)
        c = lax.broadcasted_iota(jnp.int32, (sub, sub), 1)
        after = jnp.where(r < c, 1.0, 0.0).astype(BF16)
        ident = jnp.where(r == c, 1.0, 0.0).astype(BF16)
        cm = jnp.zeros((1, LANES), F32)
        for u in range(nsub):
            rs = slice(u * sub, (u + 1) * sub)
            hu = h_ref[0, rs, :]
            gu = _dot(hu, wsgu_ref[...])
            act = _silu(gu[:, :EXPERT_FF]) * gu[:, EXPERT_FF:]
            acc[rs, :] = _dot(act.astype(BF16), wsdn_ref[...])
            wd = wd_ref[0, rs, :]
            active = jnp.logical_and(wd != 0.0, lane < N_EXPERTS)
            a = jnp.where(active, 1.0, 0.0)
            ab = a.astype(BF16)
            rank_r = _dot_tn(ab, after)
            a_r = _dot_tn(ab, ident)
            rrm[u] = jnp.where(a_r > 0.5, rank_r, -1.0)
            wrm[u] = _dot_tn(wd.astype(BF16), ident)
            cm = jnp.maximum(cm, jnp.sum(a, axis=0, keepdims=True))
        cmax[...] = jnp.broadcast_to(cm, cmax.shape)

    experts = [order_ref[g * MOE_GRP + k] for k in range(MOE_GRP)]
    in_group = functools.reduce(jnp.logical_or, [lane == e for e in experts])
    n_max = jnp.max(jnp.where(in_group, cmax[0:1, :], 0.0))
    n_pass = (n_max.astype(jnp.int32) + (cap - 1)) // cap

    def slot_expert(l):
        return sum(((l >= k * cap).astype(jnp.int32) for k in range(1, MOE_GRP)), jnp.zeros_like(l))

    l_col = lax.broadcasted_iota(jnp.int32, (sub, 1), 0)
    j_col = jnp.where(l_col < nslot, l_col - cap * slot_expert(l_col), -1000).astype(F32)

    def per_slot(table, u, fill):
        rows = [jnp.broadcast_to(table[u, pl.ds(experts[k], 1), :], (cap, sub)) for k in range(MOE_GRP)]
        if nslot < sub:
            rows.append(jnp.full((sub - nslot, sub), fill, F32))
        return jnp.concatenate(rows, axis=0)

    def one_pass(p, carry):
        base = (p * cap).astype(F32)
        xg, expanders = [], []
        for u in range(nsub):
            rs = slice(u * sub, (u + 1) * sub)
            match = per_slot(rrm, u, -1.0) == j_col + base
            gather = jnp.where(match, 1.0, 0.0).astype(BF16)
            expanders.append(jnp.where(match, per_slot(wrm, u, 0.0), 0.0).astype(BF16))
            xg.append(_dot(gather, h_ref[0, rs, :]).astype(BF16))
        ys = []
        for k in range(MOE_GRP):
            es = slice(k * cap, (k + 1) * cap)
            x_e = jnp.concatenate([xg[u][es] for u in range(nsub)], axis=0)
            gu = _dot(x_e, wgu_refs[k][0, 0])
            act = _silu(gu[:, :EXPERT_FF]) * gu[:, EXPERT_FF:]
            ys.append(_dot(act.astype(BF16), wdn_refs[k][0, 0]).astype(BF16))
        for u in range(nsub):
            rs = slice(u * sub, (u + 1) * sub)
            parts = [ys[k][u * cap:(u + 1) * cap] for k in range(MOE_GRP)]
            if nslot < sub:
                parts.append(jnp.zeros((sub - nslot, parts[0].shape[1]), BF16))
            y_u = jnp.concatenate(parts, axis=0)
            acc[rs, :] += _dot_tn(expanders[u], y_u)
        return carry

    lax.fori_loop(0, n_pass, one_pass, 0)

    @pl.when(g == n_grp - 1)
    def _():
        f_ref[0] = acc[...]


def moe_grouped_experts(order, h, wd, wgu, wdn, layer, wsgu, wsdn, tm):
    b, s, d = h.shape
    n_grp = wgu.shape[1] // MOE_GRP
    row = lambda width: pl.BlockSpec((1, tm, width), lambda bi, i, g, o: (bi, i, 0))

    def expert(shape, k):
        return pl.BlockSpec((1, 1) + shape, lambda bi, i, g, o: (layer, o[g * MOE_GRP + k], 0, 0))

    in_specs = ([row(d), row(LANES)]
                + [expert((d, 2 * EXPERT_FF), k) for k in range(MOE_GRP)]
                + [expert((EXPERT_FF, d), k) for k in range(MOE_GRP)]
                + [pl.BlockSpec((d, 2 * EXPERT_FF), lambda bi, i, g, o: (0, 0)),
                   pl.BlockSpec((EXPERT_FF, d), lambda bi, i, g, o: (0, 0))])
    grid_spec = pltpu.PrefetchScalarGridSpec(
        num_scalar_prefetch=1, grid=(b, s // tm, n_grp), in_specs=in_specs, out_specs=row(d),
        scratch_shapes=[pltpu.VMEM((tm, d), F32),
                        pltpu.VMEM((tm // MOE_SUB, LANES, MOE_SUB), F32),
                        pltpu.VMEM((tm // MOE_SUB, LANES, MOE_SUB), F32),
                        pltpu.VMEM((8, LANES), F32)])
    return pl.pallas_call(
        _moe_grouped_kernel,
        out_shape=jax.ShapeDtypeStruct((b, s, d), F32),
        grid_spec=grid_spec,
        compiler_params=_cparams(("arbitrary", "arbitrary", "arbitrary")),
        name="moe_grouped_experts",
    )(order, h, wd, *([wgu] * MOE_GRP), *([wdn] * MOE_GRP), wsgu, wsdn)


def _post_norm_kernel(x_ref, f_ref, gate_ref, lg_ref, lb_ref, out_ref, *, n_lat_rows):
    gate = _select_mod(gate_ref, x_ref.shape[1], n_lat_rows)
    out_ref[0] = _post_norm(x_ref[0], f_ref[0].astype(F32), gate, lg_ref[...], lb_ref[...])


def post_norm_rows(x, f, gate, lg, lb, n_lat_rows, rows_out=None):
    b, s, d = x.shape
    rows_out = s if rows_out is None else rows_out
    tm = _pick_tile(rows_out, (1280, 1024, 640, 512, 256))
    row = pl.BlockSpec((1, tm, d), lambda bi, i: (bi, i, 0))
    vec = pl.BlockSpec((1, d), lambda bi, i: (0, 0))
    return pl.pallas_call(
        functools.partial(_post_norm_kernel, n_lat_rows=n_lat_rows),
        out_shape=jax.ShapeDtypeStruct((b, rows_out, d), F32), grid=(b, rows_out // tm),
        in_specs=[row, row, pl.BlockSpec((1, 2, 1, d), lambda bi, i: (bi, 0, 0, 0)), vec, vec],
        out_specs=row, compiler_params=_cparams(("arbitrary", "arbitrary")), name="post_norm_rows",
    )(x, f, gate, lg.reshape(1, d), lb.reshape(1, d))


def _rope_tables(n_lat, n_ctx):
    rows = n_lat // GRID_W
    rowp = jnp.repeat(jnp.arange(rows), GRID_W).astype(F32)
    colp = jnp.tile(jnp.arange(GRID_W), rows).astype(F32)
    n_freq = DA_HEAD_DIM // 4
    inv = 1.0 / (ROPE_BASE ** (jnp.arange(n_freq, dtype=F32) / n_freq))
    ang = jnp.concatenate([rowp[:, None] * inv, colp[:, None] * inv], -1)
    cos, sin = jnp.cos(ang), jnp.sin(ang)
    cos_t = jnp.tile(cos, (1, 4))
    sin_t = jnp.tile(jnp.concatenate([-sin, sin], -1), (1, 2))
    cos_t = jnp.concatenate([cos_t, jnp.ones((n_ctx, LANES), F32)], 0)
    sin_t = jnp.concatenate([sin_t, jnp.zeros((n_ctx, LANES), F32)], 0)
    return cos_t, sin_t


def _pick_tile(total, cands):
    for t in cands:
        if total % t == 0:
            return t
    raise ValueError(f"no tile for {total}")


def _pad_cols(w, n):
    return jnp.pad(w, ((0, 0), (0, n - w.shape[1])))


def _flash_both(p_all, lam_vec, subln, lam_init, n_lat, n_ctx):
    s = n_lat + n_ctx
    tq = _pick_tile(n_lat, (1024, 512, 256))
    tk = _pick_tile(s, (3328, 1280, 1024, 512, 256))
    o = diff_flash_attention(p_all, lam_vec, subln, lam_init, tq=tq, tk=tk, nq=n_lat // tq, nk=s // tk,
                             q_off=0, k_off=0)
    return diff_flash_attention(p_all, lam_vec, subln, lam_init, tq=n_ctx, tk=n_ctx, nq=1, nk=1,
                                q_off=n_lat // n_ctx, k_off=n_lat // n_ctx, prev=o)


def kernel(x, c, ctx, c_ctx, ada_w, ada_b, ln_g, ln_b, da_w_in, da_w_o, da_lambda, da_subln, gdn_w_in, gdn_conv, gdn_a_log, gdn_dt_bias, gdn_norm, gdn_w_o, gla_w_in, gla_w_gate, gla_gate_b, gla_norm, gla_w_o, moe_router, moe_router_b, moe_w_gu, moe_w_dn, moe_ws_gu, moe_ws_dn):
    b, n, d = x.shape
    lc = ctx.shape[1]
    assert lc == ROW_TILE and n % SCAN_TILE == 0 and d == D_MODEL
    s = n + lc
    n_lat_tiles = n // ROW_TILE
    depth = ada_w.shape[0]

    xall = jnp.concatenate([x, ctx], axis=1)
    c8 = jnp.concatenate([c, c_ctx[None], jnp.zeros((8 - b - 1, d), F32)], 0)
    mods = ada_modulation(c8, ada_w, ada_b)
    rope = _rope_tables(n, lc)
    moe_tm = _pick_tile(s, (1280, 1024, 512, 256))
    moe_w_gu_bf = moe_w_gu.astype(BF16)
    moe_w_dn_bf = moe_w_dn.astype(BF16)
    big_tm = _pick_tile(s, (1280, 1024, 512, 256))
    mid_tm = _pick_tile(s, (640, 512, 256))

    for i in range(depth):
        kind, j = i % N_MIXERS, i // N_MIXERS
        m = mods[i].reshape(8, ADA_CHUNKS, d)
        mod = jnp.stack([m[:b], jnp.broadcast_to(m[b], (b, ADA_CHUNKS, d))], axis=1)
        mod = [mod[:, :, k][:, :, None, :] for k in range(ADA_CHUNKS)]

        if kind == 0:
            lam_init = 0.8 - 0.6 * math.exp(-0.3 * i)
            w = da_w_in[j]
            w = jnp.concatenate([w[:, :d] * (DA_HEAD_DIM ** -0.5 * math.log2(math.e)), w[:, d:]], 1).astype(BF16)
            p_all = modulated_projection(xall, mod[1], mod[0], w, rope=rope, n_rope=2 * d,
                                         n_lat_rows=n, head_major=True, tm=big_tm)
            o = _flash_both(p_all, da_lambda[j], da_subln[j], lam_init, n, lc)
            xall = out_projection_post_norm([o], da_w_o[j].astype(BF16), xall, mod[2],
                                            ln_g[i, 0], ln_b[i, 0], n, tm=big_tm)
        elif kind == 1:
            w = gdn_w_in[j]
            n_main = 2 * GDN_KEY_DIM + 2 * GDN_VAL_DIM
            p_main, gates_raw = modulated_projection(
                xall, mod[1], mod[0], w[:, :n_main].astype(BF16),
                w_small=_pad_cols(w[:, n_main:], LANES).astype(BF16), n_lat_rows=n, tm=mid_tm)
            alog_row = _pad_cols(gdn_a_log[j].reshape(1, -1), LANES)
            dtb_row = _pad_cols(gdn_dt_bias[j].reshape(1, -1), LANES)
            qn, kn, vv, gates = gdn_prep(p_main, gates_raw, gdn_conv[j], alog_row, dtb_row, n_lat_tiles)
            hv, hk = GDN_V_HEADS, GDN_K_HEADS
            def per_khead(t):
                t = t.reshape(b, s, 2, hk, 2)
                return jnp.transpose(t, (0, 3, 1, 2, 4)).reshape(b, hk, s, 4)
            gcol = jnp.concatenate([per_khead(gates[..., :2 * hv]), per_khead(gates[..., 2 * hv:4 * hv])], -1)
            grow = jnp.swapaxes(gcol, 2, 3)
            zeros = jnp.zeros((b, hv, GDN_HEAD_DIM, GDN_HEAD_DIM), F32)
            of, ob, scf, scb = gdn_scan(qn, kn, vv, gcol, grow, zeros, zeros, nblk=lc // SCAN_TILE,
                                        off=n // SCAN_TILE)
            of, ob, _, _ = gdn_scan(qn, kn, vv, gcol, grow, scf, scb, nblk=n // SCAN_TILE, off=0,
                                    prev=(of, ob))
            xall = out_projection_post_norm([of, ob, p_main, gdn_norm[j]], gdn_w_o[j].astype(BF16), xall,
                                            mod[2], ln_g[i, 0], ln_b[i, 0], n,
                                            gated=(GDN_V_HEADS, GDN_HEAD_DIM, 2), tm=mid_tm)
        else:
            w = gla_w_in[j]
            n_main = 2 * GLA_KEY_DIM + 2 * GLA_VAL_DIM
            p_main, gr = modulated_projection(
                xall, mod[1], mod[0], w[:, :n_main].astype(BF16),
                w_small=_pad_cols(w[:, n_main:], LANES).astype(BF16), n_lat_rows=n, tm=big_tm)
            wg = gla_w_gate[j]
            wg_pad = jnp.zeros((2, LANES, GLA_KEY_DIM), F32)
            wg_pad = wg_pad.at[0, :GLA_GATE_RANK].set(wg[0]).at[1, GLA_GATE_RANK:2 * GLA_GATE_RANK].set(wg[1])
            zeros = jnp.zeros((b, GLA_HEADS, GLA_DV, GLA_DK), F32)
            of, ob, scf, scb = gla_scan(p_main, gr, wg_pad, gla_gate_b[j], zeros, zeros,
                                        nblk=lc // SCAN_TILE, off=n // SCAN_TILE)
            of, ob, _, _ = gla_scan(p_main, gr, wg_pad, gla_gate_b[j], scf, scb, nblk=n // SCAN_TILE, off=0,
                                    prev=(of, ob))
            xall = out_projection_post_norm([of, ob, p_main, gla_norm[j]], gla_w_o[j].astype(BF16), xall,
                                            mod[2], ln_g[i, 0], ln_b[i, 0], n,
                                            gated=(GLA_HEADS, GLA_DV, 2), tm=mid_tm)

        h, wd = moe_route(xall, mod[4], mod[3], moe_router[i].T, moe_router_b[i].reshape(-1, 1), n, tm=big_tm)
        popularity = jnp.sum((wd[..., :N_EXPERTS] != 0.0).astype(jnp.int32), axis=(0, 1))
        order = jnp.argsort(popularity).astype(jnp.int32)
        f = moe_grouped_experts(order, h, wd, moe_w_gu_bf, moe_w_dn_bf, i,
                                moe_ws_gu[i].astype(BF16), moe_ws_dn[i].astype(BF16), moe_tm)
        xall = post_norm_rows(xall, f, mod[5], ln_g[i, 1], ln_b[i, 1], n,
                              rows_out=n if i == depth - 1 else None)

    return xall
```

```python
import functools
import math

import jax
import jax.numpy as jnp
from jax import lax
from jax.experimental import pallas as pl
from jax.experimental.pallas import tpu as pltpu

F32 = jnp.float32
BF16 = jnp.bfloat16

D_MODEL = 1024
DEPTH = 4
GRID_W = 64
N_MIXERS = 3
DN_ALPHA = (2 * DEPTH) ** 0.25
LN_EPS = 1e-5
NORM_EPS = 1e-6
ADA_CHUNKS = 6

DA_HEADS = 8
DA_HEAD_DIM = 64
ROPE_BASE = 10000.0

GDN_K_HEADS = 8
GDN_V_HEADS = 16
GDN_HEAD_DIM = 128
GDN_KEY_DIM = 1024
GDN_VAL_DIM = 2048
GDN_CONV_W = 5
CHUNK = 64

GLA_HEADS = 4
GLA_KEY_DIM = 512
GLA_VAL_DIM = 1024
GLA_DK = 128
GLA_DV = 256
GLA_GATE_RANK = 16
GLA_TAU = 16.0

N_EXPERTS = 64
TOP_K = 8
EXPERT_FF = 256
ROUTE_SCALE = 2.5

LANES = 128
ROW_TILE = 256
SCAN_TILE = 256
GDN_HEADS_PER_STEP = 4
GLA_HEADS_PER_STEP = 4
HALO = 16
VMEM_LIMIT = 56 * 1024 * 1024


def _cparams(sem, flags=None):
    return pltpu.CompilerParams(dimension_semantics=sem, vmem_limit_bytes=VMEM_LIMIT, flags=flags)


def _dot(a, b):
    return jnp.dot(a, b, preferred_element_type=F32)


def _dot_nt(a, b):
    return lax.dot_general(a, b, (((1,), (1,)), ((), ())), preferred_element_type=F32)


def _dot_tn(a, b):
    return lax.dot_general(a, b, (((0,), (0,)), ((), ())), preferred_element_type=F32)


def _split2(a):
    hi = a.astype(BF16)
    lo = (a - hi.astype(F32)).astype(BF16)
    return hi, lo


def _split3(a):
    a0 = a.astype(BF16)
    r = a - a0.astype(F32)
    a1 = r.astype(BF16)
    a2 = (r - a1.astype(F32)).astype(BF16)
    return a0, a1, a2


def _dot3(a, b):
    a0, a1 = _split2(a)
    b0, b1 = _split2(b)
    return _dot(a0, b0) + (_dot(a0, b1) + _dot(a1, b0))


def _select_mod(ref, tile_rows, n_lat_rows):
    row = pl.program_id(1) * tile_rows + lax.broadcasted_iota(jnp.int32, (tile_rows, 1), 0)
    return jnp.where(row >= n_lat_rows, ref[0, 1], ref[0, 0])


def _silu(x):
    return x * jax.nn.sigmoid(x)


def _softplus(x):
    return jnp.maximum(x, 0.0) + jnp.log(1.0 + jnp.exp(-jnp.abs(x)))


def _ada_kernel(c_ref, w_ref, b_ref, o_ref):
    s = _silu(c_ref[...])
    o_ref[0] = _dot3(s, w_ref[0]) + b_ref[0]


def ada_modulation(c8, ada_w, ada_b):
    depth, d, n = ada_w.shape
    tn = 1536
    return pl.pallas_call(
        _ada_kernel,
        out_shape=jax.ShapeDtypeStruct((depth, 8, n), F32),
        grid=(depth, n // tn),
        in_specs=[
            pl.BlockSpec((8, d), lambda i, j: (0, 0)),
            pl.BlockSpec((1, d, tn), lambda i, j: (i, 0, j)),
            pl.BlockSpec((1, 1, tn), lambda i, j: (i, 0, j)),
        ],
        out_specs=pl.BlockSpec((1, 8, tn), lambda i, j: (i, 0, j)),
        compiler_params=_cparams(("arbitrary", "arbitrary")),
        name="ada_modulation",
    )(c8, ada_w, ada_b.reshape(depth, 1, n))


def _proj_kernel(*refs, n_main, n_rope, has_small, cn, head_major, n_lat_rows):
    it = iter(refs)
    x_ref, sc_ref, sh_ref, w_ref = next(it), next(it), next(it), next(it)
    ws_ref = next(it) if has_small else None
    cos_ref = next(it) if n_rope else None
    sin_ref = next(it) if n_rope else None
    o_ref = next(it)
    os_ref = next(it) if has_small else None

    tm = x_ref.shape[1]
    h = x_ref[0] * (1.0 + _select_mod(sc_ref, tm, n_lat_rows)) + _select_mod(sh_ref, tm, n_lat_rows)
    hb = h.astype(BF16)
    if n_rope:
        cos = cos_ref[...]
        sin = sin_ref[...]
        lane = lax.broadcasted_iota(jnp.int32, (tm, LANES), 1)
        low_half = (lane & 32) == 0
    for j in range(n_main // cn):
        p = _dot(hb, w_ref[:, j * cn:(j + 1) * cn])
        for g in range(cn // LANES):
            pg = p[:, g * LANES:(g + 1) * LANES]
            col = j * cn + g * LANES
            if col < n_rope:
                partner = jnp.where(low_half, pltpu.roll(pg, LANES - 32, 1), pltpu.roll(pg, 32, 1))
                pg = pg * cos + partner * sin
            if head_major:
                o_ref[0, col // LANES] = pg.astype(BF16)
            else:
                o_ref[0, :, col:col + LANES] = pg.astype(BF16)
    if has_small:
        os_ref[0] = _dot(hb, ws_ref[...])


def modulated_projection(x, sc, sh, w, w_small=None, rope=None, n_rope=0, n_lat_rows=0, head_major=False, tm=ROW_TILE):
    b, s, d = x.shape
    n_main = w.shape[1]
    grid = (b, s // tm)
    mod_spec = pl.BlockSpec((1, 2, 1, d), lambda bi, i: (bi, 0, 0, 0))
    in_specs = [pl.BlockSpec((1, tm, d), lambda bi, i: (bi, i, 0)), mod_spec, mod_spec,
                pl.BlockSpec((d, n_main), lambda bi, i: (0, 0))]
    args = [x, sc, sh, w]
    if head_major:
        out_shape = [jax.ShapeDtypeStruct((b, n_main // LANES, s, LANES), BF16)]
        out_specs = [pl.BlockSpec((1, n_main // LANES, tm, LANES), lambda bi, i: (bi, 0, i, 0))]
    else:
        out_shape = [jax.ShapeDtypeStruct((b, s, n_main), BF16)]
        out_specs = [pl.BlockSpec((1, tm, n_main), lambda bi, i: (bi, i, 0))]
    if w_small is not None:
        in_specs.append(pl.BlockSpec((d, LANES), lambda bi, i: (0, 0)))
        args.append(w_small)
        out_shape.append(jax.ShapeDtypeStruct((b, s, LANES), F32))
        out_specs.append(pl.BlockSpec((1, tm, LANES), lambda bi, i: (bi, i, 0)))
    if n_rope:
        tab = pl.BlockSpec((tm, LANES), lambda bi, i: (i, 0))
        in_specs += [tab, tab]
        args += [rope[0], rope[1]]
    kern = functools.partial(_proj_kernel, n_main=n_main, n_rope=n_rope,
                             has_small=w_small is not None, cn=512, head_major=head_major,
                             n_lat_rows=n_lat_rows)
    out = pl.pallas_call(
        kern, out_shape=out_shape, grid=grid, in_specs=in_specs, out_specs=out_specs,
        compiler_params=_cparams(("arbitrary", "arbitrary")), name="modulated_projection",
    )(*args)
    return out if w_small is not None else out[0]


NEG_INIT = -1e30
FLASH_ROW_BLOCK = 512
FLASH_LOOKAHEAD = 1


def _flash_kernel(lam_ref, q_ref, k_ref, v_ref, sub_ref, *rest, lam_init, nk, aliased):
    if aliased:
        rest = rest[1:]
    o_ref, m_sc, l_sc, acc_sc, s_ring = rest
    ki = pl.program_id(3)
    tq = m_sc.shape[1]
    tk = k_ref.shape[2]
    nring = s_ring.shape[0]

    @pl.when(ki == 0)
    def _():
        m_sc[...] = jnp.full(m_sc.shape, NEG_INIT, F32)
        l_sc[...] = jnp.zeros(l_sc.shape, F32)
        acc_sc[...] = jnp.zeros(acc_sc.shape, F32)

    q = q_ref[0, 0]
    k = k_ref[0, 0]
    v = v_ref[0, 0]
    hd = DA_HEAD_DIM
    rb = min(tq, FLASH_ROW_BLOCK)
    kc = 2 * LANES
    blocks = [(c, r0) for c in range(2) for r0 in range(0, tq, rb)]
    for i in range(len(blocks) + FLASH_LOOKAHEAD):
        if i < len(blocks):
            c, r0 = blocks[i]
            s_ring[i % nring] = _dot_nt(q[r0:r0 + rb, c * hd:(c + 1) * hd], k[:, c * hd:(c + 1) * hd])
        j = i - FLASH_LOOKAHEAD
        if j < 0:
            continue
        c, r0 = blocks[j]
        rows = slice(r0, r0 + rb)
        s_blk = s_ring.at[j % nring]
        m_prev = m_sc[c, rows, :]
        m_new = jnp.maximum(m_prev, jnp.max(s_blk[...], axis=-1, keepdims=True))
        alpha = jnp.exp2(m_prev - m_new)
        m2 = jnp.concatenate([m_new, m_new], axis=-1)
        lsum = None
        pv = None
        for t in range(tk // kc):
            pj = jnp.exp2(s_blk[:, t * kc:(t + 1) * kc] - m2)
            lj = pj[:, :LANES] + pj[:, LANES:]
            lsum = lj if lsum is None else lsum + lj
            d = _dot(pj.astype(BF16), v[t * kc:(t + 1) * kc, :])
            pv = d if pv is None else pv + d
        l_sc[c, rows, :] = alpha * l_sc[c, rows, :] + lsum
        acc_sc[c, rows, :] = alpha * acc_sc[c, rows, :] + pv
        m_sc[c, rows, :] = m_new

    @pl.when(ki == nk - 1)
    def _():
        lv = lam_ref[...]
        lam = (jnp.exp(jnp.sum(lv[0:1] * lv[1:2], axis=-1, keepdims=True))
               - jnp.exp(jnp.sum(lv[2:3] * lv[3:4], axis=-1, keepdims=True)) + lam_init)
        l0 = jnp.sum(l_sc[0], axis=-1, keepdims=True)
        l1 = jnp.sum(l_sc[1], axis=-1, keepdims=True)
        o = acc_sc[0] / l0 - lam * (acc_sc[1] / l1)
        ms = jnp.mean(o * o, axis=-1, keepdims=True)
        o = o * lax.rsqrt(ms + NORM_EPS) * sub_ref[...] * (1.0 - lam_init)
        o_ref[0] = o.astype(BF16)


def diff_flash_attention(p_all, lam_vec, subln, lam_init, *, tq, tk, nq, nk, q_off, k_off, prev=None):
    b, _, s, _ = p_all.shape
    hh = DA_HEADS
    in_specs = [
        pl.BlockSpec((4, DA_HEAD_DIM), lambda bi, h, qi, ki: (0, 0)),
        pl.BlockSpec((1, 1, tq, LANES), lambda bi, h, qi, ki: (bi, h, qi + q_off, 0)),
        pl.BlockSpec((1, 1, tk, LANES), lambda bi, h, qi, ki: (bi, hh + h, ki + k_off, 0)),
        pl.BlockSpec((1, 1, tk, LANES), lambda bi, h, qi, ki: (bi, 2 * hh + h, ki + k_off, 0)),
        pl.BlockSpec((1, LANES), lambda bi, h, qi, ki: (0, 0)),
    ]
    args = [lam_vec, p_all, p_all, p_all, subln.reshape(1, LANES)]
    aliases = {}
    if prev is not None:
        in_specs.append(pl.BlockSpec(memory_space=pl.ANY))
        args.append(prev)
        aliases = {5: 0}
    kern = functools.partial(_flash_kernel, lam_init=lam_init, nk=nk, aliased=prev is not None)
    return pl.pallas_call(
        kern,
        out_shape=jax.ShapeDtypeStruct((b, s, hh * LANES), BF16),
        grid=(b, hh, nq, nk),
        in_specs=in_specs,
        out_specs=pl.BlockSpec((1, tq, LANES), lambda bi, h, qi, ki: (bi, qi + q_off, h)),
        scratch_shapes=[pltpu.VMEM((2, tq, LANES), F32), pltpu.VMEM((2, tq, LANES), F32),
                        pltpu.VMEM((2, tq, LANES), F32),
                        pltpu.VMEM((FLASH_LOOKAHEAD + 1, min(tq, FLASH_ROW_BLOCK), tk), F32)],
        input_output_aliases=aliases,
        compiler_params=_cparams(("arbitrary", "arbitrary", "arbitrary", "arbitrary")),
        name="diff_flash_attention",
    )(*args)


def _post_norm(x, y, gate, lg, lb):
    r = DN_ALPHA * x + gate * y
    mu = jnp.mean(r, axis=-1, keepdims=True)
    rc = r - mu
    var = jnp.mean(rc * rc, axis=-1, keepdims=True)
    return rc * lax.rsqrt(var + LN_EPS) * lg + lb


def _outproj_kernel(*refs, gated, n_heads, dh, n_lat_rows):
    if gated:
        of_ref, ob_ref, z_ref, ng_ref, w_ref, x_ref, gate_ref, lg_ref, lb_ref, out_ref = refs
        y = None
        for h in range(n_heads):
            sl = slice(h * dh, (h + 1) * dh)
            o = of_ref[0, :, sl].astype(F32) + ob_ref[0, :, sl].astype(F32)
            ms = jnp.mean(o * o, axis=-1, keepdims=True)
            o = o * lax.rsqrt(ms + NORM_EPS) * ng_ref[...] * _silu(z_ref[0, :, sl].astype(F32))
            t = _dot(o.astype(BF16), w_ref[sl, :])
            y = t if y is None else y + t
    else:
        o_ref, w_ref, x_ref, gate_ref, lg_ref, lb_ref, out_ref = refs
        y = _dot(o_ref[0], w_ref[...])
    gate = _select_mod(gate_ref, x_ref.shape[1], n_lat_rows)
    out_ref[0] = _post_norm(x_ref[0], y, gate, lg_ref[...], lb_ref[...])


def out_projection_post_norm(o_args, w_o, x, gate, lg, lb, n_lat_rows, gated=None, tm=ROW_TILE):
    b, s, d = x.shape
    kdim = w_o.shape[0]
    row = lambda width, cb=0: pl.BlockSpec((1, tm, width), lambda bi, i: (bi, i, cb))
    full = lambda shape: pl.BlockSpec(shape, lambda bi, i: (0,) * len(shape))
    if gated is None:
        in_specs = [row(kdim)]
        args = list(o_args)
        kern = functools.partial(_outproj_kernel, gated=False, n_heads=0, dh=0, n_lat_rows=n_lat_rows)
    else:
        n_heads, dh, z_cb = gated
        of, ob, z, ng = o_args
        in_specs = [row(kdim), row(kdim), row(kdim, z_cb), full((1, dh))]
        args = [of, ob, z, ng.reshape(1, dh)]
        kern = functools.partial(_outproj_kernel, gated=True, n_heads=n_heads, dh=dh, n_lat_rows=n_lat_rows)
    in_specs += [full((kdim, d)), row(d),
                 pl.BlockSpec((1, 2, 1, d), lambda bi, i: (bi, 0, 0, 0)),
                 full((1, d)), full((1, d))]
    args += [w_o, x, gate, lg.reshape(1, d), lb.reshape(1, d)]
    return pl.pallas_call(
        kern, out_shape=jax.ShapeDtypeStruct((b, s, d), F32), grid=(b, s // tm),
        in_specs=in_specs, out_specs=row(d),
        compiler_params=_cparams(("arbitrary", "arbitrary")), name="out_projection_post_norm",
    )(*args)


def _gdn_prep_kernel(cur_ref, prev_ref, next_ref, gate_ref, cw_ref, alog_ref, dtb_ref,
                     q_ref, k_ref, v_ref, g_ref, ext, *, n_lat_tiles, n_tiles, cn):
    i = pl.program_id(1)
    tm = cur_ref.shape[1]
    first = jnp.logical_or(i == 0, i == n_lat_tiles)
    last = jnp.logical_or(i == n_lat_tiles - 1, i == n_tiles - 1)
    pmask = jnp.where(first, 0.0, 1.0)
    nmask = jnp.where(last, 0.0, 1.0)
    pad = GDN_CONV_W // 2
    dh = GDN_HEAD_DIM
    n_qk = 2 * GDN_KEY_DIM
    for cc in range(cur_ref.shape[2] // cn):
        cs = slice(cc * cn, (cc + 1) * cn)
        ext[0:HALO, :] = prev_ref[0, :, cs].astype(F32) * pmask
        ext[HALO:HALO + tm, :] = cur_ref[0, :, cs].astype(F32)
        ext[HALO + tm:2 * HALO + tm, :] = next_ref[0, :, cs].astype(F32) * nmask
        acc = None
        for j in range(GDN_CONV_W):
            t = ext[pl.ds(HALO - pad + j, tm), :] * cw_ref[j:j + 1, cs]
            acc = t if acc is None else acc + t
        y = _silu(acc)
        for g in range(cn // dh):
            col = cc * cn + g * dh
            yg = y[:, g * dh:(g + 1) * dh]
            if col < n_qk:
                yg = yg * lax.rsqrt(jnp.sum(yg * yg, axis=-1, keepdims=True) + NORM_EPS)
                if col < GDN_KEY_DIM:
                    q_ref[0, col // dh] = (yg * (dh ** -0.5)).astype(BF16)
                else:
                    k_ref[0, (col - GDN_KEY_DIM) // dh] = yg.astype(BF16)
            else:
                v_ref[0, (col - n_qk) // dh] = yg.astype(BF16)
    a = gate_ref[0]
    lane = lax.broadcasted_iota(jnp.int32, a.shape, 1)
    gdec = -jnp.exp(alog_ref[...]) * _softplus(a + dtb_ref[...])
    g_ref[0] = jnp.where(lane < 2 * GDN_V_HEADS, gdec, jax.nn.sigmoid(a))


def gdn_prep(p_main, gates_raw, conv_w, alog_row, dtb_row, n_lat_tiles):
    b, s, _ = p_main.shape
    tm = ROW_TILE
    nt = s // tm
    nch = 2 * GDN_KEY_DIM + GDN_VAL_DIM
    hpt = tm // HALO
    nh = s // HALO
    kern = functools.partial(_gdn_prep_kernel, n_lat_tiles=n_lat_tiles, n_tiles=nt, cn=512)
    head_out = lambda nheads: pl.BlockSpec((1, nheads, tm, GDN_HEAD_DIM), lambda bi, i: (bi, 0, i, 0))
    return pl.pallas_call(
        kern,
        out_shape=[jax.ShapeDtypeStruct((b, GDN_K_HEADS, s, GDN_HEAD_DIM), BF16),
                   jax.ShapeDtypeStruct((b, GDN_K_HEADS, s, GDN_HEAD_DIM), BF16),
                   jax.ShapeDtypeStruct((b, GDN_V_HEADS, s, GDN_HEAD_DIM), BF16),
                   jax.ShapeDtypeStruct((b, s, LANES), F32)],
        grid=(b, nt),
        in_specs=[
            pl.BlockSpec((1, tm, nch), lambda bi, i: (bi, i, 0)),
            pl.BlockSpec((1, HALO, nch), lambda bi, i: (bi, jnp.maximum(i * hpt - 1, 0), 0)),
            pl.BlockSpec((1, HALO, nch), lambda bi, i: (bi, jnp.minimum((i + 1) * hpt, nh - 1), 0)),
            pl.BlockSpec((1, tm, LANES), lambda bi, i: (bi, i, 0)),
            pl.BlockSpec((GDN_CONV_W, nch), lambda bi, i: (0, 0)),
            pl.BlockSpec((1, LANES), lambda bi, i: (0, 0)),
            pl.BlockSpec((1, LANES), lambda bi, i: (0, 0)),
        ],
        out_specs=[head_out(GDN_K_HEADS), head_out(GDN_K_HEADS), head_out(GDN_V_HEADS),
                   pl.BlockSpec((1, tm, LANES), lambda bi, i: (bi, i, 0))],
        scratch_shapes=[pltpu.VMEM((tm + 2 * HALO, 512), F32)],
        compiler_params=_cparams(("arbitrary", "arbitrary")),
        name="gdn_prep",
    )(p_main, p_main, p_main, gates_raw, conv_w, alog_row, dtb_row)


def _chunk_masks(n, reverse):
    r = lax.broadcasted_iota(jnp.int32, (n, n), 0)
    c = lax.broadcasted_iota(jnp.int32, (n, n), 1)
    if reverse:
        return r <= c, r < c
    return r >= c, r > c


def _block_cumsum_mats(ct, reverse):
    r = lax.broadcasted_iota(jnp.int32, (ct, ct), 0)
    c = lax.broadcasted_iota(jnp.int32, (ct, ct), 1)
    same = (r // CHUNK) == (c // CHUNK)
    lower = jnp.logical_and(same, c <= r)
    upper = jnp.logical_and(same, c >= r)
    lo = jnp.where(lower, 1.0, 0.0).astype(BF16)
    up = jnp.where(upper, 1.0, 0.0).astype(BF16)
    return (up, lo) if reverse else (lo, up)


def _cumsum_cols(mat, x):
    x0, x1, x2 = _split3(x)
    return _dot(mat, x0) + (_dot(mat, x1) + _dot(mat, x2))


def _cumsum_rows(x, mat):
    x0, x1, x2 = _split3(x)
    return _dot(x0, mat) + (_dot(x1, mat) + _dot(x2, mat))


def _gdn_block(dirs):
    dh = GDN_HEAD_DIM
    eye = jnp.where(lax.broadcasted_iota(jnp.int32, (CHUNK, CHUNK), 0)
                    == lax.broadcasted_iota(jnp.int32, (CHUNK, CHUNK), 1), 1.0, 0.0).astype(F32)
    chains = []
    for q_ref, k_ref, v_ref, gc_ref, gr_ref, o_ref, s_sc, reverse in dirs:
        ct = q_ref.shape[2]
        goff = 2 if reverse else 0
        m_col, m_row = _block_cumsum_mats(ct, reverse)
        tril, strict = _chunk_masks(CHUNK, reverse)
        for kh in range(q_ref.shape[1]):
            gcol = gc_ref[0, kh]
            grow = gr_ref[0, kh]
            gcum_c = _cumsum_cols(m_col, gcol)
            gcum_r = _cumsum_rows(grow, m_row)
            for c in range(ct // CHUNK):
                rs = slice(c * CHUNK, (c + 1) * CHUNK)
                last = c * CHUNK if reverse else (c + 1) * CHUNK - 1
                q = q_ref[0, kh, rs, :]
                k = k_ref[0, kh, rs, :]
                kk = _dot_nt(k, k)
                qk = _dot_nt(q, k)
                for hh in range(2):
                    gi = goff + hh
                    chains.append(dict(
                        q=q, k=k, kk=kk, qk=qk, rs=rs, c=c, hh=2 * kh + hh, reverse=reverse, tril=tril,
                        strict=strict, v_ref=v_ref, o_ref=o_ref, s_sc=s_sc,
                        gc=gcum_c[rs, gi:gi + 1], gr=gcum_r[gi:gi + 1, rs],
                        glast=gcum_c[last:last + 1, gi:gi + 1], beta=gcol[rs, 4 + gi:5 + gi]))
    for ch in chains:
        tril = ch["tril"]
        ch["decay"] = jnp.where(tril, jnp.exp(jnp.where(tril, ch["gc"] - ch["gr"], 0.0)), 0.0)
        a = -jnp.where(ch["strict"], ch["kk"] * ch["beta"] * ch["decay"], 0.0)
        ch["tmat"] = eye + a
        ch["pw"] = a
    for _ in range(5):
        for ch in chains:
            pwb = ch["pw"].astype(BF16)
            ch["pw"] = _dot(pwb, pwb)
        for ch in chains:
            ch["tmat"] = ch["tmat"] + _dot(ch["tmat"].astype(BF16), ch["pw"].astype(BF16))
    for ch in chains:
        beta = ch["beta"]
        eg = jnp.exp(ch["gc"])
        vb = ch["v_ref"][0, ch["hh"], ch["rs"], :].astype(F32) * beta
        kbg = ch["k"].astype(F32) * (beta * eg)
        uw = _dot(ch["tmat"].astype(BF16), jnp.concatenate([vb, kbg], axis=-1).astype(BF16))
        ch["u"] = uw[:, :dh]
        ch["w"] = uw[:, dh:].astype(BF16)
        ch["eg"] = eg
        ch["attn"] = jnp.where(ch["tril"], ch["qk"] * ch["decay"], 0.0).astype(BF16)
    nchunk = max(ch["c"] for ch in chains) + 1
    for step in range(nchunk):
        cur = [ch for ch in chains if ch["c"] == (nchunk - 1 - step if ch["reverse"] else step)]
        for ch in cur:
            state = ch["s_sc"][ch["hh"]]
            sb = state.astype(BF16)
            ch["state"] = state
            ch["ws"] = _dot(ch["w"], sb)
            ch["qs"] = _dot(ch["q"], sb)
        for ch in cur:
            v_new = ch["u"] - ch["ws"]
            ch["kgv"] = (v_new * jnp.exp(ch["glast"] - ch["gc"])).astype(BF16)
            ch["o"] = ch["eg"] * ch["qs"] + _dot(ch["attn"], v_new.astype(BF16))
        for ch in cur:
            ch["s_sc"][ch["hh"]] = ch["state"] * jnp.exp(ch["glast"]) + _dot_tn(ch["k"], ch["kgv"])
            ch["o_ref"][0, ch["rs"], ch["hh"] * dh:(ch["hh"] + 1) * dh] = ch["o"].astype(BF16)


def _gdn_scan_kernel(qf, kf, vf, gcf, grf, qb, kb, vb, gcb, grb, s0f, s0b, *rest, nblk, aliased):
    if aliased:
        rest = rest[2:]
    of_ref, ob_ref, sff, sfb, sf_sc, sb_sc = rest
    j = pl.program_id(2)

    @pl.when(j == 0)
    def _():
        sf_sc[...] = s0f[0]
        sb_sc[...] = s0b[0]

    _gdn_block([(qf, kf, vf, gcf, grf, of_ref, sf_sc, False),
                (qb, kb, vb, gcb, grb, ob_ref, sb_sc, True)])

    @pl.when(j == nblk - 1)
    def _():
        sff[0] = sf_sc[...]
        sfb[0] = sb_sc[...]


def gdn_scan(qn, kn, vv, gcol, grow, s0f, s0b, *, nblk, off, prev=None):
    b, _, s, dh = qn.shape
    ct = SCAN_TILE
    g = GDN_HEADS_PER_STEP
    fwd = lambda j: j + off
    bwd = lambda j: nblk - 1 - j + off
    def specs(pos):
        return [
            pl.BlockSpec((1, g, ct, dh), lambda bi, h, j: (bi, h, pos(j), 0)),
            pl.BlockSpec((1, g, ct, dh), lambda bi, h, j: (bi, h, pos(j), 0)),
            pl.BlockSpec((1, 2 * g, ct, dh), lambda bi, h, j: (bi, h, pos(j), 0)),
            pl.BlockSpec((1, g, ct, 8), lambda bi, h, j: (bi, h, pos(j), 0)),
            pl.BlockSpec((1, g, 8, ct), lambda bi, h, j: (bi, h, 0, pos(j))),
        ]
    st_spec = pl.BlockSpec((1, 2 * g, dh, dh), lambda bi, h, j: (bi, h, 0, 0))
    in_specs = specs(fwd) + specs(bwd) + [st_spec, st_spec]
    args = [qn, kn, vv, gcol, grow] * 2 + [s0f, s0b]
    aliases = {}
    if prev is not None:
        in_specs += [pl.BlockSpec(memory_space=pl.ANY)] * 2
        args += list(prev)
        aliases = {12: 0, 13: 1}
    o_shape = jax.ShapeDtypeStruct((b, s, GDN_VAL_DIM), BF16)
    st_shape = jax.ShapeDtypeStruct((b, GDN_V_HEADS, dh, dh), F32)
    kern = functools.partial(_gdn_scan_kernel, nblk=nblk, aliased=prev is not None)
    return pl.pallas_call(
        kern,
        out_shape=[o_shape, o_shape, st_shape, st_shape],
        grid=(b, GDN_K_HEADS // g, nblk),
        in_specs=in_specs,
        out_specs=[pl.BlockSpec((1, ct, 2 * g * dh), lambda bi, h, j: (bi, fwd(j), h)),
                   pl.BlockSpec((1, ct, 2 * g * dh), lambda bi, h, j: (bi, bwd(j), h)),
                   st_spec, st_spec],
        scratch_shapes=[pltpu.VMEM((2 * g, dh, dh), F32), pltpu.VMEM((2 * g, dh, dh), F32)],
        input_output_aliases=aliases,
        compiler_params=_cparams(("arbitrary", "arbitrary", "arbitrary")),
        name="gdn_scan",
    )(*args)


def _gla_block(dirs, wg_ref, gb_ref):
    chains = []
    pre = []
    for q_ref, k_ref, v_ref, gr_ref, o_ref, st_sc, z in dirs:
        pre.append(_dot3(gr_ref[0], wg_ref[z]) + gb_ref[z:z + 1, :])
    for (q_ref, k_ref, v_ref, gr_ref, o_ref, st_sc, z), logit in zip(dirs, pre):
        reverse = z == 1
        ct = q_ref.shape[1]
        glog = -_softplus(-logit) / GLA_TAU
        m_col, _ = _block_cumsum_mats(ct, reverse)
        bcum = _cumsum_cols(m_col, glog)
        tril, _ = _chunk_masks(CHUNK, reverse)
        for hh in range(st_sc.shape[0]):
            ks = slice(hh * GLA_DK, (hh + 1) * GLA_DK)
            vs = slice(hh * GLA_DV, (hh + 1) * GLA_DV)
            for c in range(ct // CHUNK):
                rs = slice(c * CHUNK, (c + 1) * CHUNK)
                last = c * CHUNK if reverse else (c + 1) * CHUNK - 1
                bc = bcum[rs, ks]
                bl = bcum[last:last + 1, ks]
                qf = q_ref[0, rs, ks].astype(F32) * (GLA_DK ** -0.5)
                kf = k_ref[0, rs, ks].astype(F32)
                chains.append(dict(
                    c=c, rs=rs, hh=hh, vs=vs, reverse=reverse, tril=tril, o_ref=o_ref, st_sc=st_sc,
                    v=v_ref[0, rs, vs],
                    qe=(qf * jnp.exp(bc)).astype(BF16), ke=(kf * jnp.exp(-bc)).astype(BF16),
                    kg=(kf * jnp.exp(bl - bc)).astype(BF16), gl=jnp.exp(bl)))
    for ch in chains:
        ch["attn"] = jnp.where(ch["tril"], _dot_nt(ch["qe"], ch["ke"]), 0.0).astype(BF16)
    for ch in chains:
        ch["o"] = _dot(ch["attn"], ch["v"])
        ch["kv"] = _dot_tn(ch["v"], ch["kg"])
    nchunk = max(ch["c"] for ch in chains) + 1
    for step in range(nchunk):
        cur = [ch for ch in chains if ch["c"] == (nchunk - 1 - step if ch["reverse"] else step)]
        for ch in cur:
            st = ch["st_sc"][ch["hh"]]
            ch["o"] = ch["o"] + _dot_nt(ch["qe"], st.astype(BF16))
            ch["st_sc"][ch["hh"]] = st * ch["gl"] + ch["kv"]
        for ch in cur:
            ch["o_ref"][0, ch["rs"], ch["vs"]] = ch["o"].astype(BF16)


def _gla_scan_kernel(qf, kf, vf, grf, qb, kb, vb, grb, wg, gb, s0f, s0b, *rest, nblk, aliased):
    if aliased:
        rest = rest[2:]
    of_ref, ob_ref, sff, sfb, sf_sc, sb_sc = rest
    j = pl.program_id(2)

    @pl.when(j == 0)
    def _():
        sf_sc[...] = s0f[0]
        sb_sc[...] = s0b[0]

    _gla_block([(qf, kf, vf, grf, of_ref, sf_sc, 0), (qb, kb, vb, grb, ob_ref, sb_sc, 1)], wg, gb)

    @pl.when(j == nblk - 1)
    def _():
        sff[0] = sf_sc[...]
        sfb[0] = sb_sc[...]


def gla_scan(p_main, gr, wg_pad, gate_b, s0f, s0b, *, nblk, off, prev=None):
    b, s, _ = p_main.shape
    ct = SCAN_TILE
    nh, dk, dv = GLA_HEADS, GLA_DK, GLA_DV
    g = GLA_HEADS_PER_STEP
    ng = nh // g
    fwd = lambda j: j + off
    bwd = lambda j: nblk - 1 - j + off
    def specs(pos):
        return [
            pl.BlockSpec((1, ct, g * dk), lambda bi, h, j: (bi, pos(j), h)),
            pl.BlockSpec((1, ct, g * dk), lambda bi, h, j: (bi, pos(j), ng + h)),
            pl.BlockSpec((1, ct, g * dv), lambda bi, h, j: (bi, pos(j), ng + h)),
            pl.BlockSpec((1, ct, LANES), lambda bi, h, j: (bi, pos(j), 0)),
        ]
    st_spec = pl.BlockSpec((1, g, dv, dk), lambda bi, h, j: (bi, h, 0, 0))
    in_specs = specs(fwd) + specs(bwd) + [
        pl.BlockSpec((2, LANES, g * dk), lambda bi, h, j: (0, 0, h)),
        pl.BlockSpec((2, g * dk), lambda bi, h, j: (0, h)),
        st_spec, st_spec]
    args = [p_main, p_main, p_main, gr] * 2 + [wg_pad, gate_b, s0f, s0b]
    aliases = {}
    if prev is not None:
        in_specs += [pl.BlockSpec(memory_space=pl.ANY)] * 2
        args += list(prev)
        aliases = {12: 0, 13: 1}
    o_shape = jax.ShapeDtypeStruct((b, s, GLA_VAL_DIM), BF16)
    st_shape = jax.ShapeDtypeStruct((b, nh, dv, dk), F32)
    kern = functools.partial(_gla_scan_kernel, nblk=nblk, aliased=prev is not None)
    return pl.pallas_call(
        kern,
        out_shape=[o_shape, o_shape, st_shape, st_shape],
        grid=(b, ng, nblk),
        in_specs=in_specs,
        out_specs=[pl.BlockSpec((1, ct, g * dv), lambda bi, h, j: (bi, fwd(j), h)),
                   pl.BlockSpec((1, ct, g * dv), lambda bi, h, j: (bi, bwd(j), h)),
                   st_spec, st_spec],
        scratch_shapes=[pltpu.VMEM((g, dv, dk), F32), pltpu.VMEM((g, dv, dk), F32)],
        input_output_aliases=aliases,
        compiler_params=_cparams(("arbitrary", "arbitrary", "arbitrary")),
        name="gla_scan",
    )(*args)


def _router_kernel(x_ref, sc_ref, sh_ref, rwt_ref, rb_ref, h_ref, wd_ref, *, n_lat_rows):
    tm = x_ref.shape[1]
    h = x_ref[0] * (1.0 + _select_mod(sc_ref, tm, n_lat_rows)) + _select_mod(sh_ref, tm, n_lat_rows)
    h_ref[0] = h.astype(BF16)
    w0, w1 = _split2(rwt_ref[...])
    h0, h1 = _split2(h)
    scores = jax.nn.sigmoid(_dot_nt(w0, h0) + (_dot_nt(w0, h1) + _dot_nt(w1, h0)))
    ne = scores.shape[0]
    row = lax.broadcasted_iota(jnp.int32, scores.shape, 0)
    neg = jnp.float32(-jnp.inf)
    sel = scores + rb_ref[...]
    chosen = jnp.zeros(scores.shape, jnp.bool_)
    for _ in range(TOP_K):
        mx = jnp.max(sel, axis=0, keepdims=True)
        first = jnp.min(jnp.where(sel == mx, row, ne), axis=0, keepdims=True)
        pick = row == first
        chosen = jnp.logical_or(chosen, pick)
        sel = jnp.where(pick, neg, sel)
    picked = jnp.where(chosen, scores, 0.0)
    wt = picked / jnp.sum(picked, axis=0, keepdims=True) * ROUTE_SCALE
    eye = jnp.where(lax.broadcasted_iota(jnp.int32, (ne, LANES), 0)
                    == lax.broadcasted_iota(jnp.int32, (ne, LANES), 1), 1.0, 0.0).astype(BF16)
    t0, t1, t2 = _split3(wt)
    wd_ref[0] = _dot_tn(t0, eye) + (_dot_tn(t1, eye) + _dot_tn(t2, eye))


def moe_route(x, sc, sh, rw_t, rb_col, n_lat_rows, tm=ROW_TILE):
    b, s, d = x.shape
    mod_spec = pl.BlockSpec((1, 2, 1, d), lambda bi, i: (bi, 0, 0, 0))
    return pl.pallas_call(
        functools.partial(_router_kernel, n_lat_rows=n_lat_rows),
        out_shape=[jax.ShapeDtypeStruct((b, s, d), BF16), jax.ShapeDtypeStruct((b, s, LANES), F32)],
        grid=(b, s // tm),
        in_specs=[pl.BlockSpec((1, tm, d), lambda bi, i: (bi, i, 0)), mod_spec, mod_spec,
                  pl.BlockSpec((N_EXPERTS, d), lambda bi, i: (0, 0)),
                  pl.BlockSpec((N_EXPERTS, 1), lambda bi, i: (0, 0))],
        out_specs=[pl.BlockSpec((1, tm, d), lambda bi, i: (bi, i, 0)),
                   pl.BlockSpec((1, tm, LANES), lambda bi, i: (bi, i, 0))],
        compiler_params=_cparams(("arbitrary", "arbitrary")),
        name="moe_router",
    )(x, sc, sh, rw_t, rb_col)


def _moe_kernel(h_ref, wd_ref, wgu_ref, wdn_ref, x_ref, gate_ref, lg_ref, lb_ref, out_ref, acc, *,
                n_exp, n_lat_rows):
    e = pl.program_id(2)

    @pl.when(e == 0)
    def _():
        acc[...] = jnp.zeros(acc.shape, F32)

    gu = _dot(h_ref[0], wgu_ref[0])
    act = _silu(gu[:, :EXPERT_FF]) * gu[:, EXPERT_FF:]
    wd = wd_ref[0]
    lane = lax.broadcasted_iota(jnp.int32, wd.shape, 1)
    wcol = jnp.sum(jnp.where(lane == e, wd, 0.0), axis=-1, keepdims=True)
    acc[...] += wcol * _dot(act.astype(BF16), wdn_ref[0])

    @pl.when(e == n_exp - 1)
    def _():
        tm = acc.shape[0]
        row = pl.program_id(1) * tm + lax.broadcasted_iota(jnp.int32, (tm, 1), 0)
        gate = jnp.where(row >= n_lat_rows, gate_ref[0, 1], gate_ref[0, 0])
        out_ref[0] = _post_norm(x_ref[0], acc[...], gate, lg_ref[...], lb_ref[...])


def moe_experts_post_norm(h, wd, wgu, wdn, x, gate, lg, lb, n_lat_rows, tm):
    b, s, d = x.shape
    n_exp = wgu.shape[0]
    row = lambda width: pl.BlockSpec((1, tm, width), lambda bi, i, e: (bi, i, 0))
    kern = functools.partial(_moe_kernel, n_exp=n_exp, n_lat_rows=n_lat_rows)
    return pl.pallas_call(
        kern,
        out_shape=jax.ShapeDtypeStruct((b, s, d), F32),
        grid=(b, s // tm, n_exp),
        in_specs=[row(d), row(LANES),
                  pl.BlockSpec((1, d, 2 * EXPERT_FF), lambda bi, i, e: (e, 0, 0)),
                  pl.BlockSpec((1, EXPERT_FF, d), lambda bi, i, e: (e, 0, 0)),
                  row(d),
                  pl.BlockSpec((1, 2, 1, d), lambda bi, i, e: (bi, 0, 0, 0)),
                  pl.BlockSpec((1, d), lambda bi, i, e: (0, 0)),
                  pl.BlockSpec((1, d), lambda bi, i, e: (0, 0))],
        out_specs=row(d),
        scratch_shapes=[pltpu.VMEM((tm, d), F32)],
        compiler_params=_cparams(("arbitrary", "arbitrary", "arbitrary")),
        name="moe_experts_post_norm",
    )(h, wd, wgu, wdn, x, gate, lg.reshape(1, d), lb.reshape(1, d))


MOE_SUB = 256
MOE_CAP = 64
MOE_GRP = 4


def _moe_grouped_kernel(order_ref, h_ref, wd_ref, *rest):
    wgu_refs = rest[:MOE_GRP]
    wdn_refs = rest[MOE_GRP:2 * MOE_GRP]
    wsgu_ref, wsdn_ref, f_ref, acc, rrm, wrm, cmax = rest[2 * MOE_GRP:]
    g = pl.program_id(2)
    n_grp = pl.num_programs(2)
    tm = acc.shape[0]
    nsub = tm // MOE_SUB
    sub, cap, nslot = MOE_SUB, MOE_CAP, MOE_GRP * MOE_CAP
    lane = lax.broadcasted_iota(jnp.int32, (1, LANES), 1)

    @pl.when(g == 0)
    def _():
        r = lax.broadcasted_iota(jnp.int32, (sub, sub), 0)
        c = lax.broadcasted_iota(jnp.int32, (sub, sub), 1)
        after = jnp.where(r < c, 1.0, 0.0).astype(BF16)
        ident = jnp.where(r == c, 1.0, 0.0).astype(BF16)
        cm = jnp.zeros((1, LANES), F32)
        for u in range(nsub):
            rs = slice(u * sub, (u + 1) * sub)
            hu = h_ref[0, rs, :]
            gu = _dot(hu, wsgu_ref[...])
            act = _silu(gu[:, :EXPERT_FF]) * gu[:, EXPERT_FF:]
            acc[rs, :] = _dot(act.astype(BF16), wsdn_ref[...])
            wd = wd_ref[0, rs, :]
            active = jnp.logical_and(wd != 0.0, lane < N_EXPERTS)
            a = jnp.where(active, 1.0, 0.0)
            ab = a.astype(BF16)
            rank_r = _dot_tn(ab, after)
            a_r = _dot_tn(ab, ident)
            rrm[u] = jnp.where(a_r > 0.5, rank_r, -1.0)
            wrm[u] = _dot_tn(wd.astype(BF16), ident)
            cm = jnp.maximum(cm, jnp.sum(a, axis=0, keepdims=True))
        cmax[...] = jnp.broadcast_to(cm, cmax.shape)

    experts = [order_ref[g * MOE_GRP + k] for k in range(MOE_GRP)]
    in_group = functools.reduce(jnp.logical_or, [lane == e for e in experts])
    n_max = jnp.max(jnp.where(in_group, cmax[0:1, :], 0.0))
    n_pass = (n_max.astype(jnp.int32) + (cap - 1)) // cap

    def slot_expert(l):
        return sum(((l >= k * cap).astype(jnp.int32) for k in range(1, MOE_GRP)), jnp.zeros_like(l))

    l_col = lax.broadcasted_iota(jnp.int32, (sub, 1), 0)
    j_col = jnp.where(l_col < nslot, l_col - cap * slot_expert(l_col), -1000).astype(F32)

    def per_slot(table, u, fill):
        rows = [jnp.broadcast_to(table[u, pl.ds(experts[k], 1), :], (cap, sub)) for k in range(MOE_GRP)]
        if nslot < sub:
            rows.append(jnp.full((sub - nslot, sub), fill, F32))
        return jnp.concatenate(rows, axis=0)

    def one_pass(p, carry):
        base = (p * cap).astype(F32)
        xg, expanders = [], []
        for u in range(nsub):
            rs = slice(u * sub, (u + 1) * sub)
            match = per_slot(rrm, u, -1.0) == j_col + base
            gather = jnp.where(match, 1.0, 0.0).astype(BF16)
            expanders.append(jnp.where(match, per_slot(wrm, u, 0.0), 0.0).astype(BF16))
            xg.append(_dot(gather, h_ref[0, rs, :]).astype(BF16))
        ys = []
        for k in range(MOE_GRP):
            es = slice(k * cap, (k + 1) * cap)
            x_e = jnp.concatenate([xg[u][es] for u in range(nsub)], axis=0)
            gu = _dot(x_e, wgu_refs[k][0, 0])
            act = _silu(gu[:, :EXPERT_FF]) * gu[:, EXPERT_FF:]
            ys.append(_dot(act.astype(BF16), wdn_refs[k][0, 0]).astype(BF16))
        for u in range(nsub):
            rs = slice(u * sub, (u + 1) * sub)
            parts = [ys[k][u * cap:(u + 1) * cap] for k in range(MOE_GRP)]
            if nslot < sub:
                parts.append(jnp.zeros((sub - nslot, parts[0].shape[1]), BF16))
            y_u = jnp.concatenate(parts, axis=0)
            acc[rs, :] += _dot_tn(expanders[u], y_u)
        return carry

    lax.fori_loop(0, n_pass, one_pass, 0)

    @pl.when(g == n_grp - 1)
    def _():
        f_ref[0] = acc[...]


def moe_grouped_experts(order, h, wd, wgu, wdn, layer, wsgu, wsdn, tm):
    b, s, d = h.shape
    n_grp = wgu.shape[1] // MOE_GRP
    row = lambda width: pl.BlockSpec((1, tm, width), lambda bi, i, g, o: (bi, i, 0))

    def expert(shape, k):
        return pl.BlockSpec((1, 1) + shape, lambda bi, i, g, o: (layer, o[g * MOE_GRP + k], 0, 0))

    in_specs = ([row(d), row(LANES)]
                + [expert((d, 2 * EXPERT_FF), k) for k in range(MOE_GRP)]
                + [expert((EXPERT_FF, d), k) for k in range(MOE_GRP)]
                + [pl.BlockSpec((d, 2 * EXPERT_FF), lambda bi, i, g, o: (0, 0)),
                   pl.BlockSpec((EXPERT_FF, d), lambda bi, i, g, o: (0, 0))])
    grid_spec = pltpu.PrefetchScalarGridSpec(
        num_scalar_prefetch=1, grid=(b, s // tm, n_grp), in_specs=in_specs, out_specs=row(d),
        scratch_shapes=[pltpu.VMEM((tm, d), F32),
                        pltpu.VMEM((tm // MOE_SUB, LANES, MOE_SUB), F32),
                        pltpu.VMEM((tm // MOE_SUB, LANES, MOE_SUB), F32),
                        pltpu.VMEM((8, LANES), F32)])
    return pl.pallas_call(
        _moe_grouped_kernel,
        out_shape=jax.ShapeDtypeStruct((b, s, d), F32),
        grid_spec=grid_spec,
        compiler_params=_cparams(("arbitrary", "arbitrary", "arbitrary")),
        name="moe_grouped_experts",
    )(order, h, wd, *([wgu] * MOE_GRP), *([wdn] * MOE_GRP), wsgu, wsdn)


def _post_norm_kernel(x_ref, f_ref, gate_ref, lg_ref, lb_ref, out_ref, *, n_lat_rows):
    gate = _select_mod(gate_ref, x_ref.shape[1], n_lat_rows)
    out_ref[0] = _post_norm(x_ref[0], f_ref[0].astype(F32), gate, lg_ref[...], lb_ref[...])


def post_norm_rows(x, f, gate, lg, lb, n_lat_rows, rows_out=None):
    b, s, d = x.shape
    rows_out = s if rows_out is None else rows_out
    tm = _pick_tile(rows_out, (1280, 1024, 640, 512, 256))
    row = pl.BlockSpec((1, tm, d), lambda bi, i: (bi, i, 0))
    vec = pl.BlockSpec((1, d), lambda bi, i: (0, 0))
    return pl.pallas_call(
        functools.partial(_post_norm_kernel, n_lat_rows=n_lat_rows),
        out_shape=jax.ShapeDtypeStruct((b, rows_out, d), F32), grid=(b, rows_out // tm),
        in_specs=[row, row, pl.BlockSpec((1, 2, 1, d), lambda bi, i: (bi, 0, 0, 0)), vec, vec],
        out_specs=row, compiler_params=_cparams(("arbitrary", "arbitrary")), name="post_norm_rows",
    )(x, f, gate, lg.reshape(1, d), lb.reshape(1, d))


def _rope_tables(n_lat, n_ctx):
    rows = n_lat // GRID_W
    rowp = jnp.repeat(jnp.arange(rows), GRID_W).astype(F32)
    colp = jnp.tile(jnp.arange(GRID_W), rows).astype(F32)
    n_freq = DA_HEAD_DIM // 4
    inv = 1.0 / (ROPE_BASE ** (jnp.arange(n_freq, dtype=F32) / n_freq))
    ang = jnp.concatenate([rowp[:, None] * inv, colp[:, None] * inv], -1)
    cos, sin = jnp.cos(ang), jnp.sin(ang)
    cos_t = jnp.tile(cos, (1, 4))
    sin_t = jnp.tile(jnp.concatenate([-sin, sin], -1), (1, 2))
    cos_t = jnp.concatenate([cos_t, jnp.ones((n_ctx, LANES), F32)], 0)
    sin_t = jnp.concatenate([sin_t, jnp.zeros((n_ctx, LANES), F32)], 0)
    return cos_t, sin_t


def _pick_tile(total, cands):
    for t in cands:
        if total % t == 0:
            return t
    raise ValueError(f"no tile for {total}")


def _pad_cols(w, n):
    return jnp.pad(w, ((0, 0), (0, n - w.shape[1])))


def _flash_both(p_all, lam_vec, subln, lam_init, n_lat, n_ctx):
    s = n_lat + n_ctx
    tq = _pick_tile(n_lat, (2048, 1024, 512, 256))
    tk = _pick_tile(s, (3328, 1280, 1024, 512, 256))
    o = diff_flash_attention(p_all, lam_vec, subln, lam_init, tq=tq, tk=tk, nq=n_lat // tq, nk=s // tk,
                             q_off=0, k_off=0)
    return diff_flash_attention(p_all, lam_vec, subln, lam_init, tq=n_ctx, tk=n_ctx, nq=1, nk=1,
                                q_off=n_lat // n_ctx, k_off=n_lat // n_ctx, prev=o)


def kernel(x, c, ctx, c_ctx, ada_w, ada_b, ln_g, ln_b, da_w_in, da_w_o, da_lambda, da_subln, gdn_w_in, gdn_conv, gdn_a_log, gdn_dt_bias, gdn_norm, gdn_w_o, gla_w_in, gla_w_gate, gla_gate_b, gla_norm, gla_w_o, moe_router, moe_router_b, moe_w_gu, moe_w_dn, moe_ws_gu, moe_ws_dn):
    b, n, d = x.shape
    lc = ctx.shape[1]
    assert lc == ROW_TILE and n % SCAN_TILE == 0 and d == D_MODEL
    s = n + lc
    n_lat_tiles = n // ROW_TILE
    depth = ada_w.shape[0]

    xall = jnp.concatenate([x, ctx], axis=1)
    c8 = jnp.concatenate([c, c_ctx[None], jnp.zeros((8 - b - 1, d), F32)], 0)
    mods = ada_modulation(c8, ada_w, ada_b)
    rope = _rope_tables(n, lc)
    moe_tm = _pick_tile(s, (1280, 1024, 512, 256))
    moe_w_gu_bf = moe_w_gu.astype(BF16)
    moe_w_dn_bf = moe_w_dn.astype(BF16)
    big_tm = _pick_tile(s, (1280, 1024, 512, 256))
    mid_tm = _pick_tile(s, (640, 512, 256))

    for i in range(depth):
        kind, j = i % N_MIXERS, i // N_MIXERS
        m = mods[i].reshape(8, ADA_CHUNKS, d)
        mod = jnp.stack([m[:b], jnp.broadcast_to(m[b], (b, ADA_CHUNKS, d))], axis=1)
        mod = [mod[:, :, k][:, :, None, :] for k in range(ADA_CHUNKS)]

        if kind == 0:
            lam_init = 0.8 - 0.6 * math.exp(-0.3 * i)
            w = da_w_in[j]
            w = jnp.concatenate([w[:, :d] * (DA_HEAD_DIM ** -0.5 * math.log2(math.e)), w[:, d:]], 1).astype(BF16)
            p_all = modulated_projection(xall, mod[1], mod[0], w, rope=rope, n_rope=2 * d,
                                         n_lat_rows=n, head_major=True, tm=big_tm)
            o = _flash_both(p_all, da_lambda[j], da_subln[j], lam_init, n, lc)
            xall = out_projection_post_norm([o], da_w_o[j].astype(BF16), xall, mod[2],
                                            ln_g[i, 0], ln_b[i, 0], n, tm=big_tm)
        elif kind == 1:
            w = gdn_w_in[j]
            n_main = 2 * GDN_KEY_DIM + 2 * GDN_VAL_DIM
            p_main, gates_raw = modulated_projection(
                xall, mod[1], mod[0], w[:, :n_main].astype(BF16),
                w_small=_pad_cols(w[:, n_main:], LANES).astype(BF16), n_lat_rows=n, tm=mid_tm)
            alog_row = _pad_cols(gdn_a_log[j].reshape(1, -1), LANES)
            dtb_row = _pad_cols(gdn_dt_bias[j].reshape(1, -1), LANES)
            qn, kn, vv, gates = gdn_prep(p_main, gates_raw, gdn_conv[j], alog_row, dtb_row, n_lat_tiles)
            hv, hk = GDN_V_HEADS, GDN_K_HEADS
            def per_khead(t):
                t = t.reshape(b, s, 2, hk, 2)
                return jnp.transpose(t, (0, 3, 1, 2, 4)).reshape(b, hk, s, 4)
            gcol = jnp.concatenate([per_khead(gates[..., :2 * hv]), per_khead(gates[..., 2 * hv:4 * hv])], -1)
            grow = jnp.swapaxes(gcol, 2, 3)
            zeros = jnp.zeros((b, hv, GDN_HEAD_DIM, GDN_HEAD_DIM), F32)
            of, ob, scf, scb = gdn_scan(qn, kn, vv, gcol, grow, zeros, zeros, nblk=lc // SCAN_TILE,
                                        off=n // SCAN_TILE)
            of, ob, _, _ = gdn_scan(qn, kn, vv, gcol, grow, scf, scb, nblk=n // SCAN_TILE, off=0,
                                    prev=(of, ob))
            xall = out_projection_post_norm([of, ob, p_main, gdn_norm[j]], gdn_w_o[j].astype(BF16), xall,
                                            mod[2], ln_g[i, 0], ln_b[i, 0], n,
                                            gated=(GDN_V_HEADS, GDN_HEAD_DIM, 2), tm=mid_tm)
        else:
            w = gla_w_in[j]
            n_main = 2 * GLA_KEY_DIM + 2 * GLA_VAL_DIM
            p_main, gr = modulated_projection(
                xall, mod[1], mod[0], w[:, :n_main].astype(BF16),
                w_small=_pad_cols(w[:, n_main:], LANES).astype(BF16), n_lat_rows=n, tm=big_tm)
            wg = gla_w_gate[j]
            wg_pad = jnp.zeros((2, LANES, GLA_KEY_DIM), F32)
            wg_pad = wg_pad.at[0, :GLA_GATE_RANK].set(wg[0]).at[1, GLA_GATE_RANK:2 * GLA_GATE_RANK].set(wg[1])
            zeros = jnp.zeros((b, GLA_HEADS, GLA_DV, GLA_DK), F32)
            of, ob, scf, scb = gla_scan(p_main, gr, wg_pad, gla_gate_b[j], zeros, zeros,
                                        nblk=lc // SCAN_TILE, off=n // SCAN_TILE)
            of, ob, _, _ = gla_scan(p_main, gr, wg_pad, gla_gate_b[j], scf, scb, nblk=n // SCAN_TILE, off=0,
                                    prev=(of, ob))
            xall = out_projection_post_norm([of, ob, p_main, gla_norm[j]], gla_w_o[j].astype(BF16), xall,
                                            mod[2], ln_g[i, 0], ln_b[i, 0], n,
                                            gated=(GLA_HEADS, GLA_DV, 2), tm=mid_tm)

        h, wd = moe_route(xall, mod[4], mod[3], moe_router[i].T, moe_router_b[i].reshape(-1, 1), n, tm=big_tm)
        popularity = jnp.sum((wd[..., :N_EXPERTS] != 0.0).astype(jnp.int32), axis=(0, 1))
        order = jnp.argsort(popularity).astype(jnp.int32)
        f = moe_grouped_experts(order, h, wd, moe_w_gu_bf, moe_w_dn_bf, i,
                                moe_ws_gu[i].astype(BF16), moe_ws_dn[i].astype(BF16), moe_tm)
        xall = post_norm_rows(xall, f, mod[5], ln_g[i, 1], ln_b[i, 1], n,
                              rows_out=n if i == depth - 1 else None)

    return xall
```

```python
import functools
import math

import jax
import jax.numpy as jnp
from jax import lax
from jax.experimental import pallas as pl
from jax.experimental.pallas import tpu as pltpu

F32 = jnp.float32
BF16 = jnp.bfloat16

D_MODEL = 1024
DEPTH = 4
GRID_W = 64
N_MIXERS = 3
DN_ALPHA = (2 * DEPTH) ** 0.25
LN_EPS = 1e-5
NORM_EPS = 1e-6
ADA_CHUNKS = 6

DA_HEADS = 8
DA_HEAD_DIM = 64
ROPE_BASE = 10000.0

GDN_K_HEADS = 8
GDN_V_HEADS = 16
GDN_HEAD_DIM = 128
GDN_KEY_DIM = 1024
GDN_VAL_DIM = 2048
GDN_CONV_W = 5
CHUNK = 64

GLA_HEADS = 4
GLA_KEY_DIM = 512
GLA_VAL_DIM = 1024
GLA_DK = 128
GLA_DV = 256
GLA_GATE_RANK = 16
GLA_TAU = 16.0

N_EXPERTS = 64
TOP_K = 8
EXPERT_FF = 256
ROUTE_SCALE = 2.5

LANES = 128
ROW_TILE = 256
SCAN_TILE = 256
GDN_HEADS_PER_STEP = 4
GLA_HEADS_PER_STEP = 4
HALO = 16
VMEM_LIMIT = 56 * 1024 * 1024


def _cparams(sem, flags=None):
    return pltpu.CompilerParams(dimension_semantics=sem, vmem_limit_bytes=VMEM_LIMIT, flags=flags)


def _dot(a, b):
    return jnp.dot(a, b, preferred_element_type=F32)


def _dot_nt(a, b):
    return lax.dot_general(a, b, (((1,), (1,)), ((), ())), preferred_element_type=F32)


def _dot_tn(a, b):
    return lax.dot_general(a, b, (((0,), (0,)), ((), ())), preferred_element_type=F32)


def _split2(a):
    hi = a.astype(BF16)
    lo = (a - hi.astype(F32)).astype(BF16)
    return hi, lo


def _split3(a):
    a0 = a.astype(BF16)
    r = a - a0.astype(F32)
    a1 = r.astype(BF16)
    a2 = (r - a1.astype(F32)).astype(BF16)
    return a0, a1, a2


def _dot3(a, b):
    a0, a1 = _split2(a)
    b0, b1 = _split2(b)
    return _dot(a0, b0) + (_dot(a0, b1) + _dot(a1, b0))


def _select_mod(ref, tile_rows, n_lat_rows):
    row = pl.program_id(1) * tile_rows + lax.broadcasted_iota(jnp.int32, (tile_rows, 1), 0)
    return jnp.where(row >= n_lat_rows, ref[0, 1], ref[0, 0])


def _silu(x):
    return x * jax.nn.sigmoid(x)


def _softplus(x):
    return jnp.maximum(x, 0.0) + jnp.log(1.0 + jnp.exp(-jnp.abs(x)))


def _ada_kernel(c_ref, w_ref, b_ref, o_ref):
    s = _silu(c_ref[...])
    o_ref[0] = _dot3(s, w_ref[0]) + b_ref[0]


def ada_modulation(c8, ada_w, ada_b):
    depth, d, n = ada_w.shape
    tn = 1536
    return pl.pallas_call(
        _ada_kernel,
        out_shape=jax.ShapeDtypeStruct((depth, 8, n), F32),
        grid=(depth, n // tn),
        in_specs=[
            pl.BlockSpec((8, d), lambda i, j: (0, 0)),
            pl.BlockSpec((1, d, tn), lambda i, j: (i, 0, j)),
            pl.BlockSpec((1, 1, tn), lambda i, j: (i, 0, j)),
        ],
        out_specs=pl.BlockSpec((1, 8, tn), lambda i, j: (i, 0, j)),
        compiler_params=_cparams(("arbitrary", "arbitrary")),
        name="ada_modulation",
    )(c8, ada_w, ada_b.reshape(depth, 1, n))


def _proj_kernel(*refs, n_main, n_rope, has_small, cn, head_major, n_lat_rows):
    it = iter(refs)
    x_ref, sc_ref, sh_ref, w_ref = next(it), next(it), next(it), next(it)
    ws_ref = next(it) if has_small else None
    cos_ref = next(it) if n_rope else None
    sin_ref = next(it) if n_rope else None
    o_ref = next(it)
    os_ref = next(it) if has_small else None

    tm = x_ref.shape[1]
    h = x_ref[0] * (1.0 + _select_mod(sc_ref, tm, n_lat_rows)) + _select_mod(sh_ref, tm, n_lat_rows)
    hb = h.astype(BF16)
    if n_rope:
        cos = cos_ref[...]
        sin = sin_ref[...]
        lane = lax.broadcasted_iota(jnp.int32, (tm, LANES), 1)
        low_half = (lane & 32) == 0
    for j in range(n_main // cn):
        p = _dot(hb, w_ref[:, j * cn:(j + 1) * cn])
        for g in range(cn // LANES):
            pg = p[:, g * LANES:(g + 1) * LANES]
            col = j * cn + g * LANES
            if col < n_rope:
                partner = jnp.where(low_half, pltpu.roll(pg, LANES - 32, 1), pltpu.roll(pg, 32, 1))
                pg = pg * cos + partner * sin
            if head_major:
                o_ref[0, col // LANES] = pg.astype(BF16)
            else:
                o_ref[0, :, col:col + LANES] = pg.astype(BF16)
    if has_small:
        os_ref[0] = _dot(hb, ws_ref[...])


def modulated_projection(x, sc, sh, w, w_small=None, rope=None, n_rope=0, n_lat_rows=0, head_major=False, tm=ROW_TILE):
    b, s, d = x.shape
    n_main = w.shape[1]
    grid = (b, s // tm)
    mod_spec = pl.BlockSpec((1, 2, 1, d), lambda bi, i: (bi, 0, 0, 0))
    in_specs = [pl.BlockSpec((1, tm, d), lambda bi, i: (bi, i, 0)), mod_spec, mod_spec,
                pl.BlockSpec((d, n_main), lambda bi, i: (0, 0))]
    args = [x, sc, sh, w]
    if head_major:
        out_shape = [jax.ShapeDtypeStruct((b, n_main // LANES, s, LANES), BF16)]
        out_specs = [pl.BlockSpec((1, n_main // LANES, tm, LANES), lambda bi, i: (bi, 0, i, 0))]
    else:
        out_shape = [jax.ShapeDtypeStruct((b, s, n_main), BF16)]
        out_specs = [pl.BlockSpec((1, tm, n_main), lambda bi, i: (bi, i, 0))]
    if w_small is not None:
        in_specs.append(pl.BlockSpec((d, LANES), lambda bi, i: (0, 0)))
        args.append(w_small)
        out_shape.append(jax.ShapeDtypeStruct((b, s, LANES), F32))
        out_specs.append(pl.BlockSpec((1, tm, LANES), lambda bi, i: (bi, i, 0)))
    if n_rope:
        tab = pl.BlockSpec((tm, LANES), lambda bi, i: (i, 0))
        in_specs += [tab, tab]
        args += [rope[0], rope[1]]
    kern = functools.partial(_proj_kernel, n_main=n_main, n_rope=n_rope,
                             has_small=w_small is not None, cn=512, head_major=head_major,
                             n_lat_rows=n_lat_rows)
    out = pl.pallas_call(
        kern, out_shape=out_shape, grid=grid, in_specs=in_specs, out_specs=out_specs,
        compiler_params=_cparams(("arbitrary", "arbitrary")), name="modulated_projection",
    )(*args)
    return out if w_small is not None else out[0]


NEG_INIT = -1e30
FLASH_ROW_BLOCK = 512
FLASH_LOOKAHEAD = 1


def _flash_kernel(lam_ref, q_ref, k_ref, v_ref, sub_ref, *rest, lam_init, nk, aliased):
    if aliased:
        rest = rest[1:]
    o_ref, m_sc, l_sc, acc_sc, s_ring = rest
    ki = pl.program_id(3)
    tq = m_sc.shape[1]
    tk = k_ref.shape[2]
    nring = s_ring.shape[0]

    @pl.when(ki == 0)
    def _():
        m_sc[...] = jnp.full(m_sc.shape, NEG_INIT, F32)
        l_sc[...] = jnp.zeros(l_sc.shape, F32)
        acc_sc[...] = jnp.zeros(acc_sc.shape, F32)

    q = q_ref[0, 0]
    k = k_ref[0, 0]
    v = v_ref[0, 0]
    hd = DA_HEAD_DIM
    rb = min(tq, FLASH_ROW_BLOCK)
    kc = 2 * LANES
    blocks = [(c, r0) for c in range(2) for r0 in range(0, tq, rb)]
    for i in range(len(blocks) + FLASH_LOOKAHEAD):
        if i < len(blocks):
            c, r0 = blocks[i]
            s_ring[i % nring] = _dot_nt(q[r0:r0 + rb, c * hd:(c + 1) * hd], k[:, c * hd:(c + 1) * hd])
        j = i - FLASH_LOOKAHEAD
        if j < 0:
            continue
        c, r0 = blocks[j]
        rows = slice(r0, r0 + rb)
        s_blk = s_ring.at[j % nring]
        m_prev = m_sc[c, rows, :]
        m_new = jnp.maximum(m_prev, jnp.max(s_blk[...], axis=-1, keepdims=True))
        alpha = jnp.exp2(m_prev - m_new)
        m2 = jnp.concatenate([m_new, m_new], axis=-1)
        lsum = None
        pv = None
        for t in range(tk // kc):
            pj = jnp.exp2(s_blk[:, t * kc:(t + 1) * kc] - m2)
            lj = pj[:, :LANES] + pj[:, LANES:]
            lsum = lj if lsum is None else lsum + lj
            d = _dot(pj.astype(BF16), v[t * kc:(t + 1) * kc, :])
            pv = d if pv is None else pv + d
        l_sc[c, rows, :] = alpha * l_sc[c, rows, :] + lsum
        acc_sc[c, rows, :] = alpha * acc_sc[c, rows, :] + pv
        m_sc[c, rows, :] = m_new

    @pl.when(ki == nk - 1)
    def _():
        lv = lam_ref[...]
        lam = (jnp.exp(jnp.sum(lv[0:1] * lv[1:2], axis=-1, keepdims=True))
               - jnp.exp(jnp.sum(lv[2:3] * lv[3:4], axis=-1, keepdims=True)) + lam_init)
        l0 = jnp.sum(l_sc[0], axis=-1, keepdims=True)
        l1 = jnp.sum(l_sc[1], axis=-1, keepdims=True)
        o = acc_sc[0] / l0 - lam * (acc_sc[1] / l1)
        ms = jnp.mean(o * o, axis=-1, keepdims=True)
        o = o * lax.rsqrt(ms + NORM_EPS) * sub_ref[...] * (1.0 - lam_init)
        o_ref[0] = o.astype(BF16)


def diff_flash_attention(p_all, lam_vec, subln, lam_init, *, tq, tk, nq, nk, q_off, k_off, prev=None):
    b, _, s, _ = p_all.shape
    hh = DA_HEADS
    in_specs = [
        pl.BlockSpec((4, DA_HEAD_DIM), lambda bi, h, qi, ki: (0, 0)),
        pl.BlockSpec((1, 1, tq, LANES), lambda bi, h, qi, ki: (bi, h, qi + q_off, 0)),
        pl.BlockSpec((1, 1, tk, LANES), lambda bi, h, qi, ki: (bi, hh + h, ki + k_off, 0)),
        pl.BlockSpec((1, 1, tk, LANES), lambda bi, h, qi, ki: (bi, 2 * hh + h, ki + k_off, 0)),
        pl.BlockSpec((1, LANES), lambda bi, h, qi, ki: (0, 0)),
    ]
    args = [lam_vec, p_all, p_all, p_all, subln.reshape(1, LANES)]
    aliases = {}
    if prev is not None:
        in_specs.append(pl.BlockSpec(memory_space=pl.ANY))
        args.append(prev)
        aliases = {5: 0}
    kern = functools.partial(_flash_kernel, lam_init=lam_init, nk=nk, aliased=prev is not None)
    return pl.pallas_call(
        kern,
        out_shape=jax.ShapeDtypeStruct((b, s, hh * LANES), BF16),
        grid=(b, hh, nq, nk),
        in_specs=in_specs,
        out_specs=pl.BlockSpec((1, tq, LANES), lambda bi, h, qi, ki: (bi, qi + q_off, h)),
        scratch_shapes=[pltpu.VMEM((2, tq, LANES), F32), pltpu.VMEM((2, tq, LANES), F32),
                        pltpu.VMEM((2, tq, LANES), F32),
                        pltpu.VMEM((FLASH_LOOKAHEAD + 1, min(tq, FLASH_ROW_BLOCK), tk), F32)],
        input_output_aliases=aliases,
        compiler_params=_cparams(("arbitrary", "arbitrary", "arbitrary", "arbitrary")),
        name="diff_flash_attention",
    )(*args)


def _post_norm(x, y, gate, lg, lb):
    r = DN_ALPHA * x + gate * y
    mu = jnp.mean(r, axis=-1, keepdims=True)
    rc = r - mu
    var = jnp.mean(rc * rc, axis=-1, keepdims=True)
    return rc * lax.rsqrt(var + LN_EPS) * lg + lb


def _outproj_kernel(*refs, gated, n_heads, dh, n_lat_rows):
    if gated:
        of_ref, ob_ref, z_ref, ng_ref, w_ref, x_ref, gate_ref, lg_ref, lb_ref, out_ref = refs
        y = None
        for h in range(n_heads):
            sl = slice(h * dh, (h + 1) * dh)
            o = of_ref[0, :, sl].astype(F32) + ob_ref[0, :, sl].astype(F32)
            ms = jnp.mean(o * o, axis=-1, keepdims=True)
            o = o * lax.rsqrt(ms + NORM_EPS) * ng_ref[...] * _silu(z_ref[0, :, sl].astype(F32))
            t = _dot(o.astype(BF16), w_ref[sl, :])
            y = t if y is None else y + t
    else:
        o_ref, w_ref, x_ref, gate_ref, lg_ref, lb_ref, out_ref = refs
        y = _dot(o_ref[0], w_ref[...])
    gate = _select_mod(gate_ref, x_ref.shape[1], n_lat_rows)
    out_ref[0] = _post_norm(x_ref[0], y, gate, lg_ref[...], lb_ref[...])


def out_projection_post_norm(o_args, w_o, x, gate, lg, lb, n_lat_rows, gated=None, tm=ROW_TILE):
    b, s, d = x.shape
    kdim = w_o.shape[0]
    row = lambda width, cb=0: pl.BlockSpec((1, tm, width), lambda bi, i: (bi, i, cb))
    full = lambda shape: pl.BlockSpec(shape, lambda bi, i: (0,) * len(shape))
    if gated is None:
        in_specs = [row(kdim)]
        args = list(o_args)
        kern = functools.partial(_outproj_kernel, gated=False, n_heads=0, dh=0, n_lat_rows=n_lat_rows)
    else:
        n_heads, dh, z_cb = gated
        of, ob, z, ng = o_args
        in_specs = [row(kdim), row(kdim), row(kdim, z_cb), full((1, dh))]
        args = [of, ob, z, ng.reshape(1, dh)]
        kern = functools.partial(_outproj_kernel, gated=True, n_heads=n_heads, dh=dh, n_lat_rows=n_lat_rows)
    in_specs += [full((kdim, d)), row(d),
                 pl.BlockSpec((1, 2, 1, d), lambda bi, i: (bi, 0, 0, 0)),
                 full((1, d)), full((1, d))]
    args += [w_o, x, gate, lg.reshape(1, d), lb.reshape(1, d)]
    return pl.pallas_call(
        kern, out_shape=jax.ShapeDtypeStruct((b, s, d), F32), grid=(b, s // tm),
        in_specs=in_specs, out_specs=row(d),
        compiler_params=_cparams(("arbitrary", "arbitrary")), name="out_projection_post_norm",
    )(*args)


def _gdn_prep_kernel(cur_ref, prev_ref, next_ref, gate_ref, cw_ref, alog_ref, dtb_ref,
                     q_ref, k_ref, v_ref, g_ref, ext, *, n_lat_tiles, n_tiles, cn):
    i = pl.program_id(1)
    tm = cur_ref.shape[1]
    first = jnp.logical_or(i == 0, i == n_lat_tiles)
    last = jnp.logical_or(i == n_lat_tiles - 1, i == n_tiles - 1)
    pmask = jnp.where(first, 0.0, 1.0)
    nmask = jnp.where(last, 0.0, 1.0)
    pad = GDN_CONV_W // 2
    dh = GDN_HEAD_DIM
    n_qk = 2 * GDN_KEY_DIM
    for cc in range(cur_ref.shape[2] // cn):
        cs = slice(cc * cn, (cc + 1) * cn)
        ext[0:HALO, :] = prev_ref[0, :, cs].astype(F32) * pmask
        ext[HALO:HALO + tm, :] = cur_ref[0, :, cs].astype(F32)
        ext[HALO + tm:2 * HALO + tm, :] = next_ref[0, :, cs].astype(F32) * nmask
        acc = None
        for j in range(GDN_CONV_W):
            t = ext[pl.ds(HALO - pad + j, tm), :] * cw_ref[j:j + 1, cs]
            acc = t if acc is None else acc + t
        y = _silu(acc)
        for g in range(cn // dh):
            col = cc * cn + g * dh
            yg = y[:, g * dh:(g + 1) * dh]
            if col < n_qk:
                yg = yg * lax.rsqrt(jnp.sum(yg * yg, axis=-1, keepdims=True) + NORM_EPS)
                if col < GDN_KEY_DIM:
                    q_ref[0, col // dh] = (yg * (dh ** -0.5)).astype(BF16)
                else:
                    k_ref[0, (col - GDN_KEY_DIM) // dh] = yg.astype(BF16)
            else:
                v_ref[0, (col - n_qk) // dh] = yg.astype(BF16)
    a = gate_ref[0]
    lane = lax.broadcasted_iota(jnp.int32, a.shape, 1)
    gdec = -jnp.exp(alog_ref[...]) * _softplus(a + dtb_ref[...])
    g_ref[0] = jnp.where(lane < 2 * GDN_V_HEADS, gdec, jax.nn.sigmoid(a))


def gdn_prep(p_main, gates_raw, conv_w, alog_row, dtb_row, n_lat_tiles):
    b, s, _ = p_main.shape
    tm = ROW_TILE
    nt = s // tm
    nch = 2 * GDN_KEY_DIM + GDN_VAL_DIM
    hpt = tm // HALO
    nh = s // HALO
    kern = functools.partial(_gdn_prep_kernel, n_lat_tiles=n_lat_tiles, n_tiles=nt, cn=512)
    head_out = lambda nheads: pl.BlockSpec((1, nheads, tm, GDN_HEAD_DIM), lambda bi, i: (bi, 0, i, 0))
    return pl.pallas_call(
        kern,
        out_shape=[jax.ShapeDtypeStruct((b, GDN_K_HEADS, s, GDN_HEAD_DIM), BF16),
                   jax.ShapeDtypeStruct((b, GDN_K_HEADS, s, GDN_HEAD_DIM), BF16),
                   jax.ShapeDtypeStruct((b, GDN_V_HEADS, s, GDN_HEAD_DIM), BF16),
                   jax.ShapeDtypeStruct((b, s, LANES), F32)],
        grid=(b, nt),
        in_specs=[
            pl.BlockSpec((1, tm, nch), lambda bi, i: (bi, i, 0)),
            pl.BlockSpec((1, HALO, nch), lambda bi, i: (bi, jnp.maximum(i * hpt - 1, 0), 0)),
            pl.BlockSpec((1, HALO, nch), lambda bi, i: (bi, jnp.minimum((i + 1) * hpt, nh - 1), 0)),
            pl.BlockSpec((1, tm, LANES), lambda bi, i: (bi, i, 0)),
            pl.BlockSpec((GDN_CONV_W, nch), lambda bi, i: (0, 0)),
            pl.BlockSpec((1, LANES), lambda bi, i: (0, 0)),
            pl.BlockSpec((1, LANES), lambda bi, i: (0, 0)),
        ],
        out_specs=[head_out(GDN_K_HEADS), head_out(GDN_K_HEADS), head_out(GDN_V_HEADS),
                   pl.BlockSpec((1, tm, LANES), lambda bi, i: (bi, i, 0))],
        scratch_shapes=[pltpu.VMEM((tm + 2 * HALO, 512), F32)],
        compiler_params=_cparams(("arbitrary", "arbitrary")),
        name="gdn_prep",
    )(p_main, p_main, p_main, gates_raw, conv_w, alog_row, dtb_row)


def _chunk_masks(n, reverse):
    r = lax.broadcasted_iota(jnp.int32, (n, n), 0)
    c = lax.broadcasted_iota(jnp.int32, (n, n), 1)
    if reverse:
        return r <= c, r < c
    return r >= c, r > c


def _block_cumsum_mats(ct, reverse):
    r = lax.broadcasted_iota(jnp.int32, (ct, ct), 0)
    c = lax.broadcasted_iota(jnp.int32, (ct, ct), 1)
    same = (r // CHUNK) == (c // CHUNK)
    lower = jnp.logical_and(same, c <= r)
    upper = jnp.logical_and(same, c >= r)
    lo = jnp.where(lower, 1.0, 0.0).astype(BF16)
    up = jnp.where(upper, 1.0, 0.0).astype(BF16)
    return (up, lo) if reverse else (lo, up)


def _cumsum_cols(mat, x):
    x0, x1, x2 = _split3(x)
    return _dot(mat, x0) + (_dot(mat, x1) + _dot(mat, x2))


def _cumsum_rows(x, mat):
    x0, x1, x2 = _split3(x)
    return _dot(x0, mat) + (_dot(x1, mat) + _dot(x2, mat))


def _gdn_block(dirs):
    dh = GDN_HEAD_DIM
    eye = jnp.where(lax.broadcasted_iota(jnp.int32, (CHUNK, CHUNK), 0)
                    == lax.broadcasted_iota(jnp.int32, (CHUNK, CHUNK), 1), 1.0, 0.0).astype(F32)
    chains = []
    for q_ref, k_ref, v_ref, gc_ref, gr_ref, o_ref, s_sc, reverse in dirs:
        ct = q_ref.shape[2]
        goff = 2 if reverse else 0
        m_col, m_row = _block_cumsum_mats(ct, reverse)
        tril, strict = _chunk_masks(CHUNK, reverse)
        for kh in range(q_ref.shape[1]):
            gcol = gc_ref[0, kh]
            grow = gr_ref[0, kh]
            gcum_c = _cumsum_cols(m_col, gcol)
            gcum_r = _cumsum_rows(grow, m_row)
            for c in range(ct // CHUNK):
                rs = slice(c * CHUNK, (c + 1) * CHUNK)
                last = c * CHUNK if reverse else (c + 1) * CHUNK - 1
                q = q_ref[0, kh, rs, :]
                k = k_ref[0, kh, rs, :]
                kk = _dot_nt(k, k)
                qk = _dot_nt(q, k)
                for hh in range(2):
                    gi = goff + hh
                    chains.append(dict(
                        q=q, k=k, kk=kk, qk=qk, rs=rs, c=c, hh=2 * kh + hh, reverse=reverse, tril=tril,
                        strict=strict, v_ref=v_ref, o_ref=o_ref, s_sc=s_sc,
                        gc=gcum_c[rs, gi:gi + 1], gr=gcum_r[gi:gi + 1, rs],
                        glast=gcum_c[last:last + 1, gi:gi + 1], beta=gcol[rs, 4 + gi:5 + gi]))
    for ch in chains:
        tril = ch["tril"]
        ch["decay"] = jnp.where(tril, jnp.exp(jnp.where(tril, ch["gc"] - ch["gr"], 0.0)), 0.0)
        a = -jnp.where(ch["strict"], ch["kk"] * ch["beta"] * ch["decay"], 0.0)
        ch["tmat"] = eye + a
        ch["pw"] = a
    for _ in range(5):
        for ch in chains:
            pwb = ch["pw"].astype(BF16)
            ch["pw"] = _dot(pwb, pwb)
        for ch in chains:
            ch["tmat"] = ch["tmat"] + _dot(ch["tmat"].astype(BF16), ch["pw"].astype(BF16))
    for ch in chains:
        beta = ch["beta"]
        eg = jnp.exp(ch["gc"])
        vb = ch["v_ref"][0, ch["hh"], ch["rs"], :].astype(F32) * beta
        kbg = ch["k"].astype(F32) * (beta * eg)
        uw = _dot(ch["tmat"].astype(BF16), jnp.concatenate([vb, kbg], axis=-1).astype(BF16))
        ch["u"] = uw[:, :dh]
        ch["w"] = uw[:, dh:].astype(BF16)
        ch["eg"] = eg
        ch["attn"] = jnp.where(ch["tril"], ch["qk"] * ch["decay"], 0.0).astype(BF16)
    nchunk = max(ch["c"] for ch in chains) + 1
    for step in range(nchunk):
        cur = [ch for ch in chains if ch["c"] == (nchunk - 1 - step if ch["reverse"] else step)]
        for ch in cur:
            state = ch["s_sc"][ch["hh"]]
            sb = state.astype(BF16)
            ch["state"] = state
            ch["ws"] = _dot(ch["w"], sb)
            ch["qs"] = _dot(ch["q"], sb)
        for ch in cur:
            v_new = ch["u"] - ch["ws"]
            ch["kgv"] = (v_new * jnp.exp(ch["glast"] - ch["gc"])).astype(BF16)
            ch["o"] = ch["eg"] * ch["qs"] + _dot(ch["attn"], v_new.astype(BF16))
        for ch in cur:
            ch["s_sc"][ch["hh"]] = ch["state"] * jnp.exp(ch["glast"]) + _dot_tn(ch["k"], ch["kgv"])
            ch["o_ref"][0, ch["rs"], ch["hh"] * dh:(ch["hh"] + 1) * dh] = ch["o"].astype(BF16)


def _gdn_scan_kernel(qf, kf, vf, gcf, grf, qb, kb, vb, gcb, grb, s0f, s0b, *rest, nblk, aliased):
    if aliased:
        rest = rest[2:]
    of_ref, ob_ref, sff, sfb, sf_sc, sb_sc = rest
    j = pl.program_id(2)

    @pl.when(j == 0)
    def _():
        sf_sc[...] = s0f[0]
        sb_sc[...] = s0b[0]

    _gdn_block([(qf, kf, vf, gcf, grf, of_ref, sf_sc, False),
                (qb, kb, vb, gcb, grb, ob_ref, sb_sc, True)])

    @pl.when(j == nblk - 1)
    def _():
        sff[0] = sf_sc[...]
        sfb[0] = sb_sc[...]


def gdn_scan(qn, kn, vv, gcol, grow, s0f, s0b, *, nblk, off, prev=None):
    b, _, s, dh = qn.shape
    ct = SCAN_TILE
    g = GDN_HEADS_PER_STEP
    fwd = lambda j: j + off
    bwd = lambda j: nblk - 1 - j + off
    def specs(pos):
        return [
            pl.BlockSpec((1, g, ct, dh), lambda bi, h, j: (bi, h, pos(j), 0)),
            pl.BlockSpec((1, g, ct, dh), lambda bi, h, j: (bi, h, pos(j), 0)),
            pl.BlockSpec((1, 2 * g, ct, dh), lambda bi, h, j: (bi, h, pos(j), 0)),
            pl.BlockSpec((1, g, ct, 8), lambda bi, h, j: (bi, h, pos(j), 0)),
            pl.BlockSpec((1, g, 8, ct), lambda bi, h, j: (bi, h, 0, pos(j))),
        ]
    st_spec = pl.BlockSpec((1, 2 * g, dh, dh), lambda bi, h, j: (bi, h, 0, 0))
    in_specs = specs(fwd) + specs(bwd) + [st_spec, st_spec]
    args = [qn, kn, vv, gcol, grow] * 2 + [s0f, s0b]
    aliases = {}
    if prev is not None:
        in_specs += [pl.BlockSpec(memory_space=pl.ANY)] * 2
        args += list(prev)
        aliases = {12: 0, 13: 1}
    o_shape = jax.ShapeDtypeStruct((b, s, GDN_VAL_DIM), BF16)
    st_shape = jax.ShapeDtypeStruct((b, GDN_V_HEADS, dh, dh), F32)
    kern = functools.partial(_gdn_scan_kernel, nblk=nblk, aliased=prev is not None)
    return pl.pallas_call(
        kern,
        out_shape=[o_shape, o_shape, st_shape, st_shape],
        grid=(b, GDN_K_HEADS // g, nblk),
        in_specs=in_specs,
        out_specs=[pl.BlockSpec((1, ct, 2 * g * dh), lambda bi, h, j: (bi, fwd(j), h)),
                   pl.BlockSpec((1, ct, 2 * g * dh), lambda bi, h, j: (bi, bwd(j), h)),
                   st_spec, st_spec],
        scratch_shapes=[pltpu.VMEM((2 * g, dh, dh), F32), pltpu.VMEM((2 * g, dh, dh), F32)],
        input_output_aliases=aliases,
        compiler_params=_cparams(("arbitrary", "arbitrary", "arbitrary")),
        name="gdn_scan",
    )(*args)


def _gla_block(dirs, wg_ref, gb_ref):
    chains = []
    pre = []
    for q_ref, k_ref, v_ref, gr_ref, o_ref, st_sc, z in dirs:
        pre.append(_dot3(gr_ref[0], wg_ref[z]) + gb_ref[z:z + 1, :])
    for (q_ref, k_ref, v_ref, gr_ref, o_ref, st_sc, z), logit in zip(dirs, pre):
        reverse = z == 1
        ct = q_ref.shape[1]
        glog = -_softplus(-logit) / GLA_TAU
        m_col, _ = _block_cumsum_mats(ct, reverse)
        bcum = _cumsum_cols(m_col, glog)
        tril, _ = _chunk_masks(CHUNK, reverse)
        for hh in range(st_sc.shape[0]):
            ks = slice(hh * GLA_DK, (hh + 1) * GLA_DK)
            vs = slice(hh * GLA_DV, (hh + 1) * GLA_DV)
            for c in range(ct // CHUNK):
                rs = slice(c * CHUNK, (c + 1) * CHUNK)
                last = c * CHUNK if reverse else (c + 1) * CHUNK - 1
                bc = bcum[rs, ks]
                bl = bcum[last:last + 1, ks]
                qf = q_ref[0, rs, ks].astype(F32) * (GLA_DK ** -0.5)
                kf = k_ref[0, rs, ks].astype(F32)
                chains.append(dict(
                    c=c, rs=rs, hh=hh, vs=vs, reverse=reverse, tril=tril, o_ref=o_ref, st_sc=st_sc,
                    v=v_ref[0, rs, vs],
                    qe=(qf * jnp.exp(bc)).astype(BF16), ke=(kf * jnp.exp(-bc)).astype(BF16),
                    kg=(kf * jnp.exp(bl - bc)).astype(BF16), gl=jnp.exp(bl)))
    for ch in chains:
        ch["attn"] = jnp.where(ch["tril"], _dot_nt(ch["qe"], ch["ke"]), 0.0).astype(BF16)
    for ch in chains:
        ch["o"] = _dot(ch["attn"], ch["v"])
        ch["kv"] = _dot_tn(ch["v"], ch["kg"])
    nchunk = max(ch["c"] for ch in chains) + 1
    for step in range(nchunk):
        cur = [ch for ch in chains if ch["c"] == (nchunk - 1 - step if ch["reverse"] else step)]
        for ch in cur:
            st = ch["st_sc"][ch["hh"]]
            ch["o"] = ch["o"] + _dot_nt(ch["qe"], st.astype(BF16))
            ch["st_sc"][ch["hh"]] = st * ch["gl"] + ch["kv"]
        for ch in cur:
            ch["o_ref"][0, ch["rs"], ch["vs"]] = ch["o"].astype(BF16)


def _gla_scan_kernel(qf, kf, vf, grf, qb, kb, vb, grb, wg, gb, s0f, s0b, *rest, nblk, aliased):
    if aliased:
        rest = rest[2:]
    of_ref, ob_ref, sff, sfb, sf_sc, sb_sc = rest
    j = pl.program_id(2)

    @pl.when(j == 0)
    def _():
        sf_sc[...] = s0f[0]
        sb_sc[...] = s0b[0]

    _gla_block([(qf, kf, vf, grf, of_ref, sf_sc, 0), (qb, kb, vb, grb, ob_ref, sb_sc, 1)], wg, gb)

    @pl.when(j == nblk - 1)
    def _():
        sff[0] = sf_sc[...]
        sfb[0] = sb_sc[...]


def gla_scan(p_main, gr, wg_pad, gate_b, s0f, s0b, *, nblk, off, prev=None):
    b, s, _ = p_main.shape
    ct = SCAN_TILE
    nh, dk, dv = GLA_HEADS, GLA_DK, GLA_DV
    g = GLA_HEADS_PER_STEP
    ng = nh // g
    fwd = lambda j: j + off
    bwd = lambda j: nblk - 1 - j + off
    def specs(pos):
        return [
            pl.BlockSpec((1, ct, g * dk), lambda bi, h, j: (bi, pos(j), h)),
            pl.BlockSpec((1, ct, g * dk), lambda bi, h, j: (bi, pos(j), ng + h)),
            pl.BlockSpec((1, ct, g * dv), lambda bi, h, j: (bi, pos(j), ng + h)),
            pl.BlockSpec((1, ct, LANES), lambda bi, h, j: (bi, pos(j), 0)),
        ]
    st_spec = pl.BlockSpec((1, g, dv, dk), lambda bi, h, j: (bi, h, 0, 0))
    in_specs = specs(fwd) + specs(bwd) + [
        pl.BlockSpec((2, LANES, g * dk), lambda bi, h, j: (0, 0, h)),
        pl.BlockSpec((2, g * dk), lambda bi, h, j: (0, h)),
        st_spec, st_spec]
    args = [p_main, p_main, p_main, gr] * 2 + [wg_pad, gate_b, s0f, s0b]
    aliases = {}
    if prev is not None:
        in_specs += [pl.BlockSpec(memory_space=pl.ANY)] * 2
        args += list(prev)
        aliases = {12: 0, 13: 1}
    o_shape = jax.ShapeDtypeStruct((b, s, GLA_VAL_DIM), BF16)
    st_shape = jax.ShapeDtypeStruct((b, nh, dv, dk), F32)
    kern = functools.partial(_gla_scan_kernel, nblk=nblk, aliased=prev is not None)
    return pl.pallas_call(
        kern,
        out_shape=[o_shape, o_shape, st_shape, st_shape],
        grid=(b, ng, nblk),
        in_specs=in_specs,
        out_specs=[pl.BlockSpec((1, ct, g * dv), lambda bi, h, j: (bi, fwd(j), h)),
                   pl.BlockSpec((1, ct, g * dv), lambda bi, h, j: (bi, bwd(j), h)),
                   st_spec, st_spec],
        scratch_shapes=[pltpu.VMEM((g, dv, dk), F32), pltpu.VMEM((g, dv, dk), F32)],
        input_output_aliases=aliases,
        compiler_params=_cparams(("arbitrary", "arbitrary", "arbitrary")),
        name="gla_scan",
    )(*args)


def _router_kernel(x_ref, sc_ref, sh_ref, rwt_ref, rb_ref, h_ref, wd_ref, *, n_lat_rows):
    tm = x_ref.shape[1]
    h = x_ref[0] * (1.0 + _select_mod(sc_ref, tm, n_lat_rows)) + _select_mod(sh_ref, tm, n_lat_rows)
    h_ref[0] = h.astype(BF16)
    w0, w1 = _split2(rwt_ref[...])
    h0, h1 = _split2(h)
    scores = jax.nn.sigmoid(_dot_nt(w0, h0) + (_dot_nt(w0, h1) + _dot_nt(w1, h0)))
    ne = scores.shape[0]
    row = lax.broadcasted_iota(jnp.int32, scores.shape, 0)
    neg = jnp.float32(-jnp.inf)
    sel = scores + rb_ref[...]
    chosen = jnp.zeros(scores.shape, jnp.bool_)
    for _ in range(TOP_K):
        mx = jnp.max(sel, axis=0, keepdims=True)
        first = jnp.min(jnp.where(sel == mx, row, ne), axis=0, keepdims=True)
        pick = row == first
        chosen = jnp.logical_or(chosen, pick)
        sel = jnp.where(pick, neg, sel)
    picked = jnp.where(chosen, scores, 0.0)
    wt = picked / jnp.sum(picked, axis=0, keepdims=True) * ROUTE_SCALE
    eye = jnp.where(lax.broadcasted_iota(jnp.int32, (ne, LANES), 0)
                    == lax.broadcasted_iota(jnp.int32, (ne, LANES), 1), 1.0, 0.0).astype(BF16)
    t0, t1, t2 = _split3(wt)
    wd_ref[0] = _dot_tn(t0, eye) + (_dot_tn(t1, eye) + _dot_tn(t2, eye))


def moe_route(x, sc, sh, rw_t, rb_col, n_lat_rows, tm=ROW_TILE):
    b, s, d = x.shape
    mod_spec = pl.BlockSpec((1, 2, 1, d), lambda bi, i: (bi, 0, 0, 0))
    return pl.pallas_call(
        functools.partial(_router_kernel, n_lat_rows=n_lat_rows),
        out_shape=[jax.ShapeDtypeStruct((b, s, d), BF16), jax.ShapeDtypeStruct((b, s, LANES), F32)],
        grid=(b, s // tm),
        in_specs=[pl.BlockSpec((1, tm, d), lambda bi, i: (bi, i, 0)), mod_spec, mod_spec,
                  pl.BlockSpec((N_EXPERTS, d), lambda bi, i: (0, 0)),
                  pl.BlockSpec((N_EXPERTS, 1), lambda bi, i: (0, 0))],
        out_specs=[pl.BlockSpec((1, tm, d), lambda bi, i: (bi, i, 0)),
                   pl.BlockSpec((1, tm, LANES), lambda bi, i: (bi, i, 0))],
        compiler_params=_cparams(("arbitrary", "arbitrary")),
        name="moe_router",
    )(x, sc, sh, rw_t, rb_col)


MOE_SUB = 256
MOE_CAP = 64
MOE_GRP = 4


def _moe_grouped_kernel(order_ref, h_ref, wd_ref, *rest):
    wgu_refs = rest[:MOE_GRP]
    wdn_refs = rest[MOE_GRP:2 * MOE_GRP]
    wsgu_ref, wsdn_ref, f_ref, acc, rrm, wrm, cmax = rest[2 * MOE_GRP:]
    g = pl.program_id(2)
    n_grp = pl.num_programs(2)
    tm = acc.shape[0]
    nsub = tm // MOE_SUB
    sub, cap, nslot = MOE_SUB, MOE_CAP, MOE_GRP * MOE_CAP
    lane = lax.broadcasted_iota(jnp.int32, (1, LANES), 1)

    @pl.when(g == 0)
    def _():
        r = lax.broadcasted_iota(jnp.int32, (sub, sub), 0)
        c = lax.broadcasted_iota(jnp.int32, (sub, sub), 1)
        after = jnp.where(r < c, 1.0, 0.0).astype(BF16)
        ident = jnp.where(r == c, 1.0, 0.0).astype(BF16)
        cm = jnp.zeros((1, LANES), F32)
        for u in range(nsub):
            rs = slice(u * sub, (u + 1) * sub)
            hu = h_ref[0, rs, :]
            gu = _dot(hu, wsgu_ref[...])
            act = _silu(gu[:, :EXPERT_FF]) * gu[:, EXPERT_FF:]
            acc[rs, :] = _dot(act.astype(BF16), wsdn_ref[...])
            wd = wd_ref[0, rs, :]
            active = jnp.logical_and(wd != 0.0, lane < N_EXPERTS)
            a = jnp.where(active, 1.0, 0.0)
            ab = a.astype(BF16)
            rank_r = _dot_tn(ab, after)
            a_r = _dot_tn(ab, ident)
            rrm[u] = jnp.where(a_r > 0.5, rank_r, -1.0)
            wrm[u] = _dot_tn(wd.astype(BF16), ident)
            cm = jnp.maximum(cm, jnp.sum(a, axis=0, keepdims=True))
        cmax[...] = jnp.broadcast_to(cm, cmax.shape)

    experts = [order_ref[g * MOE_GRP + k] for k in range(MOE_GRP)]
    in_group = functools.reduce(jnp.logical_or, [lane == e for e in experts])
    n_max = jnp.max(jnp.where(in_group, cmax[0:1, :], 0.0))
    n_pass = (n_max.astype(jnp.int32) + (cap - 1)) // cap

    def slot_expert(l):
        return sum(((l >= k * cap).astype(jnp.int32) for k in range(1, MOE_GRP)), jnp.zeros_like(l))

    l_col = lax.broadcasted_iota(jnp.int32, (sub, 1), 0)
    j_col = jnp.where(l_col < nslot, l_col - cap * slot_expert(l_col), -1000).astype(F32)

    def per_slot(table, u, fill):
        rows = [jnp.broadcast_to(table[u, pl.ds(experts[k], 1), :], (cap, sub)) for k in range(MOE_GRP)]
        if nslot < sub:
            rows.append(jnp.full((sub - nslot, sub), fill, F32))
        return jnp.concatenate(rows, axis=0)

    def one_pass(p, carry):
        base = (p * cap).astype(F32)
        xg, expanders = [], []
        for u in range(nsub):
            rs = slice(u * sub, (u + 1) * sub)
            match = per_slot(rrm, u, -1.0) == j_col + base
            gather = jnp.where(match, 1.0, 0.0).astype(BF16)
            expanders.append(jnp.where(match, per_slot(wrm, u, 0.0), 0.0).astype(BF16))
            xg.append(_dot(gather, h_ref[0, rs, :]).astype(BF16))
        ys = []
        for k in range(MOE_GRP):
            es = slice(k * cap, (k + 1) * cap)
            x_e = jnp.concatenate([xg[u][es] for u in range(nsub)], axis=0)
            gu = _dot(x_e, wgu_refs[k][0, 0])
            act = _silu(gu[:, :EXPERT_FF]) * gu[:, EXPERT_FF:]
            ys.append(_dot(act.astype(BF16), wdn_refs[k][0, 0]).astype(BF16))
        for u in range(nsub):
            rs = slice(u * sub, (u + 1) * sub)
            parts = [ys[k][u * cap:(u + 1) * cap] for k in range(MOE_GRP)]
            if nslot < sub:
                parts.append(jnp.zeros((sub - nslot, parts[0].shape[1]), BF16))
            y_u = jnp.concatenate(parts, axis=0)
            acc[rs, :] += _dot_tn(expanders[u], y_u)
        return carry

    lax.fori_loop(0, n_pass, one_pass, 0)

    @pl.when(g == n_grp - 1)
    def _():
        f_ref[0] = acc[...]


def moe_grouped_experts(order, h, wd, wgu, wdn, layer, wsgu, wsdn, tm):
    b, s, d = h.shape
    n_grp = wgu.shape[1] // MOE_GRP
    row = lambda width: pl.BlockSpec((1, tm, width), lambda bi, i, g, o: (bi, i, 0))

    def expert(shape, k):
        return pl.BlockSpec((1, 1) + shape, lambda bi, i, g, o: (layer, o[g * MOE_GRP + k], 0, 0))

    in_specs = ([row(d), row(LANES)]
                + [expert((d, 2 * EXPERT_FF), k) for k in range(MOE_GRP)]
                + [expert((EXPERT_FF, d), k) for k in range(MOE_GRP)]
                + [pl.BlockSpec((d, 2 * EXPERT_FF), lambda bi, i, g, o: (0, 0)),
                   pl.BlockSpec((EXPERT_FF, d), lambda bi, i, g, o: (0, 0))])
    grid_spec = pltpu.PrefetchScalarGridSpec(
        num_scalar_prefetch=1, grid=(b, s // tm, n_grp), in_specs=in_specs, out_specs=row(d),
        scratch_shapes=[pltpu.VMEM((tm, d), F32),
                        pltpu.VMEM((tm // MOE_SUB, LANES, MOE_SUB), F32),
                        pltpu.VMEM((tm // MOE_SUB, LANES, MOE_SUB), F32),
                        pltpu.VMEM((8, LANES), F32)])
    return pl.pallas_call(
        _moe_grouped_kernel,
        out_shape=jax.ShapeDtypeStruct((b, s, d), F32),
        grid_spec=grid_spec,
        compiler_params=_cparams(("arbitrary", "arbitrary", "arbitrary")),
        name="moe_grouped_experts",
    )(order, h, wd, *([wgu] * MOE_GRP), *([wdn] * MOE_GRP), wsgu, wsdn)


def _post_norm_kernel(x_ref, f_ref, gate_ref, lg_ref, lb_ref, out_ref, *, n_lat_rows):
    gate = _select_mod(gate_ref, x_ref.shape[1], n_lat_rows)
    out_ref[0] = _post_norm(x_ref[0], f_ref[0].astype(F32), gate, lg_ref[...], lb_ref[...])


def post_norm_rows(x, f, gate, lg, lb, n_lat_rows, rows_out=None):
    b, s, d = x.shape
    rows_out = s if rows_out is None else rows_out
    tm = _pick_tile(rows_out, (1280, 1024, 640, 512, 256))
    row = pl.BlockSpec((1, tm, d), lambda bi, i: (bi, i, 0))
    vec = pl.BlockSpec((1, d), lambda bi, i: (0, 0))
    return pl.pallas_call(
        functools.partial(_post_norm_kernel, n_lat_rows=n_lat_rows),
        out_shape=jax.ShapeDtypeStruct((b, rows_out, d), F32), grid=(b, rows_out // tm),
        in_specs=[row, row, pl.BlockSpec((1, 2, 1, d), lambda bi, i: (bi, 0, 0, 0)), vec, vec],
        out_specs=row, compiler_params=_cparams(("arbitrary", "arbitrary")), name="post_norm_rows",
    )(x, f, gate, lg.reshape(1, d), lb.reshape(1, d))


def _rope_tables(n_lat, n_ctx):
    rows = n_lat // GRID_W
    rowp = jnp.repeat(jnp.arange(rows), GRID_W).astype(F32)
    colp = jnp.tile(jnp.arange(GRID_W), rows).astype(F32)
    n_freq = DA_HEAD_DIM // 4
    inv = 1.0 / (ROPE_BASE ** (jnp.arange(n_freq, dtype=F32) / n_freq))
    ang = jnp.concatenate([rowp[:, None] * inv, colp[:, None] * inv], -1)
    cos, sin = jnp.cos(ang), jnp.sin(ang)
    cos_t = jnp.tile(cos, (1, 4))
    sin_t = jnp.tile(jnp.concatenate([-sin, sin], -1), (1, 2))
    cos_t = jnp.concatenate([cos_t, jnp.ones((n_ctx, LANES), F32)], 0)
    sin_t = jnp.concatenate([sin_t, jnp.zeros((n_ctx, LANES), F32)], 0)
    return cos_t, sin_t


def _pick_tile(total, cands):
    for t in cands:
        if total % t == 0:
            return t
    raise ValueError(f"no tile for {total}")


def _pad_cols(w, n):
    return jnp.pad(w, ((0, 0), (0, n - w.shape[1])))


def _flash_both(p_all, lam_vec, subln, lam_init, n_lat, n_ctx):
    s = n_lat + n_ctx
    tq = _pick_tile(n_lat, (2048, 1024, 512, 256))
    tk = _pick_tile(s, (3328, 1280, 1024, 512, 256))
    o = diff_flash_attention(p_all, lam_vec, subln, lam_init, tq=tq, tk=tk, nq=n_lat // tq, nk=s // tk,
                             q_off=0, k_off=0)
    return diff_flash_attention(p_all, lam_vec, subln, lam_init, tq=n_ctx, tk=n_ctx, nq=1, nk=1,
                                q_off=n_lat // n_ctx, k_off=n_lat // n_ctx, prev=o)


def kernel(x, c, ctx, c_ctx, ada_w, ada_b, ln_g, ln_b, da_w_in, da_w_o, da_lambda, da_subln, gdn_w_in, gdn_conv, gdn_a_log, gdn_dt_bias, gdn_norm, gdn_w_o, gla_w_in, gla_w_gate, gla_gate_b, gla_norm, gla_w_o, moe_router, moe_router_b, moe_w_gu, moe_w_dn, moe_ws_gu, moe_ws_dn):
    b, n, d = x.shape
    lc = ctx.shape[1]
    assert lc == ROW_TILE and n % SCAN_TILE == 0 and d == D_MODEL
    s = n + lc
    n_lat_tiles = n // ROW_TILE
    depth = ada_w.shape[0]

    xall = jnp.concatenate([x, ctx], axis=1)
    c8 = jnp.concatenate([c, c_ctx[None], jnp.zeros((8 - b - 1, d), F32)], 0)
    mods = ada_modulation(c8, ada_w, ada_b)
    rope = _rope_tables(n, lc)
    moe_tm = _pick_tile(s, (1280, 1024, 512, 256))
    moe_w_gu_bf = moe_w_gu.astype(BF16)
    moe_w_dn_bf = moe_w_dn.astype(BF16)
    big_tm = _pick_tile(s, (1280, 1024, 512, 256))
    mid_tm = _pick_tile(s, (640, 512, 256))

    for i in range(depth):
        kind, j = i % N_MIXERS, i // N_MIXERS
        m = mods[i].reshape(8, ADA_CHUNKS, d)
        mod = jnp.stack([m[:b], jnp.broadcast_to(m[b], (b, ADA_CHUNKS, d))], axis=1)
        mod = [mod[:, :, k][:, :, None, :] for k in range(ADA_CHUNKS)]

        if kind == 0:
            lam_init = 0.8 - 0.6 * math.exp(-0.3 * i)
            w = da_w_in[j]
            w = jnp.concatenate([w[:, :d] * (DA_HEAD_DIM ** -0.5 * math.log2(math.e)), w[:, d:]], 1).astype(BF16)
            p_all = modulated_projection(xall, mod[1], mod[0], w, rope=rope, n_rope=2 * d,
                                         n_lat_rows=n, head_major=True, tm=big_tm)
            o = _flash_both(p_all, da_lambda[j], da_subln[j], lam_init, n, lc)
            xall = out_projection_post_norm([o], da_w_o[j].astype(BF16), xall, mod[2],
                                            ln_g[i, 0], ln_b[i, 0], n, tm=big_tm)
        elif kind == 1:
            w = gdn_w_in[j]
            n_main = 2 * GDN_KEY_DIM + 2 * GDN_VAL_DIM
            p_main, gates_raw = modulated_projection(
                xall, mod[1], mod[0], w[:, :n_main].astype(BF16),
                w_small=_pad_cols(w[:, n_main:], LANES).astype(BF16), n_lat_rows=n, tm=mid_tm)
            alog_row = _pad_cols(gdn_a_log[j].reshape(1, -1), LANES)
            dtb_row = _pad_cols(gdn_dt_bias[j].reshape(1, -1), LANES)
            qn, kn, vv, gates = gdn_prep(p_main, gates_raw, gdn_conv[j], alog_row, dtb_row, n_lat_tiles)
            hv, hk = GDN_V_HEADS, GDN_K_HEADS
            def per_khead(t):
                t = t.reshape(b, s, 2, hk, 2)
                return jnp.transpose(t, (0, 3, 1, 2, 4)).reshape(b, hk, s, 4)
            gcol = jnp.concatenate([per_khead(gates[..., :2 * hv]), per_khead(gates[..., 2 * hv:4 * hv])], -1)
            grow = jnp.swapaxes(gcol, 2, 3)
            zeros = jnp.zeros((b, hv, GDN_HEAD_DIM, GDN_HEAD_DIM), F32)
            of, ob, scf, scb = gdn_scan(qn, kn, vv, gcol, grow, zeros, zeros, nblk=lc // SCAN_TILE,
                                        off=n // SCAN_TILE)
            of, ob, _, _ = gdn_scan(qn, kn, vv, gcol, grow, scf, scb, nblk=n // SCAN_TILE, off=0,
                                    prev=(of, ob))
            xall = out_projection_post_norm([of, ob, p_main, gdn_norm[j]], gdn_w_o[j].astype(BF16), xall,
                                            mod[2], ln_g[i, 0], ln_b[i, 0], n,
                                            gated=(GDN_V_HEADS, GDN_HEAD_DIM, 2), tm=mid_tm)
        else:
            w = gla_w_in[j]
            n_main = 2 * GLA_KEY_DIM + 2 * GLA_VAL_DIM
            p_main, gr = modulated_projection(
                xall, mod[1], mod[0], w[:, :n_main].astype(BF16),
                w_small=_pad_cols(w[:, n_main:], LANES).astype(BF16), n_lat_rows=n, tm=big_tm)
            wg = gla_w_gate[j]
            wg_pad = jnp.zeros((2, LANES, GLA_KEY_DIM), F32)
            wg_pad = wg_pad.at[0, :GLA_GATE_RANK].set(wg[0]).at[1, GLA_GATE_RANK:2 * GLA_GATE_RANK].set(wg[1])
            zeros = jnp.zeros((b, GLA_HEADS, GLA_DV, GLA_DK), F32)
            of, ob, scf, scb = gla_scan(p_main, gr, wg_pad, gla_gate_b[j], zeros, zeros,
                                        nblk=lc // SCAN_TILE, off=n // SCAN_TILE)
            of, ob, _, _ = gla_scan(p_main, gr, wg_pad, gla_gate_b[j], scf, scb, nblk=n // SCAN_TILE, off=0,
                                    prev=(of, ob))
            xall = out_projection_post_norm([of, ob, p_main, gla_norm[j]], gla_w_o[j].astype(BF16), xall,
                                            mod[2], ln_g[i, 0], ln_b[i, 0], n,
                                            gated=(GLA_HEADS, GLA_DV, 2), tm=mid_tm)

        h, wd = moe_route(xall, mod[4], mod[3], moe_router[i].T, moe_router_b[i].reshape(-1, 1), n, tm=big_tm)
        popularity = jnp.sum((wd[..., :N_EXPERTS] != 0.0).astype(jnp.int32), axis=(0, 1))
        order = jnp.argsort(popularity).astype(jnp.int32)
        f = moe_grouped_experts(order, h, wd, moe_w_gu_bf, moe_w_dn_bf, i,
                                moe_ws_gu[i].astype(BF16), moe_ws_dn[i].astype(BF16), moe_tm)
        xall = post_norm_rows(xall, f, mod[5], ln_g[i, 1], ln_b[i, 1], n,
                              rows_out=n if i == depth - 1 else None)

    return xall
```

```python
import functools
import math

import jax
import jax.numpy as jnp
from jax import lax
from jax.experimental import pallas as pl
from jax.experimental.pallas import tpu as pltpu

F32 = jnp.float32
BF16 = jnp.bfloat16

D_MODEL = 1024
DEPTH = 4
GRID_W = 64
N_MIXERS = 3
DN_ALPHA = (2 * DEPTH) ** 0.25
LN_EPS = 1e-5
NORM_EPS = 1e-6
ADA_CHUNKS = 6

DA_HEADS = 8
DA_HEAD_DIM = 64
ROPE_BASE = 10000.0

GDN_K_HEADS = 8
GDN_V_HEADS = 16
GDN_HEAD_DIM = 128
GDN_KEY_DIM = 1024
GDN_VAL_DIM = 2048
GDN_CONV_W = 5
CHUNK = 64

GLA_HEADS = 4
GLA_KEY_DIM = 512
GLA_VAL_DIM = 1024
GLA_DK = 128
GLA_DV = 256
GLA_GATE_RANK = 16
GLA_TAU = 16.0

N_EXPERTS = 64
TOP_K = 8
EXPERT_FF = 256
ROUTE_SCALE = 2.5

LANES = 128
ROW_TILE = 256
SCAN_TILE = 256
GDN_HEADS_PER_STEP = 4
GLA_HEADS_PER_STEP = 4
HALO = 16
VMEM_LIMIT = 56 * 1024 * 1024


def _cparams(sem, flags=None):
    return pltpu.CompilerParams(dimension_semantics=sem, vmem_limit_bytes=VMEM_LIMIT, flags=flags)


def _dot(a, b):
    return jnp.dot(a, b, preferred_element_type=F32)


def _dot_nt(a, b):
    return lax.dot_general(a, b, (((1,), (1,)), ((), ())), preferred_element_type=F32)


def _dot_tn(a, b):
    return lax.dot_general(a, b, (((0,), (0,)), ((), ())), preferred_element_type=F32)


def _split2(a):
    hi = a.astype(BF16)
    lo = (a - hi.astype(F32)).astype(BF16)
    return hi, lo


def _split3(a):
    a0 = a.astype(BF16)
    r = a - a0.astype(F32)
    a1 = r.astype(BF16)
    a2 = (r - a1.astype(F32)).astype(BF16)
    return a0, a1, a2


def _dot3(a, b):
    a0, a1 = _split2(a)
    b0, b1 = _split2(b)
    return _dot(a0, b0) + (_dot(a0, b1) + _dot(a1, b0))


def _select_mod(ref, tile_rows, n_lat_rows):
    row = pl.program_id(1) * tile_rows + lax.broadcasted_iota(jnp.int32, (tile_rows, 1), 0)
    return jnp.where(row >= n_lat_rows, ref[0, 1], ref[0, 0])


def _silu(x):
    return x * jax.nn.sigmoid(x)


def _softplus(x):
    return jnp.maximum(x, 0.0) + jnp.log(1.0 + jnp.exp(-jnp.abs(x)))


def _ada_kernel(c_ref, w_ref, b_ref, o_ref):
    s = _silu(c_ref[...])
    o_ref[0] = _dot3(s, w_ref[0]) + b_ref[0]


def ada_modulation(c8, ada_w, ada_b):
    depth, d, n = ada_w.shape
    tn = 1536
    return pl.pallas_call(
        _ada_kernel,
        out_shape=jax.ShapeDtypeStruct((depth, 8, n), F32),
        grid=(depth, n // tn),
        in_specs=[
            pl.BlockSpec((8, d), lambda i, j: (0, 0)),
            pl.BlockSpec((1, d, tn), lambda i, j: (i, 0, j)),
            pl.BlockSpec((1, 1, tn), lambda i, j: (i, 0, j)),
        ],
        out_specs=pl.BlockSpec((1, 8, tn), lambda i, j: (i, 0, j)),
        compiler_params=_cparams(("arbitrary", "arbitrary")),
        name="ada_modulation",
    )(c8, ada_w, ada_b.reshape(depth, 1, n))


def _proj_kernel(*refs, n_main, n_rope, has_small, cn, head_major, n_lat_rows):
    it = iter(refs)
    x_ref, sc_ref, sh_ref, w_ref = next(it), next(it), next(it), next(it)
    ws_ref = next(it) if has_small else None
    cos_ref = next(it) if n_rope else None
    sin_ref = next(it) if n_rope else None
    o_ref = next(it)
    os_ref = next(it) if has_small else None

    tm = x_ref.shape[1]
    h = x_ref[0] * (1.0 + _select_mod(sc_ref, tm, n_lat_rows)) + _select_mod(sh_ref, tm, n_lat_rows)
    hb = h.astype(BF16)
    if n_rope:
        cos = cos_ref[...]
        sin = sin_ref[...]
        lane = lax.broadcasted_iota(jnp.int32, (tm, LANES), 1)
        low_half = (lane & 32) == 0
    for j in range(n_main // cn):
        p = _dot(hb, w_ref[:, j * cn:(j + 1) * cn])
        for g in range(cn // LANES):
            pg = p[:, g * LANES:(g + 1) * LANES]
            col = j * cn + g * LANES
            if col < n_rope:
                partner = jnp.where(low_half, pltpu.roll(pg, LANES - 32, 1), pltpu.roll(pg, 32, 1))
                pg = pg * cos + partner * sin
            if head_major:
                o_ref[0, col // LANES] = pg.astype(BF16)
            else:
                o_ref[0, :, col:col + LANES] = pg.astype(BF16)
    if has_small:
        os_ref[0] = _dot(hb, ws_ref[...])


def modulated_projection(x, sc, sh, w, w_small=None, rope=None, n_rope=0, n_lat_rows=0, head_major=False, tm=ROW_TILE):
    b, s, d = x.shape
    n_main = w.shape[1]
    grid = (b, s // tm)
    mod_spec = pl.BlockSpec((1, 2, 1, d), lambda bi, i: (bi, 0, 0, 0))
    in_specs = [pl.BlockSpec((1, tm, d), lambda bi, i: (bi, i, 0)), mod_spec, mod_spec,
                pl.BlockSpec((d, n_main), lambda bi, i: (0, 0))]
    args = [x, sc, sh, w]
    if head_major:
        out_shape = [jax.ShapeDtypeStruct((b, n_main // LANES, s, LANES), BF16)]
        out_specs = [pl.BlockSpec((1, n_main // LANES, tm, LANES), lambda bi, i: (bi, 0, i, 0))]
    else:
        out_shape = [jax.ShapeDtypeStruct((b, s, n_main), BF16)]
        out_specs = [pl.BlockSpec((1, tm, n_main), lambda bi, i: (bi, i, 0))]
    if w_small is not None:
        in_specs.append(pl.BlockSpec((d, LANES), lambda bi, i: (0, 0)))
        args.append(w_small)
        out_shape.append(jax.ShapeDtypeStruct((b, s, LANES), F32))
        out_specs.append(pl.BlockSpec((1, tm, LANES), lambda bi, i: (bi, i, 0)))
    if n_rope:
        tab = pl.BlockSpec((tm, LANES), lambda bi, i: (i, 0))
        in_specs += [tab, tab]
        args += [rope[0], rope[1]]
    kern = functools.partial(_proj_kernel, n_main=n_main, n_rope=n_rope,
                             has_small=w_small is not None, cn=512, head_major=head_major,
                             n_lat_rows=n_lat_rows)
    out = pl.pallas_call(
        kern, out_shape=out_shape, grid=grid, in_specs=in_specs, out_specs=out_specs,
        compiler_params=_cparams(("arbitrary", "arbitrary")), name="modulated_projection",
    )(*args)
    return out if w_small is not None else out[0]


NEG_INIT = -1e30
FLASH_ROW_BLOCK = 512
FLASH_LOOKAHEAD = 1


def _flash_kernel(lam_ref, q_ref, k_ref, v_ref, sub_ref, *rest, lam_init, nk, aliased):
    if aliased:
        rest = rest[1:]
    o_ref, m_sc, l_sc, acc_sc, s_ring = rest
    ki = pl.program_id(3)
    tq = m_sc.shape[1]
    tk = k_ref.shape[2]
    nring = s_ring.shape[0]

    @pl.when(ki == 0)
    def _():
        m_sc[...] = jnp.full(m_sc.shape, NEG_INIT, F32)
        l_sc[...] = jnp.zeros(l_sc.shape, F32)
        acc_sc[...] = jnp.zeros(acc_sc.shape, F32)

    q = q_ref[0, 0]
    k = k_ref[0, 0]
    v = v_ref[0, 0]
    hd = DA_HEAD_DIM
    rb = min(tq, FLASH_ROW_BLOCK)
    kc = 2 * LANES
    blocks = [(c, r0) for c in range(2) for r0 in range(0, tq, rb)]
    for i in range(len(blocks) + FLASH_LOOKAHEAD):
        if i < len(blocks):
            c, r0 = blocks[i]
            s_ring[i % nring] = _dot_nt(q[r0:r0 + rb, c * hd:(c + 1) * hd], k[:, c * hd:(c + 1) * hd])
        j = i - FLASH_LOOKAHEAD
        if j < 0:
            continue
        c, r0 = blocks[j]
        rows = slice(r0, r0 + rb)
        s_blk = s_ring.at[j % nring]
        m_prev = m_sc[c, rows, :]
        m_new = jnp.maximum(m_prev, jnp.max(s_blk[...], axis=-1, keepdims=True))
        alpha = jnp.exp2(m_prev - m_new)
        m2 = jnp.concatenate([m_new, m_new], axis=-1)
        lsum = None
        pv = None
        for t in range(tk // kc):
            pj = jnp.exp2(s_blk[:, t * kc:(t + 1) * kc] - m2)
            lj = pj[:, :LANES] + pj[:, LANES:]
            lsum = lj if lsum is None else lsum + lj
            d = _dot(pj.astype(BF16), v[t * kc:(t + 1) * kc, :])
            pv = d if pv is None else pv + d
        l_sc[c, rows, :] = alpha * l_sc[c, rows, :] + lsum
        acc_sc[c, rows, :] = alpha * acc_sc[c, rows, :] + pv
        m_sc[c, rows, :] = m_new

    @pl.when(ki == nk - 1)
    def _():
        lv = lam_ref[...]
        lam = (jnp.exp(jnp.sum(lv[0:1] * lv[1:2], axis=-1, keepdims=True))
               - jnp.exp(jnp.sum(lv[2:3] * lv[3:4], axis=-1, keepdims=True)) + lam_init)
        l0 = jnp.sum(l_sc[0], axis=-1, keepdims=True)
        l1 = jnp.sum(l_sc[1], axis=-1, keepdims=True)
        o = acc_sc[0] / l0 - lam * (acc_sc[1] / l1)
        ms = jnp.mean(o * o, axis=-1, keepdims=True)
        o = o * lax.rsqrt(ms + NORM_EPS) * sub_ref[...] * (1.0 - lam_init)
        o_ref[0] = o.astype(BF16)


def diff_flash_attention(p_all, lam_vec, subln, lam_init, *, tq, tk, nq, nk, q_off, k_off, prev=None):
    b, _, s, _ = p_all.shape
    hh = DA_HEADS
    in_specs = [
        pl.BlockSpec((4, DA_HEAD_DIM), lambda bi, h, qi, ki: (0, 0)),
        pl.BlockSpec((1, 1, tq, LANES), lambda bi, h, qi, ki: (bi, h, qi + q_off, 0)),
        pl.BlockSpec((1, 1, tk, LANES), lambda bi, h, qi, ki: (bi, hh + h, ki + k_off, 0)),
        pl.BlockSpec((1, 1, tk, LANES), lambda bi, h, qi, ki: (bi, 2 * hh + h, ki + k_off, 0)),
        pl.BlockSpec((1, LANES), lambda bi, h, qi, ki: (0, 0)),
    ]
    args = [lam_vec, p_all, p_all, p_all, subln.reshape(1, LANES)]
    aliases = {}
    if prev is not None:
        in_specs.append(pl.BlockSpec(memory_space=pl.ANY))
        args.append(prev)
        aliases = {5: 0}
    kern = functools.partial(_flash_kernel, lam_init=lam_init, nk=nk, aliased=prev is not None)
    return pl.pallas_call(
        kern,
        out_shape=jax.ShapeDtypeStruct((b, s, hh * LANES), BF16),
        grid=(b, hh, nq, nk),
        in_specs=in_specs,
        out_specs=pl.BlockSpec((1, tq, LANES), lambda bi, h, qi, ki: (bi, qi + q_off, h)),
        scratch_shapes=[pltpu.VMEM((2, tq, LANES), F32), pltpu.VMEM((2, tq, LANES), F32),
                        pltpu.VMEM((2, tq, LANES), F32),
                        pltpu.VMEM((FLASH_LOOKAHEAD + 1, min(tq, FLASH_ROW_BLOCK), tk), F32)],
        input_output_aliases=aliases,
        compiler_params=_cparams(("arbitrary", "arbitrary", "arbitrary", "arbitrary")),
        name="diff_flash_attention",
    )(*args)


def _post_norm(x, y, gate, lg, lb):
    r = DN_ALPHA * x + gate * y
    mu = jnp.mean(r, axis=-1, keepdims=True)
    rc = r - mu
    var = jnp.mean(rc * rc, axis=-1, keepdims=True)
    return rc * lax.rsqrt(var + LN_EPS) * lg + lb


def _outproj_kernel(*refs, gated, n_heads, dh, n_lat_rows):
    if gated:
        of_ref, ob_ref, z_ref, ng_ref, w_ref, x_ref, gate_ref, lg_ref, lb_ref, out_ref = refs
        y = None
        for h in range(n_heads):
            sl = slice(h * dh, (h + 1) * dh)
            o = of_ref[0, :, sl].astype(F32) + ob_ref[0, :, sl].astype(F32)
            ms = jnp.mean(o * o, axis=-1, keepdims=True)
            o = o * lax.rsqrt(ms + NORM_EPS) * ng_ref[...] * _silu(z_ref[0, :, sl].astype(F32))
            t = _dot(o.astype(BF16), w_ref[sl, :])
            y = t if y is None else y + t
    else:
        o_ref, w_ref, x_ref, gate_ref, lg_ref, lb_ref, out_ref = refs
        y = _dot(o_ref[0], w_ref[...])
    gate = _select_mod(gate_ref, x_ref.shape[1], n_lat_rows)
    out_ref[0] = _post_norm(x_ref[0], y, gate, lg_ref[...], lb_ref[...])


def out_projection_post_norm(o_args, w_o, x, gate, lg, lb, n_lat_rows, gated=None, tm=ROW_TILE):
    b, s, d = x.shape
    kdim = w_o.shape[0]
    row = lambda width, cb=0: pl.BlockSpec((1, tm, width), lambda bi, i: (bi, i, cb))
    full = lambda shape: pl.BlockSpec(shape, lambda bi, i: (0,) * len(shape))
    if gated is None:
        in_specs = [row(kdim)]
        args = list(o_args)
        kern = functools.partial(_outproj_kernel, gated=False, n_heads=0, dh=0, n_lat_rows=n_lat_rows)
    else:
        n_heads, dh, z_cb = gated
        of, ob, z, ng = o_args
        in_specs = [row(kdim), row(kdim), row(kdim, z_cb), full((1, dh))]
        args = [of, ob, z, ng.reshape(1, dh)]
        kern = functools.partial(_outproj_kernel, gated=True, n_heads=n_heads, dh=dh, n_lat_rows=n_lat_rows)
    in_specs += [full((kdim, d)), row(d),
                 pl.BlockSpec((1, 2, 1, d), lambda bi, i: (bi, 0, 0, 0)),
                 full((1, d)), full((1, d))]
    args += [w_o, x, gate, lg.reshape(1, d), lb.reshape(1, d)]
    return pl.pallas_call(
        kern, out_shape=jax.ShapeDtypeStruct((b, s, d), F32), grid=(b, s // tm),
        in_specs=in_specs, out_specs=row(d),
        compiler_params=_cparams(("arbitrary", "arbitrary")), name="out_projection_post_norm",
    )(*args)


def _gdn_prep_kernel(cur_ref, prev_ref, next_ref, gate_ref, cw_ref, alog_ref, dtb_ref,
                     q_ref, k_ref, v_ref, g_ref, *, n_lat_tiles, n_tiles, cn):
    i = pl.program_id(1)
    tm = cur_ref.shape[1]
    first = jnp.logical_or(i == 0, i == n_lat_tiles)
    last = jnp.logical_or(i == n_lat_tiles - 1, i == n_tiles - 1)
    pmask = jnp.where(first, 0.0, 1.0)
    nmask = jnp.where(last, 0.0, 1.0)
    pad = GDN_CONV_W // 2
    dh = GDN_HEAD_DIM
    n_qk = 2 * GDN_KEY_DIM
    r = lax.broadcasted_iota(jnp.int32, (tm, tm), 0)
    c = lax.broadcasted_iota(jnp.int32, (tm, tm), 1)
    hr = lax.broadcasted_iota(jnp.int32, (HALO, HALO), 0)
    hc = lax.broadcasted_iota(jnp.int32, (HALO, HALO), 1)
    shifts = {}
    for j in range(GDN_CONV_W):
        dlt = j - pad
        if dlt == 0:
            continue
        inner = jnp.where(c == r + dlt, 1.0, 0.0).astype(BF16)
        if dlt < 0:
            edge = jnp.where(hc == HALO + hr + dlt, 1.0, 0.0).astype(BF16)
        else:
            edge = jnp.where(hc == hr + dlt - HALO, 1.0, 0.0).astype(BF16)
        shifts[j] = (inner, edge)
    for cc in range(cur_ref.shape[2] // cn):
        cs = slice(cc * cn, (cc + 1) * cn)
        cur = cur_ref[0, :, cs]
        prev = prev_ref[0, :, cs]
        nxt = next_ref[0, :, cs]
        acc = None
        top = None
        bot = None
        for j in range(GDN_CONV_W):
            wj = cw_ref[j:j + 1, cs]
            if j == pad:
                t = cur.astype(F32) * wj
            else:
                inner, edge = shifts[j]
                t = _dot(inner, cur) * wj
                if j < pad:
                    e = _dot(edge, prev) * wj
                    top = e if top is None else top + e
                else:
                    e = _dot(edge, nxt) * wj
                    bot = e if bot is None else bot + e
            acc = t if acc is None else acc + t
        acc = jnp.concatenate([acc[:HALO] + top * pmask, acc[HALO:tm - HALO], acc[tm - HALO:] + bot * nmask], axis=0)
        y = _silu(acc)
        for g in range(cn // dh):
            col = cc * cn + g * dh
            yg = y[:, g * dh:(g + 1) * dh]
            if col < n_qk:
                yg = yg * lax.rsqrt(jnp.sum(yg * yg, axis=-1, keepdims=True) + NORM_EPS)
                if col < GDN_KEY_DIM:
                    q_ref[0, col // dh] = (yg * (dh ** -0.5)).astype(BF16)
                else:
                    k_ref[0, (col - GDN_KEY_DIM) // dh] = yg.astype(BF16)
            else:
                v_ref[0, (col - n_qk) // dh] = yg.astype(BF16)
    a = gate_ref[0]
    lane = lax.broadcasted_iota(jnp.int32, a.shape, 1)
    gdec = -jnp.exp(alog_ref[...]) * _softplus(a + dtb_ref[...])
    g_ref[0] = jnp.where(lane < 2 * GDN_V_HEADS, gdec, jax.nn.sigmoid(a))


def gdn_prep(p_main, gates_raw, conv_w, alog_row, dtb_row, n_lat_tiles):
    b, s, _ = p_main.shape
    tm = ROW_TILE
    nt = s // tm
    nch = 2 * GDN_KEY_DIM + GDN_VAL_DIM
    hpt = tm // HALO
    nh = s // HALO
    kern = functools.partial(_gdn_prep_kernel, n_lat_tiles=n_lat_tiles, n_tiles=nt, cn=512)
    head_out = lambda nheads: pl.BlockSpec((1, nheads, tm, GDN_HEAD_DIM), lambda bi, i: (bi, 0, i, 0))
    return pl.pallas_call(
        kern,
        out_shape=[jax.ShapeDtypeStruct((b, GDN_K_HEADS, s, GDN_HEAD_DIM), BF16),
                   jax.ShapeDtypeStruct((b, GDN_K_HEADS, s, GDN_HEAD_DIM), BF16),
                   jax.ShapeDtypeStruct((b, GDN_V_HEADS, s, GDN_HEAD_DIM), BF16),
                   jax.ShapeDtypeStruct((b, s, LANES), F32)],
        grid=(b, nt),
        in_specs=[
            pl.BlockSpec((1, tm, nch), lambda bi, i: (bi, i, 0)),
            pl.BlockSpec((1, HALO, nch), lambda bi, i: (bi, jnp.maximum(i * hpt - 1, 0), 0)),
            pl.BlockSpec((1, HALO, nch), lambda bi, i: (bi, jnp.minimum((i + 1) * hpt, nh - 1), 0)),
            pl.BlockSpec((1, tm, LANES), lambda bi, i: (bi, i, 0)),
            pl.BlockSpec((GDN_CONV_W, nch), lambda bi, i: (0, 0)),
            pl.BlockSpec((1, LANES), lambda bi, i: (0, 0)),
            pl.BlockSpec((1, LANES), lambda bi, i: (0, 0)),
        ],
        out_specs=[head_out(GDN_K_HEADS), head_out(GDN_K_HEADS), head_out(GDN_V_HEADS),
                   pl.BlockSpec((1, tm, LANES), lambda bi, i: (bi, i, 0))],
        compiler_params=_cparams(("arbitrary", "arbitrary")),
        name="gdn_prep",
    )(p_main, p_main, p_main, gates_raw, conv_w, alog_row, dtb_row)


def _chunk_masks(n, reverse):
    r = lax.broadcasted_iota(jnp.int32, (n, n), 0)
    c = lax.broadcasted_iota(jnp.int32, (n, n), 1)
    if reverse:
        return r <= c, r < c
    return r >= c, r > c


def _block_cumsum_mats(ct, reverse):
    r = lax.broadcasted_iota(jnp.int32, (ct, ct), 0)
    c = lax.broadcasted_iota(jnp.int32, (ct, ct), 1)
    same = (r // CHUNK) == (c // CHUNK)
    lower = jnp.logical_and(same, c <= r)
    upper = jnp.logical_and(same, c >= r)
    lo = jnp.where(lower, 1.0, 0.0).astype(BF16)
    up = jnp.where(upper, 1.0, 0.0).astype(BF16)
    return (up, lo) if reverse else (lo, up)


def _cumsum_cols(mat, x):
    x0, x1, x2 = _split3(x)
    return _dot(mat, x0) + (_dot(mat, x1) + _dot(mat, x2))


def _cumsum_rows(x, mat):
    x0, x1, x2 = _split3(x)
    return _dot(x0, mat) + (_dot(x1, mat) + _dot(x2, mat))


def _gdn_block(dirs):
    dh = GDN_HEAD_DIM
    eye = jnp.where(lax.broadcasted_iota(jnp.int32, (CHUNK, CHUNK), 0)
                    == lax.broadcasted_iota(jnp.int32, (CHUNK, CHUNK), 1), 1.0, 0.0).astype(F32)
    chains = []
    for q_ref, k_ref, v_ref, gc_ref, gr_ref, o_ref, s_sc, reverse in dirs:
        ct = q_ref.shape[2]
        goff = 2 if reverse else 0
        m_col, m_row = _block_cumsum_mats(ct, reverse)
        tril, strict = _chunk_masks(CHUNK, reverse)
        for kh in range(q_ref.shape[1]):
            gcol = gc_ref[0, kh]
            grow = gr_ref[0, kh]
            gcum_c = _cumsum_cols(m_col, gcol)
            gcum_r = _cumsum_rows(grow, m_row)
            for c in range(ct // CHUNK):
                rs = slice(c * CHUNK, (c + 1) * CHUNK)
                last = c * CHUNK if reverse else (c + 1) * CHUNK - 1
                q = q_ref[0, kh, rs, :]
                k = k_ref[0, kh, rs, :]
                kk = _dot_nt(k, k)
                qk = _dot_nt(q, k)
                for hh in range(2):
                    gi = goff + hh
                    chains.append(dict(
                        q=q, k=k, kk=kk, qk=qk, rs=rs, c=c, hh=2 * kh + hh, reverse=reverse, tril=tril,
                        strict=strict, v_ref=v_ref, o_ref=o_ref, s_sc=s_sc,
                        gc=gcum_c[rs, gi:gi + 1], gr=gcum_r[gi:gi + 1, rs],
                        glast=gcum_c[last:last + 1, gi:gi + 1], beta=gcol[rs, 4 + gi:5 + gi]))
    for ch in chains:
        tril = ch["tril"]
        ch["decay"] = jnp.where(tril, jnp.exp(jnp.where(tril, ch["gc"] - ch["gr"], 0.0)), 0.0)
        a = -jnp.where(ch["strict"], ch["kk"] * ch["beta"] * ch["decay"], 0.0)
        ch["tmat"] = eye + a
        ch["pw"] = a
    for _ in range(5):
        for ch in chains:
            pwb = ch["pw"].astype(BF16)
            ch["pw"] = _dot(pwb, pwb)
        for ch in chains:
            ch["tmat"] = ch["tmat"] + _dot(ch["tmat"].astype(BF16), ch["pw"].astype(BF16))
    for ch in chains:
        beta = ch["beta"]
        eg = jnp.exp(ch["gc"])
        vb = ch["v_ref"][0, ch["hh"], ch["rs"], :].astype(F32) * beta
        kbg = ch["k"].astype(F32) * (beta * eg)
        uw = _dot(ch["tmat"].astype(BF16), jnp.concatenate([vb, kbg], axis=-1).astype(BF16))
        ch["u"] = uw[:, :dh]
        ch["w"] = uw[:, dh:].astype(BF16)
        ch["eg"] = eg
        ch["attn"] = jnp.where(ch["tril"], ch["qk"] * ch["decay"], 0.0).astype(BF16)
    nchunk = max(ch["c"] for ch in chains) + 1
    for step in range(nchunk):
        cur = [ch for ch in chains if ch["c"] == (nchunk - 1 - step if ch["reverse"] else step)]
        for ch in cur:
            state = ch["s_sc"][ch["hh"]]
            sb = state.astype(BF16)
            ch["state"] = state
            ch["ws"] = _dot(ch["w"], sb)
            ch["qs"] = _dot(ch["q"], sb)
        for ch in cur:
            v_new = ch["u"] - ch["ws"]
            ch["kgv"] = (v_new * jnp.exp(ch["glast"] - ch["gc"])).astype(BF16)
            ch["o"] = ch["eg"] * ch["qs"] + _dot(ch["attn"], v_new.astype(BF16))
        for ch in cur:
            ch["s_sc"][ch["hh"]] = ch["state"] * jnp.exp(ch["glast"]) + _dot_tn(ch["k"], ch["kgv"])
            ch["o_ref"][0, ch["rs"], ch["hh"] * dh:(ch["hh"] + 1) * dh] = ch["o"].astype(BF16)


def _gdn_scan_kernel(qf, kf, vf, gcf, grf, qb, kb, vb, gcb, grb, s0f, s0b, *rest, nblk, aliased):
    if aliased:
        rest = rest[2:]
    of_ref, ob_ref, sff, sfb, sf_sc, sb_sc = rest
    j = pl.program_id(2)

    @pl.when(j == 0)
    def _():
        sf_sc[...] = s0f[0]
        sb_sc[...] = s0b[0]

    _gdn_block([(qf, kf, vf, gcf, grf, of_ref, sf_sc, False),
                (qb, kb, vb, gcb, grb, ob_ref, sb_sc, True)])

    @pl.when(j == nblk - 1)
    def _():
        sff[0] = sf_sc[...]
        sfb[0] = sb_sc[...]


def gdn_scan(qn, kn, vv, gcol, grow, s0f, s0b, *, nblk, off, prev=None):
    b, _, s, dh = qn.shape
    ct = SCAN_TILE
    g = GDN_HEADS_PER_STEP
    fwd = lambda j: j + off
    bwd = lambda j: nblk - 1 - j + off
    def specs(pos):
        return [
            pl.BlockSpec((1, g, ct, dh), lambda bi, h, j: (bi, h, pos(j), 0)),
            pl.BlockSpec((1, g, ct, dh), lambda bi, h, j: (bi, h, pos(j), 0)),
            pl.BlockSpec((1, 2 * g, ct, dh), lambda bi, h, j: (bi, h, pos(j), 0)),
            pl.BlockSpec((1, g, ct, 8), lambda bi, h, j: (bi, h, pos(j), 0)),
            pl.BlockSpec((1, g, 8, ct), lambda bi, h, j: (bi, h, 0, pos(j))),
        ]
    st_spec = pl.BlockSpec((1, 2 * g, dh, dh), lambda bi, h, j: (bi, h, 0, 0))
    in_specs = specs(fwd) + specs(bwd) + [st_spec, st_spec]
    args = [qn, kn, vv, gcol, grow] * 2 + [s0f, s0b]
    aliases = {}
    if prev is not None:
        in_specs += [pl.BlockSpec(memory_space=pl.ANY)] * 2
        args += list(prev)
        aliases = {12: 0, 13: 1}
    o_shape = jax.ShapeDtypeStruct((b, s, GDN_VAL_DIM), BF16)
    st_shape = jax.ShapeDtypeStruct((b, GDN_V_HEADS, dh, dh), F32)
    kern = functools.partial(_gdn_scan_kernel, nblk=nblk, aliased=prev is not None)
    return pl.pallas_call(
        kern,
        out_shape=[o_shape, o_shape, st_shape, st_shape],
        grid=(b, GDN_K_HEADS // g, nblk),
        in_specs=in_specs,
        out_specs=[pl.BlockSpec((1, ct, 2 * g * dh), lambda bi, h, j: (bi, fwd(j), h)),
                   pl.BlockSpec((1, ct, 2 * g * dh), lambda bi, h, j: (bi, bwd(j), h)),
                   st_spec, st_spec],
        scratch_shapes=[pltpu.VMEM((2 * g, dh, dh), F32), pltpu.VMEM((2 * g, dh, dh), F32)],
        input_output_aliases=aliases,
        compiler_params=_cparams(("arbitrary", "arbitrary", "arbitrary")),
        name="gdn_scan",
    )(*args)


def _gla_block(dirs, wg_ref, gb_ref):
    chains = []
    pre = []
    for q_ref, k_ref, v_ref, gr_ref, o_ref, st_sc, z in dirs:
        pre.append(_dot3(gr_ref[0], wg_ref[z]) + gb_ref[z:z + 1, :])
    for (q_ref, k_ref, v_ref, gr_ref, o_ref, st_sc, z), logit in zip(dirs, pre):
        reverse = z == 1
        ct = q_ref.shape[1]
        glog = -_softplus(-logit) / GLA_TAU
        m_col, _ = _block_cumsum_mats(ct, reverse)
        bcum = _cumsum_cols(m_col, glog)
        tril, _ = _chunk_masks(CHUNK, reverse)
        for hh in range(st_sc.shape[0]):
            ks = slice(hh * GLA_DK, (hh + 1) * GLA_DK)
            vs = slice(hh * GLA_DV, (hh + 1) * GLA_DV)
            for c in range(ct // CHUNK):
                rs = slice(c * CHUNK, (c + 1) * CHUNK)
                last = c * CHUNK if reverse else (c + 1) * CHUNK - 1
                bc = bcum[rs, ks]
                bl = bcum[last:last + 1, ks]
                qf = q_ref[0, rs, ks].astype(F32) * (GLA_DK ** -0.5)
                kf = k_ref[0, rs, ks].astype(F32)
                chains.append(dict(
                    c=c, rs=rs, hh=hh, vs=vs, reverse=reverse, tril=tril, o_ref=o_ref, st_sc=st_sc,
                    v=v_ref[0, rs, vs],
                    qe=(qf * jnp.exp(bc)).astype(BF16), ke=(kf * jnp.exp(-bc)).astype(BF16),
                    kg=(kf * jnp.exp(bl - bc)).astype(BF16), gl=jnp.exp(bl)))
    for ch in chains:
        ch["attn"] = jnp.where(ch["tril"], _dot_nt(ch["qe"], ch["ke"]), 0.0).astype(BF16)
    for ch in chains:
        ch["o"] = _dot(ch["attn"], ch["v"])
        ch["kv"] = _dot_tn(ch["v"], ch["kg"])
    nchunk = max(ch["c"] for ch in chains) + 1
    for step in range(nchunk):
        cur = [ch for ch in chains if ch["c"] == (nchunk - 1 - step if ch["reverse"] else step)]
        for ch in cur:
            st = ch["st_sc"][ch["hh"]]
            ch["o"] = ch["o"] + _dot_nt(ch["qe"], st.astype(BF16))
            ch["st_sc"][ch["hh"]] = st * ch["gl"] + ch["kv"]
        for ch in cur:
            ch["o_ref"][0, ch["rs"], ch["vs"]] = ch["o"].astype(BF16)


def _gla_scan_kernel(qf, kf, vf, grf, qb, kb, vb, grb, wg, gb, s0f, s0b, *rest, nblk, aliased):
    if aliased:
        rest = rest[2:]
    of_ref, ob_ref, sff, sfb, sf_sc, sb_sc = rest
    j = pl.program_id(2)

    @pl.when(j == 0)
    def _():
        sf_sc[...] = s0f[0]
        sb_sc[...] = s0b[0]

    _gla_block([(qf, kf, vf, grf, of_ref, sf_sc, 0), (qb, kb, vb, grb, ob_ref, sb_sc, 1)], wg, gb)

    @pl.when(j == nblk - 1)
    def _():
        sff[0] = sf_sc[...]
        sfb[0] = sb_sc[...]


def gla_scan(p_main, gr, wg_pad, gate_b, s0f, s0b, *, nblk, off, prev=None):
    b, s, _ = p_main.shape
    ct = SCAN_TILE
    nh, dk, dv = GLA_HEADS, GLA_DK, GLA_DV
    g = GLA_HEADS_PER_STEP
    ng = nh // g
    fwd = lambda j: j + off
    bwd = lambda j: nblk - 1 - j + off
    def specs(pos):
        return [
            pl.BlockSpec((1, ct, g * dk), lambda bi, h, j: (bi, pos(j), h)),
            pl.BlockSpec((1, ct, g * dk), lambda bi, h, j: (bi, pos(j), ng + h)),
            pl.BlockSpec((1, ct, g * dv), lambda bi, h, j: (bi, pos(j), ng + h)),
            pl.BlockSpec((1, ct, LANES), lambda bi, h, j: (bi, pos(j), 0)),
        ]
    st_spec = pl.BlockSpec((1, g, dv, dk), lambda bi, h, j: (bi, h, 0, 0))
    in_specs = specs(fwd) + specs(bwd) + [
        pl.BlockSpec((2, LANES, g * dk), lambda bi, h, j: (0, 0, h)),
        pl.BlockSpec((2, g * dk), lambda bi, h, j: (0, h)),
        st_spec, st_spec]
    args = [p_main, p_main, p_main, gr] * 2 + [wg_pad, gate_b, s0f, s0b]
    aliases = {}
    if prev is not None:
        in_specs += [pl.BlockSpec(memory_space=pl.ANY)] * 2
        args += list(prev)
        aliases = {12: 0, 13: 1}
    o_shape = jax.ShapeDtypeStruct((b, s, GLA_VAL_DIM), BF16)
    st_shape = jax.ShapeDtypeStruct((b, nh, dv, dk), F32)
    kern = functools.partial(_gla_scan_kernel, nblk=nblk, aliased=prev is not None)
    return pl.pallas_call(
        kern,
        out_shape=[o_shape, o_shape, st_shape, st_shape],
        grid=(b, ng, nblk),
        in_specs=in_specs,
        out_specs=[pl.BlockSpec((1, ct, g * dv), lambda bi, h, j: (bi, fwd(j), h)),
                   pl.BlockSpec((1, ct, g * dv), lambda bi, h, j: (bi, bwd(j), h)),
                   st_spec, st_spec],
        scratch_shapes=[pltpu.VMEM((g, dv, dk), F32), pltpu.VMEM((g, dv, dk), F32)],
        input_output_aliases=aliases,
        compiler_params=_cparams(("arbitrary", "arbitrary", "arbitrary")),
        name="gla_scan",
    )(*args)


def _router_kernel(x_ref, sc_ref, sh_ref, rwt_ref, rb_ref, h_ref, wd_ref, *, n_lat_rows):
    tm = x_ref.shape[1]
    h = x_ref[0] * (1.0 + _select_mod(sc_ref, tm, n_lat_rows)) + _select_mod(sh_ref, tm, n_lat_rows)
    h_ref[0] = h.astype(BF16)
    w0, w1 = _split2(rwt_ref[...])
    h0, h1 = _split2(h)
    scores = jax.nn.sigmoid(_dot_nt(w0, h0) + (_dot_nt(w0, h1) + _dot_nt(w1, h0)))
    ne = scores.shape[0]
    row = lax.broadcasted_iota(jnp.int32, scores.shape, 0)
    neg = jnp.float32(-jnp.inf)
    sel = scores + rb_ref[...]
    chosen = jnp.zeros(scores.shape, jnp.bool_)
    for _ in range(TOP_K):
        mx = jnp.max(sel, axis=0, keepdims=True)
        first = jnp.min(jnp.where(sel == mx, row, ne), axis=0, keepdims=True)
        pick = row == first
        chosen = jnp.logical_or(chosen, pick)
        sel = jnp.where(pick, neg, sel)
    picked = jnp.where(chosen, scores, 0.0)
    wt = picked / jnp.sum(picked, axis=0, keepdims=True) * ROUTE_SCALE
    eye = jnp.where(lax.broadcasted_iota(jnp.int32, (ne, LANES), 0)
                    == lax.broadcasted_iota(jnp.int32, (ne, LANES), 1), 1.0, 0.0).astype(BF16)
    t0, t1, t2 = _split3(wt)
    wd_ref[0] = _dot_tn(t0, eye) + (_dot_tn(t1, eye) + _dot_tn(t2, eye))


def moe_route(x, sc, sh, rw_t, rb_col, n_lat_rows, tm=ROW_TILE):
    b, s, d = x.shape
    mod_spec = pl.BlockSpec((1, 2, 1, d), lambda bi, i: (bi, 0, 0, 0))
    return pl.pallas_call(
        functools.partial(_router_kernel, n_lat_rows=n_lat_rows),
        out_shape=[jax.ShapeDtypeStruct((b, s, d), BF16), jax.ShapeDtypeStruct((b, s, LANES), F32)],
        grid=(b, s // tm),
        in_specs=[pl.BlockSpec((1, tm, d), lambda bi, i: (bi, i, 0)), mod_spec, mod_spec,
                  pl.BlockSpec((N_EXPERTS, d), lambda bi, i: (0, 0)),
                  pl.BlockSpec((N_EXPERTS, 1), lambda bi, i: (0, 0))],
        out_specs=[pl.BlockSpec((1, tm, d), lambda bi, i: (bi, i, 0)),
                   pl.BlockSpec((1, tm, LANES), lambda bi, i: (bi, i, 0))],
        compiler_params=_cparams(("arbitrary", "arbitrary")),
        name="moe_router",
    )(x, sc, sh, rw_t, rb_col)


MOE_SUB = 256
MOE_CAP = 64
MOE_GRP = 4


def _moe_grouped_kernel(order_ref, h_ref, wd_ref, *rest):
    wgu_refs = rest[:MOE_GRP]
    wdn_refs = rest[MOE_GRP:2 * MOE_GRP]
    wsgu_ref, wsdn_ref, f_ref, acc, rrm, wrm, cmax = rest[2 * MOE_GRP:]
    g = pl.program_id(2)
    n_grp = pl.num_programs(2)
    tm = acc.shape[0]
    nsub = tm // MOE_SUB
    sub, cap, nslot = MOE_SUB, MOE_CAP, MOE_GRP * MOE_CAP
    lane = lax.broadcasted_iota(jnp.int32, (1, LANES), 1)

    @pl.when(g == 0)
    def _():
        r = lax.broadcasted_iota(jnp.int32, (sub, sub), 0)
        c = lax.broadcasted_iota(jnp.int32, (sub, sub), 1)
        after = jnp.where(r < c, 1.0, 0.0).astype(BF16)
        ident = jnp.where(r == c, 1.0, 0.0).astype(BF16)
        cm = jnp.zeros((1, LANES), F32)
        for u in range(nsub):
            rs = slice(u * sub, (u + 1) * sub)
            hu = h_ref[0, rs, :]
            gu = _dot(hu, wsgu_ref[...])
            act = _silu(gu[:, :EXPERT_FF]) * gu[:, EXPERT_FF:]
            acc[rs, :] = _dot(act.astype(BF16), wsdn_ref[...])
            wd = wd_ref[0, rs, :]
            active = jnp.logical_and(wd != 0.0, lane < N_EXPERTS)
            a = jnp.where(active, 1.0, 0.0)
            ab = a.astype(BF16)
            rank_r = _dot_tn(ab, after)
            a_r = _dot_tn(ab, ident)
            rrm[u] = jnp.where(a_r > 0.5, rank_r, -1.0)
            wrm[u] = _dot_tn(wd.astype(BF16), ident)
            cm = jnp.maximum(cm, jnp.sum(a, axis=0, keepdims=True))
        cmax[...] = jnp.broadcast_to(cm, cmax.shape)

    experts = [order_ref[g * MOE_GRP + k] for k in range(MOE_GRP)]
    in_group = functools.reduce(jnp.logical_or, [lane == e for e in experts])
    n_max = jnp.max(jnp.where(in_group, cmax[0:1, :], 0.0))
    n_pass = (n_max.astype(jnp.int32) + (cap - 1)) // cap

    def slot_expert(l):
        return sum(((l >= k * cap).astype(jnp.int32) for k in range(1, MOE_GRP)), jnp.zeros_like(l))

    l_col = lax.broadcasted_iota(jnp.int32, (sub, 1), 0)
    j_col = jnp.where(l_col < nslot, l_col - cap * slot_expert(l_col), -1000).astype(F32)

    def per_slot(table, u, fill):
        rows = [jnp.broadcast_to(table[u, pl.ds(experts[k], 1), :], (cap, sub)) for k in range(MOE_GRP)]
        if nslot < sub:
            rows.append(jnp.full((sub - nslot, sub), fill, F32))
        return jnp.concatenate(rows, axis=0)

    def one_pass(p, carry):
        base = (p * cap).astype(F32)
        xg, expanders = [], []
        for u in range(nsub):
            rs = slice(u * sub, (u + 1) * sub)
            match = per_slot(rrm, u, -1.0) == j_col + base
            gather = jnp.where(match, 1.0, 0.0).astype(BF16)
            expanders.append(jnp.where(match, per_slot(wrm, u, 0.0), 0.0).astype(BF16))
            xg.append(_dot(gather, h_ref[0, rs, :]).astype(BF16))
        ys = []
        for k in range(MOE_GRP):
            es = slice(k * cap, (k + 1) * cap)
            x_e = jnp.concatenate([xg[u][es] for u in range(nsub)], axis=0)
            gu = _dot(x_e, wgu_refs[k][0, 0])
            act = _silu(gu[:, :EXPERT_FF]) * gu[:, EXPERT_FF:]
            ys.append(_dot(act.astype(BF16), wdn_refs[k][0, 0]).astype(BF16))
        for u in range(nsub):
            rs = slice(u * sub, (u + 1) * sub)
            parts = [ys[k][u * cap:(u + 1) * cap] for k in range(MOE_GRP)]
            if nslot < sub:
                parts.append(jnp.zeros((sub - nslot, parts[0].shape[1]), BF16))
            y_u = jnp.concatenate(parts, axis=0)
            acc[rs, :] += _dot_tn(expanders[u], y_u)
        return carry

    lax.fori_loop(0, n_pass, one_pass, 0)

    @pl.when(g == n_grp - 1)
    def _():
        f_ref[0] = acc[...]


def moe_grouped_experts(order, h, wd, wgu, wdn, layer, wsgu, wsdn, tm):
    b, s, d = h.shape
    n_grp = wgu.shape[1] // MOE_GRP
    row = lambda width: pl.BlockSpec((1, tm, width), lambda bi, i, g, o: (bi, i, 0))

    def expert(shape, k):
        return pl.BlockSpec((1, 1) + shape, lambda bi, i, g, o: (layer, o[g * MOE_GRP + k], 0, 0))

    in_specs = ([row(d), row(LANES)]
                + [expert((d, 2 * EXPERT_FF), k) for k in range(MOE_GRP)]
                + [expert((EXPERT_FF, d), k) for k in range(MOE_GRP)]
                + [pl.BlockSpec((d, 2 * EXPERT_FF), lambda bi, i, g, o: (0, 0)),
                   pl.BlockSpec((EXPERT_FF, d), lambda bi, i, g, o: (0, 0))])
    grid_spec = pltpu.PrefetchScalarGridSpec(
        num_scalar_prefetch=1, grid=(b, s // tm, n_grp), in_specs=in_specs, out_specs=row(d),
        scratch_shapes=[pltpu.VMEM((tm, d), F32),
                        pltpu.VMEM((tm // MOE_SUB, LANES, MOE_SUB), F32),
                        pltpu.VMEM((tm // MOE_SUB, LANES, MOE_SUB), F32),
                        pltpu.VMEM((8, LANES), F32)])
    return pl.pallas_call(
        _moe_grouped_kernel,
        out_shape=jax.ShapeDtypeStruct((b, s, d), F32),
        grid_spec=grid_spec,
        compiler_params=_cparams(("arbitrary", "arbitrary", "arbitrary")),
        name="moe_grouped_experts",
    )(order, h, wd, *([wgu] * MOE_GRP), *([wdn] * MOE_GRP), wsgu, wsdn)


def _post_norm_kernel(x_ref, f_ref, gate_ref, lg_ref, lb_ref, out_ref, *, n_lat_rows):
    gate = _select_mod(gate_ref, x_ref.shape[1], n_lat_rows)
    out_ref[0] = _post_norm(x_ref[0], f_ref[0].astype(F32), gate, lg_ref[...], lb_ref[...])


def post_norm_rows(x, f, gate, lg, lb, n_lat_rows, rows_out=None):
    b, s, d = x.shape
    rows_out = s if rows_out is None else rows_out
    tm = _pick_tile(rows_out, (1280, 1024, 640, 512, 256))
    row = pl.BlockSpec((1, tm, d), lambda bi, i: (bi, i, 0))
    vec = pl.BlockSpec((1, d), lambda bi, i: (0, 0))
    return pl.pallas_call(
        functools.partial(_post_norm_kernel, n_lat_rows=n_lat_rows),
        out_shape=jax.ShapeDtypeStruct((b, rows_out, d), F32), grid=(b, rows_out // tm),
        in_specs=[row, row, pl.BlockSpec((1, 2, 1, d), lambda bi, i: (bi, 0, 0, 0)), vec, vec],
        out_specs=row, compiler_params=_cparams(("arbitrary", "arbitrary")), name="post_norm_rows",
    )(x, f, gate, lg.reshape(1, d), lb.reshape(1, d))


def _rope_tables(n_lat, n_ctx):
    rows = n_lat // GRID_W
    rowp = jnp.repeat(jnp.arange(rows), GRID_W).astype(F32)
    colp = jnp.tile(jnp.arange(GRID_W), rows).astype(F32)
    n_freq = DA_HEAD_DIM // 4
    inv = 1.0 / (ROPE_BASE ** (jnp.arange(n_freq, dtype=F32) / n_freq))
    ang = jnp.concatenate([rowp[:, None] * inv, colp[:, None] * inv], -1)
    cos, sin = jnp.cos(ang), jnp.sin(ang)
    cos_t = jnp.tile(cos, (1, 4))
    sin_t = jnp.tile(jnp.concatenate([-sin, sin], -1), (1, 2))
    cos_t = jnp.concatenate([cos_t, jnp.ones((n_ctx, LANES), F32)], 0)
    sin_t = jnp.concatenate([sin_t, jnp.zeros((n_ctx, LANES), F32)], 0)
    return cos_t, sin_t


def _pick_tile(total, cands):
    for t in cands:
        if total % t == 0:
            return t
    raise ValueError(f"no tile for {total}")


def _pad_cols(w, n):
    return jnp.pad(w, ((0, 0), (0, n - w.shape[1])))


def _flash_both(p_all, lam_vec, subln, lam_init, n_lat, n_ctx):
    s = n_lat + n_ctx
    tq = _pick_tile(n_lat, (2048, 1024, 512, 256))
    tk = _pick_tile(s, (3328, 1280, 1024, 512, 256))
    o = diff_flash_attention(p_all, lam_vec, subln, lam_init, tq=tq, tk=tk, nq=n_lat // tq, nk=s // tk,
                             q_off=0, k_off=0)
    return diff_flash_attention(p_all, lam_vec, subln, lam_init, tq=n_ctx, tk=n_ctx, nq=1, nk=1,
                                q_off=n_lat // n_ctx, k_off=n_lat // n_ctx, prev=o)


def kernel(x, c, ctx, c_ctx, ada_w, ada_b, ln_g, ln_b, da_w_in, da_w_o, da_lambda, da_subln, gdn_w_in, gdn_conv, gdn_a_log, gdn_dt_bias, gdn_norm, gdn_w_o, gla_w_in, gla_w_gate, gla_gate_b, gla_norm, gla_w_o, moe_router, moe_router_b, moe_w_gu, moe_w_dn, moe_ws_gu, moe_ws_dn):
    b, n, d = x.shape
    lc = ctx.shape[1]
    assert lc == ROW_TILE and n % SCAN_TILE == 0 and d == D_MODEL
    s = n + lc
    n_lat_tiles = n // ROW_TILE
    depth = ada_w.shape[0]

    xall = jnp.concatenate([x, ctx], axis=1)
    c8 = jnp.concatenate([c, c_ctx[None], jnp.zeros((8 - b - 1, d), F32)], 0)
    mods = ada_modulation(c8, ada_w, ada_b)
    rope = _rope_tables(n, lc)
    moe_tm = _pick_tile(s, (1280, 1024, 512, 256))
    moe_w_gu_bf = moe_w_gu.astype(BF16)
    moe_w_dn_bf = moe_w_dn.astype(BF16)
    big_tm = _pick_tile(s, (1280, 1024, 512, 256))
    mid_tm = _pick_tile(s, (640, 512, 256))

    for i in range(depth):
        kind, j = i % N_MIXERS, i // N_MIXERS
        m = mods[i].reshape(8, ADA_CHUNKS, d)
        mod = jnp.stack([m[:b], jnp.broadcast_to(m[b], (b, ADA_CHUNKS, d))], axis=1)
        mod = [mod[:, :, k][:, :, None, :] for k in range(ADA_CHUNKS)]

        if kind == 0:
            lam_init = 0.8 - 0.6 * math.exp(-0.3 * i)
            w = da_w_in[j]
            w = jnp.concatenate([w[:, :d] * (DA_HEAD_DIM ** -0.5 * math.log2(math.e)), w[:, d:]], 1).astype(BF16)
            p_all = modulated_projection(xall, mod[1], mod[0], w, rope=rope, n_rope=2 * d,
                                         n_lat_rows=n, head_major=True, tm=big_tm)
            o = _flash_both(p_all, da_lambda[j], da_subln[j], lam_init, n, lc)
            xall = out_projection_post_norm([o], da_w_o[j].astype(BF16), xall, mod[2],
                                            ln_g[i, 0], ln_b[i, 0], n, tm=big_tm)
        elif kind == 1:
            w = gdn_w_in[j]
            n_main = 2 * GDN_KEY_DIM + 2 * GDN_VAL_DIM
            p_main, gates_raw = modulated_projection(
                xall, mod[1], mod[0], w[:, :n_main].astype(BF16),
                w_small=_pad_cols(w[:, n_main:], LANES).astype(BF16), n_lat_rows=n, tm=mid_tm)
            alog_row = _pad_cols(gdn_a_log[j].reshape(1, -1), LANES)
            dtb_row = _pad_cols(gdn_dt_bias[j].reshape(1, -1), LANES)
            qn, kn, vv, gates = gdn_prep(p_main, gates_raw, gdn_conv[j], alog_row, dtb_row, n_lat_tiles)
            hv, hk = GDN_V_HEADS, GDN_K_HEADS
            def per_khead(t):
                t = t.reshape(b, s, 2, hk, 2)
                return jnp.transpose(t, (0, 3, 1, 2, 4)).reshape(b, hk, s, 4)
            gcol = jnp.concatenate([per_khead(gates[..., :2 * hv]), per_khead(gates[..., 2 * hv:4 * hv])], -1)
            grow = jnp.swapaxes(gcol, 2, 3)
            zeros = jnp.zeros((b, hv, GDN_HEAD_DIM, GDN_HEAD_DIM), F32)
            of, ob, scf, scb = gdn_scan(qn, kn, vv, gcol, grow, zeros, zeros, nblk=lc // SCAN_TILE,
                                        off=n // SCAN_TILE)
            of, ob, _, _ = gdn_scan(qn, kn, vv, gcol, grow, scf, scb, nblk=n // SCAN_TILE, off=0,
                                    prev=(of, ob))
            xall = out_projection_post_norm([of, ob, p_main, gdn_norm[j]], gdn_w_o[j].astype(BF16), xall,
                                            mod[2], ln_g[i, 0], ln_b[i, 0], n,
                                            gated=(GDN_V_HEADS, GDN_HEAD_DIM, 2), tm=mid_tm)
        else:
            w = gla_w_in[j]
            n_main = 2 * GLA_KEY_DIM + 2 * GLA_VAL_DIM
            p_main, gr = modulated_projection(
                xall, mod[1], mod[0], w[:, :n_main].astype(BF16),
                w_small=_pad_cols(w[:, n_main:], LANES).astype(BF16), n_lat_rows=n, tm=big_tm)
            wg = gla_w_gate[j]
            wg_pad = jnp.zeros((2, LANES, GLA_KEY_DIM), F32)
            wg_pad = wg_pad.at[0, :GLA_GATE_RANK].set(wg[0]).at[1, GLA_GATE_RANK:2 * GLA_GATE_RANK].set(wg[1])
            zeros = jnp.zeros((b, GLA_HEADS, GLA_DV, GLA_DK), F32)
            of, ob, scf, scb = gla_scan(p_main, gr, wg_pad, gla_gate_b[j], zeros, zeros,
                                        nblk=lc // SCAN_TILE, off=n // SCAN_TILE)
            of, ob, _, _ = gla_scan(p_main, gr, wg_pad, gla_gate_b[j], scf, scb, nblk=n // SCAN_TILE, off=0,
                                    prev=(of, ob))
            xall = out_projection_post_norm([of, ob, p_main, gla_norm[j]], gla_w_o[j].astype(BF16), xall,
                                            mod[2], ln_g[i, 0], ln_b[i, 0], n,
                                            gated=(GLA_HEADS, GLA_DV, 2), tm=mid_tm)

        h, wd = moe_route(xall, mod[4], mod[3], moe_router[i].T, moe_router_b[i].reshape(-1, 1), n, tm=big_tm)
        popularity = jnp.sum((wd[..., :N_EXPERTS] != 0.0).astype(jnp.int32), axis=(0, 1))
        order = jnp.argsort(popularity).astype(jnp.int32)
        f = moe_grouped_experts(order, h, wd, moe_w_gu_bf, moe_w_dn_bf, i,
                                moe_ws_gu[i].astype(BF16), moe_ws_dn[i].astype(BF16), moe_tm)
        xall = post_norm_rows(xall, f, mod[5], ln_g[i, 1], ln_b[i, 1], n,
                              rows_out=n if i == depth - 1 else None)

    return xall
```
